```python
import math
import jax
import jax.numpy as jnp
from jax import lax
import numpy as np

D_MODEL = 1024
BATCH = 16
SEQ = 4096
DEPTH = 4
DEC_BATCH = 32
DEC_SEQ = 2048
PAST_LEN = 128

HEAD_DIM = 64
N_HEADS = 8
N_KV_HEADS = 2
GROUP = N_HEADS // N_KV_HEADS
MIX_WIDTH = N_HEADS * HEAD_DIM
KV_WIDTH = N_KV_HEADS * HEAD_DIM
N_MIXERS = 3
QKV_WIDTH = MIX_WIDTH + 2 * KV_WIDTH
IN_WIDTH = N_MIXERS * QKV_WIDTH + N_MIXERS * D_MODEL
ATTN_SCALE = HEAD_DIM ** -0.5
N_META = 16
GRID_W = 64
BLOCK = 128
WINDOW = 128
NA_ROWS = 8
NA_COLS = 16
T5_BUCKETS = 32
T5_MAX_DIST = 128
ROPE_THETA = 10000.0
N_EXPERTS = 64
TOP_K = 8
N_EXPERT_GROUPS = 8
TOPK_GROUPS = 4
D_EXPERT = 256
ROUTED_SCALE = 2.5
DEEPNORM_ALPHA = (2 * DEPTH) ** 0.25
DEEPNORM_BETA = (8 * DEPTH) ** -0.25
NEG_INF = -1e30

kernel_name = 'hybrid_gated_encoder'


def layer_norm(x, gain, bias, eps=1e-5):
    xf = x.astype(jnp.float32)
    xc = xf - xf.mean(-1, keepdims=True)
    var = (xc * xc).mean(-1, keepdims=True)
    return (xc * lax.rsqrt(var + eps) * gain.astype(jnp.float32) + bias.astype(jnp.float32)).astype(x.dtype)


def rms_norm(x, gain, eps=1e-6):
    xf = x.astype(jnp.float32)
    return (xf * lax.rsqrt((xf * xf).mean(-1, keepdims=True) + eps) * gain.astype(jnp.float32)).astype(x.dtype)


def rope_1d(x, pos):
    half = x.shape[-1] // 2
    freq = ROPE_THETA ** (-jnp.arange(half, dtype=jnp.float32) / half)
    ang = pos.astype(jnp.float32)[:, None] * freq
    cos = jnp.cos(ang)[:, None, :]
    sin = jnp.sin(ang)[:, None, :]
    xf = x.astype(jnp.float32)
    x1, x2 = xf[..., :half], xf[..., half:]
    return jnp.concatenate([x1 * cos - x2 * sin, x1 * sin + x2 * cos], axis=-1).astype(x.dtype)


def axial_rope(x, pos_row, pos_col):
    half = HEAD_DIM // 2
    return jnp.concatenate([rope_1d(x[..., :half], pos_row), rope_1d(x[..., half:], pos_col)], axis=-1)


def gqa_scores(q, k):
    return jnp.einsum('bqkgd,bskd->bkgqs', q, k, preferred_element_type=jnp.float32) * ATTN_SCALE


def gqa_out(p, v):
    o = jnp.einsum('bkgqs,bskd->bqkgd', p.astype(v.dtype), v)
    return o.reshape(o.shape[0], o.shape[1], MIX_WIDTH)


def t5_bucket(rel):
    half = T5_BUCKETS // 2
    max_exact = half // 2
    n = jnp.abs(rel)
    large = max_exact + (jnp.log(jnp.maximum(n, 1).astype(jnp.float32) / max_exact)
                         / math.log(T5_MAX_DIST / max_exact) * (half - max_exact)).astype(jnp.int32)
    large = jnp.minimum(large, half - 1)
    return jnp.where(rel > 0, half, 0) + jnp.where(n < max_exact, n, large)


def t5_bias(table, rel):
    b = table[t5_bucket(rel)].astype(jnp.float32)
    return jnp.transpose(b.reshape(rel.shape[0], rel.shape[1], N_KV_HEADS, GROUP), (2, 3, 0, 1))


def sink_attend(q, k, v, bias, valid, sink):
    s = jnp.where(valid, gqa_scores(q, k) + bias, NEG_INF)
    sink_b = sink.astype(jnp.float32)[None, :, :, None, None]
    m = jnp.maximum(s.max(-1, keepdims=True), sink_b)
    e = jnp.exp(s - m)
    p = e / (e.sum(-1, keepdims=True) + jnp.exp(sink_b - m))
    return gqa_out(p, v)


def mixer_global(q, k, v, q_gain, k_gain, pos_row, pos_col):
    B, L = q.shape[:2]
    S = L - N_META
    q = axial_rope(rms_norm(q, q_gain), pos_row, pos_col).reshape(B, L, N_KV_HEADS, GROUP, HEAD_DIM)
    k = axial_rope(rms_norm(k, k_gain), pos_row, pos_col)

    def attend_block(qb):
        return gqa_out(jax.nn.softmax(gqa_scores(qb, k), axis=-1), v)

    out_meta = attend_block(q[:, :N_META])
    q_blocks = jnp.moveaxis(q[:, N_META:].reshape(B, S // BLOCK, BLOCK, N_KV_HEADS, GROUP, HEAD_DIM), 1, 0)
    out_real = jnp.moveaxis(lax.map(attend_block, q_blocks), 0, 1).reshape(B, S, MIX_WIDTH)
    return jnp.concatenate([out_meta, out_real], axis=1)


def mixer_window(q, k, v, sink, t5_table):
    B, L = q.shape[:2]
    S = L - N_META
    nb = S // BLOCK
    band = 3 * BLOCK
    q = q.reshape(B, L, N_KV_HEADS, GROUP, HEAD_DIM)
    sink = sink.reshape(N_KV_HEADS, GROUP)
    k_meta, v_meta = k[:, :N_META], v[:, :N_META]
    k_real, v_real = k[:, N_META:], v[:, N_META:]
    meta_pos = jnp.arange(N_META)

    key_pos = jnp.concatenate([meta_pos, N_META + jnp.arange(BLOCK)])
    rel = key_pos[None, :] - meta_pos[:, None]
    valid = (key_pos[None, :] < N_META) | (jnp.abs(rel) <= WINDOW)
    out_meta = sink_attend(q[:, :N_META], jnp.concatenate([k_meta, k_real[:, :BLOCK]], axis=1),
                           jnp.concatenate([v_meta, v_real[:, :BLOCK]], axis=1),
                           t5_bias(t5_table, rel), valid, sink)

    pad = jnp.zeros((B, BLOCK, N_KV_HEADS, HEAD_DIM), k.dtype)
    k_pad = jnp.concatenate([pad, k_real, pad], axis=1)
    v_pad = jnp.concatenate([pad, v_real, pad], axis=1)
    q_blocks = jnp.moveaxis(q[:, N_META:].reshape(B, nb, BLOCK, N_KV_HEADS, GROUP, HEAD_DIM), 1, 0)

    def attend_block(args):
        b, qb = args
        kb = jnp.concatenate([k_meta, lax.dynamic_slice_in_dim(k_pad, b * BLOCK, band, axis=1)], axis=1)
        vb = jnp.concatenate([v_meta, lax.dynamic_slice_in_dim(v_pad, b * BLOCK, band, axis=1)], axis=1)
        qi = b * BLOCK + jnp.arange(BLOCK)
        kj = (b - 1) * BLOCK + jnp.arange(band)
        kpos = jnp.concatenate([meta_pos, N_META + kj])
        rel_b = kpos[None, :] - (N_META + qi)[:, None]
        in_range = jnp.concatenate([jnp.ones((N_META,), bool), (kj >= 0) & (kj < S)])
        valid_b = in_range[None, :] & ((kpos[None, :] < N_META) | (jnp.abs(rel_b) <= WINDOW))
        return sink_attend(qb, kb, vb, t5_bias(t5_table, rel_b), valid_b, sink)

    out_real = jnp.moveaxis(lax.map(attend_block, (jnp.arange(nb), q_blocks)), 0, 1).reshape(B, S, MIX_WIDTH)
    return jnp.concatenate([out_meta, out_real], axis=1)


def mixer_neighbourhood(q, k, v, rpb, meta_bias):
    B, L = q.shape[:2]
    S = L - N_META
    rows = S // GRID_W
    kr = min(NA_ROWS, rows)
    q = q.reshape(B, L, N_KV_HEADS, GROUP, HEAD_DIM)
    meta_bias = meta_bias.astype(jnp.float32).reshape(N_KV_HEADS, GROUP, 1, N_META)
    k_meta, v_meta = k[:, :N_META], v[:, :N_META]

    out_meta = gqa_out(jax.nn.softmax(gqa_scores(q[:, :N_META], k_meta) + meta_bias, axis=-1), v_meta)

    q_grid = q[:, N_META:].reshape(B, rows, GRID_W, N_KV_HEADS, GROUP, HEAD_DIM)
    k_grid = k[:, N_META:].reshape(B, rows, GRID_W, N_KV_HEADS, HEAD_DIM)
    v_grid = v[:, N_META:].reshape(B, rows, GRID_W, N_KV_HEADS, HEAD_DIM)
    cols = np.arange(GRID_W)
    col_start = np.clip(cols - NA_COLS // 2, 0, GRID_W - NA_COLS)
    col_idx = col_start[:, None] + np.arange(NA_COLS)
    dc_idx = col_idx - cols[:, None] + (NA_COLS - 1)
    n_win = kr * NA_COLS

    def attend_row(r):
        rs = jnp.clip(r - kr // 2, 0, rows - kr)
        q_row = lax.dynamic_index_in_dim(q_grid, r, axis=1, keepdims=False)
        k_win = lax.dynamic_slice_in_dim(k_grid, rs, kr, axis=1)[:, :, col_idx]
        v_win = lax.dynamic_slice_in_dim(v_grid, rs, kr, axis=1)[:, :, col_idx]
        dr_idx = rs + jnp.arange(kr) - r + (NA_ROWS - 1)
        bias = rpb[:, dr_idx[:, None, None], dc_idx[None]].astype(jnp.float32)
        bias = jnp.transpose(bias, (0, 2, 1, 3)).reshape(N_KV_HEADS, GROUP, GRID_W, n_win)
        s_win = jnp.einsum('bwkgd,brwckd->bkgwrc', q_row, k_win,
                           preferred_element_type=jnp.float32).reshape(B, N_KV_HEADS, GROUP, GRID_W, n_win)
        s_win = s_win * ATTN_SCALE + bias
        s_meta = gqa_scores(q_row, k_meta) + meta_bias
        p = jax.nn.softmax(jnp.concatenate([s_win, s_meta], axis=-1), axis=-1).astype(v.dtype)
        p_win = p[..., :n_win].reshape(B, N_KV_HEADS, GROUP, GRID_W, kr, NA_COLS)
        o = (jnp.einsum('bkgwrc,brwckd->bwkgd', p_win, v_win)
             + jnp.einsum('bkgwm,bmkd->bwkgd', p[..., n_win:], v_meta))
        return o.reshape(B, GRID_W, MIX_WIDTH)

    out_real = jnp.moveaxis(lax.map(attend_row, jnp.arange(rows)), 0, 1).reshape(B, S, MIX_WIDTH)
    return jnp.concatenate([out_meta, out_real], axis=1)


def token_mixer(h, w_in, q_gain, k_gain, sink, t5_table, na_rpb, na_meta_bias, w_branch, w_out, pos_row, pos_col):
    B, L, _ = h.shape
    proj = h @ w_in
    heads = []
    off = 0
    for _ in range(N_MIXERS):
        q = proj[..., off:off + MIX_WIDTH].reshape(B, L, N_HEADS, HEAD_DIM)
        k = proj[..., off + MIX_WIDTH:off + MIX_WIDTH + KV_WIDTH].reshape(B, L, N_KV_HEADS, HEAD_DIM)
        v = proj[..., off + MIX_WIDTH + KV_WIDTH:off + QKV_WIDTH].reshape(B, L, N_KV_HEADS, HEAD_DIM)
        heads.append((q, k, v))
        off += QKV_WIDTH
    gate_logits = proj[..., off:].reshape(B, L, N_MIXERS, D_MODEL)
    (qa, ka, va), (qb, kb, vb), (qc, kc, vc) = heads
    o = jnp.stack([mixer_global(qa, ka, va, q_gain, k_gain, pos_row, pos_col),
                   mixer_window(qb, kb, vb, sink, t5_table),
                   mixer_neighbourhood(qc, kc, vc, na_rpb, na_meta_bias)], axis=2)
    branches = jnp.einsum('blnw,nwd->blnd', o, w_branch)
    merged = (jax.nn.sigmoid(gate_logits) * branches).sum(axis=2)
    return merged @ w_out


def swiglu(x, w_gate_up, w_down):
    gu = x @ w_gate_up
    return (jax.nn.silu(gu[..., :D_EXPERT]) * gu[..., D_EXPERT:]) @ w_down


def moe_ffn(h, w_router, router_bias, w_expert_gate_up, w_expert_down, w_shared_gate_up, w_shared_down):
    B, L, D = h.shape
    x = h.reshape(B * L, D)
    T = x.shape[0]
    scores = jax.nn.sigmoid(jnp.dot(x, w_router, preferred_element_type=jnp.float32))
    sel = scores + router_bias.astype(jnp.float32)
    per_group = N_EXPERTS // N_EXPERT_GROUPS
    group_score = lax.top_k(sel.reshape(T, N_EXPERT_GROUPS, per_group), 2)[0].sum(-1)
    _, top_groups = lax.top_k(group_score, TOPK_GROUPS)
    group_mask = (top_groups[..., None] == jnp.arange(N_EXPERT_GROUPS)).any(axis=1)
    expert_mask = jnp.repeat(group_mask, per_group, axis=1)
    _, top_idx = lax.top_k(jnp.where(expert_mask, sel, NEG_INF), TOP_K)
    w = jnp.take_along_axis(scores, top_idx, axis=1)
    w = w / w.sum(-1, keepdims=True) * ROUTED_SCALE
    gates = jnp.zeros((T, N_EXPERTS), jnp.float32).at[jnp.arange(T)[:, None], top_idx].set(w)

    def expert_step(acc, expert):
        wgu, wd, g = expert
        return acc + g[:, None].astype(x.dtype) * swiglu(x, wgu, wd), None

    routed, _ = lax.scan(expert_step, jnp.zeros_like(x), (w_expert_gate_up, w_expert_down, gates.T))
    return (routed + swiglu(x, w_shared_gate_up, w_shared_down)).reshape(B, L, D)


def encoder_trunk(x, params):
    (meta_tokens, ln_in_g, ln_in_b, t5_table, w_in, q_gain, k_gain, sink, na_rpb, na_meta_bias,
     w_branch, w_out, ln1_g, ln1_b, w_router, router_bias, w_expert_gate_up, w_expert_down,
     w_shared_gate_up, w_shared_down, ln2_g, ln2_b) = params
    B, S, D = x.shape
    t = jnp.arange(S)
    meta_pos = jnp.arange(N_META) - N_META
    pos_row = jnp.concatenate([meta_pos, t // GRID_W])
    pos_col = jnp.concatenate([meta_pos, t % GRID_W])
    h = jnp.concatenate([jnp.broadcast_to(meta_tokens.astype(x.dtype)[None], (B, N_META, D)), x], axis=1)
    h = layer_norm(h, ln_in_g, ln_in_b)
    for l in range(DEPTH):
        mix = token_mixer(h, w_in[l], q_gain[l], k_gain[l], sink[l], t5_table, na_rpb[l], na_meta_bias[l],
                          w_branch[l], w_out[l], pos_row, pos_col)
        h = layer_norm(DEEPNORM_ALPHA * h + mix, ln1_g[l], ln1_b[l])
        ffn = moe_ffn(h, w_router[l], router_bias[l], w_expert_gate_up[l], w_expert_down[l],
                      w_shared_gate_up[l], w_shared_down[l])
        h = layer_norm(DEEPNORM_ALPHA * h + ffn, ln2_g[l], ln2_b[l])
    return h[:, N_META:]


def setup_inputs(seed: int = 0) -> dict:
    key = jax.random.key(seed)
    ks = jax.random.split(key, 26)
    nrm = jax.random.normal
    D = D_MODEL
    return {
        'x_prompt': nrm(ks[0], (BATCH, SEQ, D), jnp.float32),
        'x_sample': nrm(ks[1], (DEC_BATCH, DEC_SEQ, D), jnp.float32),
        'meta_tokens': nrm(ks[2], (N_META, D), jnp.float32),
        'ln_in_g': 1.0 + 0.02 * nrm(ks[3], (D,), jnp.float32),
        'ln_in_b': 0.02 * nrm(ks[4], (D,), jnp.float32),
        't5_table': 0.1 * nrm(ks[5], (T5_BUCKETS, N_HEADS), jnp.float32),
        'w_in': nrm(ks[6], (DEPTH, D, IN_WIDTH), jnp.float32) * D ** -0.5,
        'q_gain': 1.0 + 0.02 * nrm(ks[7], (DEPTH, HEAD_DIM), jnp.float32),
        'k_gain': 1.0 + 0.02 * nrm(ks[8], (DEPTH, HEAD_DIM), jnp.float32),
        'sink': 0.5 * nrm(ks[9], (DEPTH, N_HEADS), jnp.float32),
        'na_rpb': 0.1 * nrm(ks[10], (DEPTH, N_HEADS, 2 * NA_ROWS - 1, 2 * NA_COLS - 1), jnp.float32),
        'na_meta_bias': 0.1 * nrm(ks[11], (DEPTH, N_HEADS, N_META), jnp.float32),
        'w_branch': nrm(ks[12], (DEPTH, N_MIXERS, MIX_WIDTH, D), jnp.float32) * (MIX_WIDTH ** -0.5 * DEEPNORM_BETA),
        'w_out': nrm(ks[13], (DEPTH, D, D), jnp.float32) * (D ** -0.5 * DEEPNORM_BETA),
        'ln1_g': 1.0 + 0.02 * nrm(ks[14], (DEPTH, D), jnp.float32),
        'ln1_b': 0.02 * nrm(ks[15], (DEPTH, D), jnp.float32),
        'w_router': nrm(ks[16], (DEPTH, D, N_EXPERTS), jnp.float32) * D ** -0.5,
        'router_bias': 0.01 * nrm(ks[17], (DEPTH, N_EXPERTS), jnp.float32),
        'w_expert_gate_up': nrm(ks[18], (DEPTH, N_EXPERTS, D, 2 * D_EXPERT), jnp.float32) * D ** -0.5,
        'w_expert_down': nrm(ks[19], (DEPTH, N_EXPERTS, D_EXPERT, D), jnp.float32) * (D_EXPERT ** -0.5 * DEEPNORM_BETA),
        'w_shared_gate_up': nrm(ks[20], (DEPTH, D, 2 * D_EXPERT), jnp.float32) * D ** -0.5,
        'w_shared_down': nrm(ks[21], (DEPTH, D_EXPERT, D), jnp.float32) * (D_EXPERT ** -0.5 * DEEPNORM_BETA),
        'ln2_g': 1.0 + 0.02 * nrm(ks[22], (DEPTH, D), jnp.float32),
        'ln2_b': 0.02 * nrm(ks[23], (DEPTH, D), jnp.float32),
    }


def reference(x_prompt, x_sample, meta_tokens, ln_in_g, ln_in_b, t5_table, w_in, q_gain, k_gain, sink,
              na_rpb, na_meta_bias, w_branch, w_out, ln1_g, ln1_b, w_router, router_bias,
              w_expert_gate_up, w_expert_down, w_shared_gate_up, w_shared_down, ln2_g, ln2_b):
    params = (meta_tokens, ln_in_g, ln_in_b, t5_table, w_in, q_gain, k_gain, sink, na_rpb, na_meta_bias,
              w_branch, w_out, ln1_g, ln1_b, w_router, router_bias, w_expert_gate_up, w_expert_down,
              w_shared_gate_up, w_shared_down, ln2_g, ln2_b)
    y_prompt = encoder_trunk(x_prompt, params)
    y_sample = encoder_trunk(x_sample, params)
    return (y_prompt, y_sample)
```

```python
import functools
import math

import numpy as np
import jax
import jax.numpy as jnp
from jax import lax
from jax.experimental import pallas as pl
from jax.experimental.pallas import tpu as pltpu

F32 = jnp.float32
BF16 = jnp.bfloat16

D_MODEL = 1024
HEAD_DIM = 64
N_HEADS = 8
N_KV_HEADS = 2
GROUP = N_HEADS // N_KV_HEADS
MIX_WIDTH = N_HEADS * HEAD_DIM
KV_WIDTH = N_KV_HEADS * HEAD_DIM
N_MIXERS = 3
QKV_WIDTH = MIX_WIDTH + 2 * KV_WIDTH
N_META = 16
GRID_W = 64
BLOCK = 128
WINDOW = 128
NA_ROWS = 8
NA_COLS = 16
T5_BUCKETS = 32
T5_MAX_DIST = 128
ROPE_THETA = 10000.0
N_EXPERTS = 64
TOP_K = 8
N_EXPERT_GROUPS = 8
TOPK_GROUPS = 4
D_EXPERT = 256
ROUTED_SCALE = 2.5
NEG_INF = -1e30
LANES = 128

TM = 512
TQ_GLOBAL = 128
TK_GLOBAL = 512
NA_QROWS = 8
VMEM_LIMIT = 56 * 1024 * 1024


def _cparams(sem):
    return pltpu.CompilerParams(dimension_semantics=sem, vmem_limit_bytes=VMEM_LIMIT)


def _dot(a, b):
    return jnp.dot(a, b, preferred_element_type=F32)


def _dot_nt(a, b):
    return lax.dot_general(a, b, (((1,), (1,)), ((), ())), preferred_element_type=F32)


def _split_bf16(x):
    hi = x.astype(BF16)
    lo = (x - hi.astype(F32)).astype(BF16)
    return hi, lo


def _layer_norm(x, g, b):
    mu = jnp.mean(x, axis=-1, keepdims=True)
    xc = x - mu
    var = jnp.mean(xc * xc, axis=-1, keepdims=True)
    return xc * lax.rsqrt(var + 1e-5) * g + b


def _embed_ln_kernel(x_ref, g_ref, b_ref, o_ref):
    o_ref[...] = _layer_norm(x_ref[...], g_ref[...], b_ref[...])


def _embed_ln(x, g, b):
    R = x.shape[0]
    return pl.pallas_call(
        _embed_ln_kernel,
        grid=(R // TM,),
        in_specs=[pl.BlockSpec((TM, D_MODEL), lambda i: (i, 0)),
                  pl.BlockSpec((1, D_MODEL), lambda i: (0, 0)),
                  pl.BlockSpec((1, D_MODEL), lambda i: (0, 0))],
        out_specs=pl.BlockSpec((TM, D_MODEL), lambda i: (i, 0)),
        out_shape=jax.ShapeDtypeStruct((R, D_MODEL), F32),
        compiler_params=_cparams(("parallel",)),
        name="embed_ln",
    )(x, g, b)


def _rope_slot(x, cos, sin_signed, first_half):
    fwd = pltpu.roll(x, LANES - 16, 1)
    bwd = pltpu.roll(x, 16, 1)
    return x * cos + jnp.where(first_half, fwd, bwd) * sin_signed


def _head_rms(x, ones_bd, gain):
    hi, lo = _split_bf16(x * x)
    ss = _dot(hi, ones_bd) + _dot(lo, ones_bd)
    return x * lax.rsqrt(ss * (1.0 / HEAD_DIM) + 1e-6) * gain


def _inproj_kernel(h_ref, w_ref, cos_ref, sin_ref, qg_ref, kg_ref, ones_ref, q_ref, k_ref, v_ref):
    x = h_ref[...].astype(BF16)
    cos = cos_ref[...]
    sin = sin_ref[...]
    lane = lax.broadcasted_iota(jnp.int32, cos.shape, 1)
    first_half = (lane % 32) < 16
    scale = HEAD_DIM ** -0.5
    qw = N_MIXERS * MIX_WIDTH
    qa = _head_rms(_dot(x, w_ref[:, 0:MIX_WIDTH]), ones_ref[...], qg_ref[...])
    for s in range(MIX_WIDTH // LANES):
        sl = slice(s * LANES, (s + 1) * LANES)
        q_ref[:, sl] = (_rope_slot(qa[:, sl], cos, sin, first_half) * scale).astype(BF16)
    ka = _head_rms(_dot(x, w_ref[:, qw:qw + KV_WIDTH]), ones_ref[0:LANES, 0:LANES], kg_ref[...])
    k_ref[:, 0:KV_WIDTH] = _rope_slot(ka, cos, sin, first_half).astype(BF16)
    for n in range(1, N_MIXERS):
        q_ref[:, n * MIX_WIDTH:(n + 1) * MIX_WIDTH] = (
            _dot(x, w_ref[:, n * MIX_WIDTH:(n + 1) * MIX_WIDTH]) * scale).astype(BF16)
        k_ref[:, n * KV_WIDTH:(n + 1) * KV_WIDTH] = _dot(
            x, w_ref[:, qw + n * KV_WIDTH:qw + (n + 1) * KV_WIDTH]).astype(BF16)
    vw = qw + N_MIXERS * KV_WIDTH
    v_ref[...] = _dot(x, w_ref[:, vw:vw + N_MIXERS * KV_WIDTH]).astype(BF16)


def _inproj(h, w_qkv, cos_tab, sin_tab, q_gain, k_gain, ones_bd, pos_block):
    R = h.shape[0]
    const = lambda i: (0, 0)
    return pl.pallas_call(
        _inproj_kernel,
        grid=(R // TM,),
        in_specs=[pl.BlockSpec((TM, D_MODEL), lambda i: (i, 0)),
                  pl.BlockSpec(w_qkv.shape, const),
                  pl.BlockSpec((TM, LANES), lambda i: (pos_block(i), 0)),
                  pl.BlockSpec((TM, LANES), lambda i: (pos_block(i), 0)),
                  pl.BlockSpec((1, MIX_WIDTH), const),
                  pl.BlockSpec((1, KV_WIDTH), const),
                  pl.BlockSpec((MIX_WIDTH, MIX_WIDTH), const)],
        out_specs=[pl.BlockSpec((TM, N_MIXERS * MIX_WIDTH), lambda i: (i, 0)),
                   pl.BlockSpec((TM, N_MIXERS * KV_WIDTH), lambda i: (i, 0)),
                   pl.BlockSpec((TM, N_MIXERS * KV_WIDTH), lambda i: (i, 0))],
        out_shape=[jax.ShapeDtypeStruct((R, N_MIXERS * MIX_WIDTH), BF16),
                   jax.ShapeDtypeStruct((R, N_MIXERS * KV_WIDTH), BF16),
                   jax.ShapeDtypeStruct((R, N_MIXERS * KV_WIDTH), BF16)],
        compiler_params=_cparams(("parallel",)),
        name="inproj",
    )(h, w_qkv, cos_tab, sin_tab, q_gain, k_gain, ones_bd)


def _group_queries(q, j):
    lane = lax.broadcasted_iota(jnp.int32, (q.shape[0], LANES), 1)
    keep = (lane < HEAD_DIM) if j == 0 else (lane >= HEAD_DIM)
    parts = []
    for hh in range(GROUP):
        h = GROUP * j + hh
        slot = q[:, (h // 2) * LANES:(h // 2 + 1) * LANES]
        if h % 2 != j:
            slot = pltpu.roll(slot, HEAD_DIM, 1)
        parts.append(jnp.where(keep, slot, 0.0))
    return jnp.concatenate(parts, axis=0).astype(BF16)


def _ungroup_outputs(out, j, T):
    lane = lax.broadcasted_iota(jnp.int32, (T, LANES), 1)
    lo = lane < HEAD_DIM
    slots = []
    for s in range(2):
        even = out[(2 * s) * T:(2 * s + 1) * T]
        odd = out[(2 * s + 1) * T:(2 * s + 2) * T]
        if j == 0:
            slots.append(jnp.where(lo, even, pltpu.roll(odd, HEAD_DIM, 1)))
        else:
            slots.append(jnp.where(lo, pltpu.roll(even, HEAD_DIM, 1), odd))
    return jnp.concatenate(slots, axis=1)


def _head_rows(vals, T):
    return jnp.concatenate([jnp.broadcast_to(v, (T, v.shape[-1])) for v in vals], axis=0)


def _global_kernel(q_ref, k_ref, v_ref, km_ref, vm_ref, *rest, S):
    o_ref = rest[-1]
    T = q_ref.shape[0]
    q = q_ref[...].astype(F32)
    km = km_ref[...]
    vm = vm_ref[...]
    for j in range(N_KV_HEADS):
        qj = _group_queries(q, j)
        s_m = _dot_nt(qj, km)
        m0 = jnp.max(s_m, axis=-1, keepdims=True)
        p_m = jnp.exp(s_m - m0)
        l0 = jnp.sum(p_m, axis=-1, keepdims=True)
        acc0 = _dot(p_m.astype(BF16), vm)

        def body(c, carry):
            m, l, acc = carry
            start = pl.multiple_of(c * TK_GLOBAL, TK_GLOBAL)
            kc = k_ref[pl.ds(start, TK_GLOBAL), :]
            vc = v_ref[pl.ds(start, TK_GLOBAL), :]
            s = _dot_nt(qj, kc)
            m_new = jnp.maximum(m, jnp.max(s, axis=-1, keepdims=True))
            a = jnp.exp(m - m_new)
            p = jnp.exp(s - m_new)
            l = a * l + jnp.sum(p, axis=-1, keepdims=True)
            acc = a * acc + _dot(p.astype(BF16), vc)
            return m_new, l, acc

        _, l, acc = lax.fori_loop(0, S // TK_GLOBAL, body, (m0, l0, acc0))
        o_ref[:, j * 2 * LANES:(j + 1) * 2 * LANES] = _ungroup_outputs(acc / l, j, T).astype(BF16)


def _window_kernel(sink_ref, q_ref, kp_ref, kc_ref, kn_ref, vp_ref, vc_ref, vn_ref, km_ref, vm_ref,
                   bband_ref, bmeta_ref, *rest, nb):
    o_ref = rest[-1]
    i = pl.program_id(1)
    T = q_ref.shape[0]
    q = q_ref[...].astype(F32)
    kband = jnp.concatenate([kp_ref[...], kc_ref[...], kn_ref[...]], axis=0)
    vband = jnp.concatenate([vp_ref[...], vc_ref[...], vn_ref[...]], axis=0)
    col = lax.broadcasted_iota(jnp.int32, (1, 3 * BLOCK), 1)
    in_range = ((col >= BLOCK) | (i > 0)) & ((col < 2 * BLOCK) | (i < nb - 1))
    for j in range(N_KV_HEADS):
        qj = _group_queries(q, j)
        hs = slice(GROUP * j, GROUP * (j + 1))
        s_b = _dot_nt(qj, kband) + bband_ref[hs].reshape(GROUP * T, 3 * BLOCK)
        s_b = jnp.where(in_range, s_b, NEG_INF)
        s_m = _dot_nt(qj, km_ref[...]) + bmeta_ref[hs].reshape(GROUP * T, N_META)
        sink = jnp.concatenate([jnp.full((T, 1), sink_ref[GROUP * j + hh], F32) for hh in range(GROUP)], axis=0)
        m = jnp.maximum(jnp.maximum(jnp.max(s_b, axis=-1, keepdims=True),
                                    jnp.max(s_m, axis=-1, keepdims=True)), sink)
        e_b = jnp.exp(s_b - m)
        e_m = jnp.exp(s_m - m)
        denom = (jnp.sum(e_b, axis=-1, keepdims=True) + jnp.sum(e_m, axis=-1, keepdims=True)
                 + jnp.exp(sink - m))
        acc = _dot(e_b.astype(BF16), vband) + _dot(e_m.astype(BF16), vm_ref[...])
        o_ref[:, j * 2 * LANES:(j + 1) * 2 * LANES] = _ungroup_outputs(acc / denom, j, T).astype(BF16)


def _na_kernel(q_ref, k_ref, v_ref, km_ref, vm_ref, bias_ref, mbias_ref, *rest, rows):
    o_ref = rest[-1]
    blk = pl.program_id(1)
    W = GRID_W
    nkeys = NA_ROWS * W
    km = km_ref[...]
    vm = vm_ref[...]
    for j in range(N_KV_HEADS):
        mb = _head_rows([mbias_ref[GROUP * j + hh:GROUP * j + hh + 1, :] for hh in range(GROUP)], W)

        def row_body(rr, carry, j=j, mb=mb):
            r = blk * NA_QROWS + rr
            rs = jnp.clip(r - NA_ROWS // 2, 0, rows - NA_ROWS)
            delta = r - rs
            qoff = pl.multiple_of(rr * W, W)
            koff = pl.multiple_of(rs * W, W)
            qj = _group_queries(q_ref[pl.ds(qoff, W), :].astype(F32), j)
            kw = k_ref[pl.ds(koff, nkeys), :]
            vw = v_ref[pl.ds(koff, nkeys), :]
            s_w = _dot_nt(qj, kw) + bias_ref[delta, pl.ds(GROUP * j, GROUP)].reshape(GROUP * W, nkeys)
            s_m = _dot_nt(qj, km) + mb
            m = jnp.maximum(jnp.max(s_w, axis=-1, keepdims=True), jnp.max(s_m, axis=-1, keepdims=True))
            e_w = jnp.exp(s_w - m)
            e_m = jnp.exp(s_m - m)
            denom = jnp.sum(e_w, axis=-1, keepdims=True) + jnp.sum(e_m, axis=-1, keepdims=True)
            acc = _dot(e_w.astype(BF16), vw) + _dot(e_m.astype(BF16), vm)
            o_ref[pl.ds(qoff, W), j * 2 * LANES:(j + 1) * 2 * LANES] = (
                _ungroup_outputs(acc / denom, j, W).astype(BF16))
            return carry

        lax.fori_loop(0, NA_QROWS, row_body, 0)


def _meta_kernel(sink_ref, q_ref, ka_ref, va_ref, kb_ref, vb_ref, km_ref, vm_ref,
                 bq_meta_ref, bq_blk_ref, mbias_ref, o_in_ref, o_ref):
    del o_in_ref
    T = N_META
    q = q_ref[...].astype(F32)
    km_all = km_ref[...]
    vm_all = vm_ref[...]

    def finish(n, j, acc, denom):
        lo = n * MIX_WIDTH + j * 2 * LANES
        o_ref[:, lo:lo + 2 * LANES] = _ungroup_outputs(acc / denom, j, T).astype(BF16)

    for j in range(N_KV_HEADS):
        hs = slice(GROUP * j, GROUP * (j + 1))
        qj = _group_queries(q[:, 0:MIX_WIDTH], j)
        km, vm = km_all[:, 0:KV_WIDTH], vm_all[:, 0:KV_WIDTH]
        s_r = _dot_nt(qj, ka_ref[...])
        s_m = _dot_nt(qj, km)
        m = jnp.maximum(jnp.max(s_r, axis=-1, keepdims=True), jnp.max(s_m, axis=-1, keepdims=True))
        e_r = jnp.exp(s_r - m)
        e_m = jnp.exp(s_m - m)
        denom = jnp.sum(e_r, axis=-1, keepdims=True) + jnp.sum(e_m, axis=-1, keepdims=True)
        finish(0, j, _dot(e_r.astype(BF16), va_ref[...]) + _dot(e_m.astype(BF16), vm), denom)
        qj = _group_queries(q[:, MIX_WIDTH:2 * MIX_WIDTH], j)
        km, vm = km_all[:, KV_WIDTH:2 * KV_WIDTH], vm_all[:, KV_WIDTH:2 * KV_WIDTH]
        s_r = _dot_nt(qj, kb_ref[...]) + bq_blk_ref[hs].reshape(GROUP * T, BLOCK)
        s_m = _dot_nt(qj, km) + bq_meta_ref[hs].reshape(GROUP * T, N_META)
        sink = jnp.concatenate([jnp.full((T, 1), sink_ref[GROUP * j + hh], F32) for hh in range(GROUP)], axis=0)
        m = jnp.maximum(jnp.maximum(jnp.max(s_r, axis=-1, keepdims=True),
                                    jnp.max(s_m, axis=-1, keepdims=True)), sink)
        e_r = jnp.exp(s_r - m)
        e_m = jnp.exp(s_m - m)
        denom = (jnp.sum(e_r, axis=-1, keepdims=True) + jnp.sum(e_m, axis=-1, keepdims=True)
                 + jnp.exp(sink - m))
        finish(1, j, _dot(e_r.astype(BF16), vb_ref[...]) + _dot(e_m.astype(BF16), vm), denom)
        qj = _group_queries(q[:, 2 * MIX_WIDTH:3 * MIX_WIDTH], j)
        km, vm = km_all[:, 2 * KV_WIDTH:3 * KV_WIDTH], vm_all[:, 2 * KV_WIDTH:3 * KV_WIDTH]
        mb = _head_rows([mbias_ref[GROUP * j + hh:GROUP * j + hh + 1, :] for hh in range(GROUP)], T)
        s_m = _dot_nt(qj, km) + mb
        m = jnp.max(s_m, axis=-1, keepdims=True)
        e_m = jnp.exp(s_m - m)
        finish(2, j, _dot(e_m.astype(BF16), vm), jnp.sum(e_m, axis=-1, keepdims=True))


class _Group:
    def __init__(self, B, S, real_base, meta_batch_base, meta_base, n_meta_blocks):
        self.B, self.S = B, S
        self.real_base = real_base
        self.meta_blk0 = meta_base // N_META + meta_batch_base
        self.n_meta_blocks = n_meta_blocks
        assert real_base % S == 0 and S % TM == 0 and meta_base % N_META == 0


def _alias_args(o_prev, n_inputs):
    if o_prev is None:
        return [], [], {}
    return [o_prev], [pl.BlockSpec(memory_space=pl.ANY)], {n_inputs: 0}


def _global_attn(grp, q_all, k_all, v_all, o_prev):
    B, S = grp.B, grp.S
    nq = S // TQ_GLOBAL
    qb0 = grp.real_base // TQ_GLOBAL
    sb0 = grp.real_base // S
    mb0 = grp.meta_blk0
    in_specs = [pl.BlockSpec((TQ_GLOBAL, MIX_WIDTH), lambda b, i: (qb0 + b * nq + i, 0)),
                pl.BlockSpec((S, KV_WIDTH), lambda b, i: (sb0 + b, 0)),
                pl.BlockSpec((S, KV_WIDTH), lambda b, i: (sb0 + b, 0)),
                pl.BlockSpec((N_META, KV_WIDTH), lambda b, i: (mb0 + b, 0)),
                pl.BlockSpec((N_META, KV_WIDTH), lambda b, i: (mb0 + b, 0))]
    extra, extra_specs, aliases = _alias_args(o_prev, len(in_specs))
    return pl.pallas_call(
        functools.partial(_global_kernel, S=S),
        grid=(B, nq),
        in_specs=in_specs + extra_specs,
        out_specs=pl.BlockSpec((TQ_GLOBAL, MIX_WIDTH), lambda b, i: (qb0 + b * nq + i, 0)),
        out_shape=jax.ShapeDtypeStruct((q_all.shape[0], N_MIXERS * MIX_WIDTH), BF16),
        input_output_aliases=aliases,
        compiler_params=_cparams(("parallel", "arbitrary")),
        name="mixer_global",
    )(q_all, k_all, v_all, k_all, v_all, *extra)


def _window_attn(grp, q_all, k_all, v_all, sink, bband, bmeta, o_prev):
    B, S = grp.B, grp.S
    nb = S // BLOCK
    qb0 = grp.real_base // BLOCK
    mb0 = grp.meta_blk0
    cur = lambda b, i, sink: (qb0 + b * nb + i, 1)
    prv = lambda b, i, sink: (qb0 + b * nb + jnp.maximum(i - 1, 0), 1)
    nxt = lambda b, i, sink: (qb0 + b * nb + jnp.minimum(i + 1, nb - 1), 1)
    met = lambda b, i, sink: (mb0 + b, 1)
    kv = lambda im: pl.BlockSpec((BLOCK, KV_WIDTH), im)
    in_specs = [pl.BlockSpec((BLOCK, MIX_WIDTH), cur),
                kv(prv), kv(cur), kv(nxt), kv(prv), kv(cur), kv(nxt),
                pl.BlockSpec((N_META, KV_WIDTH), met), pl.BlockSpec((N_META, KV_WIDTH), met),
                pl.BlockSpec((N_HEADS, BLOCK, 3 * BLOCK), lambda b, i, sink: (0, 0, 0)),
                pl.BlockSpec((N_HEADS, BLOCK, N_META), lambda b, i, sink: (0, i, 0))]
    extra, extra_specs, aliases = _alias_args(o_prev, len(in_specs) + 1)
    return pl.pallas_call(
        functools.partial(_window_kernel, nb=nb),
        grid_spec=pltpu.PrefetchScalarGridSpec(
            num_scalar_prefetch=1,
            grid=(B, nb),
            in_specs=in_specs + extra_specs,
            out_specs=pl.BlockSpec((BLOCK, MIX_WIDTH), cur)),
        out_shape=jax.ShapeDtypeStruct((q_all.shape[0], N_MIXERS * MIX_WIDTH), BF16),
        input_output_aliases=aliases,
        compiler_params=_cparams(("parallel", "arbitrary")),
        name="mixer_window",
    )(sink, q_all, k_all, k_all, k_all, v_all, v_all, v_all, k_all, v_all, bband, bmeta, *extra)


def _na_attn(grp, q_all, k_all, v_all, na_bias, na_mbias, o_prev):
    B, S = grp.B, grp.S
    rows = S // GRID_W
    tq = NA_QROWS * GRID_W
    nq = S // tq
    qb0 = grp.real_base // tq
    sb0 = grp.real_base // S
    mb0 = grp.meta_blk0
    in_specs = [pl.BlockSpec((tq, MIX_WIDTH), lambda b, i: (qb0 + b * nq + i, 2)),
                pl.BlockSpec((S, KV_WIDTH), lambda b, i: (sb0 + b, 2)),
                pl.BlockSpec((S, KV_WIDTH), lambda b, i: (sb0 + b, 2)),
                pl.BlockSpec((N_META, KV_WIDTH), lambda b, i: (mb0 + b, 2)),
                pl.BlockSpec((N_META, KV_WIDTH), lambda b, i: (mb0 + b, 2)),
                pl.BlockSpec(na_bias.shape, lambda b, i: (0, 0, 0, 0)),
                pl.BlockSpec(na_mbias.shape, lambda b, i: (0, 0))]
    extra, extra_specs, aliases = _alias_args(o_prev, len(in_specs))
    return pl.pallas_call(
        functools.partial(_na_kernel, rows=rows),
        grid=(B, nq),
        in_specs=in_specs + extra_specs,
        out_specs=pl.BlockSpec((tq, MIX_WIDTH), lambda b, i: (qb0 + b * nq + i, 2)),
        out_shape=jax.ShapeDtypeStruct((q_all.shape[0], N_MIXERS * MIX_WIDTH), BF16),
        input_output_aliases=aliases,
        compiler_params=_cparams(("parallel", "arbitrary")),
        name="mixer_neighbourhood",
    )(q_all, k_all, v_all, k_all, v_all, na_bias, na_mbias, *extra)


def _meta_attn(grp, q_all, k_all, v_all, sink, bq_meta, bq_blk, na_mbias, o_prev):
    B, S = grp.B, grp.S
    sb0 = grp.real_base // S
    bb0 = grp.real_base // BLOCK
    nb = S // BLOCK
    mb0 = grp.meta_blk0
    clamp = lambda b: jnp.minimum(b, B - 1)
    mrow = lambda b, sink: (mb0 + clamp(b), 0)
    in_specs = [pl.BlockSpec((N_META, N_MIXERS * MIX_WIDTH), mrow),
                pl.BlockSpec((S, KV_WIDTH), lambda b, sink: (sb0 + clamp(b), 0)),
                pl.BlockSpec((S, KV_WIDTH), lambda b, sink: (sb0 + clamp(b), 0)),
                pl.BlockSpec((BLOCK, KV_WIDTH), lambda b, sink: (bb0 + clamp(b) * nb, 1)),
                pl.BlockSpec((BLOCK, KV_WIDTH), lambda b, sink: (bb0 + clamp(b) * nb, 1)),
                pl.BlockSpec((N_META, N_MIXERS * KV_WIDTH), mrow),
                pl.BlockSpec((N_META, N_MIXERS * KV_WIDTH), mrow),
                pl.BlockSpec(bq_meta.shape, lambda b, sink: (0, 0, 0)),
                pl.BlockSpec(bq_blk.shape, lambda b, sink: (0, 0, 0)),
                pl.BlockSpec(na_mbias.shape, lambda b, sink: (0, 0)),
                pl.BlockSpec(memory_space=pl.ANY)]
    return pl.pallas_call(
        _meta_kernel,
        grid_spec=pltpu.PrefetchScalarGridSpec(
            num_scalar_prefetch=1,
            grid=(grp.n_meta_blocks,),
            in_specs=in_specs,
            out_specs=pl.BlockSpec((N_META, N_MIXERS * MIX_WIDTH), lambda b, sink: (mb0 + b, 0))),
        out_shape=jax.ShapeDtypeStruct((q_all.shape[0], N_MIXERS * MIX_WIDTH), BF16),
        input_output_aliases={len(in_specs): 0},
        compiler_params=_cparams(("arbitrary",)),
        name="mixer_meta_queries",
    )(sink, q_all, k_all, v_all, k_all, v_all, k_all, v_all, bq_meta, bq_blk, na_mbias, o_prev)


def _route(h1, wr_hi, wr_lo, rbias):
    T = h1.shape[0]
    x_hi, x_lo = _split_bf16(h1)
    logits = _dot_nt(wr_hi, x_hi) + _dot_nt(wr_hi, x_lo) + _dot_nt(wr_lo, x_hi)
    scores = 1.0 / (1.0 + jnp.exp(-logits))
    sel = scores + rbias
    per_group = N_EXPERTS // N_EXPERT_GROUPS
    sel3 = sel.reshape(N_EXPERT_GROUPS, per_group, T)
    idx3 = lax.broadcasted_iota(jnp.int32, sel3.shape, 1).astype(F32)
    m1 = jnp.max(sel3, axis=1, keepdims=True)
    first = jnp.min(jnp.where(sel3 == m1, idx3, float(per_group)), axis=1, keepdims=True)
    m2 = jnp.max(jnp.where(idx3 == first, -jnp.inf, sel3), axis=1, keepdims=True)
    gscore = (m1 + m2).reshape(N_EXPERT_GROUPS, T)

    def rank_of(vals):
        idx = lax.broadcasted_iota(jnp.int32, vals.shape, 0)
        rank = jnp.zeros(vals.shape, F32)
        for r in range(vals.shape[0]):
            row = vals[r:r + 1, :]
            ge = jnp.where(row >= vals, 1.0, 0.0)
            gt = jnp.where(row > vals, 1.0, 0.0)
            rank = rank + jnp.where(idx > r, ge, gt)
        return rank

    gkeep = jnp.where(rank_of(gscore) < TOPK_GROUPS, 1.0, 0.0)
    ekeep = jnp.broadcast_to(gkeep.reshape(N_EXPERT_GROUPS, 1, T), sel3.shape).reshape(N_EXPERTS, T)
    masked = jnp.where(ekeep > 0.5, sel, NEG_INF)
    w = jnp.where(rank_of(masked) < TOP_K, scores, 0.0)
    return w / jnp.sum(w, axis=0, keepdims=True) * ROUTED_SCALE


def _merge_kernel(h_ref, o_ref, wg_ref, wb_ref, wo_ref, g_ref, b_ref, wrh_ref, wrl_ref, rb_ref,
                  h1_ref, gates_ref, *, alpha):
    h = h_ref[...]
    x = h.astype(BF16)
    merged = None
    for n in range(N_MIXERS):
        logit = _dot(x, wg_ref[:, n * D_MODEL:(n + 1) * D_MODEL])
        branch = _dot(o_ref[:, n * MIX_WIDTH:(n + 1) * MIX_WIDTH], wb_ref[n])
        term = branch / (1.0 + jnp.exp(-logit))
        merged = term if merged is None else merged + term
    mix = _dot(merged.astype(BF16), wo_ref[...])
    h1 = _layer_norm(alpha * h + mix, g_ref[...], b_ref[...])
    h1_ref[...] = h1
    gates_t = _route(h1, wrh_ref[...], wrl_ref[...], rb_ref[...])
    pad = jnp.zeros((LANES - N_EXPERTS, gates_t.shape[1]), F32)
    gates_ref[...] = jnp.concatenate([gates_t, pad], axis=0).T


def _merge(h, o_all, w_gate, w_branch, w_out, ln_g, ln_b, wr_hi, wr_lo, rbias, alpha):
    R = h.shape[0]
    c2 = lambda i: (0, 0)
    return pl.pallas_call(
        functools.partial(_merge_kernel, alpha=alpha),
        grid=(R // TM,),
        in_specs=[pl.BlockSpec((TM, D_MODEL), lambda i: (i, 0)),
                  pl.BlockSpec((TM, N_MIXERS * MIX_WIDTH), lambda i: (i, 0)),
                  pl.BlockSpec(w_gate.shape, c2),
                  pl.BlockSpec(w_branch.shape, lambda i: (0, 0, 0)),
                  pl.BlockSpec(w_out.shape, c2),
                  pl.BlockSpec((1, D_MODEL), c2),
                  pl.BlockSpec((1, D_MODEL), c2),
                  pl.BlockSpec(wr_hi.shape, c2),
                  pl.BlockSpec(wr_lo.shape, c2),
                  pl.BlockSpec(rbias.shape, c2)],
        out_specs=[pl.BlockSpec((TM, D_MODEL), lambda i: (i, 0)),
                   pl.BlockSpec((TM, LANES), lambda i: (i, 0))],
        out_shape=[jax.ShapeDtypeStruct((R, D_MODEL), F32),
                   jax.ShapeDtypeStruct((R, LANES), F32)],
        compiler_params=_cparams(("parallel",)),
        name="merge_ln_route",
    )(h, o_all, w_gate, w_branch, w_out, ln_g, ln_b, wr_hi, wr_lo, rbias)


def _swiglu_act(gu):
    g = gu[:, :D_EXPERT]
    return g / (1.0 + jnp.exp(-g)) * gu[:, D_EXPERT:]


def _moe_kernel(h_ref, gates_ref, wgu_ref, wd_ref, wsgu_ref, wsd_ref, g_ref, b_ref, o_ref,
                x_sc, acc_sc, *, alpha, eb):
    e = pl.program_id(1)

    @pl.when(e == 0)
    def _():
        x_sc[...] = h_ref[...].astype(BF16)
        acc_sc[...] = jnp.zeros_like(acc_sc)

    x = x_sc[...]
    g_hi, g_lo = _split_bf16(gates_ref[...])
    row = lax.broadcasted_iota(jnp.int32, (LANES, D_EXPERT), 0)
    for u in range(eb):
        act = _swiglu_act(_dot(x, wgu_ref[u]))
        onehot = jnp.where(row == e * eb + u, 1.0, 0.0).astype(BF16)
        gate = _dot(g_hi, onehot) + _dot(g_lo, onehot)
        acc_sc[...] += _dot((act * gate).astype(BF16), wd_ref[u])

    @pl.when(e == pl.num_programs(1) - 1)
    def _():
        shared = _dot(_swiglu_act(_dot(x, wsgu_ref[...])).astype(BF16), wsd_ref[...])
        o_ref[...] = _layer_norm(alpha * h_ref[...] + acc_sc[...] + shared, g_ref[...], b_ref[...])


def _moe(h1, gates, wgu, wd, wsgu, wsd, ln_g, ln_b, alpha, tm, eb):
    R = h1.shape[0]
    c2 = lambda i, e: (0, 0)
    return pl.pallas_call(
        functools.partial(_moe_kernel, alpha=alpha, eb=eb),
        grid=(R // tm, N_EXPERTS // eb),
        in_specs=[pl.BlockSpec((tm, D_MODEL), lambda i, e: (i, 0)),
                  pl.BlockSpec((tm, LANES), lambda i, e: (i, 0)),
                  pl.BlockSpec((eb, D_MODEL, 2 * D_EXPERT), lambda i, e: (e, 0, 0)),
                  pl.BlockSpec((eb, D_EXPERT, D_MODEL), lambda i, e: (e, 0, 0)),
                  pl.BlockSpec(wsgu.shape, c2),
                  pl.BlockSpec(wsd.shape, c2),
                  pl.BlockSpec((1, D_MODEL), c2),
                  pl.BlockSpec((1, D_MODEL), c2)],
        out_specs=pl.BlockSpec((tm, D_MODEL), lambda i, e: (i, 0)),
        out_shape=jax.ShapeDtypeStruct((R, D_MODEL), F32),
        scratch_shapes=[pltpu.VMEM((tm, D_MODEL), BF16), pltpu.VMEM((tm, D_MODEL), F32)],
        compiler_params=_cparams(("parallel", "arbitrary")),
        name="moe_ln",
    )(h1, gates, wgu, wd, wsgu, wsd, ln_g, ln_b)


def _t5_bucket(rel):
    half = T5_BUCKETS // 2
    max_exact = half // 2
    n = np.abs(rel)
    ratio = np.log(np.maximum(n, 1).astype(np.float32) / np.float32(max_exact))
    ratio = ratio / np.float32(math.log(T5_MAX_DIST / max_exact)) * np.float32(half - max_exact)
    large = np.minimum(max_exact + ratio.astype(np.int32), half - 1)
    return np.where(rel > 0, half, 0) + np.where(n < max_exact, n, large)


def _t5_tables(t5_table, s_max):
    def bias(rel, valid):
        b = t5_table[_t5_bucket(rel)].astype(F32)
        return jnp.where(jnp.asarray(valid)[None], jnp.transpose(b, (2, 0, 1)), NEG_INF)

    ii = np.arange(BLOCK)[:, None]
    jj = np.arange(3 * BLOCK)[None, :]
    rel = jj - ii - BLOCK
    bband = bias(rel, np.abs(rel) <= WINDOW)
    t = np.arange(s_max)[:, None]
    m = np.arange(N_META)[None, :]
    bmeta = bias(m - (N_META + t), np.ones((s_max, N_META), bool))
    mpos = np.arange(N_META)[:, None]
    kpos = np.arange(N_META + BLOCK)[None, :]
    relq = kpos - mpos
    bq = bias(relq, (kpos < N_META) | (np.abs(relq) <= WINDOW))
    return bband, bmeta, bq[:, :, :N_META], bq[:, :, N_META:]


def _na_bias_cases(rpb):
    W = GRID_W
    delta = np.arange(NA_ROWS)[:, None, None, None]
    i = np.arange(NA_ROWS)[None, :, None, None]
    c = np.arange(W)[None, None, :, None]
    kc = np.arange(W)[None, None, None, :]
    cs = np.clip(c - NA_COLS // 2, 0, W - NA_COLS)
    valid = (kc >= cs) & (kc < cs + NA_COLS)
    dr = np.broadcast_to(i - delta + (NA_ROWS - 1), (NA_ROWS, NA_ROWS, W, W))
    dc = np.broadcast_to(np.clip(kc - c + (NA_COLS - 1), 0, 2 * NA_COLS - 2), (NA_ROWS, NA_ROWS, W, W))
    b = rpb.astype(F32)[:, dr, dc]
    b = jnp.where(jnp.asarray(np.broadcast_to(valid, dr.shape))[None], b, NEG_INF)
    return jnp.transpose(b, (1, 0, 3, 2, 4)).reshape(NA_ROWS, N_HEADS, W, NA_ROWS * W)


def _rope_tables(s_max, n_meta_rows):
    half = HEAD_DIM // 4
    freq = ROPE_THETA ** (-jnp.arange(half, dtype=F32) / half)
    t = np.arange(s_max)
    mp = np.tile(np.arange(N_META) - N_META, n_meta_rows // N_META)
    pos_row = jnp.asarray(np.concatenate([t // GRID_W, mp]), jnp.int32).astype(F32)
    pos_col = jnp.asarray(np.concatenate([t % GRID_W, mp]), jnp.int32).astype(F32)
    ar = pos_row[:, None] * freq
    ac = pos_col[:, None] * freq
    cos = jnp.concatenate([jnp.cos(ar), jnp.cos(ar), jnp.cos(ac), jnp.cos(ac)], axis=1)
    sin = jnp.concatenate([-jnp.sin(ar), jnp.sin(ar), -jnp.sin(ac), jnp.sin(ac)], axis=1)
    return jnp.tile(cos, (1, 2)), jnp.tile(sin, (1, 2))


def kernel(x_prompt, x_sample, meta_tokens, ln_in_g, ln_in_b, t5_table, w_in, q_gain, k_gain, sink,
           na_rpb, na_meta_bias, w_branch, w_out, ln1_g, ln1_b, w_router, router_bias,
           w_expert_gate_up, w_expert_down, w_shared_gate_up, w_shared_down, ln2_g, ln2_b):
    depth = w_in.shape[0]
    alpha = (2 * depth) ** 0.25
    B0, S0, D = x_prompt.shape
    B1, S1, _ = x_sample.shape
    assert D == D_MODEL
    real = B0 * S0 + B1 * S1
    n_meta_rows = -(-(B0 + B1) * N_META // TM) * TM
    R = real + n_meta_rows
    moe_tm = 1024 if R % 1024 == 0 else TM
    n_meta_blocks = n_meta_rows // N_META
    g0 = _Group(B0, S0, 0, 0, real, B0)
    g1 = _Group(B1, S1, B0 * S0, B0, real, n_meta_blocks - B0)
    s_max = max(S0, S1)

    x = jnp.concatenate([x_prompt.reshape(B0 * S0, D), x_sample.reshape(B1 * S1, D),
                         jnp.tile(meta_tokens, (n_meta_blocks, 1))], axis=0)
    h = _embed_ln(x, ln_in_g.reshape(1, D), ln_in_b.reshape(1, D))

    cos_tab, sin_tab = _rope_tables(s_max, TM)
    n0, n1 = B0 * S0 // TM, real // TM
    p0, p1, pm = S0 // TM, S1 // TM, s_max // TM

    def pos_block(i):
        return jnp.where(i < n0, i % p0, jnp.where(i < n1, (i - n0) % p1, pm))

    bband, bmeta, bq_meta, bq_blk = _t5_tables(t5_table, s_max)
    ones_bd = jnp.asarray(np.kron(np.eye(N_HEADS), np.ones((HEAD_DIM, HEAD_DIM))), BF16)

    qs, ks, vs = [], [], []
    for n in range(N_MIXERS):
        off = n * QKV_WIDTH
        qs.append(w_in[:, :, off:off + MIX_WIDTH])
        ks.append(w_in[:, :, off + MIX_WIDTH:off + MIX_WIDTH + KV_WIDTH])
        vs.append(w_in[:, :, off + MIX_WIDTH + KV_WIDTH:off + QKV_WIDTH])
    wr_t = jnp.swapaxes(w_router, 1, 2)
    wr_hi = wr_t.astype(BF16)
    layers = dict(
        w_qkv=jnp.concatenate(qs + ks + vs, axis=2).astype(BF16),
        w_gate=w_in[:, :, N_MIXERS * QKV_WIDTH:].astype(BF16),
        q_gain=jnp.tile(q_gain, (1, N_HEADS)).reshape(depth, 1, MIX_WIDTH),
        k_gain=jnp.tile(k_gain, (1, N_KV_HEADS)).reshape(depth, 1, KV_WIDTH),
        sink=sink.astype(F32),
        na_bias=jax.vmap(_na_bias_cases)(na_rpb),
        na_mbias=na_meta_bias.astype(F32),
        w_branch=w_branch.astype(BF16),
        w_out=w_out.astype(BF16),
        ln1_g=ln1_g.reshape(depth, 1, D), ln1_b=ln1_b.reshape(depth, 1, D),
        wr_hi=wr_hi, wr_lo=(wr_t - wr_hi.astype(F32)).astype(BF16),
        rbias=router_bias.astype(F32).reshape(depth, N_EXPERTS, 1),
        wgu=w_expert_gate_up.astype(BF16), wd=w_expert_down.astype(BF16),
        wsgu=w_shared_gate_up.astype(BF16), wsd=w_shared_down.astype(BF16),
        ln2_g=ln2_g.reshape(depth, 1, D), ln2_b=ln2_b.reshape(depth, 1, D),
    )

    def layer(h, p):
        q_all, k_all, v_all = _inproj(h, p["w_qkv"], cos_tab, sin_tab, p["q_gain"], p["k_gain"],
                                      ones_bd, pos_block)
        o = None
        for grp in (g0, g1):
            o = _global_attn(grp, q_all, k_all, v_all, o)
            o = _window_attn(grp, q_all, k_all, v_all, p["sink"], bband, bmeta[:, :grp.S], o)
            o = _na_attn(grp, q_all, k_all, v_all, p["na_bias"], p["na_mbias"], o)
        for grp in (g0, g1):
            o = _meta_attn(grp, q_all, k_all, v_all, p["sink"], bq_meta, bq_blk, p["na_mbias"], o)
        h1, gates = _merge(h, o, p["w_gate"], p["w_branch"], p["w_out"], p["ln1_g"], p["ln1_b"],
                           p["wr_hi"], p["wr_lo"], p["rbias"], alpha)
        h2 = _moe(h1, gates, p["wgu"], p["wd"], p["wsgu"], p["wsd"], p["ln2_g"], p["ln2_b"],
                  alpha, moe_tm, 2)
        return h2, None

    h, _ = lax.scan(layer, h, layers)
    y_prompt = h[:B0 * S0].reshape(B0, S0, D)
    y_sample = h[B0 * S0:real].reshape(B1, S1, D)
    return (y_prompt, y_sample)
```

```python
import functools
import math

import numpy as np
import jax
import jax.numpy as jnp
from jax import lax
from jax.experimental import pallas as pl
from jax.experimental.pallas import tpu as pltpu

F32 = jnp.float32
BF16 = jnp.bfloat16

D_MODEL = 1024
HEAD_DIM = 64
N_HEADS = 8
N_KV_HEADS = 2
GROUP = N_HEADS // N_KV_HEADS
MIX_WIDTH = N_HEADS * HEAD_DIM
KV_WIDTH = N_KV_HEADS * HEAD_DIM
N_MIXERS = 3
QKV_WIDTH = MIX_WIDTH + 2 * KV_WIDTH
N_META = 16
GRID_W = 64
BLOCK = 128
WINDOW = 128
NA_ROWS = 8
NA_COLS = 16
T5_BUCKETS = 32
T5_MAX_DIST = 128
ROPE_THETA = 10000.0
N_EXPERTS = 64
TOP_K = 8
N_EXPERT_GROUPS = 8
TOPK_GROUPS = 4
D_EXPERT = 256
ROUTED_SCALE = 2.5
NEG_INF = -1e30
LANES = 128

TM = 512
TQ_GLOBAL = 128
TK_GLOBAL = 512
NA_QROWS = 8
VMEM_LIMIT = 56 * 1024 * 1024


def _cparams(sem):
    return pltpu.CompilerParams(dimension_semantics=sem, vmem_limit_bytes=VMEM_LIMIT)


def _dot(a, b):
    return jnp.dot(a, b, preferred_element_type=F32)


def _dot_nt(a, b):
    return lax.dot_general(a, b, (((1,), (1,)), ((), ())), preferred_element_type=F32)


def _split_bf16(x):
    hi = x.astype(BF16)
    lo = (x - hi.astype(F32)).astype(BF16)
    return hi, lo


def _layer_norm(x, g, b):
    mu = jnp.mean(x, axis=-1, keepdims=True)
    xc = x - mu
    var = jnp.mean(xc * xc, axis=-1, keepdims=True)
    return xc * lax.rsqrt(var + 1e-5) * g + b


def _embed_ln_kernel(x_ref, g_ref, b_ref, o_ref):
    o_ref[...] = _layer_norm(x_ref[...], g_ref[...], b_ref[...])


def _embed_ln(x, g, b):
    R = x.shape[0]
    return pl.pallas_call(
        _embed_ln_kernel,
        grid=(R // TM,),
        in_specs=[pl.BlockSpec((TM, D_MODEL), lambda i: (i, 0)),
                  pl.BlockSpec((1, D_MODEL), lambda i: (0, 0)),
                  pl.BlockSpec((1, D_MODEL), lambda i: (0, 0))],
        out_specs=pl.BlockSpec((TM, D_MODEL), lambda i: (i, 0)),
        out_shape=jax.ShapeDtypeStruct((R, D_MODEL), F32),
        compiler_params=_cparams(("parallel",)),
        name="embed_ln",
    )(x, g, b)


def _rope_slot(x, cos, sin_signed, first_half):
    fwd = pltpu.roll(x, LANES - 16, 1)
    bwd = pltpu.roll(x, 16, 1)
    return x * cos + jnp.where(first_half, fwd, bwd) * sin_signed


def _head_rms(x, ones_bd, gain):
    hi, lo = _split_bf16(x * x)
    ss = _dot(hi, ones_bd) + _dot(lo, ones_bd)
    return x * lax.rsqrt(ss * (1.0 / HEAD_DIM) + 1e-6) * gain


def _inproj_kernel(h_ref, w_ref, cos_ref, sin_ref, qg_ref, kg_ref, ones_ref, q_ref, k_ref, v_ref):
    x = h_ref[...].astype(BF16)
    cos = cos_ref[...]
    sin = sin_ref[...]
    lane = lax.broadcasted_iota(jnp.int32, cos.shape, 1)
    first_half = (lane % 32) < 16
    scale = HEAD_DIM ** -0.5
    qw = N_MIXERS * MIX_WIDTH
    qa = _head_rms(_dot(x, w_ref[:, 0:MIX_WIDTH]), ones_ref[...], qg_ref[...])
    for s in range(MIX_WIDTH // LANES):
        sl = slice(s * LANES, (s + 1) * LANES)
        q_ref[:, sl] = (_rope_slot(qa[:, sl], cos, sin, first_half) * scale).astype(BF16)
    ka = _head_rms(_dot(x, w_ref[:, qw:qw + KV_WIDTH]), ones_ref[0:LANES, 0:LANES], kg_ref[...])
    k_ref[:, 0:KV_WIDTH] = _rope_slot(ka, cos, sin, first_half).astype(BF16)
    for n in range(1, N_MIXERS):
        q_ref[:, n * MIX_WIDTH:(n + 1) * MIX_WIDTH] = (
            _dot(x, w_ref[:, n * MIX_WIDTH:(n + 1) * MIX_WIDTH]) * scale).astype(BF16)
        k_ref[:, n * KV_WIDTH:(n + 1) * KV_WIDTH] = _dot(
            x, w_ref[:, qw + n * KV_WIDTH:qw + (n + 1) * KV_WIDTH]).astype(BF16)
    vw = qw + N_MIXERS * KV_WIDTH
    v_ref[...] = _dot(x, w_ref[:, vw:vw + N_MIXERS * KV_WIDTH]).astype(BF16)


def _inproj(h, w_qkv, cos_tab, sin_tab, q_gain, k_gain, ones_bd, pos_block):
    R = h.shape[0]
    const = lambda i: (0, 0)
    return pl.pallas_call(
        _inproj_kernel,
        grid=(R // TM,),
        in_specs=[pl.BlockSpec((TM, D_MODEL), lambda i: (i, 0)),
                  pl.BlockSpec(w_qkv.shape, const),
                  pl.BlockSpec((TM, LANES), lambda i: (pos_block(i), 0)),
                  pl.BlockSpec((TM, LANES), lambda i: (pos_block(i), 0)),
                  pl.BlockSpec((1, MIX_WIDTH), const),
                  pl.BlockSpec((1, KV_WIDTH), const),
                  pl.BlockSpec((MIX_WIDTH, MIX_WIDTH), const)],
        out_specs=[pl.BlockSpec((TM, N_MIXERS * MIX_WIDTH), lambda i: (i, 0)),
                   pl.BlockSpec((TM, N_MIXERS * KV_WIDTH), lambda i: (i, 0)),
                   pl.BlockSpec((TM, N_MIXERS * KV_WIDTH), lambda i: (i, 0))],
        out_shape=[jax.ShapeDtypeStruct((R, N_MIXERS * MIX_WIDTH), BF16),
                   jax.ShapeDtypeStruct((R, N_MIXERS * KV_WIDTH), BF16),
                   jax.ShapeDtypeStruct((R, N_MIXERS * KV_WIDTH), BF16)],
        compiler_params=_cparams(("parallel",)),
        name="inproj",
    )(h, w_qkv, cos_tab, sin_tab, q_gain, k_gain, ones_bd)


def _group_queries(q, j):
    lane = lax.broadcasted_iota(jnp.int32, (q.shape[0], LANES), 1)
    keep = (lane < HEAD_DIM) if j == 0 else (lane >= HEAD_DIM)
    parts = []
    for hh in range(GROUP):
        h = GROUP * j + hh
        slot = q[:, (h // 2) * LANES:(h // 2 + 1) * LANES]
        if h % 2 != j:
            slot = pltpu.roll(slot, HEAD_DIM, 1)
        parts.append(jnp.where(keep, slot, 0.0))
    return jnp.concatenate(parts, axis=0).astype(BF16)


def _ungroup_outputs(out, j, T):
    lane = lax.broadcasted_iota(jnp.int32, (T, LANES), 1)
    lo = lane < HEAD_DIM
    slots = []
    for s in range(2):
        even = out[(2 * s) * T:(2 * s + 1) * T]
        odd = out[(2 * s + 1) * T:(2 * s + 2) * T]
        if j == 0:
            slots.append(jnp.where(lo, even, pltpu.roll(odd, HEAD_DIM, 1)))
        else:
            slots.append(jnp.where(lo, pltpu.roll(even, HEAD_DIM, 1), odd))
    return jnp.concatenate(slots, axis=1)


def _head_rows(vals, T):
    return jnp.concatenate([jnp.broadcast_to(v, (T, v.shape[-1])) for v in vals], axis=0)


def _global_kernel(q_ref, k_ref, v_ref, km_ref, vm_ref, *rest, S):
    o_ref = rest[-1]
    T = q_ref.shape[0]
    q = q_ref[...].astype(F32)
    km = km_ref[...]
    vm = vm_ref[...]
    qs, state = [], []
    for j in range(N_KV_HEADS):
        qj = _group_queries(q, j)
        s_m = _dot_nt(qj, km)
        m0 = jnp.max(s_m, axis=-1, keepdims=True)
        p_m = jnp.exp(s_m - m0)
        qs.append(qj)
        state.append((m0, jnp.sum(p_m, axis=-1, keepdims=True), _dot(p_m.astype(BF16), vm)))
    for c in range(S // TK_GLOBAL):
        kc = k_ref[c * TK_GLOBAL:(c + 1) * TK_GLOBAL, :]
        vc = v_ref[c * TK_GLOBAL:(c + 1) * TK_GLOBAL, :]
        for j in range(N_KV_HEADS):
            m, l, acc = state[j]
            s = _dot_nt(qs[j], kc)
            m_new = jnp.maximum(m, jnp.max(s, axis=-1, keepdims=True))
            a = jnp.exp(m - m_new)
            p = jnp.exp(s - m_new)
            state[j] = (m_new, a * l + jnp.sum(p, axis=-1, keepdims=True),
                        a * acc + _dot(p.astype(BF16), vc))
    for j in range(N_KV_HEADS):
        _, l, acc = state[j]
        o_ref[:, j * 2 * LANES:(j + 1) * 2 * LANES] = _ungroup_outputs(acc / l, j, T).astype(BF16)


def _window_kernel(sink_ref, q_ref, kp_ref, kc_ref, kn_ref, vp_ref, vc_ref, vn_ref, km_ref, vm_ref,
                   bband_ref, bmeta_ref, *rest, nb):
    o_ref = rest[-1]
    i = pl.program_id(1)
    T = q_ref.shape[0]
    q = q_ref[...].astype(F32)
    kband = jnp.concatenate([kp_ref[...], kc_ref[...], kn_ref[...]], axis=0)
    vband = jnp.concatenate([vp_ref[...], vc_ref[...], vn_ref[...]], axis=0)
    col = lax.broadcasted_iota(jnp.int32, (1, 3 * BLOCK), 1)
    in_range = ((col >= BLOCK) | (i > 0)) & ((col < 2 * BLOCK) | (i < nb - 1))
    for j in range(N_KV_HEADS):
        qj = _group_queries(q, j)
        hs = slice(GROUP * j, GROUP * (j + 1))
        s_b = _dot_nt(qj, kband) + bband_ref[hs].reshape(GROUP * T, 3 * BLOCK)
        s_b = jnp.where(in_range, s_b, NEG_INF)
        s_m = _dot_nt(qj, km_ref[...]) + bmeta_ref[hs].reshape(GROUP * T, N_META)
        sink = jnp.concatenate([jnp.full((T, 1), sink_ref[GROUP * j + hh], F32) for hh in range(GROUP)], axis=0)
        m = jnp.maximum(jnp.maximum(jnp.max(s_b, axis=-1, keepdims=True),
                                    jnp.max(s_m, axis=-1, keepdims=True)), sink)
        e_b = jnp.exp(s_b - m)
        e_m = jnp.exp(s_m - m)
        denom = (jnp.sum(e_b, axis=-1, keepdims=True) + jnp.sum(e_m, axis=-1, keepdims=True)
                 + jnp.exp(sink - m))
        acc = _dot(e_b.astype(BF16), vband) + _dot(e_m.astype(BF16), vm_ref[...])
        o_ref[:, j * 2 * LANES:(j + 1) * 2 * LANES] = _ungroup_outputs(acc / denom, j, T).astype(BF16)


def _na_kernel(q_ref, k_ref, v_ref, km_ref, vm_ref, bias_ref, mbias_ref, *rest, rows):
    o_ref = rest[-1]
    blk = pl.program_id(1)
    W = GRID_W
    nkeys = NA_ROWS * W
    km = km_ref[...]
    vm = vm_ref[...]
    for j in range(N_KV_HEADS):
        mb = _head_rows([mbias_ref[GROUP * j + hh:GROUP * j + hh + 1, :] for hh in range(GROUP)], W)

        def row_body(rr, carry, j=j, mb=mb):
            r = blk * NA_QROWS + rr
            rs = jnp.clip(r - NA_ROWS // 2, 0, rows - NA_ROWS)
            delta = r - rs
            qoff = pl.multiple_of(rr * W, W)
            koff = pl.multiple_of(rs * W, W)
            qj = _group_queries(q_ref[pl.ds(qoff, W), :].astype(F32), j)
            kw = k_ref[pl.ds(koff, nkeys), :]
            vw = v_ref[pl.ds(koff, nkeys), :]
            s_w = _dot_nt(qj, kw) + bias_ref[delta, pl.ds(GROUP * j, GROUP)].reshape(GROUP * W, nkeys)
            s_m = _dot_nt(qj, km) + mb
            m = jnp.maximum(jnp.max(s_w, axis=-1, keepdims=True), jnp.max(s_m, axis=-1, keepdims=True))
            e_w = jnp.exp(s_w - m)
            e_m = jnp.exp(s_m - m)
            denom = jnp.sum(e_w, axis=-1, keepdims=True) + jnp.sum(e_m, axis=-1, keepdims=True)
            acc = _dot(e_w.astype(BF16), vw) + _dot(e_m.astype(BF16), vm)
            o_ref[pl.ds(qoff, W), j * 2 * LANES:(j + 1) * 2 * LANES] = (
                _ungroup_outputs(acc / denom, j, W).astype(BF16))
            return carry

        lax.fori_loop(0, NA_QROWS, row_body, 0, unroll=True)


def _meta_kernel(sink_ref, q_ref, ka_ref, va_ref, kb_ref, vb_ref, km_ref, vm_ref,
                 bq_meta_ref, bq_blk_ref, mbias_ref, o_in_ref, o_ref):
    del o_in_ref
    T = N_META
    q = q_ref[...].astype(F32)
    km_all = km_ref[...]
    vm_all = vm_ref[...]

    def finish(n, j, acc, denom):
        lo = n * MIX_WIDTH + j * 2 * LANES
        o_ref[:, lo:lo + 2 * LANES] = _ungroup_outputs(acc / denom, j, T).astype(BF16)

    for j in range(N_KV_HEADS):
        hs = slice(GROUP * j, GROUP * (j + 1))
        qj = _group_queries(q[:, 0:MIX_WIDTH], j)
        km, vm = km_all[:, 0:KV_WIDTH], vm_all[:, 0:KV_WIDTH]
        s_r = _dot_nt(qj, ka_ref[...])
        s_m = _dot_nt(qj, km)
        m = jnp.maximum(jnp.max(s_r, axis=-1, keepdims=True), jnp.max(s_m, axis=-1, keepdims=True))
        e_r = jnp.exp(s_r - m)
        e_m = jnp.exp(s_m - m)
        denom = jnp.sum(e_r, axis=-1, keepdims=True) + jnp.sum(e_m, axis=-1, keepdims=True)
        finish(0, j, _dot(e_r.astype(BF16), va_ref[...]) + _dot(e_m.astype(BF16), vm), denom)
        qj = _group_queries(q[:, MIX_WIDTH:2 * MIX_WIDTH], j)
        km, vm = km_all[:, KV_WIDTH:2 * KV_WIDTH], vm_all[:, KV_WIDTH:2 * KV_WIDTH]
        s_r = _dot_nt(qj, kb_ref[...]) + bq_blk_ref[hs].reshape(GROUP * T, BLOCK)
        s_m = _dot_nt(qj, km) + bq_meta_ref[hs].reshape(GROUP * T, N_META)
        sink = jnp.concatenate([jnp.full((T, 1), sink_ref[GROUP * j + hh], F32) for hh in range(GROUP)], axis=0)
        m = jnp.maximum(jnp.maximum(jnp.max(s_r, axis=-1, keepdims=True),
                                    jnp.max(s_m, axis=-1, keepdims=True)), sink)
        e_r = jnp.exp(s_r - m)
        e_m = jnp.exp(s_m - m)
        denom = (jnp.sum(e_r, axis=-1, keepdims=True) + jnp.sum(e_m, axis=-1, keepdims=True)
                 + jnp.exp(sink - m))
        finish(1, j, _dot(e_r.astype(BF16), vb_ref[...]) + _dot(e_m.astype(BF16), vm), denom)
        qj = _group_queries(q[:, 2 * MIX_WIDTH:3 * MIX_WIDTH], j)
        km, vm = km_all[:, 2 * KV_WIDTH:3 * KV_WIDTH], vm_all[:, 2 * KV_WIDTH:3 * KV_WIDTH]
        mb = _head_rows([mbias_ref[GROUP * j + hh:GROUP * j + hh + 1, :] for hh in range(GROUP)], T)
        s_m = _dot_nt(qj, km) + mb
        m = jnp.max(s_m, axis=-1, keepdims=True)
        e_m = jnp.exp(s_m - m)
        finish(2, j, _dot(e_m.astype(BF16), vm), jnp.sum(e_m, axis=-1, keepdims=True))


class _Group:
    def __init__(self, B, S, real_base, meta_batch_base, meta_base, n_meta_blocks):
        self.B, self.S = B, S
        self.real_base = real_base
        self.meta_blk0 = meta_base // N_META + meta_batch_base
        self.n_meta_blocks = n_meta_blocks
        assert real_base % S == 0 and S % TM == 0 and meta_base % N_META == 0


def _alias_args(o_prev, n_inputs):
    if o_prev is None:
        return [], [], {}
    return [o_prev], [pl.BlockSpec(memory_space=pl.ANY)], {n_inputs: 0}


def _global_attn(grp, q_all, k_all, v_all, o_prev):
    B, S = grp.B, grp.S
    nq = S // TQ_GLOBAL
    qb0 = grp.real_base // TQ_GLOBAL
    sb0 = grp.real_base // S
    mb0 = grp.meta_blk0
    in_specs = [pl.BlockSpec((TQ_GLOBAL, MIX_WIDTH), lambda b, i: (qb0 + b * nq + i, 0)),
                pl.BlockSpec((S, KV_WIDTH), lambda b, i: (sb0 + b, 0)),
                pl.BlockSpec((S, KV_WIDTH), lambda b, i: (sb0 + b, 0)),
                pl.BlockSpec((N_META, KV_WIDTH), lambda b, i: (mb0 + b, 0)),
                pl.BlockSpec((N_META, KV_WIDTH), lambda b, i: (mb0 + b, 0))]
    extra, extra_specs, aliases = _alias_args(o_prev, len(in_specs))
    return pl.pallas_call(
        functools.partial(_global_kernel, S=S),
        grid=(B, nq),
        in_specs=in_specs + extra_specs,
        out_specs=pl.BlockSpec((TQ_GLOBAL, MIX_WIDTH), lambda b, i: (qb0 + b * nq + i, 0)),
        out_shape=jax.ShapeDtypeStruct((q_all.shape[0], N_MIXERS * MIX_WIDTH), BF16),
        input_output_aliases=aliases,
        compiler_params=_cparams(("parallel", "arbitrary")),
        name="mixer_global",
    )(q_all, k_all, v_all, k_all, v_all, *extra)


def _window_attn(grp, q_all, k_all, v_all, sink, bband, bmeta, o_prev):
    B, S = grp.B, grp.S
    nb = S // BLOCK
    qb0 = grp.real_base // BLOCK
    mb0 = grp.meta_blk0
    cur = lambda b, i, sink: (qb0 + b * nb + i, 1)
    prv = lambda b, i, sink: (qb0 + b * nb + jnp.maximum(i - 1, 0), 1)
    nxt = lambda b, i, sink: (qb0 + b * nb + jnp.minimum(i + 1, nb - 1), 1)
    met = lambda b, i, sink: (mb0 + b, 1)
    kv = lambda im: pl.BlockSpec((BLOCK, KV_WIDTH), im)
    in_specs = [pl.BlockSpec((BLOCK, MIX_WIDTH), cur),
                kv(prv), kv(cur), kv(nxt), kv(prv), kv(cur), kv(nxt),
                pl.BlockSpec((N_META, KV_WIDTH), met), pl.BlockSpec((N_META, KV_WIDTH), met),
                pl.BlockSpec((N_HEADS, BLOCK, 3 * BLOCK), lambda b, i, sink: (0, 0, 0)),
                pl.BlockSpec((N_HEADS, BLOCK, N_META), lambda b, i, sink: (0, i, 0))]
    extra, extra_specs, aliases = _alias_args(o_prev, len(in_specs) + 1)
    return pl.pallas_call(
        functools.partial(_window_kernel, nb=nb),
        grid_spec=pltpu.PrefetchScalarGridSpec(
            num_scalar_prefetch=1,
            grid=(B, nb),
            in_specs=in_specs + extra_specs,
            out_specs=pl.BlockSpec((BLOCK, MIX_WIDTH), cur)),
        out_shape=jax.ShapeDtypeStruct((q_all.shape[0], N_MIXERS * MIX_WIDTH), BF16),
        input_output_aliases=aliases,
        compiler_params=_cparams(("parallel", "arbitrary")),
        name="mixer_window",
    )(sink, q_all, k_all, k_all, k_all, v_all, v_all, v_all, k_all, v_all, bband, bmeta, *extra)


def _na_attn(grp, q_all, k_all, v_all, na_bias, na_mbias, o_prev):
    B, S = grp.B, grp.S
    rows = S // GRID_W
    tq = NA_QROWS * GRID_W
    nq = S // tq
    qb0 = grp.real_base // tq
    sb0 = grp.real_base // S
    mb0 = grp.meta_blk0
    in_specs = [pl.BlockSpec((tq, MIX_WIDTH), lambda b, i: (qb0 + b * nq + i, 2)),
                pl.BlockSpec((S, KV_WIDTH), lambda b, i: (sb0 + b, 2)),
                pl.BlockSpec((S, KV_WIDTH), lambda b, i: (sb0 + b, 2)),
                pl.BlockSpec((N_META, KV_WIDTH), lambda b, i: (mb0 + b, 2)),
                pl.BlockSpec((N_META, KV_WIDTH), lambda b, i: (mb0 + b, 2)),
                pl.BlockSpec(na_bias.shape, lambda b, i: (0, 0, 0, 0)),
                pl.BlockSpec(na_mbias.shape, lambda b, i: (0, 0))]
    extra, extra_specs, aliases = _alias_args(o_prev, len(in_specs))
    return pl.pallas_call(
        functools.partial(_na_kernel, rows=rows),
        grid=(B, nq),
        in_specs=in_specs + extra_specs,
        out_specs=pl.BlockSpec((tq, MIX_WIDTH), lambda b, i: (qb0 + b * nq + i, 2)),
        out_shape=jax.ShapeDtypeStruct((q_all.shape[0], N_MIXERS * MIX_WIDTH), BF16),
        input_output_aliases=aliases,
        compiler_params=_cparams(("parallel", "arbitrary")),
        name="mixer_neighbourhood",
    )(q_all, k_all, v_all, k_all, v_all, na_bias, na_mbias, *extra)


def _meta_attn(grp, q_all, k_all, v_all, sink, bq_meta, bq_blk, na_mbias, o_prev):
    B, S = grp.B, grp.S
    sb0 = grp.real_base // S
    bb0 = grp.real_base // BLOCK
    nb = S // BLOCK
    mb0 = grp.meta_blk0
    clamp = lambda b: jnp.minimum(b, B - 1)
    mrow = lambda b, sink: (mb0 + clamp(b), 0)
    in_specs = [pl.BlockSpec((N_META, N_MIXERS * MIX_WIDTH), mrow),
                pl.BlockSpec((S, KV_WIDTH), lambda b, sink: (sb0 + clamp(b), 0)),
                pl.BlockSpec((S, KV_WIDTH), lambda b, sink: (sb0 + clamp(b), 0)),
                pl.BlockSpec((BLOCK, KV_WIDTH), lambda b, sink: (bb0 + clamp(b) * nb, 1)),
                pl.BlockSpec((BLOCK, KV_WIDTH), lambda b, sink: (bb0 + clamp(b) * nb, 1)),
                pl.BlockSpec((N_META, N_MIXERS * KV_WIDTH), mrow),
                pl.BlockSpec((N_META, N_MIXERS * KV_WIDTH), mrow),
                pl.BlockSpec(bq_meta.shape, lambda b, sink: (0, 0, 0)),
                pl.BlockSpec(bq_blk.shape, lambda b, sink: (0, 0, 0)),
                pl.BlockSpec(na_mbias.shape, lambda b, sink: (0, 0)),
                pl.BlockSpec(memory_space=pl.ANY)]
    return pl.pallas_call(
        _meta_kernel,
        grid_spec=pltpu.PrefetchScalarGridSpec(
            num_scalar_prefetch=1,
            grid=(grp.n_meta_blocks,),
            in_specs=in_specs,
            out_specs=pl.BlockSpec((N_META, N_MIXERS * MIX_WIDTH), lambda b, sink: (mb0 + b, 0))),
        out_shape=jax.ShapeDtypeStruct((q_all.shape[0], N_MIXERS * MIX_WIDTH), BF16),
        input_output_aliases={len(in_specs): 0},
        compiler_params=_cparams(("arbitrary",)),
        name="mixer_meta_queries",
    )(sink, q_all, k_all, v_all, k_all, v_all, k_all, v_all, bq_meta, bq_blk, na_mbias, o_prev)


def _route(h1, wr_hi, wr_lo, rbias):
    T = h1.shape[0]
    x_hi, x_lo = _split_bf16(h1)
    logits = _dot_nt(wr_hi, x_hi) + _dot_nt(wr_hi, x_lo) + _dot_nt(wr_lo, x_hi)
    scores = 1.0 / (1.0 + jnp.exp(-logits))
    sel = scores + rbias
    per_group = N_EXPERTS // N_EXPERT_GROUPS
    sel3 = sel.reshape(N_EXPERT_GROUPS, per_group, T)
    idx3 = lax.broadcasted_iota(jnp.int32, sel3.shape, 1).astype(F32)
    m1 = jnp.max(sel3, axis=1, keepdims=True)
    first = jnp.min(jnp.where(sel3 == m1, idx3, float(per_group)), axis=1, keepdims=True)
    m2 = jnp.max(jnp.where(idx3 == first, -jnp.inf, sel3), axis=1, keepdims=True)
    gscore = (m1 + m2).reshape(N_EXPERT_GROUPS, T)

    def rank_of(vals):
        idx = lax.broadcasted_iota(jnp.int32, vals.shape, 0)
        rank = jnp.zeros(vals.shape, F32)
        for r in range(vals.shape[0]):
            row = vals[r:r + 1, :]
            ge = jnp.where(row >= vals, 1.0, 0.0)
            gt = jnp.where(row > vals, 1.0, 0.0)
            rank = rank + jnp.where(idx > r, ge, gt)
        return rank

    gkeep = jnp.where(rank_of(gscore) < TOPK_GROUPS, 1.0, 0.0)
    ekeep = jnp.broadcast_to(gkeep.reshape(N_EXPERT_GROUPS, 1, T), sel3.shape).reshape(N_EXPERTS, T)
    masked = jnp.where(ekeep > 0.5, sel, NEG_INF)
    w = jnp.where(rank_of(masked) < TOP_K, scores, 0.0)
    return w / jnp.sum(w, axis=0, keepdims=True) * ROUTED_SCALE


def _merge_kernel(h_ref, o_ref, wg_ref, wb_ref, wo_ref, g_ref, b_ref, wrh_ref, wrl_ref, rb_ref,
                  h1_ref, gates_ref, *, alpha):
    h = h_ref[...]
    x = h.astype(BF16)
    merged = None
    for n in range(N_MIXERS):
        logit = _dot(x, wg_ref[:, n * D_MODEL:(n + 1) * D_MODEL])
        branch = _dot(o_ref[:, n * MIX_WIDTH:(n + 1) * MIX_WIDTH], wb_ref[n])
        term = branch / (1.0 + jnp.exp(-logit))
        merged = term if merged is None else merged + term
    mix = _dot(merged.astype(BF16), wo_ref[...])
    h1 = _layer_norm(alpha * h + mix, g_ref[...], b_ref[...])
    h1_ref[...] = h1
    gates_t = _route(h1, wrh_ref[...], wrl_ref[...], rb_ref[...])
    pad = jnp.zeros((LANES - N_EXPERTS, gates_t.shape[1]), F32)
    gates_ref[...] = jnp.concatenate([gates_t, pad], axis=0).T


def _merge(h, o_all, w_gate, w_branch, w_out, ln_g, ln_b, wr_hi, wr_lo, rbias, alpha):
    R = h.shape[0]
    c2 = lambda i: (0, 0)
    return pl.pallas_call(
        functools.partial(_merge_kernel, alpha=alpha),
        grid=(R // TM,),
        in_specs=[pl.BlockSpec((TM, D_MODEL), lambda i: (i, 0)),
                  pl.BlockSpec((TM, N_MIXERS * MIX_WIDTH), lambda i: (i, 0)),
                  pl.BlockSpec(w_gate.shape, c2),
                  pl.BlockSpec(w_branch.shape, lambda i: (0, 0, 0)),
                  pl.BlockSpec(w_out.shape, c2),
                  pl.BlockSpec((1, D_MODEL), c2),
                  pl.BlockSpec((1, D_MODEL), c2),
                  pl.BlockSpec(wr_hi.shape, c2),
                  pl.BlockSpec(wr_lo.shape, c2),
                  pl.BlockSpec(rbias.shape, c2)],
        out_specs=[pl.BlockSpec((TM, D_MODEL), lambda i: (i, 0)),
                   pl.BlockSpec((TM, LANES), lambda i: (i, 0))],
        out_shape=[jax.ShapeDtypeStruct((R, D_MODEL), F32),
                   jax.ShapeDtypeStruct((R, LANES), F32)],
        compiler_params=_cparams(("parallel",)),
        name="merge_ln_route",
    )(h, o_all, w_gate, w_branch, w_out, ln_g, ln_b, wr_hi, wr_lo, rbias)


def _swiglu_act(gu):
    g = gu[:, :D_EXPERT]
    return g / (1.0 + jnp.exp(-g)) * gu[:, D_EXPERT:]


def _moe_kernel(h_ref, gates_ref, wgu_ref, wd_ref, wsgu_ref, wsd_ref, g_ref, b_ref, o_ref,
                x_sc, acc_sc, *, alpha, eb):
    e = pl.program_id(1)

    @pl.when(e == 0)
    def _():
        x_sc[...] = h_ref[...].astype(BF16)
        acc_sc[...] = jnp.zeros_like(acc_sc)

    x = x_sc[...]
    gates = gates_ref[...]
    lane = lax.broadcasted_iota(jnp.int32, gates.shape, 1)
    for u in range(eb):
        act = _swiglu_act(_dot(x, wgu_ref[u]))
        gate = jnp.sum(jnp.where(lane == e * eb + u, gates, 0.0), axis=1, keepdims=True)
        acc_sc[...] += _dot((act * gate).astype(BF16), wd_ref[u])

    @pl.when(e == pl.num_programs(1) - 1)
    def _():
        shared = _dot(_swiglu_act(_dot(x, wsgu_ref[...])).astype(BF16), wsd_ref[...])
        o_ref[...] = _layer_norm(alpha * h_ref[...] + acc_sc[...] + shared, g_ref[...], b_ref[...])


def _moe(h1, gates, wgu, wd, wsgu, wsd, ln_g, ln_b, alpha, tm, eb):
    R = h1.shape[0]
    c2 = lambda i, e: (0, 0)
    return pl.pallas_call(
        functools.partial(_moe_kernel, alpha=alpha, eb=eb),
        grid=(R // tm, N_EXPERTS // eb),
        in_specs=[pl.BlockSpec((tm, D_MODEL), lambda i, e: (i, 0)),
                  pl.BlockSpec((tm, LANES), lambda i, e: (i, 0)),
                  pl.BlockSpec((eb, D_MODEL, 2 * D_EXPERT), lambda i, e: (e, 0, 0)),
                  pl.BlockSpec((eb, D_EXPERT, D_MODEL), lambda i, e: (e, 0, 0)),
                  pl.BlockSpec(wsgu.shape, c2),
                  pl.BlockSpec(wsd.shape, c2),
                  pl.BlockSpec((1, D_MODEL), c2),
                  pl.BlockSpec((1, D_MODEL), c2)],
        out_specs=pl.BlockSpec((tm, D_MODEL), lambda i, e: (i, 0)),
        out_shape=jax.ShapeDtypeStruct((R, D_MODEL), F32),
        scratch_shapes=[pltpu.VMEM((tm, D_MODEL), BF16), pltpu.VMEM((tm, D_MODEL), F32)],
        compiler_params=_cparams(("parallel", "arbitrary")),
        name="moe_ln",
    )(h1, gates, wgu, wd, wsgu, wsd, ln_g, ln_b)


def _t5_bucket(rel):
    half = T5_BUCKETS // 2
    max_exact = half // 2
    n = np.abs(rel)
    ratio = np.log(np.maximum(n, 1).astype(np.float32) / np.float32(max_exact))
    ratio = ratio / np.float32(math.log(T5_MAX_DIST / max_exact)) * np.float32(half - max_exact)
    large = np.minimum(max_exact + ratio.astype(np.int32), half - 1)
    return np.where(rel > 0, half, 0) + np.where(n < max_exact, n, large)


def _t5_tables(t5_table, s_max):
    def bias(rel, valid):
        b = t5_table[_t5_bucket(rel)].astype(F32)
        return jnp.where(jnp.asarray(valid)[None], jnp.transpose(b, (2, 0, 1)), NEG_INF)

    ii = np.arange(BLOCK)[:, None]
    jj = np.arange(3 * BLOCK)[None, :]
    rel = jj - ii - BLOCK
    bband = bias(rel, np.abs(rel) <= WINDOW)
    t = np.arange(s_max)[:, None]
    m = np.arange(N_META)[None, :]
    bmeta = bias(m - (N_META + t), np.ones((s_max, N_META), bool))
    mpos = np.arange(N_META)[:, None]
    kpos = np.arange(N_META + BLOCK)[None, :]
    relq = kpos - mpos
    bq = bias(relq, (kpos < N_META) | (np.abs(relq) <= WINDOW))
    return bband, bmeta, bq[:, :, :N_META], bq[:, :, N_META:]


def _na_bias_cases(rpb):
    W = GRID_W
    delta = np.arange(NA_ROWS)[:, None, None, None]
    i = np.arange(NA_ROWS)[None, :, None, None]
    c = np.arange(W)[None, None, :, None]
    kc = np.arange(W)[None, None, None, :]
    cs = np.clip(c - NA_COLS // 2, 0, W - NA_COLS)
    valid = (kc >= cs) & (kc < cs + NA_COLS)
    dc = np.clip(kc - c + (NA_COLS - 1), 0, 2 * NA_COLS - 2)[0, 0]
    t = jnp.where(jnp.asarray(valid[0, 0]), rpb.astype(F32)[:, :, dc], NEG_INF)
    cases = [jnp.transpose(t[:, NA_ROWS - 1 - d:2 * NA_ROWS - 1 - d], (0, 2, 1, 3)) for d in range(NA_ROWS)]
    return jnp.stack(cases, axis=0).reshape(NA_ROWS, N_HEADS, W, NA_ROWS * W)


def _rope_tables(s_max, n_meta_rows):
    half = HEAD_DIM // 4
    freq = ROPE_THETA ** (-jnp.arange(half, dtype=F32) / half)
    t = np.arange(s_max)
    mp = np.tile(np.arange(N_META) - N_META, n_meta_rows // N_META)
    pos_row = jnp.asarray(np.concatenate([t // GRID_W, mp]), jnp.int32).astype(F32)
    pos_col = jnp.asarray(np.concatenate([t % GRID_W, mp]), jnp.int32).astype(F32)
    ar = pos_row[:, None] * freq
    ac = pos_col[:, None] * freq
    cos = jnp.concatenate([jnp.cos(ar), jnp.cos(ar), jnp.cos(ac), jnp.cos(ac)], axis=1)
    sin = jnp.concatenate([-jnp.sin(ar), jnp.sin(ar), -jnp.sin(ac), jnp.sin(ac)], axis=1)
    return jnp.tile(cos, (1, 2)), jnp.tile(sin, (1, 2))


def kernel(x_prompt, x_sample, meta_tokens, ln_in_g, ln_in_b, t5_table, w_in, q_gain, k_gain, sink,
           na_rpb, na_meta_bias, w_branch, w_out, ln1_g, ln1_b, w_router, router_bias,
           w_expert_gate_up, w_expert_down, w_shared_gate_up, w_shared_down, ln2_g, ln2_b):
    depth = w_in.shape[0]
    alpha = (2 * depth) ** 0.25
    B0, S0, D = x_prompt.shape
    B1, S1, _ = x_sample.shape
    assert D == D_MODEL
    real = B0 * S0 + B1 * S1
    n_meta_rows = -(-(B0 + B1) * N_META // TM) * TM
    R = real + n_meta_rows
    moe_tm = 1024 if R % 1024 == 0 else TM
    n_meta_blocks = n_meta_rows // N_META
    g0 = _Group(B0, S0, 0, 0, real, B0)
    g1 = _Group(B1, S1, B0 * S0, B0, real, n_meta_blocks - B0)
    s_max = max(S0, S1)

    x = jnp.concatenate([x_prompt.reshape(B0 * S0, D), x_sample.reshape(B1 * S1, D),
                         jnp.tile(meta_tokens, (n_meta_blocks, 1))], axis=0)
    h = _embed_ln(x, ln_in_g.reshape(1, D), ln_in_b.reshape(1, D))

    cos_tab, sin_tab = _rope_tables(s_max, TM)
    n0, n1 = B0 * S0 // TM, real // TM
    p0, p1, pm = S0 // TM, S1 // TM, s_max // TM

    def pos_block(i):
        return jnp.where(i < n0, i % p0, jnp.where(i < n1, (i - n0) % p1, pm))

    bband, bmeta, bq_meta, bq_blk = _t5_tables(t5_table, s_max)
    ones_bd = jnp.asarray(np.kron(np.eye(N_HEADS), np.ones((HEAD_DIM, HEAD_DIM))), BF16)

    qs, ks, vs = [], [], []
    for n in range(N_MIXERS):
        off = n * QKV_WIDTH
        qs.append(w_in[:, :, off:off + MIX_WIDTH])
        ks.append(w_in[:, :, off + MIX_WIDTH:off + MIX_WIDTH + KV_WIDTH])
        vs.append(w_in[:, :, off + MIX_WIDTH + KV_WIDTH:off + QKV_WIDTH])
    wr_t = jnp.swapaxes(w_router, 1, 2)
    wr_hi = wr_t.astype(BF16)
    layers = dict(
        w_qkv=jnp.concatenate(qs + ks + vs, axis=2).astype(BF16),
        w_gate=w_in[:, :, N_MIXERS * QKV_WIDTH:].astype(BF16),
        q_gain=jnp.tile(q_gain, (1, N_HEADS)).reshape(depth, 1, MIX_WIDTH),
        k_gain=jnp.tile(k_gain, (1, N_KV_HEADS)).reshape(depth, 1, KV_WIDTH),
        sink=sink.astype(F32),
        na_bias=jax.vmap(_na_bias_cases)(na_rpb),
        na_mbias=na_meta_bias.astype(F32),
        w_branch=w_branch.astype(BF16),
        w_out=w_out.astype(BF16),
        ln1_g=ln1_g.reshape(depth, 1, D), ln1_b=ln1_b.reshape(depth, 1, D),
        wr_hi=wr_hi, wr_lo=(wr_t - wr_hi.astype(F32)).astype(BF16),
        rbias=router_bias.astype(F32).reshape(depth, N_EXPERTS, 1),
        wgu=w_expert_gate_up.astype(BF16), wd=w_expert_down.astype(BF16),
        wsgu=w_shared_gate_up.astype(BF16), wsd=w_shared_down.astype(BF16),
        ln2_g=ln2_g.reshape(depth, 1, D), ln2_b=ln2_b.reshape(depth, 1, D),
    )

    def layer(h, p):
        q_all, k_all, v_all = _inproj(h, p["w_qkv"], cos_tab, sin_tab, p["q_gain"], p["k_gain"],
                                      ones_bd, pos_block)
        o = None
        for grp in (g0, g1):
            o = _global_attn(grp, q_all, k_all, v_all, o)
            o = _window_attn(grp, q_all, k_all, v_all, p["sink"], bband, bmeta[:, :grp.S], o)
            o = _na_attn(grp, q_all, k_all, v_all, p["na_bias"], p["na_mbias"], o)
        for grp in (g0, g1):
            o = _meta_attn(grp, q_all, k_all, v_all, p["sink"], bq_meta, bq_blk, p["na_mbias"], o)
        h1, gates = _merge(h, o, p["w_gate"], p["w_branch"], p["w_out"], p["ln1_g"], p["ln1_b"],
                           p["wr_hi"], p["wr_lo"], p["rbias"], alpha)
        h2 = _moe(h1, gates, p["wgu"], p["wd"], p["wsgu"], p["wsd"], p["ln2_g"], p["ln2_b"],
                  alpha, moe_tm, 4)
        return h2, None

    h, _ = lax.scan(layer, h, layers)
    y_prompt = h[:B0 * S0].reshape(B0, S0, D)
    y_sample = h[B0 * S0:real].reshape(B1, S1, D)
    return (y_prompt, y_sample)
```

```python
import functools
import math

import numpy as np
import jax
import jax.numpy as jnp
from jax import lax
from jax.experimental import pallas as pl
from jax.experimental.pallas import tpu as pltpu

F32 = jnp.float32
BF16 = jnp.bfloat16

D_MODEL = 1024
HEAD_DIM = 64
N_HEADS = 8
N_KV_HEADS = 2
GROUP = N_HEADS // N_KV_HEADS
MIX_WIDTH = N_HEADS * HEAD_DIM
KV_WIDTH = N_KV_HEADS * HEAD_DIM
N_MIXERS = 3
QKV_WIDTH = MIX_WIDTH + 2 * KV_WIDTH
N_META = 16
GRID_W = 64
BLOCK = 128
WINDOW = 128
NA_ROWS = 8
NA_COLS = 16
T5_BUCKETS = 32
T5_MAX_DIST = 128
ROPE_THETA = 10000.0
N_EXPERTS = 64
TOP_K = 8
N_EXPERT_GROUPS = 8
TOPK_GROUPS = 4
D_EXPERT = 256
ROUTED_SCALE = 2.5
NEG_INF = -1e30
LANES = 128

TM = 512
TQ_GLOBAL = 128
TK_GLOBAL = 512
NA_QROWS = 8
VMEM_LIMIT = 56 * 1024 * 1024


def _cparams(sem):
    return pltpu.CompilerParams(dimension_semantics=sem, vmem_limit_bytes=VMEM_LIMIT)


def _dot(a, b):
    return jnp.dot(a, b, preferred_element_type=F32)


def _dot_nt(a, b):
    return lax.dot_general(a, b, (((1,), (1,)), ((), ())), preferred_element_type=F32)


def _split_bf16(x):
    hi = x.astype(BF16)
    lo = (x - hi.astype(F32)).astype(BF16)
    return hi, lo


def _layer_norm(x, g, b):
    mu = jnp.mean(x, axis=-1, keepdims=True)
    xc = x - mu
    var = jnp.mean(xc * xc, axis=-1, keepdims=True)
    return xc * lax.rsqrt(var + 1e-5) * g + b


def _embed_ln_kernel(x_ref, g_ref, b_ref, o_ref):
    o_ref[...] = _layer_norm(x_ref[...], g_ref[...], b_ref[...])


def _embed_ln(x, g, b):
    R = x.shape[0]
    return pl.pallas_call(
        _embed_ln_kernel,
        grid=(R // TM,),
        in_specs=[pl.BlockSpec((TM, D_MODEL), lambda i: (i, 0)),
                  pl.BlockSpec((1, D_MODEL), lambda i: (0, 0)),
                  pl.BlockSpec((1, D_MODEL), lambda i: (0, 0))],
        out_specs=pl.BlockSpec((TM, D_MODEL), lambda i: (i, 0)),
        out_shape=jax.ShapeDtypeStruct((R, D_MODEL), F32),
        compiler_params=_cparams(("parallel",)),
        name="embed_ln",
    )(x, g, b)


def _rope_slot(x, cos, sin_signed, first_half):
    fwd = pltpu.roll(x, LANES - 16, 1)
    bwd = pltpu.roll(x, 16, 1)
    return x * cos + jnp.where(first_half, fwd, bwd) * sin_signed


def _head_rms(x, ones_bd, gain):
    hi, lo = _split_bf16(x * x)
    ss = _dot(hi, ones_bd) + _dot(lo, ones_bd)
    return x * lax.rsqrt(ss * (1.0 / HEAD_DIM) + 1e-6) * gain


def _inproj_kernel(h_ref, w_ref, cos_ref, sin_ref, qg_ref, kg_ref, ones_ref, q_ref, k_ref, v_ref):
    x = h_ref[...].astype(BF16)
    cos = cos_ref[...]
    sin = sin_ref[...]
    lane = lax.broadcasted_iota(jnp.int32, cos.shape, 1)
    first_half = (lane % 32) < 16
    scale = HEAD_DIM ** -0.5
    qw = N_MIXERS * MIX_WIDTH
    qa = _head_rms(_dot(x, w_ref[:, 0:MIX_WIDTH]), ones_ref[...], qg_ref[...])
    for s in range(MIX_WIDTH // LANES):
        sl = slice(s * LANES, (s + 1) * LANES)
        q_ref[:, sl] = (_rope_slot(qa[:, sl], cos, sin, first_half) * scale).astype(BF16)
    ka = _head_rms(_dot(x, w_ref[:, qw:qw + KV_WIDTH]), ones_ref[0:LANES, 0:LANES], kg_ref[...])
    k_ref[:, 0:KV_WIDTH] = _rope_slot(ka, cos, sin, first_half).astype(BF16)
    for n in range(1, N_MIXERS):
        q_ref[:, n * MIX_WIDTH:(n + 1) * MIX_WIDTH] = (
            _dot(x, w_ref[:, n * MIX_WIDTH:(n + 1) * MIX_WIDTH]) * scale).astype(BF16)
        k_ref[:, n * KV_WIDTH:(n + 1) * KV_WIDTH] = _dot(
            x, w_ref[:, qw + n * KV_WIDTH:qw + (n + 1) * KV_WIDTH]).astype(BF16)
    vw = qw + N_MIXERS * KV_WIDTH
    v_ref[...] = _dot(x, w_ref[:, vw:vw + N_MIXERS * KV_WIDTH]).astype(BF16)


def _inproj(h, w_qkv, cos_tab, sin_tab, q_gain, k_gain, ones_bd, pos_block):
    R = h.shape[0]
    const = lambda i: (0, 0)
    return pl.pallas_call(
        _inproj_kernel,
        grid=(R // TM,),
        in_specs=[pl.BlockSpec((TM, D_MODEL), lambda i: (i, 0)),
                  pl.BlockSpec(w_qkv.shape, const),
                  pl.BlockSpec((TM, LANES), lambda i: (pos_block(i), 0)),
                  pl.BlockSpec((TM, LANES), lambda i: (pos_block(i), 0)),
                  pl.BlockSpec((1, MIX_WIDTH), const),
                  pl.BlockSpec((1, KV_WIDTH), const),
                  pl.BlockSpec((MIX_WIDTH, MIX_WIDTH), const)],
        out_specs=[pl.BlockSpec((TM, N_MIXERS * MIX_WIDTH), lambda i: (i, 0)),
                   pl.BlockSpec((TM, N_MIXERS * KV_WIDTH), lambda i: (i, 0)),
                   pl.BlockSpec((TM, N_MIXERS * KV_WIDTH), lambda i: (i, 0))],
        out_shape=[jax.ShapeDtypeStruct((R, N_MIXERS * MIX_WIDTH), BF16),
                   jax.ShapeDtypeStruct((R, N_MIXERS * KV_WIDTH), BF16),
                   jax.ShapeDtypeStruct((R, N_MIXERS * KV_WIDTH), BF16)],
        compiler_params=_cparams(("parallel",)),
        name="inproj",
    )(h, w_qkv, cos_tab, sin_tab, q_gain, k_gain, ones_bd)


def _group_queries(q, j):
    lane = lax.broadcasted_iota(jnp.int32, (q.shape[0], LANES), 1)
    keep = (lane < HEAD_DIM) if j == 0 else (lane >= HEAD_DIM)
    parts = []
    for hh in range(GROUP):
        h = GROUP * j + hh
        slot = q[:, (h // 2) * LANES:(h // 2 + 1) * LANES]
        if h % 2 != j:
            slot = pltpu.roll(slot, HEAD_DIM, 1)
        parts.append(jnp.where(keep, slot, 0.0))
    return jnp.concatenate(parts, axis=0).astype(BF16)


def _ungroup_outputs(out, j, T):
    lane = lax.broadcasted_iota(jnp.int32, (T, LANES), 1)
    lo = lane < HEAD_DIM
    slots = []
    for s in range(2):
        even = out[(2 * s) * T:(2 * s + 1) * T]
        odd = out[(2 * s + 1) * T:(2 * s + 2) * T]
        if j == 0:
            slots.append(jnp.where(lo, even, pltpu.roll(odd, HEAD_DIM, 1)))
        else:
            slots.append(jnp.where(lo, pltpu.roll(even, HEAD_DIM, 1), odd))
    return jnp.concatenate(slots, axis=1)


def _head_rows(vals, T):
    return jnp.concatenate([jnp.broadcast_to(v, (T, v.shape[-1])) for v in vals], axis=0)


def _global_kernel(q_ref, k_ref, v_ref, km_ref, vm_ref, *rest, S):
    o_ref = rest[-1]
    T = q_ref.shape[0]
    q = q_ref[...].astype(F32)
    km = km_ref[...]
    vm = vm_ref[...]
    qs, state = [], []
    for j in range(N_KV_HEADS):
        qj = _group_queries(q, j)
        s_m = _dot_nt(qj, km)
        m0 = jnp.max(s_m, axis=-1, keepdims=True)
        p_m = jnp.exp(s_m - m0)
        qs.append(qj)
        state.append((m0, jnp.sum(p_m, axis=-1, keepdims=True), _dot(p_m.astype(BF16), vm)))
    for c in range(S // TK_GLOBAL):
        kc = k_ref[c * TK_GLOBAL:(c + 1) * TK_GLOBAL, :]
        vc = v_ref[c * TK_GLOBAL:(c + 1) * TK_GLOBAL, :]
        for j in range(N_KV_HEADS):
            m, l, acc = state[j]
            s = _dot_nt(qs[j], kc)
            m_new = jnp.maximum(m, jnp.max(s, axis=-1, keepdims=True))
            a = jnp.exp(m - m_new)
            p = jnp.exp(s - m_new)
            state[j] = (m_new, a * l + jnp.sum(p, axis=-1, keepdims=True),
                        a * acc + _dot(p.astype(BF16), vc))
    for j in range(N_KV_HEADS):
        _, l, acc = state[j]
        o_ref[:, j * 2 * LANES:(j + 1) * 2 * LANES] = _ungroup_outputs(acc / l, j, T).astype(BF16)


def _window_kernel(sink_ref, q_ref, kp_ref, kc_ref, kn_ref, vp_ref, vc_ref, vn_ref, km_ref, vm_ref,
                   bband_ref, bmeta_ref, *rest, nb):
    o_ref = rest[-1]
    i = pl.program_id(1)
    T = q_ref.shape[0]
    q = q_ref[...].astype(F32)
    kband = jnp.concatenate([kp_ref[...], kc_ref[...], kn_ref[...]], axis=0)
    vband = jnp.concatenate([vp_ref[...], vc_ref[...], vn_ref[...]], axis=0)
    col = lax.broadcasted_iota(jnp.int32, (1, 3 * BLOCK), 1)
    in_range = ((col >= BLOCK) | (i > 0)) & ((col < 2 * BLOCK) | (i < nb - 1))
    for j in range(N_KV_HEADS):
        qj = _group_queries(q, j)
        hs = slice(GROUP * j, GROUP * (j + 1))
        s_b = _dot_nt(qj, kband) + bband_ref[hs].reshape(GROUP * T, 3 * BLOCK)
        s_b = jnp.where(in_range, s_b, NEG_INF)
        s_m = _dot_nt(qj, km_ref[...]) + bmeta_ref[hs].reshape(GROUP * T, N_META)
        sink = jnp.concatenate([jnp.full((T, 1), sink_ref[GROUP * j + hh], F32) for hh in range(GROUP)], axis=0)
        m = jnp.maximum(jnp.maximum(jnp.max(s_b, axis=-1, keepdims=True),
                                    jnp.max(s_m, axis=-1, keepdims=True)), sink)
        e_b = jnp.exp(s_b - m)
        e_m = jnp.exp(s_m - m)
        denom = (jnp.sum(e_b, axis=-1, keepdims=True) + jnp.sum(e_m, axis=-1, keepdims=True)
                 + jnp.exp(sink - m))
        acc = _dot(e_b.astype(BF16), vband) + _dot(e_m.astype(BF16), vm_ref[...])
        o_ref[:, j * 2 * LANES:(j + 1) * 2 * LANES] = _ungroup_outputs(acc / denom, j, T).astype(BF16)


def _na_kernel(q_ref, k_ref, v_ref, km_ref, vm_ref, bias_ref, mbias_ref, *rest, rows):
    o_ref = rest[-1]
    blk = pl.program_id(1)
    W = GRID_W
    nkeys = NA_ROWS * W
    km = km_ref[...]
    vm = vm_ref[...]
    for j in range(N_KV_HEADS):
        mb = _head_rows([mbias_ref[GROUP * j + hh:GROUP * j + hh + 1, :] for hh in range(GROUP)], W)

        def row_body(rr, carry, j=j, mb=mb):
            r = blk * NA_QROWS + rr
            rs = jnp.clip(r - NA_ROWS // 2, 0, rows - NA_ROWS)
            delta = r - rs
            qoff = pl.multiple_of(rr * W, W)
            koff = pl.multiple_of(rs * W, W)
            qj = _group_queries(q_ref[pl.ds(qoff, W), :].astype(F32), j)
            kw = k_ref[pl.ds(koff, nkeys), :]
            vw = v_ref[pl.ds(koff, nkeys), :]
            s_w = _dot_nt(qj, kw) + bias_ref[delta, pl.ds(GROUP * j, GROUP)].reshape(GROUP * W, nkeys)
            s_m = _dot_nt(qj, km) + mb
            m = jnp.maximum(jnp.max(s_w, axis=-1, keepdims=True), jnp.max(s_m, axis=-1, keepdims=True))
            e_w = jnp.exp(s_w - m)
            e_m = jnp.exp(s_m - m)
            denom = jnp.sum(e_w, axis=-1, keepdims=True) + jnp.sum(e_m, axis=-1, keepdims=True)
            acc = _dot(e_w.astype(BF16), vw) + _dot(e_m.astype(BF16), vm)
            o_ref[pl.ds(qoff, W), j * 2 * LANES:(j + 1) * 2 * LANES] = (
                _ungroup_outputs(acc / denom, j, W).astype(BF16))
            return carry

        lax.fori_loop(0, NA_QROWS, row_body, 0, unroll=True)


def _meta_kernel(sink_ref, q_ref, ka_ref, va_ref, kb_ref, vb_ref, km_ref, vm_ref,
                 bq_meta_ref, bq_blk_ref, mbias_ref, o_in_ref, o_ref):
    del o_in_ref
    T = N_META
    q = q_ref[...].astype(F32)
    km_all = km_ref[...]
    vm_all = vm_ref[...]

    def finish(n, j, acc, denom):
        lo = n * MIX_WIDTH + j * 2 * LANES
        o_ref[:, lo:lo + 2 * LANES] = _ungroup_outputs(acc / denom, j, T).astype(BF16)

    for j in range(N_KV_HEADS):
        hs = slice(GROUP * j, GROUP * (j + 1))
        qj = _group_queries(q[:, 0:MIX_WIDTH], j)
        km, vm = km_all[:, 0:KV_WIDTH], vm_all[:, 0:KV_WIDTH]
        s_r = _dot_nt(qj, ka_ref[...])
        s_m = _dot_nt(qj, km)
        m = jnp.maximum(jnp.max(s_r, axis=-1, keepdims=True), jnp.max(s_m, axis=-1, keepdims=True))
        e_r = jnp.exp(s_r - m)
        e_m = jnp.exp(s_m - m)
        denom = jnp.sum(e_r, axis=-1, keepdims=True) + jnp.sum(e_m, axis=-1, keepdims=True)
        finish(0, j, _dot(e_r.astype(BF16), va_ref[...]) + _dot(e_m.astype(BF16), vm), denom)
        qj = _group_queries(q[:, MIX_WIDTH:2 * MIX_WIDTH], j)
        km, vm = km_all[:, KV_WIDTH:2 * KV_WIDTH], vm_all[:, KV_WIDTH:2 * KV_WIDTH]
        s_r = _dot_nt(qj, kb_ref[...]) + bq_blk_ref[hs].reshape(GROUP * T, BLOCK)
        s_m = _dot_nt(qj, km) + bq_meta_ref[hs].reshape(GROUP * T, N_META)
        sink = jnp.concatenate([jnp.full((T, 1), sink_ref[GROUP * j + hh], F32) for hh in range(GROUP)], axis=0)
        m = jnp.maximum(jnp.maximum(jnp.max(s_r, axis=-1, keepdims=True),
                                    jnp.max(s_m, axis=-1, keepdims=True)), sink)
        e_r = jnp.exp(s_r - m)
        e_m = jnp.exp(s_m - m)
        denom = (jnp.sum(e_r, axis=-1, keepdims=True) + jnp.sum(e_m, axis=-1, keepdims=True)
                 + jnp.exp(sink - m))
        finish(1, j, _dot(e_r.astype(BF16), vb_ref[...]) + _dot(e_m.astype(BF16), vm), denom)
        qj = _group_queries(q[:, 2 * MIX_WIDTH:3 * MIX_WIDTH], j)
        km, vm = km_all[:, 2 * KV_WIDTH:3 * KV_WIDTH], vm_all[:, 2 * KV_WIDTH:3 * KV_WIDTH]
        mb = _head_rows([mbias_ref[GROUP * j + hh:GROUP * j + hh + 1, :] for hh in range(GROUP)], T)
        s_m = _dot_nt(qj, km) + mb
        m = jnp.max(s_m, axis=-1, keepdims=True)
        e_m = jnp.exp(s_m - m)
        finish(2, j, _dot(e_m.astype(BF16), vm), jnp.sum(e_m, axis=-1, keepdims=True))


class _Group:
    def __init__(self, B, S, real_base, meta_batch_base, meta_base, n_meta_blocks):
        self.B, self.S = B, S
        self.real_base = real_base
        self.meta_blk0 = meta_base // N_META + meta_batch_base
        self.n_meta_blocks = n_meta_blocks
        assert real_base % S == 0 and S % TM == 0 and meta_base % N_META == 0


def _alias_args(o_prev, n_inputs):
    if o_prev is None:
        return [], [], {}
    return [o_prev], [pl.BlockSpec(memory_space=pl.ANY)], {n_inputs: 0}


def _global_attn(grp, q_all, k_all, v_all, o_prev):
    B, S = grp.B, grp.S
    nq = S // TQ_GLOBAL
    qb0 = grp.real_base // TQ_GLOBAL
    sb0 = grp.real_base // S
    mb0 = grp.meta_blk0
    in_specs = [pl.BlockSpec((TQ_GLOBAL, MIX_WIDTH), lambda b, i: (qb0 + b * nq + i, 0)),
                pl.BlockSpec((S, KV_WIDTH), lambda b, i: (sb0 + b, 0)),
                pl.BlockSpec((S, KV_WIDTH), lambda b, i: (sb0 + b, 0)),
                pl.BlockSpec((N_META, KV_WIDTH), lambda b, i: (mb0 + b, 0)),
                pl.BlockSpec((N_META, KV_WIDTH), lambda b, i: (mb0 + b, 0))]
    extra, extra_specs, aliases = _alias_args(o_prev, len(in_specs))
    return pl.pallas_call(
        functools.partial(_global_kernel, S=S),
        grid=(B, nq),
        in_specs=in_specs + extra_specs,
        out_specs=pl.BlockSpec((TQ_GLOBAL, MIX_WIDTH), lambda b, i: (qb0 + b * nq + i, 0)),
        out_shape=jax.ShapeDtypeStruct((q_all.shape[0], N_MIXERS * MIX_WIDTH), BF16),
        input_output_aliases=aliases,
        compiler_params=_cparams(("parallel", "arbitrary")),
        name="mixer_global",
    )(q_all, k_all, v_all, k_all, v_all, *extra)


def _window_attn(grp, q_all, k_all, v_all, sink, bband, bmeta, o_prev):
    B, S = grp.B, grp.S
    nb = S // BLOCK
    qb0 = grp.real_base // BLOCK
    mb0 = grp.meta_blk0
    cur = lambda b, i, sink: (qb0 + b * nb + i, 1)
    prv = lambda b, i, sink: (qb0 + b * nb + jnp.maximum(i - 1, 0), 1)
    nxt = lambda b, i, sink: (qb0 + b * nb + jnp.minimum(i + 1, nb - 1), 1)
    met = lambda b, i, sink: (mb0 + b, 1)
    kv = lambda im: pl.BlockSpec((BLOCK, KV_WIDTH), im)
    in_specs = [pl.BlockSpec((BLOCK, MIX_WIDTH), cur),
                kv(prv), kv(cur), kv(nxt), kv(prv), kv(cur), kv(nxt),
                pl.BlockSpec((N_META, KV_WIDTH), met), pl.BlockSpec((N_META, KV_WIDTH), met),
                pl.BlockSpec((N_HEADS, BLOCK, 3 * BLOCK), lambda b, i, sink: (0, 0, 0)),
                pl.BlockSpec((N_HEADS, BLOCK, N_META), lambda b, i, sink: (0, i, 0))]
    extra, extra_specs, aliases = _alias_args(o_prev, len(in_specs) + 1)
    return pl.pallas_call(
        functools.partial(_window_kernel, nb=nb),
        grid_spec=pltpu.PrefetchScalarGridSpec(
            num_scalar_prefetch=1,
            grid=(B, nb),
            in_specs=in_specs + extra_specs,
            out_specs=pl.BlockSpec((BLOCK, MIX_WIDTH), cur)),
        out_shape=jax.ShapeDtypeStruct((q_all.shape[0], N_MIXERS * MIX_WIDTH), BF16),
        input_output_aliases=aliases,
        compiler_params=_cparams(("parallel", "arbitrary")),
        name="mixer_window",
    )(sink, q_all, k_all, k_all, k_all, v_all, v_all, v_all, k_all, v_all, bband, bmeta, *extra)


def _na_attn(grp, q_all, k_all, v_all, na_bias, na_mbias, o_prev):
    B, S = grp.B, grp.S
    rows = S // GRID_W
    tq = NA_QROWS * GRID_W
    nq = S // tq
    qb0 = grp.real_base // tq
    sb0 = grp.real_base // S
    mb0 = grp.meta_blk0
    in_specs = [pl.BlockSpec((tq, MIX_WIDTH), lambda b, i: (qb0 + b * nq + i, 2)),
                pl.BlockSpec((S, KV_WIDTH), lambda b, i: (sb0 + b, 2)),
                pl.BlockSpec((S, KV_WIDTH), lambda b, i: (sb0 + b, 2)),
                pl.BlockSpec((N_META, KV_WIDTH), lambda b, i: (mb0 + b, 2)),
                pl.BlockSpec((N_META, KV_WIDTH), lambda b, i: (mb0 + b, 2)),
                pl.BlockSpec(na_bias.shape, lambda b, i: (0, 0, 0, 0)),
                pl.BlockSpec(na_mbias.shape, lambda b, i: (0, 0))]
    extra, extra_specs, aliases = _alias_args(o_prev, len(in_specs))
    return pl.pallas_call(
        functools.partial(_na_kernel, rows=rows),
        grid=(B, nq),
        in_specs=in_specs + extra_specs,
        out_specs=pl.BlockSpec((tq, MIX_WIDTH), lambda b, i: (qb0 + b * nq + i, 2)),
        out_shape=jax.ShapeDtypeStruct((q_all.shape[0], N_MIXERS * MIX_WIDTH), BF16),
        input_output_aliases=aliases,
        compiler_params=_cparams(("parallel", "arbitrary")),
        name="mixer_neighbourhood",
    )(q_all, k_all, v_all, k_all, v_all, na_bias, na_mbias, *extra)


def _meta_attn(grp, q_all, k_all, v_all, sink, bq_meta, bq_blk, na_mbias, o_prev):
    B, S = grp.B, grp.S
    sb0 = grp.real_base // S
    bb0 = grp.real_base // BLOCK
    nb = S // BLOCK
    mb0 = grp.meta_blk0
    clamp = lambda b: jnp.minimum(b, B - 1)
    mrow = lambda b, sink: (mb0 + clamp(b), 0)
    in_specs = [pl.BlockSpec((N_META, N_MIXERS * MIX_WIDTH), mrow),
                pl.BlockSpec((S, KV_WIDTH), lambda b, sink: (sb0 + clamp(b), 0)),
                pl.BlockSpec((S, KV_WIDTH), lambda b, sink: (sb0 + clamp(b), 0)),
                pl.BlockSpec((BLOCK, KV_WIDTH), lambda b, sink: (bb0 + clamp(b) * nb, 1)),
                pl.BlockSpec((BLOCK, KV_WIDTH), lambda b, sink: (bb0 + clamp(b) * nb, 1)),
                pl.BlockSpec((N_META, N_MIXERS * KV_WIDTH), mrow),
                pl.BlockSpec((N_META, N_MIXERS * KV_WIDTH), mrow),
                pl.BlockSpec(bq_meta.shape, lambda b, sink: (0, 0, 0)),
                pl.BlockSpec(bq_blk.shape, lambda b, sink: (0, 0, 0)),
                pl.BlockSpec(na_mbias.shape, lambda b, sink: (0, 0)),
                pl.BlockSpec(memory_space=pl.ANY)]
    return pl.pallas_call(
        _meta_kernel,
        grid_spec=pltpu.PrefetchScalarGridSpec(
            num_scalar_prefetch=1,
            grid=(grp.n_meta_blocks,),
            in_specs=in_specs,
            out_specs=pl.BlockSpec((N_META, N_MIXERS * MIX_WIDTH), lambda b, sink: (mb0 + b, 0))),
        out_shape=jax.ShapeDtypeStruct((q_all.shape[0], N_MIXERS * MIX_WIDTH), BF16),
        input_output_aliases={len(in_specs): 0},
        compiler_params=_cparams(("arbitrary",)),
        name="mixer_meta_queries",
    )(sink, q_all, k_all, v_all, k_all, v_all, k_all, v_all, bq_meta, bq_blk, na_mbias, o_prev)


def _route(h1, wr_hi, wr_lo, rbias):
    T = h1.shape[0]
    x_hi, x_lo = _split_bf16(h1)
    logits = _dot_nt(wr_hi, x_hi) + _dot_nt(wr_hi, x_lo) + _dot_nt(wr_lo, x_hi)
    scores = 1.0 / (1.0 + jnp.exp(-logits))
    sel = scores + rbias
    per_group = N_EXPERTS // N_EXPERT_GROUPS
    sel3 = sel.reshape(N_EXPERT_GROUPS, per_group, T)
    idx3 = lax.broadcasted_iota(jnp.int32, sel3.shape, 1).astype(F32)
    m1 = jnp.max(sel3, axis=1, keepdims=True)
    first = jnp.min(jnp.where(sel3 == m1, idx3, float(per_group)), axis=1, keepdims=True)
    m2 = jnp.max(jnp.where(idx3 == first, -jnp.inf, sel3), axis=1, keepdims=True)
    gscore = (m1 + m2).reshape(N_EXPERT_GROUPS, T)

    def rank_of(vals):
        idx = lax.broadcasted_iota(jnp.int32, vals.shape, 0)
        rank = jnp.zeros(vals.shape, F32)
        for r in range(vals.shape[0]):
            row = vals[r:r + 1, :]
            ge = jnp.where(row >= vals, 1.0, 0.0)
            gt = jnp.where(row > vals, 1.0, 0.0)
            rank = rank + jnp.where(idx > r, ge, gt)
        return rank

    gkeep = jnp.where(rank_of(gscore) < TOPK_GROUPS, 1.0, 0.0)
    ekeep = jnp.broadcast_to(gkeep.reshape(N_EXPERT_GROUPS, 1, T), sel3.shape).reshape(N_EXPERTS, T)
    masked = jnp.where(ekeep > 0.5, sel, NEG_INF)
    w = jnp.where(rank_of(masked) < TOP_K, scores, 0.0)
    return w / jnp.sum(w, axis=0, keepdims=True) * ROUTED_SCALE


def _merge_kernel(h_ref, o_ref, wg_ref, wb_ref, wo_ref, g_ref, b_ref, wrh_ref, wrl_ref, rb_ref,
                  h1_ref, gates_ref, *, alpha):
    h = h_ref[...]
    x = h.astype(BF16)
    merged = None
    for n in range(N_MIXERS):
        logit = _dot(x, wg_ref[:, n * D_MODEL:(n + 1) * D_MODEL])
        branch = _dot(o_ref[:, n * MIX_WIDTH:(n + 1) * MIX_WIDTH], wb_ref[n])
        term = branch / (1.0 + jnp.exp(-logit))
        merged = term if merged is None else merged + term
    mix = _dot(merged.astype(BF16), wo_ref[...])
    h1 = _layer_norm(alpha * h + mix, g_ref[...], b_ref[...])
    h1_ref[...] = h1
    gates_t = _route(h1, wrh_ref[...], wrl_ref[...], rb_ref[...])
    pad = jnp.zeros((LANES - N_EXPERTS, gates_t.shape[1]), F32)
    gates_ref[...] = jnp.concatenate([gates_t, pad], axis=0).T


def _merge(h, o_all, w_gate, w_branch, w_out, ln_g, ln_b, wr_hi, wr_lo, rbias, alpha):
    R = h.shape[0]
    c2 = lambda i: (0, 0)
    return pl.pallas_call(
        functools.partial(_merge_kernel, alpha=alpha),
        grid=(R // TM,),
        in_specs=[pl.BlockSpec((TM, D_MODEL), lambda i: (i, 0)),
                  pl.BlockSpec((TM, N_MIXERS * MIX_WIDTH), lambda i: (i, 0)),
                  pl.BlockSpec(w_gate.shape, c2),
                  pl.BlockSpec(w_branch.shape, lambda i: (0, 0, 0)),
                  pl.BlockSpec(w_out.shape, c2),
                  pl.BlockSpec((1, D_MODEL), c2),
                  pl.BlockSpec((1, D_MODEL), c2),
                  pl.BlockSpec(wr_hi.shape, c2),
                  pl.BlockSpec(wr_lo.shape, c2),
                  pl.BlockSpec(rbias.shape, c2)],
        out_specs=[pl.BlockSpec((TM, D_MODEL), lambda i: (i, 0)),
                   pl.BlockSpec((TM, LANES), lambda i: (i, 0))],
        out_shape=[jax.ShapeDtypeStruct((R, D_MODEL), F32),
                   jax.ShapeDtypeStruct((R, LANES), F32)],
        compiler_params=_cparams(("parallel",)),
        name="merge_ln_route",
    )(h, o_all, w_gate, w_branch, w_out, ln_g, ln_b, wr_hi, wr_lo, rbias)


TD = 256
CH = 16
SLOTS = 3072
NCH = SLOTS // CH
MT = 512
CPM = MT // CH


def _swiglu_act(gu):
    g = gu[:, :D_EXPERT]
    return g / (1.0 + jnp.exp(-g)) * gu[:, D_EXPERT:]


def _slot_of_token(gates, lo_row):
    routed = gates > 0.0
    r = lax.broadcasted_iota(jnp.int32, (TD, TD), 0)
    c = lax.broadcasted_iota(jnp.int32, (TD, TD), 1)
    earlier = jnp.where(c < r, 1.0, 0.0).astype(BF16)
    rank = _dot(earlier, jnp.where(routed, 1.0, 0.0).astype(BF16))
    return jnp.where(routed, lo_row + rank + 1.0, 0.0)


def _split64(x):
    a = jnp.floor(x * (1.0 / 64.0))
    return a.astype(BF16), (x - 64.0 * a).astype(BF16)


def _dispatch_kernel(h_ref, g_ref, lohi_ref, x_ref, w_ref):
    lohi = lohi_ref[0]
    lo_row, hi_row = lohi[0:1], lohi[1:2]
    gates = g_ref[...]
    a, b = _split64(_slot_of_token(gates, lo_row).T)
    gates_t = gates.T.astype(BF16)
    x = h_ref[...].astype(BF16)
    for blk in range(SLOTS // MT):
        s = (lax.broadcasted_iota(jnp.int32, (MT, LANES), 0) + blk * MT).astype(F32)
        owner = jnp.where(s >= lo_row, jnp.where(s < hi_row, 1.0, 0.0), 0.0).astype(BF16)
        want = 64.0 * _dot(owner, a) + _dot(owner, b)
        s1 = (lax.broadcasted_iota(jnp.int32, (MT, TD), 0) + (blk * MT + 1)).astype(F32)
        hit = want == s1
        x_ref[blk * MT:(blk + 1) * MT, :] = _dot(jnp.where(hit, 1.0, 0.0).astype(BF16), x).astype(BF16)
        weight = jnp.where(hit, _dot(owner, gates_t), 0.0)
        w_ref[:, blk * MT:(blk + 1) * MT] = weight.T.astype(BF16)


def _dispatch(h1, gates, lohi):
    n = h1.shape[0] // TD
    return pl.pallas_call(
        _dispatch_kernel,
        grid=(n,),
        in_specs=[pl.BlockSpec((TD, D_MODEL), lambda i: (i, 0)),
                  pl.BlockSpec((TD, LANES), lambda i: (i, 0)),
                  pl.BlockSpec((1, 8, LANES), lambda i: (i, 0, 0))],
        out_specs=[pl.BlockSpec((SLOTS, D_MODEL), lambda i: (i, 0)),
                   pl.BlockSpec((TD, SLOTS), lambda i: (i, 0))],
        out_shape=[jax.ShapeDtypeStruct((n * SLOTS, D_MODEL), BF16),
                   jax.ShapeDtypeStruct((n * TD, SLOTS), BF16)],
        compiler_params=_cparams(("parallel",)),
        name="moe_dispatch",
    )(h1, gates, lohi)


def _chunk_gather(table_ref, first, n_chunks, src_hbm, buf, sem):
    copies = []
    for c in range(n_chunks):
        row = pl.multiple_of(table_ref[first + c] * CH, CH)
        copies.append(pltpu.make_async_copy(src_hbm.at[pl.ds(row, CH)], buf.at[pl.ds(c * CH, CH)], sem))
    return copies


def _chunk_wait(n_chunks, src_hbm, buf, sem):
    for c in range(n_chunks):
        pltpu.make_async_copy(src_hbm.at[pl.ds(0, CH)], buf.at[pl.ds(c * CH, CH)], sem).wait()


def _expert_kernel(te_ref, src_ref, nu_ref, x_hbm, wgu_ref, wd_ref, y_ref, xbuf, sem):
    del te_ref
    m = pl.program_id(0)
    n_used = nu_ref[0]

    def start(step):
        slot = step % 2
        for cp in _chunk_gather(src_ref, step * CPM, CPM, x_hbm, xbuf.at[slot], sem.at[slot]):
            cp.start()

    @pl.when(m == 0)
    def _():
        start(m)

    @pl.when(m + 1 < n_used)
    def _():
        start(m + 1)

    @pl.when(m < n_used)
    def _():
        slot = m % 2
        _chunk_wait(CPM, x_hbm, xbuf.at[slot], sem.at[slot])
        act = _swiglu_act(_dot(xbuf[slot], wgu_ref[0]))
        y_ref[...] = _dot(act.astype(BF16), wd_ref[0]).astype(BF16)


def _experts(x_disp, wgu, wd, tile_expert, src_chunk, n_used):
    n_steps = tile_expert.shape[0]
    return pl.pallas_call(
        _expert_kernel,
        grid_spec=pltpu.PrefetchScalarGridSpec(
            num_scalar_prefetch=3,
            grid=(n_steps,),
            in_specs=[pl.BlockSpec(memory_space=pl.ANY),
                      pl.BlockSpec((1, D_MODEL, 2 * D_EXPERT), lambda m, te, src, nu: (te[m], 0, 0)),
                      pl.BlockSpec((1, D_EXPERT, D_MODEL), lambda m, te, src, nu: (te[m], 0, 0))],
            out_specs=pl.BlockSpec((MT, D_MODEL), lambda m, te, src, nu: (jnp.minimum(m, nu[0] - 1), 0)),
            scratch_shapes=[pltpu.VMEM((2, MT, D_MODEL), BF16), pltpu.SemaphoreType.DMA((2,))]),
        out_shape=jax.ShapeDtypeStruct((n_steps * MT, D_MODEL), BF16),
        compiler_params=_cparams(("arbitrary",)),
        name="moe_experts",
    )(tile_expert, src_chunk, n_used, x_disp, wgu, wd)


def _combine_kernel(dst_ref, y_hbm, w_ref, h_ref, wsgu_ref, wsd_ref, lg_ref, lb_ref, o_ref, ybuf, sem, *, alpha):
    i = pl.program_id(0)

    def start(tile):
        slot = tile % 2
        for cp in _chunk_gather(dst_ref, tile * NCH, NCH, y_hbm, ybuf.at[slot], sem.at[slot]):
            cp.start()

    @pl.when(i == 0)
    def _():
        start(i)

    @pl.when(i + 1 < pl.num_programs(0))
    def _():
        start(i + 1)

    h = h_ref[...]
    shared = _dot(_swiglu_act(_dot(h.astype(BF16), wsgu_ref[...])).astype(BF16), wsd_ref[...])
    slot = i % 2
    _chunk_wait(NCH, y_hbm, ybuf.at[slot], sem.at[slot])
    routed = _dot(w_ref[...], ybuf[slot])
    o_ref[...] = _layer_norm(alpha * h + shared + routed, lg_ref[...], lb_ref[...])


def _combine(y_sorted, w_t, h1, dst_chunk, wsgu, wsd, ln_g, ln_b, alpha):
    n = h1.shape[0] // TD
    c2 = lambda i, dst: (0, 0)
    return pl.pallas_call(
        functools.partial(_combine_kernel, alpha=alpha),
        grid_spec=pltpu.PrefetchScalarGridSpec(
            num_scalar_prefetch=1,
            grid=(n,),
            in_specs=[pl.BlockSpec(memory_space=pl.ANY),
                      pl.BlockSpec((TD, SLOTS), lambda i, dst: (i, 0)),
                      pl.BlockSpec((TD, D_MODEL), lambda i, dst: (i, 0)),
                      pl.BlockSpec(wsgu.shape, c2),
                      pl.BlockSpec(wsd.shape, c2),
                      pl.BlockSpec((1, D_MODEL), c2),
                      pl.BlockSpec((1, D_MODEL), c2)],
            out_specs=pl.BlockSpec((TD, D_MODEL), lambda i, dst: (i, 0)),
            scratch_shapes=[pltpu.VMEM((2, SLOTS, D_MODEL), BF16), pltpu.SemaphoreType.DMA((2,))]),
        out_shape=jax.ShapeDtypeStruct((h1.shape[0], D_MODEL), F32),
        compiler_params=_cparams(("arbitrary",)),
        name="moe_combine_ln",
    )(dst_chunk, y_sorted, w_t, h1, wsgu, wsd, ln_g, ln_b)


def _routing_tables(gates):
    n = gates.shape[0] // TD
    cnt = jnp.sum((gates[:, :N_EXPERTS] > 0.0).reshape(n, TD, N_EXPERTS), axis=1, dtype=jnp.int32)
    nch = (cnt + (CH - 1)) // CH
    hi16 = jnp.cumsum(nch, axis=1)
    lo16 = hi16 - nch
    nct = hi16[:, -1:]
    pad = jnp.broadcast_to(nct, (n, LANES - N_EXPERTS))
    lohi = jnp.zeros((n, 8, LANES), F32)
    lohi = lohi.at[:, 0, :].set((jnp.concatenate([lo16, pad], axis=1) * CH).astype(F32))
    lohi = lohi.at[:, 1, :].set((jnp.concatenate([hi16, pad], axis=1) * CH).astype(F32))
    tot = jnp.sum(nch, axis=0)
    seg_len = (tot + (CPM - 1)) // CPM * CPM
    seg_end = jnp.cumsum(seg_len)
    seg_start = seg_end - seg_len
    gpos = seg_start[None, :] + jnp.cumsum(nch, axis=0) - nch
    n_steps = (n * NCH + N_EXPERTS * CPM) // CPM
    n_used = (seg_end[-1] // CPM).astype(jnp.int32).reshape(1)
    step = jnp.arange(n_steps, dtype=jnp.int32)
    tile_expert = jnp.sum(seg_end[None, :] // CPM <= jnp.minimum(step, n_used - 1)[:, None], axis=1, dtype=jnp.int32)
    tile_expert = jnp.minimum(tile_expert, N_EXPERTS - 1)
    gpos_f = gpos.T.reshape(-1)
    nch_f = nch.T.reshape(-1)
    src_f = (jnp.arange(n, dtype=jnp.int32)[None, :] * NCH + lo16.T).reshape(-1)
    g = jnp.arange(n_steps * CPM, dtype=jnp.int32)
    f = jnp.clip(jnp.searchsorted(gpos_f, g, side="right") - 1, 0, gpos_f.shape[0] - 1)
    c = g - gpos_f[f]
    src_chunk = jnp.where(c < nch_f[f], src_f[f] + c, 0).astype(jnp.int32)
    k = jnp.arange(NCH, dtype=jnp.int32)
    e_of_k = jnp.minimum(jnp.sum(hi16[:, None, :] <= k[None, :, None], axis=2, dtype=jnp.int32), N_EXPERTS - 1)
    pos = jnp.take_along_axis(gpos, e_of_k, axis=1) + k[None, :] - jnp.take_along_axis(lo16, e_of_k, axis=1)
    dst_chunk = jnp.where(k[None, :] < nct, pos, 0).astype(jnp.int32).reshape(-1)
    return lohi, tile_expert, src_chunk, n_used, dst_chunk


def _moe(h1, gates, wgu, wd, wsgu, wsd, ln_g, ln_b, alpha):
    lohi, tile_expert, src_chunk, n_used, dst_chunk = _routing_tables(gates)
    x_disp, w_t = _dispatch(h1, gates, lohi)
    y_sorted = _experts(x_disp, wgu, wd, tile_expert, src_chunk, n_used)
    return _combine(y_sorted, w_t, h1, dst_chunk, wsgu, wsd, ln_g, ln_b, alpha)


def _t5_bucket(rel):
    half = T5_BUCKETS // 2
    max_exact = half // 2
    n = np.abs(rel)
    ratio = np.log(np.maximum(n, 1).astype(np.float32) / np.float32(max_exact))
    ratio = ratio / np.float32(math.log(T5_MAX_DIST / max_exact)) * np.float32(half - max_exact)
    large = np.minimum(max_exact + ratio.astype(np.int32), half - 1)
    return np.where(rel > 0, half, 0) + np.where(n < max_exact, n, large)


def _t5_tables(t5_table, s_max):
    def bias(rel, valid):
        b = t5_table[_t5_bucket(rel)].astype(F32)
        return jnp.where(jnp.asarray(valid)[None], jnp.transpose(b, (2, 0, 1)), NEG_INF)

    ii = np.arange(BLOCK)[:, None]
    jj = np.arange(3 * BLOCK)[None, :]
    rel = jj - ii - BLOCK
    bband = bias(rel, np.abs(rel) <= WINDOW)
    t = np.arange(s_max)[:, None]
    m = np.arange(N_META)[None, :]
    bmeta = bias(m - (N_META + t), np.ones((s_max, N_META), bool))
    mpos = np.arange(N_META)[:, None]
    kpos = np.arange(N_META + BLOCK)[None, :]
    relq = kpos - mpos
    bq = bias(relq, (kpos < N_META) | (np.abs(relq) <= WINDOW))
    return bband, bmeta, bq[:, :, :N_META], bq[:, :, N_META:]


def _na_bias_cases(rpb):
    W = GRID_W
    delta = np.arange(NA_ROWS)[:, None, None, None]
    i = np.arange(NA_ROWS)[None, :, None, None]
    c = np.arange(W)[None, None, :, None]
    kc = np.arange(W)[None, None, None, :]
    cs = np.clip(c - NA_COLS // 2, 0, W - NA_COLS)
    valid = (kc >= cs) & (kc < cs + NA_COLS)
    dc = np.clip(kc - c + (NA_COLS - 1), 0, 2 * NA_COLS - 2)[0, 0]
    t = jnp.where(jnp.asarray(valid[0, 0]), rpb.astype(F32)[:, :, dc], NEG_INF)
    cases = [jnp.transpose(t[:, NA_ROWS - 1 - d:2 * NA_ROWS - 1 - d], (0, 2, 1, 3)) for d in range(NA_ROWS)]
    return jnp.stack(cases, axis=0).reshape(NA_ROWS, N_HEADS, W, NA_ROWS * W)


def _rope_tables(s_max, n_meta_rows):
    half = HEAD_DIM // 4
    freq = ROPE_THETA ** (-jnp.arange(half, dtype=F32) / half)
    t = np.arange(s_max)
    mp = np.tile(np.arange(N_META) - N_META, n_meta_rows // N_META)
    pos_row = jnp.asarray(np.concatenate([t // GRID_W, mp]), jnp.int32).astype(F32)
    pos_col = jnp.asarray(np.concatenate([t % GRID_W, mp]), jnp.int32).astype(F32)
    ar = pos_row[:, None] * freq
    ac = pos_col[:, None] * freq
    cos = jnp.concatenate([jnp.cos(ar), jnp.cos(ar), jnp.cos(ac), jnp.cos(ac)], axis=1)
    sin = jnp.concatenate([-jnp.sin(ar), jnp.sin(ar), -jnp.sin(ac), jnp.sin(ac)], axis=1)
    return jnp.tile(cos, (1, 2)), jnp.tile(sin, (1, 2))


def kernel(x_prompt, x_sample, meta_tokens, ln_in_g, ln_in_b, t5_table, w_in, q_gain, k_gain, sink,
           na_rpb, na_meta_bias, w_branch, w_out, ln1_g, ln1_b, w_router, router_bias,
           w_expert_gate_up, w_expert_down, w_shared_gate_up, w_shared_down, ln2_g, ln2_b):
    depth = w_in.shape[0]
    alpha = (2 * depth) ** 0.25
    B0, S0, D = x_prompt.shape
    B1, S1, _ = x_sample.shape
    assert D == D_MODEL
    real = B0 * S0 + B1 * S1
    n_meta_rows = -(-(B0 + B1) * N_META // TM) * TM
    R = real + n_meta_rows
    n_meta_blocks = n_meta_rows // N_META
    g0 = _Group(B0, S0, 0, 0, real, B0)
    g1 = _Group(B1, S1, B0 * S0, B0, real, n_meta_blocks - B0)
    s_max = max(S0, S1)

    x = jnp.concatenate([x_prompt.reshape(B0 * S0, D), x_sample.reshape(B1 * S1, D),
                         jnp.tile(meta_tokens, (n_meta_blocks, 1))], axis=0)
    h = _embed_ln(x, ln_in_g.reshape(1, D), ln_in_b.reshape(1, D))

    cos_tab, sin_tab = _rope_tables(s_max, TM)
    n0, n1 = B0 * S0 // TM, real // TM
    p0, p1, pm = S0 // TM, S1 // TM, s_max // TM

    def pos_block(i):
        return jnp.where(i < n0, i % p0, jnp.where(i < n1, (i - n0) % p1, pm))

    bband, bmeta, bq_meta, bq_blk = _t5_tables(t5_table, s_max)
    ones_bd = jnp.asarray(np.kron(np.eye(N_HEADS), np.ones((HEAD_DIM, HEAD_DIM))), BF16)

    qs, ks, vs = [], [], []
    for n in range(N_MIXERS):
        off = n * QKV_WIDTH
        qs.append(w_in[:, :, off:off + MIX_WIDTH])
        ks.append(w_in[:, :, off + MIX_WIDTH:off + MIX_WIDTH + KV_WIDTH])
        vs.append(w_in[:, :, off + MIX_WIDTH + KV_WIDTH:off + QKV_WIDTH])
    wr_t = jnp.swapaxes(w_router, 1, 2)
    wr_hi = wr_t.astype(BF16)
    layers = dict(
        w_qkv=jnp.concatenate(qs + ks + vs, axis=2).astype(BF16),
        w_gate=w_in[:, :, N_MIXERS * QKV_WIDTH:].astype(BF16),
        q_gain=jnp.tile(q_gain, (1, N_HEADS)).reshape(depth, 1, MIX_WIDTH),
        k_gain=jnp.tile(k_gain, (1, N_KV_HEADS)).reshape(depth, 1, KV_WIDTH),
        sink=sink.astype(F32),
        na_bias=jax.vmap(_na_bias_cases)(na_rpb),
        na_mbias=na_meta_bias.astype(F32),
        w_branch=w_branch.astype(BF16),
        w_out=w_out.astype(BF16),
        ln1_g=ln1_g.reshape(depth, 1, D), ln1_b=ln1_b.reshape(depth, 1, D),
        wr_hi=wr_hi, wr_lo=(wr_t - wr_hi.astype(F32)).astype(BF16),
        rbias=router_bias.astype(F32).reshape(depth, N_EXPERTS, 1),
        wgu=w_expert_gate_up.astype(BF16), wd=w_expert_down.astype(BF16),
        wsgu=w_shared_gate_up.astype(BF16), wsd=w_shared_down.astype(BF16),
        ln2_g=ln2_g.reshape(depth, 1, D), ln2_b=ln2_b.reshape(depth, 1, D),
    )

    def layer(h, p):
        q_all, k_all, v_all = _inproj(h, p["w_qkv"], cos_tab, sin_tab, p["q_gain"], p["k_gain"],
                                      ones_bd, pos_block)
        o = None
        for grp in (g0, g1):
            o = _global_attn(grp, q_all, k_all, v_all, o)
            o = _window_attn(grp, q_all, k_all, v_all, p["sink"], bband, bmeta[:, :grp.S], o)
            o = _na_attn(grp, q_all, k_all, v_all, p["na_bias"], p["na_mbias"], o)
        for grp in (g0, g1):
            o = _meta_attn(grp, q_all, k_all, v_all, p["sink"], bq_meta, bq_blk, p["na_mbias"], o)
        h1, gates = _merge(h, o, p["w_gate"], p["w_branch"], p["w_out"], p["ln1_g"], p["ln1_b"],
                           p["wr_hi"], p["wr_lo"], p["rbias"], alpha)
        h2 = _moe(h1, gates, p["wgu"], p["wd"], p["wsgu"], p["wsd"], p["ln2_g"], p["ln2_b"],
                  alpha)
        return h2, None

    h, _ = lax.scan(layer, h, layers)
    y_prompt = h[:B0 * S0].reshape(B0, S0, D)
    y_sample = h[B0 * S0:real].reshape(B1, S1, D)
    return (y_prompt, y_sample)
```

```python
import functools
import math

import numpy as np
import jax
import jax.numpy as jnp
from jax import lax
from jax.experimental import pallas as pl
from jax.experimental.pallas import tpu as pltpu

F32 = jnp.float32
BF16 = jnp.bfloat16

D_MODEL = 1024
HEAD_DIM = 64
N_HEADS = 8
N_KV_HEADS = 2
GROUP = N_HEADS // N_KV_HEADS
MIX_WIDTH = N_HEADS * HEAD_DIM
KV_WIDTH = N_KV_HEADS * HEAD_DIM
N_MIXERS = 3
QKV_WIDTH = MIX_WIDTH + 2 * KV_WIDTH
N_META = 16
GRID_W = 64
BLOCK = 128
WINDOW = 128
NA_ROWS = 8
NA_COLS = 16
T5_BUCKETS = 32
T5_MAX_DIST = 128
ROPE_THETA = 10000.0
N_EXPERTS = 64
TOP_K = 8
N_EXPERT_GROUPS = 8
TOPK_GROUPS = 4
D_EXPERT = 256
ROUTED_SCALE = 2.5
NEG_INF = -1e30
LANES = 128

TM = 512
TQ_GLOBAL = 128
TK_GLOBAL = 512
NA_QROWS = 8
VMEM_LIMIT = 56 * 1024 * 1024


def _cparams(sem):
    return pltpu.CompilerParams(dimension_semantics=sem, vmem_limit_bytes=VMEM_LIMIT)


def _dot(a, b):
    return jnp.dot(a, b, preferred_element_type=F32)


def _dot_nt(a, b):
    return lax.dot_general(a, b, (((1,), (1,)), ((), ())), preferred_element_type=F32)


def _split_bf16(x):
    hi = x.astype(BF16)
    lo = (x - hi.astype(F32)).astype(BF16)
    return hi, lo


def _layer_norm(x, g, b):
    mu = jnp.mean(x, axis=-1, keepdims=True)
    xc = x - mu
    var = jnp.mean(xc * xc, axis=-1, keepdims=True)
    return xc * lax.rsqrt(var + 1e-5) * g + b


def _embed_ln_kernel(x_ref, g_ref, b_ref, o_ref):
    o_ref[...] = _layer_norm(x_ref[...], g_ref[...], b_ref[...])


def _embed_ln(x, g, b):
    R = x.shape[0]
    return pl.pallas_call(
        _embed_ln_kernel,
        grid=(R // TM,),
        in_specs=[pl.BlockSpec((TM, D_MODEL), lambda i: (i, 0)),
                  pl.BlockSpec((1, D_MODEL), lambda i: (0, 0)),
                  pl.BlockSpec((1, D_MODEL), lambda i: (0, 0))],
        out_specs=pl.BlockSpec((TM, D_MODEL), lambda i: (i, 0)),
        out_shape=jax.ShapeDtypeStruct((R, D_MODEL), F32),
        compiler_params=_cparams(("parallel",)),
        name="embed_ln",
    )(x, g, b)


def _rope_slot(x, cos, sin_signed, first_half):
    fwd = pltpu.roll(x, LANES - 16, 1)
    bwd = pltpu.roll(x, 16, 1)
    return x * cos + jnp.where(first_half, fwd, bwd) * sin_signed


def _head_rms(x, ones_bd, gain):
    hi, lo = _split_bf16(x * x)
    ss = _dot(hi, ones_bd) + _dot(lo, ones_bd)
    return x * lax.rsqrt(ss * (1.0 / HEAD_DIM) + 1e-6) * gain


def _inproj_kernel(h_ref, w_ref, cos_ref, sin_ref, qg_ref, kg_ref, ones_ref, q_ref, k_ref, v_ref):
    x = h_ref[...].astype(BF16)
    cos = cos_ref[...]
    sin = sin_ref[...]
    lane = lax.broadcasted_iota(jnp.int32, cos.shape, 1)
    first_half = (lane % 32) < 16
    scale = HEAD_DIM ** -0.5
    qw = N_MIXERS * MIX_WIDTH
    qa = _head_rms(_dot(x, w_ref[:, 0:MIX_WIDTH]), ones_ref[...], qg_ref[...])
    for s in range(MIX_WIDTH // LANES):
        sl = slice(s * LANES, (s + 1) * LANES)
        q_ref[:, sl] = (_rope_slot(qa[:, sl], cos, sin, first_half) * scale).astype(BF16)
    ka = _head_rms(_dot(x, w_ref[:, qw:qw + KV_WIDTH]), ones_ref[0:LANES, 0:LANES], kg_ref[...])
    k_ref[:, 0:KV_WIDTH] = _rope_slot(ka, cos, sin, first_half).astype(BF16)
    for n in range(1, N_MIXERS):
        q_ref[:, n * MIX_WIDTH:(n + 1) * MIX_WIDTH] = (
            _dot(x, w_ref[:, n * MIX_WIDTH:(n + 1) * MIX_WIDTH]) * scale).astype(BF16)
        k_ref[:, n * KV_WIDTH:(n + 1) * KV_WIDTH] = _dot(
            x, w_ref[:, qw + n * KV_WIDTH:qw + (n + 1) * KV_WIDTH]).astype(BF16)
    vw = qw + N_MIXERS * KV_WIDTH
    v_ref[...] = _dot(x, w_ref[:, vw:vw + N_MIXERS * KV_WIDTH]).astype(BF16)


def _inproj(h, w_qkv, cos_tab, sin_tab, q_gain, k_gain, ones_bd, pos_block):
    R = h.shape[0]
    const = lambda i: (0, 0)
    return pl.pallas_call(
        _inproj_kernel,
        grid=(R // TM,),
        in_specs=[pl.BlockSpec((TM, D_MODEL), lambda i: (i, 0)),
                  pl.BlockSpec(w_qkv.shape, const),
                  pl.BlockSpec((TM, LANES), lambda i: (pos_block(i), 0)),
                  pl.BlockSpec((TM, LANES), lambda i: (pos_block(i), 0)),
                  pl.BlockSpec((1, MIX_WIDTH), const),
                  pl.BlockSpec((1, KV_WIDTH), const),
                  pl.BlockSpec((MIX_WIDTH, MIX_WIDTH), const)],
        out_specs=[pl.BlockSpec((TM, N_MIXERS * MIX_WIDTH), lambda i: (i, 0)),
                   pl.BlockSpec((TM, N_MIXERS * KV_WIDTH), lambda i: (i, 0)),
                   pl.BlockSpec((TM, N_MIXERS * KV_WIDTH), lambda i: (i, 0))],
        out_shape=[jax.ShapeDtypeStruct((R, N_MIXERS * MIX_WIDTH), BF16),
                   jax.ShapeDtypeStruct((R, N_MIXERS * KV_WIDTH), BF16),
                   jax.ShapeDtypeStruct((R, N_MIXERS * KV_WIDTH), BF16)],
        compiler_params=_cparams(("parallel",)),
        name="inproj",
    )(h, w_qkv, cos_tab, sin_tab, q_gain, k_gain, ones_bd)


def _group_queries(q, j):
    lane = lax.broadcasted_iota(jnp.int32, (q.shape[0], LANES), 1)
    keep = (lane < HEAD_DIM) if j == 0 else (lane >= HEAD_DIM)
    parts = []
    for hh in range(GROUP):
        h = GROUP * j + hh
        slot = q[:, (h // 2) * LANES:(h // 2 + 1) * LANES]
        if h % 2 != j:
            slot = pltpu.roll(slot, HEAD_DIM, 1)
        parts.append(jnp.where(keep, slot, 0.0))
    return jnp.concatenate(parts, axis=0).astype(BF16)


def _ungroup_outputs(out, j, T):
    lane = lax.broadcasted_iota(jnp.int32, (T, LANES), 1)
    lo = lane < HEAD_DIM
    slots = []
    for s in range(2):
        even = out[(2 * s) * T:(2 * s + 1) * T]
        odd = out[(2 * s + 1) * T:(2 * s + 2) * T]
        if j == 0:
            slots.append(jnp.where(lo, even, pltpu.roll(odd, HEAD_DIM, 1)))
        else:
            slots.append(jnp.where(lo, pltpu.roll(even, HEAD_DIM, 1), odd))
    return jnp.concatenate(slots, axis=1)


def _head_rows(vals, T):
    return jnp.concatenate([jnp.broadcast_to(v, (T, v.shape[-1])) for v in vals], axis=0)


def _global_kernel(q_ref, k_ref, v_ref, km_ref, vm_ref, *rest, S):
    o_ref = rest[-1]
    T = q_ref.shape[0]
    q = q_ref[...].astype(F32)
    km = km_ref[...]
    vm = vm_ref[...]
    qs, state = [], []
    for j in range(N_KV_HEADS):
        qj = _group_queries(q, j)
        s_m = _dot_nt(qj, km)
        m0 = jnp.max(s_m, axis=-1, keepdims=True)
        p_m = jnp.exp(s_m - m0)
        qs.append(qj)
        state.append((m0, jnp.sum(p_m, axis=-1, keepdims=True), _dot(p_m.astype(BF16), vm)))
    for c in range(S // TK_GLOBAL):
        kc = k_ref[c * TK_GLOBAL:(c + 1) * TK_GLOBAL, :]
        vc = v_ref[c * TK_GLOBAL:(c + 1) * TK_GLOBAL, :]
        for j in range(N_KV_HEADS):
            m, l, acc = state[j]
            s = _dot_nt(qs[j], kc)
            m_new = jnp.maximum(m, jnp.max(s, axis=-1, keepdims=True))
            a = jnp.exp(m - m_new)
            p = jnp.exp(s - m_new)
            state[j] = (m_new, a * l + jnp.sum(p, axis=-1, keepdims=True),
                        a * acc + _dot(p.astype(BF16), vc))
    for j in range(N_KV_HEADS):
        _, l, acc = state[j]
        o_ref[:, j * 2 * LANES:(j + 1) * 2 * LANES] = _ungroup_outputs(acc / l, j, T).astype(BF16)


def _window_kernel(sink_ref, q_ref, kp_ref, kc_ref, kn_ref, vp_ref, vc_ref, vn_ref, km_ref, vm_ref,
                   bband_ref, bmeta_ref, *rest, nb):
    o_ref = rest[-1]
    i = pl.program_id(1)
    T = q_ref.shape[0]
    q = q_ref[...].astype(F32)
    kband = jnp.concatenate([kp_ref[...], kc_ref[...], kn_ref[...]], axis=0)
    vband = jnp.concatenate([vp_ref[...], vc_ref[...], vn_ref[...]], axis=0)
    col = lax.broadcasted_iota(jnp.int32, (1, 3 * BLOCK), 1)
    in_range = ((col >= BLOCK) | (i > 0)) & ((col < 2 * BLOCK) | (i < nb - 1))
    for j in range(N_KV_HEADS):
        qj = _group_queries(q, j)
        hs = slice(GROUP * j, GROUP * (j + 1))
        s_b = _dot_nt(qj, kband) + bband_ref[hs].reshape(GROUP * T, 3 * BLOCK)
        s_b = jnp.where(in_range, s_b, NEG_INF)
        s_m = _dot_nt(qj, km_ref[...]) + bmeta_ref[hs].reshape(GROUP * T, N_META)
        sink = jnp.concatenate([jnp.full((T, 1), sink_ref[GROUP * j + hh], F32) for hh in range(GROUP)], axis=0)
        m = jnp.maximum(jnp.maximum(jnp.max(s_b, axis=-1, keepdims=True),
                                    jnp.max(s_m, axis=-1, keepdims=True)), sink)
        e_b = jnp.exp(s_b - m)
        e_m = jnp.exp(s_m - m)
        denom = (jnp.sum(e_b, axis=-1, keepdims=True) + jnp.sum(e_m, axis=-1, keepdims=True)
                 + jnp.exp(sink - m))
        acc = _dot(e_b.astype(BF16), vband) + _dot(e_m.astype(BF16), vm_ref[...])
        o_ref[:, j * 2 * LANES:(j + 1) * 2 * LANES] = _ungroup_outputs(acc / denom, j, T).astype(BF16)


def _na_kernel(q_ref, k_ref, v_ref, km_ref, vm_ref, bias_ref, mbias_ref, *rest, rows):
    o_ref = rest[-1]
    blk = pl.program_id(1)
    W = GRID_W
    nkeys = NA_ROWS * W
    km = km_ref[...]
    vm = vm_ref[...]
    for j in range(N_KV_HEADS):
        mb = _head_rows([mbias_ref[GROUP * j + hh:GROUP * j + hh + 1, :] for hh in range(GROUP)], W)

        def row_body(rr, carry, j=j, mb=mb):
            r = blk * NA_QROWS + rr
            rs = jnp.clip(r - NA_ROWS // 2, 0, rows - NA_ROWS)
            delta = r - rs
            qoff = pl.multiple_of(rr * W, W)
            koff = pl.multiple_of(rs * W, W)
            qj = _group_queries(q_ref[pl.ds(qoff, W), :].astype(F32), j)
            kw = k_ref[pl.ds(koff, nkeys), :]
            vw = v_ref[pl.ds(koff, nkeys), :]
            s_w = _dot_nt(qj, kw) + bias_ref[delta, pl.ds(GROUP * j, GROUP)].reshape(GROUP * W, nkeys)
            s_m = _dot_nt(qj, km) + mb
            m = jnp.maximum(jnp.max(s_w, axis=-1, keepdims=True), jnp.max(s_m, axis=-1, keepdims=True))
            e_w = jnp.exp(s_w - m)
            e_m = jnp.exp(s_m - m)
            denom = jnp.sum(e_w, axis=-1, keepdims=True) + jnp.sum(e_m, axis=-1, keepdims=True)
            acc = _dot(e_w.astype(BF16), vw) + _dot(e_m.astype(BF16), vm)
            o_ref[pl.ds(qoff, W), j * 2 * LANES:(j + 1) * 2 * LANES] = (
                _ungroup_outputs(acc / denom, j, W).astype(BF16))
            return carry

        lax.fori_loop(0, NA_QROWS, row_body, 0, unroll=True)


def _meta_kernel(sink_ref, q_ref, ka_ref, va_ref, kb_ref, vb_ref, km_ref, vm_ref,
                 bq_meta_ref, bq_blk_ref, mbias_ref, o_in_ref, o_ref):
    del o_in_ref
    T = N_META
    q = q_ref[...].astype(F32)
    km_all = km_ref[...]
    vm_all = vm_ref[...]

    def finish(n, j, acc, denom):
        lo = n * MIX_WIDTH + j * 2 * LANES
        o_ref[:, lo:lo + 2 * LANES] = _ungroup_outputs(acc / denom, j, T).astype(BF16)

    for j in range(N_KV_HEADS):
        hs = slice(GROUP * j, GROUP * (j + 1))
        qj = _group_queries(q[:, 0:MIX_WIDTH], j)
        km, vm = km_all[:, 0:KV_WIDTH], vm_all[:, 0:KV_WIDTH]
        s_r = _dot_nt(qj, ka_ref[...])
        s_m = _dot_nt(qj, km)
        m = jnp.maximum(jnp.max(s_r, axis=-1, keepdims=True), jnp.max(s_m, axis=-1, keepdims=True))
        e_r = jnp.exp(s_r - m)
        e_m = jnp.exp(s_m - m)
        denom = jnp.sum(e_r, axis=-1, keepdims=True) + jnp.sum(e_m, axis=-1, keepdims=True)
        finish(0, j, _dot(e_r.astype(BF16), va_ref[...]) + _dot(e_m.astype(BF16), vm), denom)
        qj = _group_queries(q[:, MIX_WIDTH:2 * MIX_WIDTH], j)
        km, vm = km_all[:, KV_WIDTH:2 * KV_WIDTH], vm_all[:, KV_WIDTH:2 * KV_WIDTH]
        s_r = _dot_nt(qj, kb_ref[...]) + bq_blk_ref[hs].reshape(GROUP * T, BLOCK)
        s_m = _dot_nt(qj, km) + bq_meta_ref[hs].reshape(GROUP * T, N_META)
        sink = jnp.concatenate([jnp.full((T, 1), sink_ref[GROUP * j + hh], F32) for hh in range(GROUP)], axis=0)
        m = jnp.maximum(jnp.maximum(jnp.max(s_r, axis=-1, keepdims=True),
                                    jnp.max(s_m, axis=-1, keepdims=True)), sink)
        e_r = jnp.exp(s_r - m)
        e_m = jnp.exp(s_m - m)
        denom = (jnp.sum(e_r, axis=-1, keepdims=True) + jnp.sum(e_m, axis=-1, keepdims=True)
                 + jnp.exp(sink - m))
        finish(1, j, _dot(e_r.astype(BF16), vb_ref[...]) + _dot(e_m.astype(BF16), vm), denom)
        qj = _group_queries(q[:, 2 * MIX_WIDTH:3 * MIX_WIDTH], j)
        km, vm = km_all[:, 2 * KV_WIDTH:3 * KV_WIDTH], vm_all[:, 2 * KV_WIDTH:3 * KV_WIDTH]
        mb = _head_rows([mbias_ref[GROUP * j + hh:GROUP * j + hh + 1, :] for hh in range(GROUP)], T)
        s_m = _dot_nt(qj, km) + mb
        m = jnp.max(s_m, axis=-1, keepdims=True)
        e_m = jnp.exp(s_m - m)
        finish(2, j, _dot(e_m.astype(BF16), vm), jnp.sum(e_m, axis=-1, keepdims=True))


class _Group:
    def __init__(self, B, S, real_base, meta_batch_base, meta_base, n_meta_blocks):
        self.B, self.S = B, S
        self.real_base = real_base
        self.meta_blk0 = meta_base // N_META + meta_batch_base
        self.n_meta_blocks = n_meta_blocks
        assert real_base % S == 0 and S % TM == 0 and meta_base % N_META == 0


def _alias_args(o_prev, n_inputs):
    if o_prev is None:
        return [], [], {}
    return [o_prev], [pl.BlockSpec(memory_space=pl.ANY)], {n_inputs: 0}


def _global_attn(grp, q_all, k_all, v_all, o_prev):
    B, S = grp.B, grp.S
    nq = S // TQ_GLOBAL
    qb0 = grp.real_base // TQ_GLOBAL
    sb0 = grp.real_base // S
    mb0 = grp.meta_blk0
    in_specs = [pl.BlockSpec((TQ_GLOBAL, MIX_WIDTH), lambda b, i: (qb0 + b * nq + i, 0)),
                pl.BlockSpec((S, KV_WIDTH), lambda b, i: (sb0 + b, 0)),
                pl.BlockSpec((S, KV_WIDTH), lambda b, i: (sb0 + b, 0)),
                pl.BlockSpec((N_META, KV_WIDTH), lambda b, i: (mb0 + b, 0)),
                pl.BlockSpec((N_META, KV_WIDTH), lambda b, i: (mb0 + b, 0))]
    extra, extra_specs, aliases = _alias_args(o_prev, len(in_specs))
    return pl.pallas_call(
        functools.partial(_global_kernel, S=S),
        grid=(B, nq),
        in_specs=in_specs + extra_specs,
        out_specs=pl.BlockSpec((TQ_GLOBAL, MIX_WIDTH), lambda b, i: (qb0 + b * nq + i, 0)),
        out_shape=jax.ShapeDtypeStruct((q_all.shape[0], N_MIXERS * MIX_WIDTH), BF16),
        input_output_aliases=aliases,
        compiler_params=_cparams(("parallel", "arbitrary")),
        name="mixer_global",
    )(q_all, k_all, v_all, k_all, v_all, *extra)


def _window_attn(grp, q_all, k_all, v_all, sink, bband, bmeta, o_prev):
    B, S = grp.B, grp.S
    nb = S // BLOCK
    qb0 = grp.real_base // BLOCK
    mb0 = grp.meta_blk0
    cur = lambda b, i, sink: (qb0 + b * nb + i, 1)
    prv = lambda b, i, sink: (qb0 + b * nb + jnp.maximum(i - 1, 0), 1)
    nxt = lambda b, i, sink: (qb0 + b * nb + jnp.minimum(i + 1, nb - 1), 1)
    met = lambda b, i, sink: (mb0 + b, 1)
    kv = lambda im: pl.BlockSpec((BLOCK, KV_WIDTH), im)
    in_specs = [pl.BlockSpec((BLOCK, MIX_WIDTH), cur),
                kv(prv), kv(cur), kv(nxt), kv(prv), kv(cur), kv(nxt),
                pl.BlockSpec((N_META, KV_WIDTH), met), pl.BlockSpec((N_META, KV_WIDTH), met),
                pl.BlockSpec((N_HEADS, BLOCK, 3 * BLOCK), lambda b, i, sink: (0, 0, 0)),
                pl.BlockSpec((N_HEADS, BLOCK, N_META), lambda b, i, sink: (0, i, 0))]
    extra, extra_specs, aliases = _alias_args(o_prev, len(in_specs) + 1)
    return pl.pallas_call(
        functools.partial(_window_kernel, nb=nb),
        grid_spec=pltpu.PrefetchScalarGridSpec(
            num_scalar_prefetch=1,
            grid=(B, nb),
            in_specs=in_specs + extra_specs,
            out_specs=pl.BlockSpec((BLOCK, MIX_WIDTH), cur)),
        out_shape=jax.ShapeDtypeStruct((q_all.shape[0], N_MIXERS * MIX_WIDTH), BF16),
        input_output_aliases=aliases,
        compiler_params=_cparams(("parallel", "arbitrary")),
        name="mixer_window",
    )(sink, q_all, k_all, k_all, k_all, v_all, v_all, v_all, k_all, v_all, bband, bmeta, *extra)


def _na_attn(grp, q_all, k_all, v_all, na_bias, na_mbias, o_prev):
    B, S = grp.B, grp.S
    rows = S // GRID_W
    tq = NA_QROWS * GRID_W
    nq = S // tq
    qb0 = grp.real_base // tq
    sb0 = grp.real_base // S
    mb0 = grp.meta_blk0
    in_specs = [pl.BlockSpec((tq, MIX_WIDTH), lambda b, i: (qb0 + b * nq + i, 2)),
                pl.BlockSpec((S, KV_WIDTH), lambda b, i: (sb0 + b, 2)),
                pl.BlockSpec((S, KV_WIDTH), lambda b, i: (sb0 + b, 2)),
                pl.BlockSpec((N_META, KV_WIDTH), lambda b, i: (mb0 + b, 2)),
                pl.BlockSpec((N_META, KV_WIDTH), lambda b, i: (mb0 + b, 2)),
                pl.BlockSpec(na_bias.shape, lambda b, i: (0, 0, 0, 0)),
                pl.BlockSpec(na_mbias.shape, lambda b, i: (0, 0))]
    extra, extra_specs, aliases = _alias_args(o_prev, len(in_specs))
    return pl.pallas_call(
        functools.partial(_na_kernel, rows=rows),
        grid=(B, nq),
        in_specs=in_specs + extra_specs,
        out_specs=pl.BlockSpec((tq, MIX_WIDTH), lambda b, i: (qb0 + b * nq + i, 2)),
        out_shape=jax.ShapeDtypeStruct((q_all.shape[0], N_MIXERS * MIX_WIDTH), BF16),
        input_output_aliases=aliases,
        compiler_params=_cparams(("parallel", "arbitrary")),
        name="mixer_neighbourhood",
    )(q_all, k_all, v_all, k_all, v_all, na_bias, na_mbias, *extra)


def _meta_attn(grp, q_all, k_all, v_all, sink, bq_meta, bq_blk, na_mbias, o_prev):
    B, S = grp.B, grp.S
    sb0 = grp.real_base // S
    bb0 = grp.real_base // BLOCK
    nb = S // BLOCK
    mb0 = grp.meta_blk0
    clamp = lambda b: jnp.minimum(b, B - 1)
    mrow = lambda b, sink: (mb0 + clamp(b), 0)
    in_specs = [pl.BlockSpec((N_META, N_MIXERS * MIX_WIDTH), mrow),
                pl.BlockSpec((S, KV_WIDTH), lambda b, sink: (sb0 + clamp(b), 0)),
                pl.BlockSpec((S, KV_WIDTH), lambda b, sink: (sb0 + clamp(b), 0)),
                pl.BlockSpec((BLOCK, KV_WIDTH), lambda b, sink: (bb0 + clamp(b) * nb, 1)),
                pl.BlockSpec((BLOCK, KV_WIDTH), lambda b, sink: (bb0 + clamp(b) * nb, 1)),
                pl.BlockSpec((N_META, N_MIXERS * KV_WIDTH), mrow),
                pl.BlockSpec((N_META, N_MIXERS * KV_WIDTH), mrow),
                pl.BlockSpec(bq_meta.shape, lambda b, sink: (0, 0, 0)),
                pl.BlockSpec(bq_blk.shape, lambda b, sink: (0, 0, 0)),
                pl.BlockSpec(na_mbias.shape, lambda b, sink: (0, 0)),
                pl.BlockSpec(memory_space=pl.ANY)]
    return pl.pallas_call(
        _meta_kernel,
        grid_spec=pltpu.PrefetchScalarGridSpec(
            num_scalar_prefetch=1,
            grid=(grp.n_meta_blocks,),
            in_specs=in_specs,
            out_specs=pl.BlockSpec((N_META, N_MIXERS * MIX_WIDTH), lambda b, sink: (mb0 + b, 0))),
        out_shape=jax.ShapeDtypeStruct((q_all.shape[0], N_MIXERS * MIX_WIDTH), BF16),
        input_output_aliases={len(in_specs): 0},
        compiler_params=_cparams(("arbitrary",)),
        name="mixer_meta_queries",
    )(sink, q_all, k_all, v_all, k_all, v_all, k_all, v_all, bq_meta, bq_blk, na_mbias, o_prev)


def _route(h1, wr_hi, wr_lo, rbias):
    T = h1.shape[0]
    x_hi, x_lo = _split_bf16(h1)
    logits = _dot_nt(wr_hi, x_hi) + _dot_nt(wr_hi, x_lo) + _dot_nt(wr_lo, x_hi)
    scores = 1.0 / (1.0 + jnp.exp(-logits))
    sel = scores + rbias
    per_group = N_EXPERTS // N_EXPERT_GROUPS
    sel3 = sel.reshape(N_EXPERT_GROUPS, per_group, T)
    idx3 = lax.broadcasted_iota(jnp.int32, sel3.shape, 1).astype(F32)
    m1 = jnp.max(sel3, axis=1, keepdims=True)
    first = jnp.min(jnp.where(sel3 == m1, idx3, float(per_group)), axis=1, keepdims=True)
    m2 = jnp.max(jnp.where(idx3 == first, -jnp.inf, sel3), axis=1, keepdims=True)
    gscore = (m1 + m2).reshape(N_EXPERT_GROUPS, T)

    def rank_of(vals):
        idx = lax.broadcasted_iota(jnp.int32, vals.shape, 0)
        rank = jnp.zeros(vals.shape, F32)
        for r in range(vals.shape[0]):
            row = vals[r:r + 1, :]
            ge = jnp.where(row >= vals, 1.0, 0.0)
            gt = jnp.where(row > vals, 1.0, 0.0)
            rank = rank + jnp.where(idx > r, ge, gt)
        return rank

    gkeep = jnp.where(rank_of(gscore) < TOPK_GROUPS, 1.0, 0.0)
    ekeep = jnp.broadcast_to(gkeep.reshape(N_EXPERT_GROUPS, 1, T), sel3.shape).reshape(N_EXPERTS, T)
    masked = jnp.where(ekeep > 0.5, sel, NEG_INF)
    w = jnp.where(rank_of(masked) < TOP_K, scores, 0.0)
    return w / jnp.sum(w, axis=0, keepdims=True) * ROUTED_SCALE


def _merge_kernel(h_ref, o_ref, wg_ref, wb_ref, wo_ref, g_ref, b_ref, wrh_ref, wrl_ref, rb_ref,
                  h1_ref, gates_ref, *, alpha):
    h = h_ref[...]
    x = h.astype(BF16)
    merged = None
    for n in range(N_MIXERS):
        logit = _dot(x, wg_ref[:, n * D_MODEL:(n + 1) * D_MODEL])
        branch = _dot(o_ref[:, n * MIX_WIDTH:(n + 1) * MIX_WIDTH], wb_ref[n])
        term = branch / (1.0 + jnp.exp(-logit))
        merged = term if merged is None else merged + term
    mix = _dot(merged.astype(BF16), wo_ref[...])
    h1 = _layer_norm(alpha * h + mix, g_ref[...], b_ref[...])
    h1_ref[...] = h1
    gates_t = _route(h1, wrh_ref[...], wrl_ref[...], rb_ref[...])
    pad = jnp.zeros((LANES - N_EXPERTS, gates_t.shape[1]), F32)
    gates_ref[...] = jnp.concatenate([gates_t, pad], axis=0).T


def _merge(h, o_all, w_gate, w_branch, w_out, ln_g, ln_b, wr_hi, wr_lo, rbias, alpha):
    R = h.shape[0]
    c2 = lambda i: (0, 0)
    return pl.pallas_call(
        functools.partial(_merge_kernel, alpha=alpha),
        grid=(R // TM,),
        in_specs=[pl.BlockSpec((TM, D_MODEL), lambda i: (i, 0)),
                  pl.BlockSpec((TM, N_MIXERS * MIX_WIDTH), lambda i: (i, 0)),
                  pl.BlockSpec(w_gate.shape, c2),
                  pl.BlockSpec(w_branch.shape, lambda i: (0, 0, 0)),
                  pl.BlockSpec(w_out.shape, c2),
                  pl.BlockSpec((1, D_MODEL), c2),
                  pl.BlockSpec((1, D_MODEL), c2),
                  pl.BlockSpec(wr_hi.shape, c2),
                  pl.BlockSpec(wr_lo.shape, c2),
                  pl.BlockSpec(rbias.shape, c2)],
        out_specs=[pl.BlockSpec((TM, D_MODEL), lambda i: (i, 0)),
                   pl.BlockSpec((TM, LANES), lambda i: (i, 0))],
        out_shape=[jax.ShapeDtypeStruct((R, D_MODEL), F32),
                   jax.ShapeDtypeStruct((R, LANES), F32)],
        compiler_params=_cparams(("parallel",)),
        name="merge_ln_route",
    )(h, o_all, w_gate, w_branch, w_out, ln_g, ln_b, wr_hi, wr_lo, rbias)


TD = 256
CH = 16
SLOTS = 3072
NCH = SLOTS // CH
MT = 512
CPM = MT // CH


def _swiglu_act(gu):
    g = gu[:, :D_EXPERT]
    return g / (1.0 + jnp.exp(-g)) * gu[:, D_EXPERT:]


def _slot_of_token(gates, lo_row):
    routed = gates > 0.0
    r = lax.broadcasted_iota(jnp.int32, (TD, TD), 0)
    c = lax.broadcasted_iota(jnp.int32, (TD, TD), 1)
    earlier = jnp.where(c < r, 1.0, 0.0).astype(BF16)
    rank = _dot(earlier, jnp.where(routed, 1.0, 0.0).astype(BF16))
    return jnp.where(routed, lo_row + rank + 1.0, 0.0)


def _split64(x):
    a = jnp.floor(x * (1.0 / 64.0))
    return a.astype(BF16), (x - 64.0 * a).astype(BF16)


def _dispatch_kernel(h_ref, g_ref, lohi_ref, x_ref, w_ref):
    lohi = lohi_ref[0]
    lo_row, hi_row = lohi[0:1], lohi[1:2]
    gates = g_ref[...]
    a, b = _split64(_slot_of_token(gates, lo_row).T)
    gates_t = gates.T.astype(BF16)
    x = h_ref[...].astype(BF16)
    for blk in range(SLOTS // MT):
        s = (lax.broadcasted_iota(jnp.int32, (MT, LANES), 0) + blk * MT).astype(F32)
        owner = jnp.where(s >= lo_row, jnp.where(s < hi_row, 1.0, 0.0), 0.0).astype(BF16)
        want = 64.0 * _dot(owner, a) + _dot(owner, b)
        s1 = (lax.broadcasted_iota(jnp.int32, (MT, TD), 0) + (blk * MT + 1)).astype(F32)
        hit = want == s1
        x_ref[blk * MT:(blk + 1) * MT, :] = _dot(jnp.where(hit, 1.0, 0.0).astype(BF16), x).astype(BF16)
        weight = jnp.where(hit, _dot(owner, gates_t), 0.0)
        w_ref[:, blk * MT:(blk + 1) * MT] = weight.T.astype(BF16)


def _dispatch(h1, gates, lohi):
    n = h1.shape[0] // TD
    return pl.pallas_call(
        _dispatch_kernel,
        grid=(n,),
        in_specs=[pl.BlockSpec((TD, D_MODEL), lambda i: (i, 0)),
                  pl.BlockSpec((TD, LANES), lambda i: (i, 0)),
                  pl.BlockSpec((1, 8, LANES), lambda i: (i, 0, 0))],
        out_specs=[pl.BlockSpec((SLOTS, D_MODEL), lambda i: (i, 0)),
                   pl.BlockSpec((TD, SLOTS), lambda i: (i, 0))],
        out_shape=[jax.ShapeDtypeStruct((n * SLOTS, D_MODEL), BF16),
                   jax.ShapeDtypeStruct((n * TD, SLOTS), BF16)],
        compiler_params=_cparams(("parallel",)),
        name="moe_dispatch",
    )(h1, gates, lohi)


def _chunk_gather(table_ref, first, n_chunks, src_hbm, buf, sem):
    copies = []
    for c in range(n_chunks):
        row = pl.multiple_of(table_ref[first + c] * CH, CH)
        copies.append(pltpu.make_async_copy(src_hbm.at[pl.ds(row, CH)], buf.at[pl.ds(c * CH, CH)], sem))
    return copies


def _chunk_wait(n_chunks, src_hbm, buf, sem):
    for c in range(n_chunks):
        pltpu.make_async_copy(src_hbm.at[pl.ds(0, CH)], buf.at[pl.ds(c * CH, CH)], sem).wait()


def _expert_kernel(te_ref, src_ref, nu_ref, x_hbm, wgu_ref, wd_ref, y_ref, xbuf, sem):
    del te_ref
    m = pl.program_id(0)
    n_used = nu_ref[0]

    def start(step):
        slot = step % 2
        for cp in _chunk_gather(src_ref, step * CPM, CPM, x_hbm, xbuf.at[slot], sem.at[slot]):
            cp.start()

    @pl.when(m == 0)
    def _():
        start(m)

    @pl.when(m + 1 < n_used)
    def _():
        start(m + 1)

    @pl.when(m < n_used)
    def _():
        slot = m % 2
        _chunk_wait(CPM, x_hbm, xbuf.at[slot], sem.at[slot])
        act = _swiglu_act(_dot(xbuf[slot], wgu_ref[0]))
        y_ref[...] = _dot(act.astype(BF16), wd_ref[0]).astype(BF16)


def _experts(x_disp, wgu, wd, tile_expert, src_chunk, n_used):
    n_steps = tile_expert.shape[0]
    return pl.pallas_call(
        _expert_kernel,
        grid_spec=pltpu.PrefetchScalarGridSpec(
            num_scalar_prefetch=3,
            grid=(n_steps,),
            in_specs=[pl.BlockSpec(memory_space=pl.ANY),
                      pl.BlockSpec((1, D_MODEL, 2 * D_EXPERT), lambda m, te, src, nu: (te[m], 0, 0)),
                      pl.BlockSpec((1, D_EXPERT, D_MODEL), lambda m, te, src, nu: (te[m], 0, 0))],
            out_specs=pl.BlockSpec((MT, D_MODEL), lambda m, te, src, nu: (jnp.minimum(m, nu[0] - 1), 0)),
            scratch_shapes=[pltpu.VMEM((2, MT, D_MODEL), BF16), pltpu.SemaphoreType.DMA((2,))]),
        out_shape=jax.ShapeDtypeStruct((n_steps * MT, D_MODEL), BF16),
        compiler_params=_cparams(("arbitrary",)),
        name="moe_experts",
    )(tile_expert, src_chunk, n_used, x_disp, wgu, wd)


def _combine_kernel(dst_ref, y_hbm, w_ref, h_ref, wsgu_ref, wsd_ref, lg_ref, lb_ref, o_ref, ybuf, sem, *, alpha):
    i = pl.program_id(0)

    def start(tile):
        slot = tile % 2
        for cp in _chunk_gather(dst_ref, tile * NCH, NCH, y_hbm, ybuf.at[slot], sem.at[slot]):
            cp.start()

    @pl.when(i == 0)
    def _():
        start(i)

    @pl.when(i + 1 < pl.num_programs(0))
    def _():
        start(i + 1)

    h = h_ref[...]
    shared = _dot(_swiglu_act(_dot(h.astype(BF16), wsgu_ref[...])).astype(BF16), wsd_ref[...])
    slot = i % 2
    _chunk_wait(NCH, y_hbm, ybuf.at[slot], sem.at[slot])
    routed = _dot(w_ref[...], ybuf[slot])
    o_ref[...] = _layer_norm(alpha * h + shared + routed, lg_ref[...], lb_ref[...])


def _combine(y_sorted, w_t, h1, dst_chunk, wsgu, wsd, ln_g, ln_b, alpha):
    n = h1.shape[0] // TD
    c2 = lambda i, dst: (0, 0)
    return pl.pallas_call(
        functools.partial(_combine_kernel, alpha=alpha),
        grid_spec=pltpu.PrefetchScalarGridSpec(
            num_scalar_prefetch=1,
            grid=(n,),
            in_specs=[pl.BlockSpec(memory_space=pl.ANY),
                      pl.BlockSpec((TD, SLOTS), lambda i, dst: (i, 0)),
                      pl.BlockSpec((TD, D_MODEL), lambda i, dst: (i, 0)),
                      pl.BlockSpec(wsgu.shape, c2),
                      pl.BlockSpec(wsd.shape, c2),
                      pl.BlockSpec((1, D_MODEL), c2),
                      pl.BlockSpec((1, D_MODEL), c2)],
            out_specs=pl.BlockSpec((TD, D_MODEL), lambda i, dst: (i, 0)),
            scratch_shapes=[pltpu.VMEM((2, SLOTS, D_MODEL), BF16), pltpu.SemaphoreType.DMA((2,))]),
        out_shape=jax.ShapeDtypeStruct((h1.shape[0], D_MODEL), F32),
        compiler_params=_cparams(("arbitrary",)),
        name="moe_combine_ln",
    )(dst_chunk, y_sorted, w_t, h1, wsgu, wsd, ln_g, ln_b)


def _routing_tables(gates):
    n = gates.shape[0] // TD
    cnt = jnp.sum((gates[:, :N_EXPERTS] > 0.0).reshape(n, TD, N_EXPERTS), axis=1, dtype=jnp.int32)
    nch = (cnt + (CH - 1)) // CH
    hi16 = jnp.cumsum(nch, axis=1)
    lo16 = hi16 - nch
    nct = hi16[:, -1:]
    pad = jnp.broadcast_to(nct, (n, LANES - N_EXPERTS))
    lohi = jnp.zeros((n, 8, LANES), F32)
    lohi = lohi.at[:, 0, :].set((jnp.concatenate([lo16, pad], axis=1) * CH).astype(F32))
    lohi = lohi.at[:, 1, :].set((jnp.concatenate([hi16, pad], axis=1) * CH).astype(F32))
    tot = jnp.sum(nch, axis=0)
    seg_len = (tot + (CPM - 1)) // CPM * CPM
    seg_end = jnp.cumsum(seg_len)
    seg_start = seg_end - seg_len
    gpos = seg_start[None, :] + jnp.cumsum(nch, axis=0) - nch
    n_steps = (n * NCH + N_EXPERTS * CPM) // CPM
    n_used = (seg_end[-1] // CPM).astype(jnp.int32).reshape(1)
    step = jnp.arange(n_steps, dtype=jnp.int32)
    tile_expert = jnp.sum(seg_end[None, :] // CPM <= jnp.minimum(step, n_used - 1)[:, None], axis=1, dtype=jnp.int32)
    tile_expert = jnp.minimum(tile_expert, N_EXPERTS - 1)
    exact = functools.partial(jnp.dot, precision=lax.Precision.HIGHEST)
    experts = jnp.arange(N_EXPERTS, dtype=jnp.int32)
    g = jnp.arange(n_steps * CPM, dtype=jnp.int32)
    e_of_g = jnp.minimum(jnp.sum(seg_end[None, :] <= g[:, None], axis=1, dtype=jnp.int32), N_EXPERTS - 1)
    pick_e = (e_of_g[:, None] == experts[None, :]).astype(F32)
    first = exact(pick_e, gpos.T.astype(F32))
    count = exact(pick_e, nch.T.astype(F32))
    base = exact(pick_e, (jnp.arange(n, dtype=jnp.int32)[:, None] * NCH + lo16).T.astype(F32))
    gf = g.astype(F32)[:, None]
    inside = (first <= gf) & (gf < first + count)
    src_chunk = jnp.sum(jnp.where(inside, base + gf - first, 0.0), axis=1).astype(jnp.int32)
    k = jnp.arange(NCH, dtype=jnp.int32)
    e_of_k = jnp.minimum(jnp.sum(hi16[:, None, :] <= k[None, :, None], axis=2, dtype=jnp.int32), N_EXPERTS - 1)
    pick_k = e_of_k[:, :, None] == experts[None, None, :]
    pos = jnp.sum(jnp.where(pick_k, (gpos - lo16)[:, None, :], 0), axis=2) + k[None, :]
    dst_chunk = jnp.where(k[None, :] < nct, pos, 0).astype(jnp.int32).reshape(-1)
    return lohi, tile_expert, src_chunk, n_used, dst_chunk


def _moe(h1, gates, wgu, wd, wsgu, wsd, ln_g, ln_b, alpha):
    lohi, tile_expert, src_chunk, n_used, dst_chunk = _routing_tables(gates)
    x_disp, w_t = _dispatch(h1, gates, lohi)
    y_sorted = _experts(x_disp, wgu, wd, tile_expert, src_chunk, n_used)
    return _combine(y_sorted, w_t, h1, dst_chunk, wsgu, wsd, ln_g, ln_b, alpha)


def _t5_bucket(rel):
    half = T5_BUCKETS // 2
    max_exact = half // 2
    n = np.abs(rel)
    ratio = np.log(np.maximum(n, 1).astype(np.float32) / np.float32(max_exact))
    ratio = ratio / np.float32(math.log(T5_MAX_DIST / max_exact)) * np.float32(half - max_exact)
    large = np.minimum(max_exact + ratio.astype(np.int32), half - 1)
    return np.where(rel > 0, half, 0) + np.where(n < max_exact, n, large)


def _t5_tables(t5_table, s_max):
    def bias(rel, valid):
        b = t5_table[_t5_bucket(rel)].astype(F32)
        return jnp.where(jnp.asarray(valid)[None], jnp.transpose(b, (2, 0, 1)), NEG_INF)

    ii = np.arange(BLOCK)[:, None]
    jj = np.arange(3 * BLOCK)[None, :]
    rel = jj - ii - BLOCK
    bband = bias(rel, np.abs(rel) <= WINDOW)
    t = np.arange(s_max)[:, None]
    m = np.arange(N_META)[None, :]
    bmeta = bias(m - (N_META + t), np.ones((s_max, N_META), bool))
    mpos = np.arange(N_META)[:, None]
    kpos = np.arange(N_META + BLOCK)[None, :]
    relq = kpos - mpos
    bq = bias(relq, (kpos < N_META) | (np.abs(relq) <= WINDOW))
    return bband, bmeta, bq[:, :, :N_META], bq[:, :, N_META:]


def _na_bias_cases(rpb):
    W = GRID_W
    delta = np.arange(NA_ROWS)[:, None, None, None]
    i = np.arange(NA_ROWS)[None, :, None, None]
    c = np.arange(W)[None, None, :, None]
    kc = np.arange(W)[None, None, None, :]
    cs = np.clip(c - NA_COLS // 2, 0, W - NA_COLS)
    valid = (kc >= cs) & (kc < cs + NA_COLS)
    dc = np.clip(kc - c + (NA_COLS - 1), 0, 2 * NA_COLS - 2)[0, 0]
    t = jnp.where(jnp.asarray(valid[0, 0]), rpb.astype(F32)[:, :, dc], NEG_INF)
    cases = [jnp.transpose(t[:, NA_ROWS - 1 - d:2 * NA_ROWS - 1 - d], (0, 2, 1, 3)) for d in range(NA_ROWS)]
    return jnp.stack(cases, axis=0).reshape(NA_ROWS, N_HEADS, W, NA_ROWS * W)


def _rope_tables(s_max, n_meta_rows):
    half = HEAD_DIM // 4
    freq = ROPE_THETA ** (-jnp.arange(half, dtype=F32) / half)
    t = np.arange(s_max)
    mp = np.tile(np.arange(N_META) - N_META, n_meta_rows // N_META)
    pos_row = jnp.asarray(np.concatenate([t // GRID_W, mp]), jnp.int32).astype(F32)
    pos_col = jnp.asarray(np.concatenate([t % GRID_W, mp]), jnp.int32).astype(F32)
    ar = pos_row[:, None] * freq
    ac = pos_col[:, None] * freq
    cos = jnp.concatenate([jnp.cos(ar), jnp.cos(ar), jnp.cos(ac), jnp.cos(ac)], axis=1)
    sin = jnp.concatenate([-jnp.sin(ar), jnp.sin(ar), -jnp.sin(ac), jnp.sin(ac)], axis=1)
    return jnp.tile(cos, (1, 2)), jnp.tile(sin, (1, 2))


def kernel(x_prompt, x_sample, meta_tokens, ln_in_g, ln_in_b, t5_table, w_in, q_gain, k_gain, sink,
           na_rpb, na_meta_bias, w_branch, w_out, ln1_g, ln1_b, w_router, router_bias,
           w_expert_gate_up, w_expert_down, w_shared_gate_up, w_shared_down, ln2_g, ln2_b):
    depth = w_in.shape[0]
    alpha = (2 * depth) ** 0.25
    B0, S0, D = x_prompt.shape
    B1, S1, _ = x_sample.shape
    assert D == D_MODEL
    real = B0 * S0 + B1 * S1
    n_meta_rows = -(-(B0 + B1) * N_META // TM) * TM
    R = real + n_meta_rows
    n_meta_blocks = n_meta_rows // N_META
    g0 = _Group(B0, S0, 0, 0, real, B0)
    g1 = _Group(B1, S1, B0 * S0, B0, real, n_meta_blocks - B0)
    s_max = max(S0, S1)

    x = jnp.concatenate([x_prompt.reshape(B0 * S0, D), x_sample.reshape(B1 * S1, D),
                         jnp.tile(meta_tokens, (n_meta_blocks, 1))], axis=0)
    h = _embed_ln(x, ln_in_g.reshape(1, D), ln_in_b.reshape(1, D))

    cos_tab, sin_tab = _rope_tables(s_max, TM)
    n0, n1 = B0 * S0 // TM, real // TM
    p0, p1, pm = S0 // TM, S1 // TM, s_max // TM

    def pos_block(i):
        return jnp.where(i < n0, i % p0, jnp.where(i < n1, (i - n0) % p1, pm))

    bband, bmeta, bq_meta, bq_blk = _t5_tables(t5_table, s_max)
    ones_bd = jnp.asarray(np.kron(np.eye(N_HEADS), np.ones((HEAD_DIM, HEAD_DIM))), BF16)

    qs, ks, vs = [], [], []
    for n in range(N_MIXERS):
        off = n * QKV_WIDTH
        qs.append(w_in[:, :, off:off + MIX_WIDTH])
        ks.append(w_in[:, :, off + MIX_WIDTH:off + MIX_WIDTH + KV_WIDTH])
        vs.append(w_in[:, :, off + MIX_WIDTH + KV_WIDTH:off + QKV_WIDTH])
    wr_t = jnp.swapaxes(w_router, 1, 2)
    wr_hi = wr_t.astype(BF16)
    layers = dict(
        w_qkv=jnp.concatenate(qs + ks + vs, axis=2).astype(BF16),
        w_gate=w_in[:, :, N_MIXERS * QKV_WIDTH:].astype(BF16),
        q_gain=jnp.tile(q_gain, (1, N_HEADS)).reshape(depth, 1, MIX_WIDTH),
        k_gain=jnp.tile(k_gain, (1, N_KV_HEADS)).reshape(depth, 1, KV_WIDTH),
        sink=sink.astype(F32),
        na_bias=jax.vmap(_na_bias_cases)(na_rpb),
        na_mbias=na_meta_bias.astype(F32),
        w_branch=w_branch.astype(BF16),
        w_out=w_out.astype(BF16),
        ln1_g=ln1_g.reshape(depth, 1, D), ln1_b=ln1_b.reshape(depth, 1, D),
        wr_hi=wr_hi, wr_lo=(wr_t - wr_hi.astype(F32)).astype(BF16),
        rbias=router_bias.astype(F32).reshape(depth, N_EXPERTS, 1),
        wgu=w_expert_gate_up.astype(BF16), wd=w_expert_down.astype(BF16),
        wsgu=w_shared_gate_up.astype(BF16), wsd=w_shared_down.astype(BF16),
        ln2_g=ln2_g.reshape(depth, 1, D), ln2_b=ln2_b.reshape(depth, 1, D),
    )

    def layer(h, p):
        q_all, k_all, v_all = _inproj(h, p["w_qkv"], cos_tab, sin_tab, p["q_gain"], p["k_gain"],
                                      ones_bd, pos_block)
        o = None
        for grp in (g0, g1):
            o = _global_attn(grp, q_all, k_all, v_all, o)
            o = _window_attn(grp, q_all, k_all, v_all, p["sink"], bband, bmeta[:, :grp.S], o)
            o = _na_attn(grp, q_all, k_all, v_all, p["na_bias"], p["na_mbias"], o)
        for grp in (g0, g1):
            o = _meta_attn(grp, q_all, k_all, v_all, p["sink"], bq_meta, bq_blk, p["na_mbias"], o)
        h1, gates = _merge(h, o, p["w_gate"], p["w_branch"], p["w_out"], p["ln1_g"], p["ln1_b"],
                           p["wr_hi"], p["wr_lo"], p["rbias"], alpha)
        h2 = _moe(h1, gates, p["wgu"], p["wd"], p["wsgu"], p["wsd"], p["ln2_g"], p["ln2_b"],
                  alpha)
        return h2, None

    h, _ = lax.scan(layer, h, layers)
    y_prompt = h[:B0 * S0].reshape(B0, S0, D)
    y_sample = h[B0 * S0:real].reshape(B1, S1, D)
    return (y_prompt, y_sample)
```

```python
import functools
import math

import numpy as np
import jax
import jax.numpy as jnp
from jax import lax
from jax.experimental import pallas as pl
from jax.experimental.pallas import tpu as pltpu

F32 = jnp.float32
BF16 = jnp.bfloat16

D_MODEL = 1024
HEAD_DIM = 64
N_HEADS = 8
N_KV_HEADS = 2
GROUP = N_HEADS // N_KV_HEADS
MIX_WIDTH = N_HEADS * HEAD_DIM
KV_WIDTH = N_KV_HEADS * HEAD_DIM
N_MIXERS = 3
QKV_WIDTH = MIX_WIDTH + 2 * KV_WIDTH
N_META = 16
GRID_W = 64
BLOCK = 128
WINDOW = 128
NA_ROWS = 8
NA_COLS = 16
T5_BUCKETS = 32
T5_MAX_DIST = 128
ROPE_THETA = 10000.0
N_EXPERTS = 64
TOP_K = 8
N_EXPERT_GROUPS = 8
TOPK_GROUPS = 4
D_EXPERT = 256
ROUTED_SCALE = 2.5
NEG_INF = -1e30
LOG2E = math.log2(math.e)
LANES = 128

TM = 512
TQ_GLOBAL = 128
TK_GLOBAL = 512
NA_QROWS = 8
VMEM_LIMIT = 56 * 1024 * 1024


def _cparams(sem):
    return pltpu.CompilerParams(dimension_semantics=sem, vmem_limit_bytes=VMEM_LIMIT)


def _dot(a, b):
    return jnp.dot(a, b, preferred_element_type=F32)


def _dot_nt(a, b):
    return lax.dot_general(a, b, (((1,), (1,)), ((), ())), preferred_element_type=F32)


def _split_bf16(x):
    hi = x.astype(BF16)
    lo = (x - hi.astype(F32)).astype(BF16)
    return hi, lo


def _layer_norm(x, g, b):
    mu = jnp.mean(x, axis=-1, keepdims=True)
    xc = x - mu
    var = jnp.mean(xc * xc, axis=-1, keepdims=True)
    return xc * lax.rsqrt(var + 1e-5) * g + b


def _embed_ln_kernel(x_ref, g_ref, b_ref, o_ref):
    o_ref[...] = _layer_norm(x_ref[...], g_ref[...], b_ref[...])


def _embed_ln(x, g, b):
    R = x.shape[0]
    return pl.pallas_call(
        _embed_ln_kernel,
        grid=(R // TM,),
        in_specs=[pl.BlockSpec((TM, D_MODEL), lambda i: (i, 0)),
                  pl.BlockSpec((1, D_MODEL), lambda i: (0, 0)),
                  pl.BlockSpec((1, D_MODEL), lambda i: (0, 0))],
        out_specs=pl.BlockSpec((TM, D_MODEL), lambda i: (i, 0)),
        out_shape=jax.ShapeDtypeStruct((R, D_MODEL), F32),
        compiler_params=_cparams(("parallel",)),
        name="embed_ln",
    )(x, g, b)


def _rope_slot(x, cos, sin_signed, first_half):
    fwd = pltpu.roll(x, LANES - 16, 1)
    bwd = pltpu.roll(x, 16, 1)
    return x * cos + jnp.where(first_half, fwd, bwd) * sin_signed


def _head_rms(x, ones_bd, gain):
    hi, lo = _split_bf16(x * x)
    ss = _dot(hi, ones_bd) + _dot(lo, ones_bd)
    return x * lax.rsqrt(ss * (1.0 / HEAD_DIM) + 1e-6) * gain


def _inproj_kernel(h_ref, w_ref, cos_ref, sin_ref, qg_ref, kg_ref, ones_ref, q_ref, k_ref, v_ref, v1_ref):
    x = h_ref[...].astype(BF16)
    cos = cos_ref[...]
    sin = sin_ref[...]
    lane = lax.broadcasted_iota(jnp.int32, cos.shape, 1)
    first_half = (lane % 32) < 16
    scale = HEAD_DIM ** -0.5
    qw = N_MIXERS * MIX_WIDTH
    qa = _head_rms(_dot(x, w_ref[:, 0:MIX_WIDTH]), ones_ref[...], qg_ref[...])
    for s in range(MIX_WIDTH // LANES):
        sl = slice(s * LANES, (s + 1) * LANES)
        q_ref[:, sl] = (_rope_slot(qa[:, sl], cos, sin, first_half) * (scale * LOG2E)).astype(BF16)
    ka = _head_rms(_dot(x, w_ref[:, qw:qw + KV_WIDTH]), ones_ref[0:LANES, 0:LANES], kg_ref[...])
    k_ref[:, 0:KV_WIDTH] = _rope_slot(ka, cos, sin, first_half).astype(BF16)
    for n in range(1, N_MIXERS):
        q_ref[:, n * MIX_WIDTH:(n + 1) * MIX_WIDTH] = (
            _dot(x, w_ref[:, n * MIX_WIDTH:(n + 1) * MIX_WIDTH]) * scale).astype(BF16)
        k_ref[:, n * KV_WIDTH:(n + 1) * KV_WIDTH] = _dot(
            x, w_ref[:, qw + n * KV_WIDTH:qw + (n + 1) * KV_WIDTH]).astype(BF16)
    vw = qw + N_MIXERS * KV_WIDTH
    v = _dot(x, w_ref[:, vw:vw + N_MIXERS * KV_WIDTH])
    v_ref[...] = v.astype(BF16)
    va = v[:, 0:KV_WIDTH]
    lo = lane < HEAD_DIM
    v1_ref[:, 0:LANES] = jnp.where(lo, va, 1.0).astype(BF16)
    v1_ref[:, LANES:2 * LANES] = jnp.where(lo, pltpu.roll(va, HEAD_DIM, 1), 1.0).astype(BF16)


def _inproj(h, w_qkv, cos_tab, sin_tab, q_gain, k_gain, ones_bd, pos_block):
    R = h.shape[0]
    const = lambda i: (0, 0)
    return pl.pallas_call(
        _inproj_kernel,
        grid=(R // TM,),
        in_specs=[pl.BlockSpec((TM, D_MODEL), lambda i: (i, 0)),
                  pl.BlockSpec(w_qkv.shape, const),
                  pl.BlockSpec((TM, LANES), lambda i: (pos_block(i), 0)),
                  pl.BlockSpec((TM, LANES), lambda i: (pos_block(i), 0)),
                  pl.BlockSpec((1, MIX_WIDTH), const),
                  pl.BlockSpec((1, KV_WIDTH), const),
                  pl.BlockSpec((MIX_WIDTH, MIX_WIDTH), const)],
        out_specs=[pl.BlockSpec((TM, N_MIXERS * MIX_WIDTH), lambda i: (i, 0)),
                   pl.BlockSpec((TM, N_MIXERS * KV_WIDTH), lambda i: (i, 0)),
                   pl.BlockSpec((TM, N_MIXERS * KV_WIDTH), lambda i: (i, 0)),
                   pl.BlockSpec((TM, N_KV_HEADS * LANES), lambda i: (i, 0))],
        out_shape=[jax.ShapeDtypeStruct((R, N_MIXERS * MIX_WIDTH), BF16),
                   jax.ShapeDtypeStruct((R, N_MIXERS * KV_WIDTH), BF16),
                   jax.ShapeDtypeStruct((R, N_MIXERS * KV_WIDTH), BF16),
                   jax.ShapeDtypeStruct((R, N_KV_HEADS * LANES), BF16)],
        compiler_params=_cparams(("parallel",)),
        name="inproj",
    )(h, w_qkv, cos_tab, sin_tab, q_gain, k_gain, ones_bd)


def _group_queries(q, j):
    lane = lax.broadcasted_iota(jnp.int32, (q.shape[0], LANES), 1)
    keep = (lane < HEAD_DIM) if j == 0 else (lane >= HEAD_DIM)
    parts = []
    for hh in range(GROUP):
        h = GROUP * j + hh
        slot = q[:, (h // 2) * LANES:(h // 2 + 1) * LANES]
        if h % 2 != j:
            slot = pltpu.roll(slot, HEAD_DIM, 1)
        parts.append(jnp.where(keep, slot, 0.0))
    return jnp.concatenate(parts, axis=0).astype(BF16)


def _ungroup_outputs(out, j, T):
    lane = lax.broadcasted_iota(jnp.int32, (T, LANES), 1)
    lo = lane < HEAD_DIM
    slots = []
    for s in range(2):
        even = out[(2 * s) * T:(2 * s + 1) * T]
        odd = out[(2 * s + 1) * T:(2 * s + 2) * T]
        if j == 0:
            slots.append(jnp.where(lo, even, pltpu.roll(odd, HEAD_DIM, 1)))
        else:
            slots.append(jnp.where(lo, pltpu.roll(even, HEAD_DIM, 1), odd))
    return jnp.concatenate(slots, axis=1)


def _head_rows(vals, T):
    return jnp.concatenate([jnp.broadcast_to(v, (T, v.shape[-1])) for v in vals], axis=0)


def _global_kernel(q_ref, k_ref, v_ref, km_ref, vm_ref, *rest, S):
    o_ref = rest[-1]
    T = q_ref.shape[0]
    q = q_ref[...].astype(F32)
    km = km_ref[...]
    qs, state = [], []
    for j in range(N_KV_HEADS):
        qj = _group_queries(q, j)
        s_m = _dot_nt(qj, km)
        m0 = jnp.max(s_m, axis=-1, keepdims=True)
        p_m = jnp.exp2((s_m - m0).astype(BF16))
        qs.append(qj)
        state.append((m0, _dot(p_m, vm_ref[:, j * LANES:(j + 1) * LANES])))
    for c in range(S // TK_GLOBAL):
        kc = k_ref[c * TK_GLOBAL:(c + 1) * TK_GLOBAL, :]
        for j in range(N_KV_HEADS):
            m, acc = state[j]
            s = _dot_nt(qs[j], kc)
            m_new = jnp.maximum(m, jnp.max(s, axis=-1, keepdims=True))
            p = jnp.exp2((s - m_new).astype(BF16))
            vc = v_ref[c * TK_GLOBAL:(c + 1) * TK_GLOBAL, j * LANES:(j + 1) * LANES]
            state[j] = (m_new, jnp.exp2(m - m_new) * acc + _dot(p, vc))
    for j in range(N_KV_HEADS):
        acc = state[j][1]
        out = acc / pltpu.roll(acc, HEAD_DIM, 1)
        o_ref[:, j * 2 * LANES:(j + 1) * 2 * LANES] = _ungroup_outputs(out, 0, T).astype(BF16)


def _window_kernel(sink_ref, q_ref, kp_ref, kc_ref, kn_ref, vp_ref, vc_ref, vn_ref, km_ref, vm_ref,
                   bband_ref, bmeta_ref, *rest, nb):
    o_ref = rest[-1]
    i = pl.program_id(1)
    T = q_ref.shape[0]
    q = q_ref[...].astype(F32)
    kband = jnp.concatenate([kp_ref[...], kc_ref[...], kn_ref[...]], axis=0)
    vband = jnp.concatenate([vp_ref[...], vc_ref[...], vn_ref[...]], axis=0)
    col = lax.broadcasted_iota(jnp.int32, (1, 3 * BLOCK), 1)
    in_range = ((col >= BLOCK) | (i > 0)) & ((col < 2 * BLOCK) | (i < nb - 1))
    for j in range(N_KV_HEADS):
        qj = _group_queries(q, j)
        hs = slice(GROUP * j, GROUP * (j + 1))
        s_b = _dot_nt(qj, kband) + bband_ref[hs].reshape(GROUP * T, 3 * BLOCK)
        s_b = jnp.where(in_range, s_b, NEG_INF)
        s_m = _dot_nt(qj, km_ref[...]) + bmeta_ref[hs].reshape(GROUP * T, N_META)
        sink = jnp.concatenate([jnp.full((T, 1), sink_ref[GROUP * j + hh], F32) for hh in range(GROUP)], axis=0)
        m = jnp.maximum(jnp.maximum(jnp.max(s_b, axis=-1, keepdims=True),
                                    jnp.max(s_m, axis=-1, keepdims=True)), sink)
        e_b = jnp.exp(s_b - m)
        e_m = jnp.exp(s_m - m)
        denom = (jnp.sum(e_b, axis=-1, keepdims=True) + jnp.sum(e_m, axis=-1, keepdims=True)
                 + jnp.exp(sink - m))
        acc = _dot(e_b.astype(BF16), vband) + _dot(e_m.astype(BF16), vm_ref[...])
        o_ref[:, j * 2 * LANES:(j + 1) * 2 * LANES] = _ungroup_outputs(acc / denom, j, T).astype(BF16)


def _na_kernel(q_ref, k_ref, v_ref, km_ref, vm_ref, bias_ref, mbias_ref, *rest, rows):
    o_ref = rest[-1]
    blk = pl.program_id(1)
    W = GRID_W
    nkeys = NA_ROWS * W
    km = km_ref[...]
    vm = vm_ref[...]
    for j in range(N_KV_HEADS):
        mb = _head_rows([mbias_ref[GROUP * j + hh:GROUP * j + hh + 1, :] for hh in range(GROUP)], W)

        def row_body(rr, carry, j=j, mb=mb):
            r = blk * NA_QROWS + rr
            rs = jnp.clip(r - NA_ROWS // 2, 0, rows - NA_ROWS)
            delta = r - rs
            qoff = pl.multiple_of(rr * W, W)
            koff = pl.multiple_of(rs * W, W)
            qj = _group_queries(q_ref[pl.ds(qoff, W), :].astype(F32), j)
            kw = k_ref[pl.ds(koff, nkeys), :]
            vw = v_ref[pl.ds(koff, nkeys), :]
            s_w = _dot_nt(qj, kw) + bias_ref[delta, pl.ds(GROUP * j, GROUP)].reshape(GROUP * W, nkeys)
            s_m = _dot_nt(qj, km) + mb
            m = jnp.maximum(jnp.max(s_w, axis=-1, keepdims=True), jnp.max(s_m, axis=-1, keepdims=True))
            e_w = jnp.exp(s_w - m)
            e_m = jnp.exp(s_m - m)
            denom = jnp.sum(e_w, axis=-1, keepdims=True) + jnp.sum(e_m, axis=-1, keepdims=True)
            acc = _dot(e_w.astype(BF16), vw) + _dot(e_m.astype(BF16), vm)
            o_ref[pl.ds(qoff, W), j * 2 * LANES:(j + 1) * 2 * LANES] = (
                _ungroup_outputs(acc / denom, j, W).astype(BF16))
            return carry

        lax.fori_loop(0, NA_QROWS, row_body, 0, unroll=True)


def _meta_kernel(sink_ref, q_ref, ka_ref, va_ref, kb_ref, vb_ref, km_ref, vm_ref,
                 bq_meta_ref, bq_blk_ref, mbias_ref, o_in_ref, o_ref):
    del o_in_ref
    T = N_META
    q = q_ref[...].astype(F32)
    km_all = km_ref[...]
    vm_all = vm_ref[...]

    def finish(n, j, acc, denom):
        lo = n * MIX_WIDTH + j * 2 * LANES
        o_ref[:, lo:lo + 2 * LANES] = _ungroup_outputs(acc / denom, j, T).astype(BF16)

    for j in range(N_KV_HEADS):
        hs = slice(GROUP * j, GROUP * (j + 1))
        qj = _group_queries(q[:, 0:MIX_WIDTH], j)
        km, vm = km_all[:, 0:KV_WIDTH], vm_all[:, 0:KV_WIDTH]
        s_r = _dot_nt(qj, ka_ref[...])
        s_m = _dot_nt(qj, km)
        m = jnp.maximum(jnp.max(s_r, axis=-1, keepdims=True), jnp.max(s_m, axis=-1, keepdims=True))
        e_r = jnp.exp2(s_r - m)
        e_m = jnp.exp2(s_m - m)
        denom = jnp.sum(e_r, axis=-1, keepdims=True) + jnp.sum(e_m, axis=-1, keepdims=True)
        finish(0, j, _dot(e_r.astype(BF16), va_ref[...]) + _dot(e_m.astype(BF16), vm), denom)
        qj = _group_queries(q[:, MIX_WIDTH:2 * MIX_WIDTH], j)
        km, vm = km_all[:, KV_WIDTH:2 * KV_WIDTH], vm_all[:, KV_WIDTH:2 * KV_WIDTH]
        s_r = _dot_nt(qj, kb_ref[...]) + bq_blk_ref[hs].reshape(GROUP * T, BLOCK)
        s_m = _dot_nt(qj, km) + bq_meta_ref[hs].reshape(GROUP * T, N_META)
        sink = jnp.concatenate([jnp.full((T, 1), sink_ref[GROUP * j + hh], F32) for hh in range(GROUP)], axis=0)
        m = jnp.maximum(jnp.maximum(jnp.max(s_r, axis=-1, keepdims=True),
                                    jnp.max(s_m, axis=-1, keepdims=True)), sink)
        e_r = jnp.exp(s_r - m)
        e_m = jnp.exp(s_m - m)
        denom = (jnp.sum(e_r, axis=-1, keepdims=True) + jnp.sum(e_m, axis=-1, keepdims=True)
                 + jnp.exp(sink - m))
        finish(1, j, _dot(e_r.astype(BF16), vb_ref[...]) + _dot(e_m.astype(BF16), vm), denom)
        qj = _group_queries(q[:, 2 * MIX_WIDTH:3 * MIX_WIDTH], j)
        km, vm = km_all[:, 2 * KV_WIDTH:3 * KV_WIDTH], vm_all[:, 2 * KV_WIDTH:3 * KV_WIDTH]
        mb = _head_rows([mbias_ref[GROUP * j + hh:GROUP * j + hh + 1, :] for hh in range(GROUP)], T)
        s_m = _dot_nt(qj, km) + mb
        m = jnp.max(s_m, axis=-1, keepdims=True)
        e_m = jnp.exp(s_m - m)
        finish(2, j, _dot(e_m.astype(BF16), vm), jnp.sum(e_m, axis=-1, keepdims=True))


class _Group:
    def __init__(self, B, S, real_base, meta_batch_base, meta_base, n_meta_blocks):
        self.B, self.S = B, S
        self.real_base = real_base
        self.meta_blk0 = meta_base // N_META + meta_batch_base
        self.n_meta_blocks = n_meta_blocks
        assert real_base % S == 0 and S % TM == 0 and meta_base % N_META == 0


def _alias_args(o_prev, n_inputs):
    if o_prev is None:
        return [], [], {}
    return [o_prev], [pl.BlockSpec(memory_space=pl.ANY)], {n_inputs: 0}


def _global_attn(grp, q_all, k_all, v_ones, o_prev):
    B, S = grp.B, grp.S
    nq = S // TQ_GLOBAL
    qb0 = grp.real_base // TQ_GLOBAL
    sb0 = grp.real_base // S
    mb0 = grp.meta_blk0
    in_specs = [pl.BlockSpec((TQ_GLOBAL, MIX_WIDTH), lambda b, i: (qb0 + b * nq + i, 0)),
                pl.BlockSpec((S, KV_WIDTH), lambda b, i: (sb0 + b, 0)),
                pl.BlockSpec((S, N_KV_HEADS * LANES), lambda b, i: (sb0 + b, 0)),
                pl.BlockSpec((N_META, KV_WIDTH), lambda b, i: (mb0 + b, 0)),
                pl.BlockSpec((N_META, N_KV_HEADS * LANES), lambda b, i: (mb0 + b, 0))]
    extra, extra_specs, aliases = _alias_args(o_prev, len(in_specs))
    return pl.pallas_call(
        functools.partial(_global_kernel, S=S),
        grid=(B, nq),
        in_specs=in_specs + extra_specs,
        out_specs=pl.BlockSpec((TQ_GLOBAL, MIX_WIDTH), lambda b, i: (qb0 + b * nq + i, 0)),
        out_shape=jax.ShapeDtypeStruct((q_all.shape[0], N_MIXERS * MIX_WIDTH), BF16),
        input_output_aliases=aliases,
        compiler_params=_cparams(("parallel", "arbitrary")),
        name="mixer_global",
    )(q_all, k_all, v_ones, k_all, v_ones, *extra)


def _window_attn(grp, q_all, k_all, v_all, sink, bband, bmeta, o_prev):
    B, S = grp.B, grp.S
    nb = S // BLOCK
    qb0 = grp.real_base // BLOCK
    mb0 = grp.meta_blk0
    cur = lambda b, i, sink: (qb0 + b * nb + i, 1)
    prv = lambda b, i, sink: (qb0 + b * nb + jnp.maximum(i - 1, 0), 1)
    nxt = lambda b, i, sink: (qb0 + b * nb + jnp.minimum(i + 1, nb - 1), 1)
    met = lambda b, i, sink: (mb0 + b, 1)
    kv = lambda im: pl.BlockSpec((BLOCK, KV_WIDTH), im)
    in_specs = [pl.BlockSpec((BLOCK, MIX_WIDTH), cur),
                kv(prv), kv(cur), kv(nxt), kv(prv), kv(cur), kv(nxt),
                pl.BlockSpec((N_META, KV_WIDTH), met), pl.BlockSpec((N_META, KV_WIDTH), met),
                pl.BlockSpec((N_HEADS, BLOCK, 3 * BLOCK), lambda b, i, sink: (0, 0, 0)),
                pl.BlockSpec((N_HEADS, BLOCK, N_META), lambda b, i, sink: (0, i, 0))]
    extra, extra_specs, aliases = _alias_args(o_prev, len(in_specs) + 1)
    return pl.pallas_call(
        functools.partial(_window_kernel, nb=nb),
        grid_spec=pltpu.PrefetchScalarGridSpec(
            num_scalar_prefetch=1,
            grid=(B, nb),
            in_specs=in_specs + extra_specs,
            out_specs=pl.BlockSpec((BLOCK, MIX_WIDTH), cur)),
        out_shape=jax.ShapeDtypeStruct((q_all.shape[0], N_MIXERS * MIX_WIDTH), BF16),
        input_output_aliases=aliases,
        compiler_params=_cparams(("parallel", "arbitrary")),
        name="mixer_window",
    )(sink, q_all, k_all, k_all, k_all, v_all, v_all, v_all, k_all, v_all, bband, bmeta, *extra)


def _na_attn(grp, q_all, k_all, v_all, na_bias, na_mbias, o_prev):
    B, S = grp.B, grp.S
    rows = S // GRID_W
    tq = NA_QROWS * GRID_W
    nq = S // tq
    qb0 = grp.real_base // tq
    sb0 = grp.real_base // S
    mb0 = grp.meta_blk0
    in_specs = [pl.BlockSpec((tq, MIX_WIDTH), lambda b, i: (qb0 + b * nq + i, 2)),
                pl.BlockSpec((S, KV_WIDTH), lambda b, i: (sb0 + b, 2)),
                pl.BlockSpec((S, KV_WIDTH), lambda b, i: (sb0 + b, 2)),
                pl.BlockSpec((N_META, KV_WIDTH), lambda b, i: (mb0 + b, 2)),
                pl.BlockSpec((N_META, KV_WIDTH), lambda b, i: (mb0 + b, 2)),
                pl.BlockSpec(na_bias.shape, lambda b, i: (0, 0, 0, 0)),
                pl.BlockSpec(na_mbias.shape, lambda b, i: (0, 0))]
    extra, extra_specs, aliases = _alias_args(o_prev, len(in_specs))
    return pl.pallas_call(
        functools.partial(_na_kernel, rows=rows),
        grid=(B, nq),
        in_specs=in_specs + extra_specs,
        out_specs=pl.BlockSpec((tq, MIX_WIDTH), lambda b, i: (qb0 + b * nq + i, 2)),
        out_shape=jax.ShapeDtypeStruct((q_all.shape[0], N_MIXERS * MIX_WIDTH), BF16),
        input_output_aliases=aliases,
        compiler_params=_cparams(("parallel", "arbitrary")),
        name="mixer_neighbourhood",
    )(q_all, k_all, v_all, k_all, v_all, na_bias, na_mbias, *extra)


def _meta_attn(grp, q_all, k_all, v_all, sink, bq_meta, bq_blk, na_mbias, o_prev):
    B, S = grp.B, grp.S
    sb0 = grp.real_base // S
    bb0 = grp.real_base // BLOCK
    nb = S // BLOCK
    mb0 = grp.meta_blk0
    clamp = lambda b: jnp.minimum(b, B - 1)
    mrow = lambda b, sink: (mb0 + clamp(b), 0)
    in_specs = [pl.BlockSpec((N_META, N_MIXERS * MIX_WIDTH), mrow),
                pl.BlockSpec((S, KV_WIDTH), lambda b, sink: (sb0 + clamp(b), 0)),
                pl.BlockSpec((S, KV_WIDTH), lambda b, sink: (sb0 + clamp(b), 0)),
                pl.BlockSpec((BLOCK, KV_WIDTH), lambda b, sink: (bb0 + clamp(b) * nb, 1)),
                pl.BlockSpec((BLOCK, KV_WIDTH), lambda b, sink: (bb0 + clamp(b) * nb, 1)),
                pl.BlockSpec((N_META, N_MIXERS * KV_WIDTH), mrow),
                pl.BlockSpec((N_META, N_MIXERS * KV_WIDTH), mrow),
                pl.BlockSpec(bq_meta.shape, lambda b, sink: (0, 0, 0)),
                pl.BlockSpec(bq_blk.shape, lambda b, sink: (0, 0, 0)),
                pl.BlockSpec(na_mbias.shape, lambda b, sink: (0, 0)),
                pl.BlockSpec(memory_space=pl.ANY)]
    return pl.pallas_call(
        _meta_kernel,
        grid_spec=pltpu.PrefetchScalarGridSpec(
            num_scalar_prefetch=1,
            grid=(grp.n_meta_blocks,),
            in_specs=in_specs,
            out_specs=pl.BlockSpec((N_META, N_MIXERS * MIX_WIDTH), lambda b, sink: (mb0 + b, 0))),
        out_shape=jax.ShapeDtypeStruct((q_all.shape[0], N_MIXERS * MIX_WIDTH), BF16),
        input_output_aliases={len(in_specs): 0},
        compiler_params=_cparams(("arbitrary",)),
        name="mixer_meta_queries",
    )(sink, q_all, k_all, v_all, k_all, v_all, k_all, v_all, bq_meta, bq_blk, na_mbias, o_prev)


def _route(h1, wr_hi, wr_lo, rbias):
    T = h1.shape[0]
    x_hi, x_lo = _split_bf16(h1)
    logits = _dot_nt(wr_hi, x_hi) + _dot_nt(wr_hi, x_lo) + _dot_nt(wr_lo, x_hi)
    scores = 1.0 / (1.0 + jnp.exp(-logits))
    sel = scores + rbias
    per_group = N_EXPERTS // N_EXPERT_GROUPS
    sel3 = sel.reshape(N_EXPERT_GROUPS, per_group, T)
    idx3 = lax.broadcasted_iota(jnp.int32, sel3.shape, 1).astype(F32)
    m1 = jnp.max(sel3, axis=1, keepdims=True)
    first = jnp.min(jnp.where(sel3 == m1, idx3, float(per_group)), axis=1, keepdims=True)
    m2 = jnp.max(jnp.where(idx3 == first, -jnp.inf, sel3), axis=1, keepdims=True)
    gscore = (m1 + m2).reshape(N_EXPERT_GROUPS, T)

    def rank_of(vals):
        idx = lax.broadcasted_iota(jnp.int32, vals.shape, 0)
        rank = jnp.zeros(vals.shape, F32)
        for r in range(vals.shape[0]):
            row = vals[r:r + 1, :]
            ge = jnp.where(row >= vals, 1.0, 0.0)
            gt = jnp.where(row > vals, 1.0, 0.0)
            rank = rank + jnp.where(idx > r, ge, gt)
        return rank

    gkeep = jnp.where(rank_of(gscore) < TOPK_GROUPS, 1.0, 0.0)
    ekeep = jnp.broadcast_to(gkeep.reshape(N_EXPERT_GROUPS, 1, T), sel3.shape).reshape(N_EXPERTS, T)
    masked = jnp.where(ekeep > 0.5, sel, NEG_INF)
    eidx = lax.broadcasted_iota(jnp.int32, masked.shape, 0).astype(F32)
    chosen = jnp.zeros(masked.shape, F32)
    for _ in range(TOP_K):
        best = jnp.max(masked, axis=0, keepdims=True)
        first = jnp.min(jnp.where(masked == best, eidx, float(N_EXPERTS)), axis=0, keepdims=True)
        hit = eidx == first
        chosen = jnp.where(hit, 1.0, chosen)
        masked = jnp.where(hit, -jnp.inf, masked)
    w = jnp.where(chosen > 0.5, scores, 0.0)
    return w / jnp.sum(w, axis=0, keepdims=True) * ROUTED_SCALE


def _merge_kernel(h_ref, o_ref, wg_ref, wb_ref, wo_ref, g_ref, b_ref, wrh_ref, wrl_ref, rb_ref,
                  h1_ref, gates_ref, *, alpha):
    h = h_ref[...]
    x = h.astype(BF16)
    merged = None
    for n in range(N_MIXERS):
        logit = _dot(x, wg_ref[:, n * D_MODEL:(n + 1) * D_MODEL])
        branch = _dot(o_ref[:, n * MIX_WIDTH:(n + 1) * MIX_WIDTH], wb_ref[n])
        term = branch / (1.0 + jnp.exp(-logit))
        merged = term if merged is None else merged + term
    mix = _dot(merged.astype(BF16), wo_ref[...])
    h1 = _layer_norm(alpha * h + mix, g_ref[...], b_ref[...])
    h1_ref[...] = h1
    gates_t = _route(h1, wrh_ref[...], wrl_ref[...], rb_ref[...])
    pad = jnp.zeros((LANES - N_EXPERTS, gates_t.shape[1]), F32)
    gates_ref[...] = jnp.concatenate([gates_t, pad], axis=0).T


def _merge(h, o_all, w_gate, w_branch, w_out, ln_g, ln_b, wr_hi, wr_lo, rbias, alpha):
    R = h.shape[0]
    c2 = lambda i: (0, 0)
    return pl.pallas_call(
        functools.partial(_merge_kernel, alpha=alpha),
        grid=(R // TM,),
        in_specs=[pl.BlockSpec((TM, D_MODEL), lambda i: (i, 0)),
                  pl.BlockSpec((TM, N_MIXERS * MIX_WIDTH), lambda i: (i, 0)),
                  pl.BlockSpec(w_gate.shape, c2),
                  pl.BlockSpec(w_branch.shape, lambda i: (0, 0, 0)),
                  pl.BlockSpec(w_out.shape, c2),
                  pl.BlockSpec((1, D_MODEL), c2),
                  pl.BlockSpec((1, D_MODEL), c2),
                  pl.BlockSpec(wr_hi.shape, c2),
                  pl.BlockSpec(wr_lo.shape, c2),
                  pl.BlockSpec(rbias.shape, c2)],
        out_specs=[pl.BlockSpec((TM, D_MODEL), lambda i: (i, 0)),
                   pl.BlockSpec((TM, LANES), lambda i: (i, 0))],
        out_shape=[jax.ShapeDtypeStruct((R, D_MODEL), F32),
                   jax.ShapeDtypeStruct((R, LANES), F32)],
        compiler_params=_cparams(("parallel",)),
        name="merge_ln_route",
    )(h, o_all, w_gate, w_branch, w_out, ln_g, ln_b, wr_hi, wr_lo, rbias)


TD = 256
CH = 16
SLOTS = 3072
NCH = SLOTS // CH
MT = 512
CPM = MT // CH
XBUFS = 3


def _swiglu_act(gu):
    g = gu[:, :D_EXPERT]
    return g / (1.0 + jnp.exp(-g)) * gu[:, D_EXPERT:]


def _slot_of_token(gates, lo_row):
    routed = gates > 0.0
    r = lax.broadcasted_iota(jnp.int32, (TD, TD), 0)
    c = lax.broadcasted_iota(jnp.int32, (TD, TD), 1)
    earlier = jnp.where(c < r, 1.0, 0.0).astype(BF16)
    rank = _dot(earlier, jnp.where(routed, 1.0, 0.0).astype(BF16))
    return jnp.where(routed, lo_row + rank + 1.0, 0.0)


def _split64(x):
    a = jnp.floor(x * (1.0 / 64.0))
    return a.astype(BF16), (x - 64.0 * a).astype(BF16)


def _dispatch_kernel(h_ref, g_ref, lohi_ref, x_ref, w_ref):
    lohi = lohi_ref[0]
    lo_row, hi_row = lohi[0:1], lohi[1:2]
    gates = g_ref[...]
    a, b = _split64(_slot_of_token(gates, lo_row).T)
    gates_t = gates.T.astype(BF16)
    x = h_ref[...].astype(BF16)
    for blk in range(SLOTS // MT):
        s = (lax.broadcasted_iota(jnp.int32, (MT, LANES), 0) + blk * MT).astype(F32)
        owner = jnp.where(s >= lo_row, jnp.where(s < hi_row, 1.0, 0.0), 0.0).astype(BF16)
        want = 64.0 * _dot(owner, a) + _dot(owner, b)
        s1 = (lax.broadcasted_iota(jnp.int32, (MT, TD), 0) + (blk * MT + 1)).astype(F32)
        hit = want == s1
        x_ref[blk * MT:(blk + 1) * MT, :] = _dot(jnp.where(hit, 1.0, 0.0).astype(BF16), x).astype(BF16)
        weight = jnp.where(hit, _dot(owner, gates_t), 0.0)
        w_ref[:, blk * MT:(blk + 1) * MT] = weight.T.astype(BF16)


def _dispatch(h1, gates, lohi):
    n = h1.shape[0] // TD
    return pl.pallas_call(
        _dispatch_kernel,
        grid=(n,),
        in_specs=[pl.BlockSpec((TD, D_MODEL), lambda i: (i, 0)),
                  pl.BlockSpec((TD, LANES), lambda i: (i, 0)),
                  pl.BlockSpec((1, 8, LANES), lambda i: (i, 0, 0))],
        out_specs=[pl.BlockSpec((SLOTS, D_MODEL), lambda i: (i, 0)),
                   pl.BlockSpec((TD, SLOTS), lambda i: (i, 0))],
        out_shape=[jax.ShapeDtypeStruct((n * SLOTS, D_MODEL), BF16),
                   jax.ShapeDtypeStruct((n * TD, SLOTS), BF16)],
        compiler_params=_cparams(("parallel",)),
        name="moe_dispatch",
    )(h1, gates, lohi)


def _chunk_gather(table_ref, first, n_chunks, src_hbm, buf, sem):
    copies = []
    for c in range(n_chunks):
        row = pl.multiple_of(table_ref[first + c] * CH, CH)
        copies.append(pltpu.make_async_copy(src_hbm.at[pl.ds(row, CH)], buf.at[pl.ds(c * CH, CH)], sem))
    return copies


def _chunk_wait(n_chunks, src_hbm, buf, sem):
    for c in range(n_chunks):
        pltpu.make_async_copy(src_hbm.at[pl.ds(0, CH)], buf.at[pl.ds(c * CH, CH)], sem).wait()


def _expert_kernel(te_ref, src_ref, nu_ref, x_hbm, wgu_ref, wd_ref, y_ref, xbuf, sem):
    del te_ref
    m = pl.program_id(0)
    n_used = nu_ref[0]

    def start(step):
        slot = step % XBUFS
        for cp in _chunk_gather(src_ref, step * CPM, CPM, x_hbm, xbuf.at[slot], sem.at[slot]):
            cp.start()

    for ahead in range(XBUFS - 1):
        @pl.when((m == 0) & (ahead < n_used))
        def _(ahead=ahead):
            start(ahead)

    @pl.when(m + (XBUFS - 1) < n_used)
    def _():
        start(m + (XBUFS - 1))

    @pl.when(m < n_used)
    def _():
        slot = m % XBUFS
        _chunk_wait(CPM, x_hbm, xbuf.at[slot], sem.at[slot])
        act = _swiglu_act(_dot(xbuf[slot], wgu_ref[0]))
        y_ref[...] = _dot(act.astype(BF16), wd_ref[0]).astype(BF16)


def _experts(x_disp, wgu, wd, tile_expert, src_chunk, n_used):
    n_steps = tile_expert.shape[0]
    return pl.pallas_call(
        _expert_kernel,
        grid_spec=pltpu.PrefetchScalarGridSpec(
            num_scalar_prefetch=3,
            grid=(n_steps,),
            in_specs=[pl.BlockSpec(memory_space=pl.ANY),
                      pl.BlockSpec((1, D_MODEL, 2 * D_EXPERT), lambda m, te, src, nu: (te[m], 0, 0)),
                      pl.BlockSpec((1, D_EXPERT, D_MODEL), lambda m, te, src, nu: (te[m], 0, 0))],
            out_specs=pl.BlockSpec((MT, D_MODEL), lambda m, te, src, nu: (jnp.minimum(m, nu[0] - 1), 0)),
            scratch_shapes=[pltpu.VMEM((XBUFS, MT, D_MODEL), BF16), pltpu.SemaphoreType.DMA((XBUFS,))]),
        out_shape=jax.ShapeDtypeStruct((n_steps * MT, D_MODEL), BF16),
        compiler_params=_cparams(("arbitrary",)),
        name="moe_experts",
    )(tile_expert, src_chunk, n_used, x_disp, wgu, wd)


def _combine_kernel(dst_ref, y_hbm, w_ref, h_ref, wsgu_ref, wsd_ref, lg_ref, lb_ref, o_ref, ybuf, sem, *, alpha):
    i = pl.program_id(0)

    def start(tile):
        slot = tile % 2
        for cp in _chunk_gather(dst_ref, tile * NCH, NCH, y_hbm, ybuf.at[slot], sem.at[slot]):
            cp.start()

    @pl.when(i == 0)
    def _():
        start(i)

    @pl.when(i + 1 < pl.num_programs(0))
    def _():
        start(i + 1)

    h = h_ref[...]
    shared = _dot(_swiglu_act(_dot(h.astype(BF16), wsgu_ref[...])).astype(BF16), wsd_ref[...])
    slot = i % 2
    _chunk_wait(NCH, y_hbm, ybuf.at[slot], sem.at[slot])
    routed = _dot(w_ref[...], ybuf[slot])
    o_ref[...] = _layer_norm(alpha * h + shared + routed, lg_ref[...], lb_ref[...])


def _combine(y_sorted, w_t, h1, dst_chunk, wsgu, wsd, ln_g, ln_b, alpha):
    n = h1.shape[0] // TD
    c2 = lambda i, dst: (0, 0)
    return pl.pallas_call(
        functools.partial(_combine_kernel, alpha=alpha),
        grid_spec=pltpu.PrefetchScalarGridSpec(
            num_scalar_prefetch=1,
            grid=(n,),
            in_specs=[pl.BlockSpec(memory_space=pl.ANY),
                      pl.BlockSpec((TD, SLOTS), lambda i, dst: (i, 0)),
                      pl.BlockSpec((TD, D_MODEL), lambda i, dst: (i, 0)),
                      pl.BlockSpec(wsgu.shape, c2),
                      pl.BlockSpec(wsd.shape, c2),
                      pl.BlockSpec((1, D_MODEL), c2),
                      pl.BlockSpec((1, D_MODEL), c2)],
            out_specs=pl.BlockSpec((TD, D_MODEL), lambda i, dst: (i, 0)),
            scratch_shapes=[pltpu.VMEM((2, SLOTS, D_MODEL), BF16), pltpu.SemaphoreType.DMA((2,))]),
        out_shape=jax.ShapeDtypeStruct((h1.shape[0], D_MODEL), F32),
        compiler_params=_cparams(("arbitrary",)),
        name="moe_combine_ln",
    )(dst_chunk, y_sorted, w_t, h1, wsgu, wsd, ln_g, ln_b)


def _routing_tables(gates):
    n = gates.shape[0] // TD
    cnt = jnp.sum((gates[:, :N_EXPERTS] > 0.0).reshape(n, TD, N_EXPERTS), axis=1, dtype=jnp.int32)
    nch = (cnt + (CH - 1)) // CH
    hi16 = jnp.cumsum(nch, axis=1)
    lo16 = hi16 - nch
    nct = hi16[:, -1:]
    pad = jnp.broadcast_to(nct, (n, LANES - N_EXPERTS))
    lohi = jnp.zeros((n, 8, LANES), F32)
    lohi = lohi.at[:, 0, :].set((jnp.concatenate([lo16, pad], axis=1) * CH).astype(F32))
    lohi = lohi.at[:, 1, :].set((jnp.concatenate([hi16, pad], axis=1) * CH).astype(F32))
    tot = jnp.sum(nch, axis=0)
    seg_len = (tot + (CPM - 1)) // CPM * CPM
    seg_end = jnp.cumsum(seg_len)
    seg_start = seg_end - seg_len
    gpos = seg_start[None, :] + jnp.cumsum(nch, axis=0) - nch
    n_steps = (n * NCH + N_EXPERTS * CPM) // CPM
    n_used = (seg_end[-1] // CPM).astype(jnp.int32).reshape(1)
    step = jnp.arange(n_steps, dtype=jnp.int32)
    tile_expert = jnp.sum(seg_end[None, :] // CPM <= jnp.minimum(step, n_used - 1)[:, None], axis=1, dtype=jnp.int32)
    tile_expert = jnp.minimum(tile_expert, N_EXPERTS - 1)
    exact = functools.partial(jnp.dot, precision=lax.Precision.HIGHEST)
    experts = jnp.arange(N_EXPERTS, dtype=jnp.int32)
    g = jnp.arange(n_steps * CPM, dtype=jnp.int32)
    e_of_g = jnp.minimum(jnp.sum(seg_end[None, :] <= g[:, None], axis=1, dtype=jnp.int32), N_EXPERTS - 1)
    pick_e = (e_of_g[:, None] == experts[None, :]).astype(F32)
    first = exact(pick_e, gpos.T.astype(F32))
    count = exact(pick_e, nch.T.astype(F32))
    base = exact(pick_e, (jnp.arange(n, dtype=jnp.int32)[:, None] * NCH + lo16).T.astype(F32))
    gf = g.astype(F32)[:, None]
    inside = (first <= gf) & (gf < first + count)
    src_chunk = jnp.sum(jnp.where(inside, base + gf - first, 0.0), axis=1).astype(jnp.int32)
    k = jnp.arange(NCH, dtype=jnp.int32)
    e_of_k = jnp.minimum(jnp.sum(hi16[:, None, :] <= k[None, :, None], axis=2, dtype=jnp.int32), N_EXPERTS - 1)
    pick_k = e_of_k[:, :, None] == experts[None, None, :]
    pos = jnp.sum(jnp.where(pick_k, (gpos - lo16)[:, None, :], 0), axis=2) + k[None, :]
    dst_chunk = jnp.where(k[None, :] < nct, pos, 0).astype(jnp.int32).reshape(-1)
    return lohi, tile_expert, src_chunk, n_used, dst_chunk


def _moe(h1, gates, wgu, wd, wsgu, wsd, ln_g, ln_b, alpha):
    lohi, tile_expert, src_chunk, n_used, dst_chunk = _routing_tables(gates)
    x_disp, w_t = _dispatch(h1, gates, lohi)
    y_sorted = _experts(x_disp, wgu, wd, tile_expert, src_chunk, n_used)
    return _combine(y_sorted, w_t, h1, dst_chunk, wsgu, wsd, ln_g, ln_b, alpha)


def _t5_bucket(rel):
    half = T5_BUCKETS // 2
    max_exact = half // 2
    n = np.abs(rel)
    ratio = np.log(np.maximum(n, 1).astype(np.float32) / np.float32(max_exact))
    ratio = ratio / np.float32(math.log(T5_MAX_DIST / max_exact)) * np.float32(half - max_exact)
    large = np.minimum(max_exact + ratio.astype(np.int32), half - 1)
    return np.where(rel > 0, half, 0) + np.where(n < max_exact, n, large)


def _t5_tables(t5_table, s_max):
    def bias(rel, valid):
        b = t5_table[_t5_bucket(rel)].astype(F32)
        return jnp.where(jnp.asarray(valid)[None], jnp.transpose(b, (2, 0, 1)), NEG_INF)

    ii = np.arange(BLOCK)[:, None]
    jj = np.arange(3 * BLOCK)[None, :]
    rel = jj - ii - BLOCK
    bband = bias(rel, np.abs(rel) <= WINDOW)
    t = np.arange(s_max)[:, None]
    m = np.arange(N_META)[None, :]
    bmeta = bias(m - (N_META + t), np.ones((s_max, N_META), bool))
    mpos = np.arange(N_META)[:, None]
    kpos = np.arange(N_META + BLOCK)[None, :]
    relq = kpos - mpos
    bq = bias(relq, (kpos < N_META) | (np.abs(relq) <= WINDOW))
    return bband, bmeta, bq[:, :, :N_META], bq[:, :, N_META:]


def _na_bias_cases(rpb):
    W = GRID_W
    delta = np.arange(NA_ROWS)[:, None, None, None]
    i = np.arange(NA_ROWS)[None, :, None, None]
    c = np.arange(W)[None, None, :, None]
    kc = np.arange(W)[None, None, None, :]
    cs = np.clip(c - NA_COLS // 2, 0, W - NA_COLS)
    valid = (kc >= cs) & (kc < cs + NA_COLS)
    dc = np.clip(kc - c + (NA_COLS - 1), 0, 2 * NA_COLS - 2)[0, 0]
    t = jnp.where(jnp.asarray(valid[0, 0]), rpb.astype(F32)[:, :, dc], NEG_INF)
    cases = [jnp.transpose(t[:, NA_ROWS - 1 - d:2 * NA_ROWS - 1 - d], (0, 2, 1, 3)) for d in range(NA_ROWS)]
    return jnp.stack(cases, axis=0).reshape(NA_ROWS, N_HEADS, W, NA_ROWS * W)


def _rope_tables(s_max, n_meta_rows):
    half = HEAD_DIM // 4
    freq = ROPE_THETA ** (-jnp.arange(half, dtype=F32) / half)
    t = np.arange(s_max)
    mp = np.tile(np.arange(N_META) - N_META, n_meta_rows // N_META)
    pos_row = jnp.asarray(np.concatenate([t // GRID_W, mp]), jnp.int32).astype(F32)
    pos_col = jnp.asarray(np.concatenate([t % GRID_W, mp]), jnp.int32).astype(F32)
    ar = pos_row[:, None] * freq
    ac = pos_col[:, None] * freq
    cos = jnp.concatenate([jnp.cos(ar), jnp.cos(ar), jnp.cos(ac), jnp.cos(ac)], axis=1)
    sin = jnp.concatenate([-jnp.sin(ar), jnp.sin(ar), -jnp.sin(ac), jnp.sin(ac)], axis=1)
    return jnp.tile(cos, (1, 2)), jnp.tile(sin, (1, 2))


def kernel(x_prompt, x_sample, meta_tokens, ln_in_g, ln_in_b, t5_table, w_in, q_gain, k_gain, sink,
           na_rpb, na_meta_bias, w_branch, w_out, ln1_g, ln1_b, w_router, router_bias,
           w_expert_gate_up, w_expert_down, w_shared_gate_up, w_shared_down, ln2_g, ln2_b):
    depth = w_in.shape[0]
    alpha = (2 * depth) ** 0.25
    B0, S0, D = x_prompt.shape
    B1, S1, _ = x_sample.shape
    assert D == D_MODEL
    real = B0 * S0 + B1 * S1
    n_meta_rows = -(-(B0 + B1) * N_META // TM) * TM
    R = real + n_meta_rows
    n_meta_blocks = n_meta_rows // N_META
    g0 = _Group(B0, S0, 0, 0, real, B0)
    g1 = _Group(B1, S1, B0 * S0, B0, real, n_meta_blocks - B0)
    s_max = max(S0, S1)

    x = jnp.concatenate([x_prompt.reshape(B0 * S0, D), x_sample.reshape(B1 * S1, D),
                         jnp.tile(meta_tokens, (n_meta_blocks, 1))], axis=0)
    h = _embed_ln(x, ln_in_g.reshape(1, D), ln_in_b.reshape(1, D))

    cos_tab, sin_tab = _rope_tables(s_max, TM)
    n0, n1 = B0 * S0 // TM, real // TM
    p0, p1, pm = S0 // TM, S1 // TM, s_max // TM

    def pos_block(i):
        return jnp.where(i < n0, i % p0, jnp.where(i < n1, (i - n0) % p1, pm))

    bband, bmeta, bq_meta, bq_blk = _t5_tables(t5_table, s_max)
    ones_bd = jnp.asarray(np.kron(np.eye(N_HEADS), np.ones((HEAD_DIM, HEAD_DIM))), BF16)

    qs, ks, vs = [], [], []
    for n in range(N_MIXERS):
        off = n * QKV_WIDTH
        qs.append(w_in[:, :, off:off + MIX_WIDTH])
        ks.append(w_in[:, :, off + MIX_WIDTH:off + MIX_WIDTH + KV_WIDTH])
        vs.append(w_in[:, :, off + MIX_WIDTH + KV_WIDTH:off + QKV_WIDTH])
    wr_t = jnp.swapaxes(w_router, 1, 2)
    wr_hi = wr_t.astype(BF16)
    layers = dict(
        w_qkv=jnp.concatenate(qs + ks + vs, axis=2).astype(BF16),
        w_gate=w_in[:, :, N_MIXERS * QKV_WIDTH:].astype(BF16),
        q_gain=jnp.tile(q_gain, (1, N_HEADS)).reshape(depth, 1, MIX_WIDTH),
        k_gain=jnp.tile(k_gain, (1, N_KV_HEADS)).reshape(depth, 1, KV_WIDTH),
        sink=sink.astype(F32),
        na_bias=jax.vmap(_na_bias_cases)(na_rpb),
        na_mbias=na_meta_bias.astype(F32),
        w_branch=w_branch.astype(BF16),
        w_out=w_out.astype(BF16),
        ln1_g=ln1_g.reshape(depth, 1, D), ln1_b=ln1_b.reshape(depth, 1, D),
        wr_hi=wr_hi, wr_lo=(wr_t - wr_hi.astype(F32)).astype(BF16),
        rbias=router_bias.astype(F32).reshape(depth, N_EXPERTS, 1),
        wgu=w_expert_gate_up.astype(BF16), wd=w_expert_down.astype(BF16),
        wsgu=w_shared_gate_up.astype(BF16), wsd=w_shared_down.astype(BF16),
        ln2_g=ln2_g.reshape(depth, 1, D), ln2_b=ln2_b.reshape(depth, 1, D),
    )

    def layer(h, p):
        q_all, k_all, v_all, v_ones = _inproj(h, p["w_qkv"], cos_tab, sin_tab, p["q_gain"], p["k_gain"],
                                      ones_bd, pos_block)
        o = None
        for grp in (g0, g1):
            o = _global_attn(grp, q_all, k_all, v_ones, o)
            o = _window_attn(grp, q_all, k_all, v_all, p["sink"], bband, bmeta[:, :grp.S], o)
            o = _na_attn(grp, q_all, k_all, v_all, p["na_bias"], p["na_mbias"], o)
        for grp in (g0, g1):
            o = _meta_attn(grp, q_all, k_all, v_all, p["sink"], bq_meta, bq_blk, p["na_mbias"], o)
        h1, gates = _merge(h, o, p["w_gate"], p["w_branch"], p["w_out"], p["ln1_g"], p["ln1_b"],
                           p["wr_hi"], p["wr_lo"], p["rbias"], alpha)
        h2 = _moe(h1, gates, p["wgu"], p["wd"], p["wsgu"], p["wsd"], p["ln2_g"], p["ln2_b"],
                  alpha)
        return h2, None

    h, _ = lax.scan(layer, h, layers)
    y_prompt = h[:B0 * S0].reshape(B0, S0, D)
    y_sample = h[B0 * S0:real].reshape(B1, S1, D)
    return (y_prompt, y_sample)
```

```python
import functools
import math

import numpy as np
import jax
import jax.numpy as jnp
from jax import lax
from jax.experimental import pallas as pl
from jax.experimental.pallas import tpu as pltpu

F32 = jnp.float32
BF16 = jnp.bfloat16

D_MODEL = 1024
HEAD_DIM = 64
N_HEADS = 8
N_KV_HEADS = 2
GROUP = N_HEADS // N_KV_HEADS
MIX_WIDTH = N_HEADS * HEAD_DIM
KV_WIDTH = N_KV_HEADS * HEAD_DIM
N_MIXERS = 3
QKV_WIDTH = MIX_WIDTH + 2 * KV_WIDTH
N_META = 16
GRID_W = 64
BLOCK = 128
WINDOW = 128
NA_ROWS = 8
NA_COLS = 16
T5_BUCKETS = 32
T5_MAX_DIST = 128
ROPE_THETA = 10000.0
N_EXPERTS = 64
TOP_K = 8
N_EXPERT_GROUPS = 8
TOPK_GROUPS = 4
D_EXPERT = 256
ROUTED_SCALE = 2.5
NEG_INF = -1e30
LOG2E = math.log2(math.e)
LANES = 128

TM = 512
TQ_GLOBAL = 128
TK_GLOBAL = 512
NA_QROWS = 8
VMEM_LIMIT = 56 * 1024 * 1024


def _cparams(sem):
    return pltpu.CompilerParams(dimension_semantics=sem, vmem_limit_bytes=VMEM_LIMIT)


def _dot(a, b):
    return jnp.dot(a, b, preferred_element_type=F32)


def _dot_nt(a, b):
    return lax.dot_general(a, b, (((1,), (1,)), ((), ())), preferred_element_type=F32)


def _split_bf16(x):
    hi = x.astype(BF16)
    lo = (x - hi.astype(F32)).astype(BF16)
    return hi, lo


def _layer_norm(x, g, b):
    mu = jnp.mean(x, axis=-1, keepdims=True)
    xc = x - mu
    var = jnp.mean(xc * xc, axis=-1, keepdims=True)
    return xc * lax.rsqrt(var + 1e-5) * g + b


def _embed_ln_kernel(x_ref, g_ref, b_ref, o_ref):
    o_ref[...] = _layer_norm(x_ref[...], g_ref[...], b_ref[...])


def _embed_ln(x, g, b):
    R = x.shape[0]
    return pl.pallas_call(
        _embed_ln_kernel,
        grid=(R // TM,),
        in_specs=[pl.BlockSpec((TM, D_MODEL), lambda i: (i, 0)),
                  pl.BlockSpec((1, D_MODEL), lambda i: (0, 0)),
                  pl.BlockSpec((1, D_MODEL), lambda i: (0, 0))],
        out_specs=pl.BlockSpec((TM, D_MODEL), lambda i: (i, 0)),
        out_shape=jax.ShapeDtypeStruct((R, D_MODEL), F32),
        compiler_params=_cparams(("parallel",)),
        name="embed_ln",
    )(x, g, b)


def _rope_slot(x, cos, sin_signed, first_half):
    fwd = pltpu.roll(x, LANES - 16, 1)
    bwd = pltpu.roll(x, 16, 1)
    return x * cos + jnp.where(first_half, fwd, bwd) * sin_signed


def _head_rms(x, ones_bd, gain):
    hi, lo = _split_bf16(x * x)
    ss = _dot(hi, ones_bd) + _dot(lo, ones_bd)
    return x * lax.rsqrt(ss * (1.0 / HEAD_DIM) + 1e-6) * gain


def _inproj_kernel(h_ref, w_ref, cos_ref, sin_ref, qg_ref, kg_ref, ones_ref, q_ref, k_ref, v_ref, v1_ref):
    x = h_ref[...].astype(BF16)
    cos = cos_ref[...]
    sin = sin_ref[...]
    lane = lax.broadcasted_iota(jnp.int32, cos.shape, 1)
    first_half = (lane % 32) < 16
    scale = HEAD_DIM ** -0.5
    qw = N_MIXERS * MIX_WIDTH
    qa = _head_rms(_dot(x, w_ref[:, 0:MIX_WIDTH]), ones_ref[...], qg_ref[...])
    for s in range(MIX_WIDTH // LANES):
        sl = slice(s * LANES, (s + 1) * LANES)
        q_ref[:, sl] = (_rope_slot(qa[:, sl], cos, sin, first_half) * (scale * LOG2E)).astype(BF16)
    ka = _head_rms(_dot(x, w_ref[:, qw:qw + KV_WIDTH]), ones_ref[0:LANES, 0:LANES], kg_ref[...])
    k_ref[:, 0:KV_WIDTH] = _rope_slot(ka, cos, sin, first_half).astype(BF16)
    for n in range(1, N_MIXERS):
        q_ref[:, n * MIX_WIDTH:(n + 1) * MIX_WIDTH] = (
            _dot(x, w_ref[:, n * MIX_WIDTH:(n + 1) * MIX_WIDTH]) * (scale * LOG2E)).astype(BF16)
        k_ref[:, n * KV_WIDTH:(n + 1) * KV_WIDTH] = _dot(
            x, w_ref[:, qw + n * KV_WIDTH:qw + (n + 1) * KV_WIDTH]).astype(BF16)
    vw = qw + N_MIXERS * KV_WIDTH
    v = _dot(x, w_ref[:, vw:vw + N_MIXERS * KV_WIDTH])
    v_ref[...] = v.astype(BF16)
    lo = lane < HEAD_DIM
    for n in range(N_MIXERS):
        vn = v[:, n * KV_WIDTH:(n + 1) * KV_WIDTH]
        v1_ref[:, 2 * n * LANES:(2 * n + 1) * LANES] = jnp.where(lo, vn, 1.0).astype(BF16)
        v1_ref[:, (2 * n + 1) * LANES:(2 * n + 2) * LANES] = jnp.where(
            lo, pltpu.roll(vn, HEAD_DIM, 1), 1.0).astype(BF16)


def _inproj(h, w_qkv, cos_tab, sin_tab, q_gain, k_gain, ones_bd, pos_block):
    R = h.shape[0]
    const = lambda i: (0, 0)
    return pl.pallas_call(
        _inproj_kernel,
        grid=(R // TM,),
        in_specs=[pl.BlockSpec((TM, D_MODEL), lambda i: (i, 0)),
                  pl.BlockSpec(w_qkv.shape, const),
                  pl.BlockSpec((TM, LANES), lambda i: (pos_block(i), 0)),
                  pl.BlockSpec((TM, LANES), lambda i: (pos_block(i), 0)),
                  pl.BlockSpec((1, MIX_WIDTH), const),
                  pl.BlockSpec((1, KV_WIDTH), const),
                  pl.BlockSpec((MIX_WIDTH, MIX_WIDTH), const)],
        out_specs=[pl.BlockSpec((TM, N_MIXERS * MIX_WIDTH), lambda i: (i, 0)),
                   pl.BlockSpec((TM, N_MIXERS * KV_WIDTH), lambda i: (i, 0)),
                   pl.BlockSpec((TM, N_MIXERS * KV_WIDTH), lambda i: (i, 0)),
                   pl.BlockSpec((TM, N_MIXERS * N_KV_HEADS * LANES), lambda i: (i, 0))],
        out_shape=[jax.ShapeDtypeStruct((R, N_MIXERS * MIX_WIDTH), BF16),
                   jax.ShapeDtypeStruct((R, N_MIXERS * KV_WIDTH), BF16),
                   jax.ShapeDtypeStruct((R, N_MIXERS * KV_WIDTH), BF16),
                   jax.ShapeDtypeStruct((R, N_MIXERS * N_KV_HEADS * LANES), BF16)],
        compiler_params=_cparams(("parallel",)),
        name="inproj",
    )(h, w_qkv, cos_tab, sin_tab, q_gain, k_gain, ones_bd)


def _group_queries(q, j):
    lane = lax.broadcasted_iota(jnp.int32, (q.shape[0], LANES), 1)
    keep = (lane < HEAD_DIM) if j == 0 else (lane >= HEAD_DIM)
    parts = []
    for hh in range(GROUP):
        h = GROUP * j + hh
        slot = q[:, (h // 2) * LANES:(h // 2 + 1) * LANES]
        if h % 2 != j:
            slot = pltpu.roll(slot, HEAD_DIM, 1)
        parts.append(jnp.where(keep, slot, 0.0))
    return jnp.concatenate(parts, axis=0).astype(BF16)


def _ungroup_outputs(out, j, T):
    lane = lax.broadcasted_iota(jnp.int32, (T, LANES), 1)
    lo = lane < HEAD_DIM
    slots = []
    for s in range(2):
        even = out[(2 * s) * T:(2 * s + 1) * T]
        odd = out[(2 * s + 1) * T:(2 * s + 2) * T]
        if j == 0:
            slots.append(jnp.where(lo, even, pltpu.roll(odd, HEAD_DIM, 1)))
        else:
            slots.append(jnp.where(lo, pltpu.roll(even, HEAD_DIM, 1), odd))
    return jnp.concatenate(slots, axis=1)


def _head_rows(vals, T):
    return jnp.concatenate([jnp.broadcast_to(v, (T, v.shape[-1])) for v in vals], axis=0)


def _global_kernel(q_ref, k_ref, v_ref, km_ref, vm_ref, *rest, S):
    o_ref = rest[-1]
    T = q_ref.shape[0]
    q = q_ref[...].astype(F32)
    km = km_ref[...]
    qs, state = [], []
    for j in range(N_KV_HEADS):
        qj = _group_queries(q, j)
        s_m = _dot_nt(qj, km)
        m0 = jnp.max(s_m, axis=-1, keepdims=True)
        p_m = jnp.exp2((s_m - m0).astype(BF16))
        qs.append(qj)
        state.append((m0, _dot(p_m, vm_ref[:, j * LANES:(j + 1) * LANES])))
    for c in range(S // TK_GLOBAL):
        kc = k_ref[c * TK_GLOBAL:(c + 1) * TK_GLOBAL, :]
        for j in range(N_KV_HEADS):
            m, acc = state[j]
            s = _dot_nt(qs[j], kc)
            m_new = jnp.maximum(m, jnp.max(s, axis=-1, keepdims=True))
            p = jnp.exp2((s - m_new).astype(BF16))
            vc = v_ref[c * TK_GLOBAL:(c + 1) * TK_GLOBAL, j * LANES:(j + 1) * LANES]
            state[j] = (m_new, jnp.exp2(m - m_new) * acc + _dot(p, vc))
    for j in range(N_KV_HEADS):
        acc = state[j][1]
        out = acc / pltpu.roll(acc, HEAD_DIM, 1)
        o_ref[:, j * 2 * LANES:(j + 1) * 2 * LANES] = _ungroup_outputs(out, 0, T).astype(BF16)


def _attend(qj, k, v_ones, bias, floor_logit=None):
    s = _dot_nt(qj, k) + bias
    m = jnp.max(s, axis=-1, keepdims=True)
    if floor_logit is not None:
        m = jnp.maximum(m, floor_logit)
    acc = _dot(jnp.exp2((s - m).astype(BF16)), v_ones)
    den = pltpu.roll(acc, HEAD_DIM, 1)
    if floor_logit is not None:
        den = den + jnp.exp2(floor_logit - m)
    return acc / den


def _window_kernel(sink_ref, q_ref, kp_ref, kc_ref, kn_ref, vp_ref, vc_ref, vn_ref, km_ref, vm_ref,
                   bband_ref, bmeta_ref, *rest, nb):
    o_ref = rest[-1]
    i = pl.program_id(1)
    T = q_ref.shape[0]
    q = q_ref[...].astype(F32)
    kband = jnp.concatenate([kp_ref[...], kc_ref[...], kn_ref[...]], axis=0)
    vband = jnp.concatenate([vp_ref[...], vc_ref[...], vn_ref[...]], axis=0)
    col = lax.broadcasted_iota(jnp.int32, (1, 3 * BLOCK), 1)
    in_range = ((col >= BLOCK) | (i > 0)) & ((col < 2 * BLOCK) | (i < nb - 1))
    for j in range(N_KV_HEADS):
        qj = _group_queries(q, j)
        hs = slice(GROUP * j, GROUP * (j + 1))
        s_b = _dot_nt(qj, kband) + bband_ref[hs].reshape(GROUP * T, 3 * BLOCK)
        s_b = jnp.where(in_range, s_b, NEG_INF)
        s_m = _dot_nt(qj, km_ref[...]) + bmeta_ref[hs].reshape(GROUP * T, N_META)
        sink = jnp.concatenate([jnp.full((T, 1), sink_ref[GROUP * j + hh], F32) for hh in range(GROUP)], axis=0)
        m = jnp.maximum(jnp.maximum(jnp.max(s_b, axis=-1, keepdims=True),
                                    jnp.max(s_m, axis=-1, keepdims=True)), sink)
        e_b = jnp.exp2(s_b - m)
        e_m = jnp.exp2(s_m - m)
        denom = (jnp.sum(e_b, axis=-1, keepdims=True) + jnp.sum(e_m, axis=-1, keepdims=True)
                 + jnp.exp2(sink - m))
        acc = _dot(e_b.astype(BF16), vband) + _dot(e_m.astype(BF16), vm_ref[...])
        o_ref[:, j * 2 * LANES:(j + 1) * 2 * LANES] = _ungroup_outputs(acc / denom, j, T).astype(BF16)


def _na_kernel(q_ref, k_ref, v_ref, km_ref, vm_ref, bias_ref, mbias_ref, *rest, rows):
    o_ref = rest[-1]
    blk = pl.program_id(1)
    W = GRID_W
    nkeys = NA_ROWS * W
    km = km_ref[...]
    vm = vm_ref[...]
    for j in range(N_KV_HEADS):
        mb = _head_rows([mbias_ref[GROUP * j + hh:GROUP * j + hh + 1, :] for hh in range(GROUP)], W)

        def row_body(rr, carry, j=j, mb=mb):
            r = blk * NA_QROWS + rr
            rs = jnp.clip(r - NA_ROWS // 2, 0, rows - NA_ROWS)
            delta = r - rs
            qoff = pl.multiple_of(rr * W, W)
            koff = pl.multiple_of(rs * W, W)
            qj = _group_queries(q_ref[pl.ds(qoff, W), :].astype(F32), j)
            keys = jnp.concatenate([k_ref[pl.ds(koff, nkeys), :], km], axis=0)
            vals = jnp.concatenate([v_ref[pl.ds(koff, nkeys), j * LANES:(j + 1) * LANES],
                                    vm[:, j * LANES:(j + 1) * LANES]], axis=0)
            bias = jnp.concatenate(
                [bias_ref[delta, pl.ds(GROUP * j, GROUP)].reshape(GROUP * W, nkeys), mb], axis=1)
            out = _attend(qj, keys, vals, bias)
            o_ref[pl.ds(qoff, W), j * 2 * LANES:(j + 1) * 2 * LANES] = _ungroup_outputs(out, 0, W).astype(BF16)
            return carry

        lax.fori_loop(0, NA_QROWS, row_body, 0, unroll=True)


def _meta_kernel(sink_ref, q_ref, ka_ref, va_ref, kb_ref, vb_ref, km_ref, vm_ref,
                 bq_meta_ref, bq_blk_ref, mbias_ref, o_in_ref, o_ref):
    del o_in_ref
    T = N_META
    q = q_ref[...].astype(F32)
    km_all = km_ref[...]
    vm_all = vm_ref[...]

    def finish(n, j, acc, denom):
        lo = n * MIX_WIDTH + j * 2 * LANES
        o_ref[:, lo:lo + 2 * LANES] = _ungroup_outputs(acc / denom, j, T).astype(BF16)

    for j in range(N_KV_HEADS):
        hs = slice(GROUP * j, GROUP * (j + 1))
        qj = _group_queries(q[:, 0:MIX_WIDTH], j)
        km, vm = km_all[:, 0:KV_WIDTH], vm_all[:, 0:KV_WIDTH]
        s_r = _dot_nt(qj, ka_ref[...])
        s_m = _dot_nt(qj, km)
        m = jnp.maximum(jnp.max(s_r, axis=-1, keepdims=True), jnp.max(s_m, axis=-1, keepdims=True))
        e_r = jnp.exp2(s_r - m)
        e_m = jnp.exp2(s_m - m)
        denom = jnp.sum(e_r, axis=-1, keepdims=True) + jnp.sum(e_m, axis=-1, keepdims=True)
        finish(0, j, _dot(e_r.astype(BF16), va_ref[...]) + _dot(e_m.astype(BF16), vm), denom)
        qj = _group_queries(q[:, MIX_WIDTH:2 * MIX_WIDTH], j)
        km, vm = km_all[:, KV_WIDTH:2 * KV_WIDTH], vm_all[:, KV_WIDTH:2 * KV_WIDTH]
        s_r = _dot_nt(qj, kb_ref[...]) + bq_blk_ref[hs].reshape(GROUP * T, BLOCK)
        s_m = _dot_nt(qj, km) + bq_meta_ref[hs].reshape(GROUP * T, N_META)
        sink = jnp.concatenate([jnp.full((T, 1), sink_ref[GROUP * j + hh], F32) for hh in range(GROUP)], axis=0)
        m = jnp.maximum(jnp.maximum(jnp.max(s_r, axis=-1, keepdims=True),
                                    jnp.max(s_m, axis=-1, keepdims=True)), sink)
        e_r = jnp.exp2(s_r - m)
        e_m = jnp.exp2(s_m - m)
        denom = (jnp.sum(e_r, axis=-1, keepdims=True) + jnp.sum(e_m, axis=-1, keepdims=True)
                 + jnp.exp2(sink - m))
        finish(1, j, _dot(e_r.astype(BF16), vb_ref[...]) + _dot(e_m.astype(BF16), vm), denom)
        qj = _group_queries(q[:, 2 * MIX_WIDTH:3 * MIX_WIDTH], j)
        km, vm = km_all[:, 2 * KV_WIDTH:3 * KV_WIDTH], vm_all[:, 2 * KV_WIDTH:3 * KV_WIDTH]
        mb = _head_rows([mbias_ref[GROUP * j + hh:GROUP * j + hh + 1, :] for hh in range(GROUP)], T)
        s_m = _dot_nt(qj, km) + mb
        m = jnp.max(s_m, axis=-1, keepdims=True)
        e_m = jnp.exp2(s_m - m)
        finish(2, j, _dot(e_m.astype(BF16), vm), jnp.sum(e_m, axis=-1, keepdims=True))


class _Group:
    def __init__(self, B, S, real_base, meta_batch_base, meta_base, n_meta_blocks):
        self.B, self.S = B, S
        self.real_base = real_base
        self.meta_blk0 = meta_base // N_META + meta_batch_base
        self.n_meta_blocks = n_meta_blocks
        assert real_base % S == 0 and S % TM == 0 and meta_base % N_META == 0


def _alias_args(o_prev, n_inputs):
    if o_prev is None:
        return [], [], {}
    return [o_prev], [pl.BlockSpec(memory_space=pl.ANY)], {n_inputs: 0}


def _global_attn(grp, q_all, k_all, v_ones, o_prev):
    B, S = grp.B, grp.S
    nq = S // TQ_GLOBAL
    qb0 = grp.real_base // TQ_GLOBAL
    sb0 = grp.real_base // S
    mb0 = grp.meta_blk0
    in_specs = [pl.BlockSpec((TQ_GLOBAL, MIX_WIDTH), lambda b, i: (qb0 + b * nq + i, 0)),
                pl.BlockSpec((S, KV_WIDTH), lambda b, i: (sb0 + b, 0)),
                pl.BlockSpec((S, N_KV_HEADS * LANES), lambda b, i: (sb0 + b, 0)),
                pl.BlockSpec((N_META, KV_WIDTH), lambda b, i: (mb0 + b, 0)),
                pl.BlockSpec((N_META, N_KV_HEADS * LANES), lambda b, i: (mb0 + b, 0))]
    extra, extra_specs, aliases = _alias_args(o_prev, len(in_specs))
    return pl.pallas_call(
        functools.partial(_global_kernel, S=S),
        grid=(B, nq),
        in_specs=in_specs + extra_specs,
        out_specs=pl.BlockSpec((TQ_GLOBAL, MIX_WIDTH), lambda b, i: (qb0 + b * nq + i, 0)),
        out_shape=jax.ShapeDtypeStruct((q_all.shape[0], N_MIXERS * MIX_WIDTH), BF16),
        input_output_aliases=aliases,
        compiler_params=_cparams(("parallel", "arbitrary")),
        name="mixer_global",
    )(q_all, k_all, v_ones, k_all, v_ones, *extra)


def _window_attn(grp, q_all, k_all, v_all, sink, bband, bmeta, o_prev):
    B, S = grp.B, grp.S
    nb = S // BLOCK
    qb0 = grp.real_base // BLOCK
    mb0 = grp.meta_blk0
    cur = lambda b, i, sink: (qb0 + b * nb + i, 1)
    prv = lambda b, i, sink: (qb0 + b * nb + jnp.maximum(i - 1, 0), 1)
    nxt = lambda b, i, sink: (qb0 + b * nb + jnp.minimum(i + 1, nb - 1), 1)
    met = lambda b, i, sink: (mb0 + b, 1)
    kv = lambda im: pl.BlockSpec((BLOCK, KV_WIDTH), im)
    in_specs = [pl.BlockSpec((BLOCK, MIX_WIDTH), cur),
                kv(prv), kv(cur), kv(nxt), kv(prv), kv(cur), kv(nxt),
                pl.BlockSpec((N_META, KV_WIDTH), met), pl.BlockSpec((N_META, KV_WIDTH), met),
                pl.BlockSpec((N_HEADS, BLOCK, 3 * BLOCK), lambda b, i, sink: (0, 0, 0)),
                pl.BlockSpec((N_HEADS, BLOCK, N_META), lambda b, i, sink: (0, i, 0))]
    extra, extra_specs, aliases = _alias_args(o_prev, len(in_specs) + 1)
    return pl.pallas_call(
        functools.partial(_window_kernel, nb=nb),
        grid_spec=pltpu.PrefetchScalarGridSpec(
            num_scalar_prefetch=1,
            grid=(B, nb),
            in_specs=in_specs + extra_specs,
            out_specs=pl.BlockSpec((BLOCK, MIX_WIDTH), cur)),
        out_shape=jax.ShapeDtypeStruct((q_all.shape[0], N_MIXERS * MIX_WIDTH), BF16),
        input_output_aliases=aliases,
        compiler_params=_cparams(("parallel", "arbitrary")),
        name="mixer_window",
    )(sink, q_all, k_all, k_all, k_all, v_all, v_all, v_all, k_all, v_all, bband, bmeta, *extra)


def _na_attn(grp, q_all, k_all, v_all, na_bias, na_mbias, o_prev):
    B, S = grp.B, grp.S
    rows = S // GRID_W
    tq = NA_QROWS * GRID_W
    nq = S // tq
    qb0 = grp.real_base // tq
    sb0 = grp.real_base // S
    mb0 = grp.meta_blk0
    in_specs = [pl.BlockSpec((tq, MIX_WIDTH), lambda b, i: (qb0 + b * nq + i, 2)),
                pl.BlockSpec((S, KV_WIDTH), lambda b, i: (sb0 + b, 2)),
                pl.BlockSpec((S, N_KV_HEADS * LANES), lambda b, i: (sb0 + b, 2)),
                pl.BlockSpec((N_META, KV_WIDTH), lambda b, i: (mb0 + b, 2)),
                pl.BlockSpec((N_META, N_KV_HEADS * LANES), lambda b, i: (mb0 + b, 2)),
                pl.BlockSpec(na_bias.shape, lambda b, i: (0, 0, 0, 0)),
                pl.BlockSpec(na_mbias.shape, lambda b, i: (0, 0))]
    extra, extra_specs, aliases = _alias_args(o_prev, len(in_specs))
    return pl.pallas_call(
        functools.partial(_na_kernel, rows=rows),
        grid=(B, nq),
        in_specs=in_specs + extra_specs,
        out_specs=pl.BlockSpec((tq, MIX_WIDTH), lambda b, i: (qb0 + b * nq + i, 2)),
        out_shape=jax.ShapeDtypeStruct((q_all.shape[0], N_MIXERS * MIX_WIDTH), BF16),
        input_output_aliases=aliases,
        compiler_params=_cparams(("parallel", "arbitrary")),
        name="mixer_neighbourhood",
    )(q_all, k_all, v_all, k_all, v_all, na_bias, na_mbias, *extra)


def _meta_attn(grp, q_all, k_all, v_all, sink, bq_meta, bq_blk, na_mbias, o_prev):
    B, S = grp.B, grp.S
    sb0 = grp.real_base // S
    bb0 = grp.real_base // BLOCK
    nb = S // BLOCK
    mb0 = grp.meta_blk0
    clamp = lambda b: jnp.minimum(b, B - 1)
    mrow = lambda b, sink: (mb0 + clamp(b), 0)
    in_specs = [pl.BlockSpec((N_META, N_MIXERS * MIX_WIDTH), mrow),
                pl.BlockSpec((S, KV_WIDTH), lambda b, sink: (sb0 + clamp(b), 0)),
                pl.BlockSpec((S, KV_WIDTH), lambda b, sink: (sb0 + clamp(b), 0)),
                pl.BlockSpec((BLOCK, KV_WIDTH), lambda b, sink: (bb0 + clamp(b) * nb, 1)),
                pl.BlockSpec((BLOCK, KV_WIDTH), lambda b, sink: (bb0 + clamp(b) * nb, 1)),
                pl.BlockSpec((N_META, N_MIXERS * KV_WIDTH), mrow),
                pl.BlockSpec((N_META, N_MIXERS * KV_WIDTH), mrow),
                pl.BlockSpec(bq_meta.shape, lambda b, sink: (0, 0, 0)),
                pl.BlockSpec(bq_blk.shape, lambda b, sink: (0, 0, 0)),
                pl.BlockSpec(na_mbias.shape, lambda b, sink: (0, 0)),
                pl.BlockSpec(memory_space=pl.ANY)]
    return pl.pallas_call(
        _meta_kernel,
        grid_spec=pltpu.PrefetchScalarGridSpec(
            num_scalar_prefetch=1,
            grid=(grp.n_meta_blocks,),
            in_specs=in_specs,
            out_specs=pl.BlockSpec((N_META, N_MIXERS * MIX_WIDTH), lambda b, sink: (mb0 + b, 0))),
        out_shape=jax.ShapeDtypeStruct((q_all.shape[0], N_MIXERS * MIX_WIDTH), BF16),
        input_output_aliases={len(in_specs): 0},
        compiler_params=_cparams(("arbitrary",)),
        name="mixer_meta_queries",
    )(sink, q_all, k_all, v_all, k_all, v_all, k_all, v_all, bq_meta, bq_blk, na_mbias, o_prev)


def _route(h1, wr_hi, wr_lo, rbias):
    T = h1.shape[0]
    x_hi, x_lo = _split_bf16(h1)
    logits = _dot_nt(wr_hi, x_hi) + _dot_nt(wr_hi, x_lo) + _dot_nt(wr_lo, x_hi)
    scores = 1.0 / (1.0 + jnp.exp(-logits))
    sel = scores + rbias
    per_group = N_EXPERTS // N_EXPERT_GROUPS
    sel3 = sel.reshape(N_EXPERT_GROUPS, per_group, T)
    idx3 = lax.broadcasted_iota(jnp.int32, sel3.shape, 1).astype(F32)
    m1 = jnp.max(sel3, axis=1, keepdims=True)
    first = jnp.min(jnp.where(sel3 == m1, idx3, float(per_group)), axis=1, keepdims=True)
    m2 = jnp.max(jnp.where(idx3 == first, -jnp.inf, sel3), axis=1, keepdims=True)
    gscore = (m1 + m2).reshape(N_EXPERT_GROUPS, T)

    def rank_of(vals):
        idx = lax.broadcasted_iota(jnp.int32, vals.shape, 0)
        rank = jnp.zeros(vals.shape, F32)
        for r in range(vals.shape[0]):
            row = vals[r:r + 1, :]
            ge = jnp.where(row >= vals, 1.0, 0.0)
            gt = jnp.where(row > vals, 1.0, 0.0)
            rank = rank + jnp.where(idx > r, ge, gt)
        return rank

    gkeep = jnp.where(rank_of(gscore) < TOPK_GROUPS, 1.0, 0.0)
    ekeep = jnp.broadcast_to(gkeep.reshape(N_EXPERT_GROUPS, 1, T), sel3.shape).reshape(N_EXPERTS, T)
    masked = jnp.where(ekeep > 0.5, sel, NEG_INF)
    eidx = lax.broadcasted_iota(jnp.int32, masked.shape, 0).astype(F32)
    chosen = jnp.zeros(masked.shape, F32)
    for _ in range(TOP_K):
        best = jnp.max(masked, axis=0, keepdims=True)
        first = jnp.min(jnp.where(masked == best, eidx, float(N_EXPERTS)), axis=0, keepdims=True)
        hit = eidx == first
        chosen = jnp.where(hit, 1.0, chosen)
        masked = jnp.where(hit, -jnp.inf, masked)
    w = jnp.where(chosen > 0.5, scores, 0.0)
    return w / jnp.sum(w, axis=0, keepdims=True) * ROUTED_SCALE


def _merge_kernel(h_ref, o_ref, wg_ref, wb_ref, wo_ref, g_ref, b_ref, wrh_ref, wrl_ref, rb_ref,
                  h1_ref, gates_ref, *, alpha):
    h = h_ref[...]
    x = h.astype(BF16)
    merged = None
    for n in range(N_MIXERS):
        logit = _dot(x, wg_ref[:, n * D_MODEL:(n + 1) * D_MODEL])
        branch = _dot(o_ref[:, n * MIX_WIDTH:(n + 1) * MIX_WIDTH], wb_ref[n])
        term = branch / (1.0 + jnp.exp(-logit))
        merged = term if merged is None else merged + term
    mix = _dot(merged.astype(BF16), wo_ref[...])
    h1 = _layer_norm(alpha * h + mix, g_ref[...], b_ref[...])
    h1_ref[...] = h1
    gates_t = _route(h1, wrh_ref[...], wrl_ref[...], rb_ref[...])
    pad = jnp.zeros((LANES - N_EXPERTS, gates_t.shape[1]), F32)
    gates_ref[...] = jnp.concatenate([gates_t, pad], axis=0).T


def _merge(h, o_all, w_gate, w_branch, w_out, ln_g, ln_b, wr_hi, wr_lo, rbias, alpha):
    R = h.shape[0]
    c2 = lambda i: (0, 0)
    return pl.pallas_call(
        functools.partial(_merge_kernel, alpha=alpha),
        grid=(R // TM,),
        in_specs=[pl.BlockSpec((TM, D_MODEL), lambda i: (i, 0)),
                  pl.BlockSpec((TM, N_MIXERS * MIX_WIDTH), lambda i: (i, 0)),
                  pl.BlockSpec(w_gate.shape, c2),
                  pl.BlockSpec(w_branch.shape, lambda i: (0, 0, 0)),
                  pl.BlockSpec(w_out.shape, c2),
                  pl.BlockSpec((1, D_MODEL), c2),
                  pl.BlockSpec((1, D_MODEL), c2),
                  pl.BlockSpec(wr_hi.shape, c2),
                  pl.BlockSpec(wr_lo.shape, c2),
                  pl.BlockSpec(rbias.shape, c2)],
        out_specs=[pl.BlockSpec((TM, D_MODEL), lambda i: (i, 0)),
                   pl.BlockSpec((TM, LANES), lambda i: (i, 0))],
        out_shape=[jax.ShapeDtypeStruct((R, D_MODEL), F32),
                   jax.ShapeDtypeStruct((R, LANES), F32)],
        compiler_params=_cparams(("parallel",)),
        name="merge_ln_route",
    )(h, o_all, w_gate, w_branch, w_out, ln_g, ln_b, wr_hi, wr_lo, rbias)


TD = 256
CH = 16
SLOTS = 3072
NCH = SLOTS // CH
MT = 512
ME = 1024
CPM = ME // CH
XBUFS = 3


def _swiglu_act(gu):
    g = gu[:, :D_EXPERT]
    return g / (1.0 + jnp.exp(-g)) * gu[:, D_EXPERT:]


def _slot_of_token(gates, lo_row):
    routed = gates > 0.0
    r = lax.broadcasted_iota(jnp.int32, (TD, TD), 0)
    c = lax.broadcasted_iota(jnp.int32, (TD, TD), 1)
    earlier = jnp.where(c < r, 1.0, 0.0).astype(BF16)
    rank = _dot(earlier, jnp.where(routed, 1.0, 0.0).astype(BF16))
    return jnp.where(routed, lo_row + rank + 1.0, 0.0)


def _split64(x):
    a = jnp.floor(x * (1.0 / 64.0))
    return a.astype(BF16), (x - 64.0 * a).astype(BF16)


def _dispatch_kernel(h_ref, g_ref, lohi_ref, x_ref, w_ref):
    lohi = lohi_ref[0]
    lo_row, hi_row = lohi[0:1], lohi[1:2]
    gates = g_ref[...]
    a, b = _split64(_slot_of_token(gates, lo_row).T)
    gates_t = gates.T.astype(BF16)
    x = h_ref[...].astype(BF16)
    for blk in range(SLOTS // MT):
        s = (lax.broadcasted_iota(jnp.int32, (MT, LANES), 0) + blk * MT).astype(F32)
        owner = jnp.where(s >= lo_row, jnp.where(s < hi_row, 1.0, 0.0), 0.0).astype(BF16)
        want = 64.0 * _dot(owner, a) + _dot(owner, b)
        s1 = (lax.broadcasted_iota(jnp.int32, (MT, TD), 0) + (blk * MT + 1)).astype(F32)
        hit = want == s1
        x_ref[blk * MT:(blk + 1) * MT, :] = _dot(jnp.where(hit, 1.0, 0.0).astype(BF16), x).astype(BF16)
        weight = jnp.where(hit, _dot(owner, gates_t), 0.0)
        w_ref[:, blk * MT:(blk + 1) * MT] = weight.T.astype(BF16)


def _dispatch(h1, gates, lohi):
    n = h1.shape[0] // TD
    return pl.pallas_call(
        _dispatch_kernel,
        grid=(n,),
        in_specs=[pl.BlockSpec((TD, D_MODEL), lambda i: (i, 0)),
                  pl.BlockSpec((TD, LANES), lambda i: (i, 0)),
                  pl.BlockSpec((1, 8, LANES), lambda i: (i, 0, 0))],
        out_specs=[pl.BlockSpec((SLOTS, D_MODEL), lambda i: (i, 0)),
                   pl.BlockSpec((TD, SLOTS), lambda i: (i, 0))],
        out_shape=[jax.ShapeDtypeStruct((n * SLOTS, D_MODEL), BF16),
                   jax.ShapeDtypeStruct((n * TD, SLOTS), BF16)],
        compiler_params=_cparams(("parallel",)),
        name="moe_dispatch",
    )(h1, gates, lohi)


def _chunk_gather(table_ref, first, n_chunks, src_hbm, buf, sem):
    copies = []
    for c in range(n_chunks):
        row = pl.multiple_of(table_ref[first + c] * CH, CH)
        copies.append(pltpu.make_async_copy(src_hbm.at[pl.ds(row, CH)], buf.at[pl.ds(c * CH, CH)], sem))
    return copies


def _chunk_wait(n_chunks, src_hbm, buf, sem):
    for c in range(n_chunks):
        pltpu.make_async_copy(src_hbm.at[pl.ds(0, CH)], buf.at[pl.ds(c * CH, CH)], sem).wait()


def _expert_kernel(te_ref, src_ref, nu_ref, x_hbm, wgu_ref, wd_ref, y_ref, xbuf, sem):
    del te_ref
    m = pl.program_id(0)
    n_used = nu_ref[0]

    def start(step):
        slot = step % XBUFS
        for cp in _chunk_gather(src_ref, step * CPM, CPM, x_hbm, xbuf.at[slot], sem.at[slot]):
            cp.start()

    for ahead in range(XBUFS - 1):
        @pl.when((m == 0) & (ahead < n_used))
        def _(ahead=ahead):
            start(ahead)

    @pl.when(m + (XBUFS - 1) < n_used)
    def _():
        start(m + (XBUFS - 1))

    @pl.when(m < n_used)
    def _():
        slot = m % XBUFS
        _chunk_wait(CPM, x_hbm, xbuf.at[slot], sem.at[slot])
        act = _swiglu_act(_dot(xbuf[slot], wgu_ref[0]))
        y_ref[...] = _dot(act.astype(BF16), wd_ref[0]).astype(BF16)


def _experts(x_disp, wgu, wd, tile_expert, src_chunk, n_used):
    n_steps = tile_expert.shape[0]
    return pl.pallas_call(
        _expert_kernel,
        grid_spec=pltpu.PrefetchScalarGridSpec(
            num_scalar_prefetch=3,
            grid=(n_steps,),
            in_specs=[pl.BlockSpec(memory_space=pl.ANY),
                      pl.BlockSpec((1, D_MODEL, 2 * D_EXPERT), lambda m, te, src, nu: (te[m], 0, 0)),
                      pl.BlockSpec((1, D_EXPERT, D_MODEL), lambda m, te, src, nu: (te[m], 0, 0))],
            out_specs=pl.BlockSpec((ME, D_MODEL), lambda m, te, src, nu: (jnp.minimum(m, nu[0] - 1), 0)),
            scratch_shapes=[pltpu.VMEM((XBUFS, ME, D_MODEL), BF16), pltpu.SemaphoreType.DMA((XBUFS,))]),
        out_shape=jax.ShapeDtypeStruct((n_steps * ME, D_MODEL), BF16),
        compiler_params=_cparams(("arbitrary",)),
        name="moe_experts",
    )(tile_expert, src_chunk, n_used, x_disp, wgu, wd)


def _combine_kernel(dst_ref, y_hbm, w_ref, h_ref, wsgu_ref, wsd_ref, lg_ref, lb_ref, o_ref, ybuf, sem, *, alpha):
    i = pl.program_id(0)

    def start(tile):
        slot = tile % 2
        for cp in _chunk_gather(dst_ref, tile * NCH, NCH, y_hbm, ybuf.at[slot], sem.at[slot]):
            cp.start()

    @pl.when(i == 0)
    def _():
        start(i)

    @pl.when(i + 1 < pl.num_programs(0))
    def _():
        start(i + 1)

    h = h_ref[...]
    shared = _dot(_swiglu_act(_dot(h.astype(BF16), wsgu_ref[...])).astype(BF16), wsd_ref[...])
    slot = i % 2
    _chunk_wait(NCH, y_hbm, ybuf.at[slot], sem.at[slot])
    routed = _dot(w_ref[...], ybuf[slot])
    o_ref[...] = _layer_norm(alpha * h + shared + routed, lg_ref[...], lb_ref[...])


def _combine(y_sorted, w_t, h1, dst_chunk, wsgu, wsd, ln_g, ln_b, alpha):
    n = h1.shape[0] // TD
    c2 = lambda i, dst: (0, 0)
    return pl.pallas_call(
        functools.partial(_combine_kernel, alpha=alpha),
        grid_spec=pltpu.PrefetchScalarGridSpec(
            num_scalar_prefetch=1,
            grid=(n,),
            in_specs=[pl.BlockSpec(memory_space=pl.ANY),
                      pl.BlockSpec((TD, SLOTS), lambda i, dst: (i, 0)),
                      pl.BlockSpec((TD, D_MODEL), lambda i, dst: (i, 0)),
                      pl.BlockSpec(wsgu.shape, c2),
                      pl.BlockSpec(wsd.shape, c2),
                      pl.BlockSpec((1, D_MODEL), c2),
                      pl.BlockSpec((1, D_MODEL), c2)],
            out_specs=pl.BlockSpec((TD, D_MODEL), lambda i, dst: (i, 0)),
            scratch_shapes=[pltpu.VMEM((2, SLOTS, D_MODEL), BF16), pltpu.SemaphoreType.DMA((2,))]),
        out_shape=jax.ShapeDtypeStruct((h1.shape[0], D_MODEL), F32),
        compiler_params=_cparams(("arbitrary",)),
        name="moe_combine_ln",
    )(dst_chunk, y_sorted, w_t, h1, wsgu, wsd, ln_g, ln_b)


def _routing_tables(gates):
    n = gates.shape[0] // TD
    cnt = jnp.sum((gates[:, :N_EXPERTS] > 0.0).reshape(n, TD, N_EXPERTS), axis=1, dtype=jnp.int32)
    nch = (cnt + (CH - 1)) // CH
    hi16 = jnp.cumsum(nch, axis=1)
    lo16 = hi16 - nch
    nct = hi16[:, -1:]
    pad = jnp.broadcast_to(nct, (n, LANES - N_EXPERTS))
    lohi = jnp.zeros((n, 8, LANES), F32)
    lohi = lohi.at[:, 0, :].set((jnp.concatenate([lo16, pad], axis=1) * CH).astype(F32))
    lohi = lohi.at[:, 1, :].set((jnp.concatenate([hi16, pad], axis=1) * CH).astype(F32))
    tot = jnp.sum(nch, axis=0)
    seg_len = (tot + (CPM - 1)) // CPM * CPM
    seg_end = jnp.cumsum(seg_len)
    seg_start = seg_end - seg_len
    gpos = seg_start[None, :] + jnp.cumsum(nch, axis=0) - nch
    n_steps = (n * NCH + N_EXPERTS * CPM) // CPM
    n_used = (seg_end[-1] // CPM).astype(jnp.int32).reshape(1)
    step = jnp.arange(n_steps, dtype=jnp.int32)
    tile_expert = jnp.sum(seg_end[None, :] // CPM <= jnp.minimum(step, n_used - 1)[:, None], axis=1, dtype=jnp.int32)
    tile_expert = jnp.minimum(tile_expert, N_EXPERTS - 1)
    exact = functools.partial(jnp.dot, precision=lax.Precision.HIGHEST)
    experts = jnp.arange(N_EXPERTS, dtype=jnp.int32)
    g = jnp.arange(n_steps * CPM, dtype=jnp.int32)
    e_of_g = jnp.minimum(jnp.sum(seg_end[None, :] <= g[:, None], axis=1, dtype=jnp.int32), N_EXPERTS - 1)
    pick_e = (e_of_g[:, None] == experts[None, :]).astype(F32)
    first = exact(pick_e, gpos.T.astype(F32))
    count = exact(pick_e, nch.T.astype(F32))
    base = exact(pick_e, (jnp.arange(n, dtype=jnp.int32)[:, None] * NCH + lo16).T.astype(F32))
    gf = g.astype(F32)[:, None]
    inside = (first <= gf) & (gf < first + count)
    src_chunk = jnp.sum(jnp.where(inside, base + gf - first, 0.0), axis=1).astype(jnp.int32)
    k = jnp.arange(NCH, dtype=jnp.int32)
    e_of_k = jnp.minimum(jnp.sum(hi16[:, None, :] <= k[None, :, None], axis=2, dtype=jnp.int32), N_EXPERTS - 1)
    pick_k = e_of_k[:, :, None] == experts[None, None, :]
    pos = jnp.sum(jnp.where(pick_k, (gpos - lo16)[:, None, :], 0), axis=2) + k[None, :]
    dst_chunk = jnp.where(k[None, :] < nct, pos, 0).astype(jnp.int32).reshape(-1)
    return lohi, tile_expert, src_chunk, n_used, dst_chunk


def _moe(h1, gates, wgu, wd, wsgu, wsd, ln_g, ln_b, alpha):
    lohi, tile_expert, src_chunk, n_used, dst_chunk = _routing_tables(gates)
    x_disp, w_t = _dispatch(h1, gates, lohi)
    y_sorted = _experts(x_disp, wgu, wd, tile_expert, src_chunk, n_used)
    return _combine(y_sorted, w_t, h1, dst_chunk, wsgu, wsd, ln_g, ln_b, alpha)


def _t5_bucket(rel):
    half = T5_BUCKETS // 2
    max_exact = half // 2
    n = np.abs(rel)
    ratio = np.log(np.maximum(n, 1).astype(np.float32) / np.float32(max_exact))
    ratio = ratio / np.float32(math.log(T5_MAX_DIST / max_exact)) * np.float32(half - max_exact)
    large = np.minimum(max_exact + ratio.astype(np.int32), half - 1)
    return np.where(rel > 0, half, 0) + np.where(n < max_exact, n, large)


def _t5_tables(t5_table, s_max):
    def bias(rel, valid):
        b = t5_table[_t5_bucket(rel)].astype(F32)
        return jnp.where(jnp.asarray(valid)[None], jnp.transpose(b, (2, 0, 1)), NEG_INF)

    ii = np.arange(BLOCK)[:, None]
    jj = np.arange(3 * BLOCK)[None, :]
    rel = jj - ii - BLOCK
    bband = bias(rel, np.abs(rel) <= WINDOW)
    t = np.arange(s_max)[:, None]
    m = np.arange(N_META)[None, :]
    bmeta = bias(m - (N_META + t), np.ones((s_max, N_META), bool))
    mpos = np.arange(N_META)[:, None]
    kpos = np.arange(N_META + BLOCK)[None, :]
    relq = kpos - mpos
    bq = bias(relq, (kpos < N_META) | (np.abs(relq) <= WINDOW))
    return bband, bmeta, bq[:, :, :N_META], bq[:, :, N_META:]


def _na_bias_cases(rpb):
    W = GRID_W
    delta = np.arange(NA_ROWS)[:, None, None, None]
    i = np.arange(NA_ROWS)[None, :, None, None]
    c = np.arange(W)[None, None, :, None]
    kc = np.arange(W)[None, None, None, :]
    cs = np.clip(c - NA_COLS // 2, 0, W - NA_COLS)
    valid = (kc >= cs) & (kc < cs + NA_COLS)
    dc = np.clip(kc - c + (NA_COLS - 1), 0, 2 * NA_COLS - 2)[0, 0]
    t = jnp.where(jnp.asarray(valid[0, 0]), rpb.astype(F32)[:, :, dc], NEG_INF)
    cases = [jnp.transpose(t[:, NA_ROWS - 1 - d:2 * NA_ROWS - 1 - d], (0, 2, 1, 3)) for d in range(NA_ROWS)]
    return jnp.stack(cases, axis=0).reshape(NA_ROWS, N_HEADS, W, NA_ROWS * W)


def _rope_tables(s_max, n_meta_rows):
    half = HEAD_DIM // 4
    freq = ROPE_THETA ** (-jnp.arange(half, dtype=F32) / half)
    t = np.arange(s_max)
    mp = np.tile(np.arange(N_META) - N_META, n_meta_rows // N_META)
    pos_row = jnp.asarray(np.concatenate([t // GRID_W, mp]), jnp.int32).astype(F32)
    pos_col = jnp.asarray(np.concatenate([t % GRID_W, mp]), jnp.int32).astype(F32)
    ar = pos_row[:, None] * freq
    ac = pos_col[:, None] * freq
    cos = jnp.concatenate([jnp.cos(ar), jnp.cos(ar), jnp.cos(ac), jnp.cos(ac)], axis=1)
    sin = jnp.concatenate([-jnp.sin(ar), jnp.sin(ar), -jnp.sin(ac), jnp.sin(ac)], axis=1)
    return jnp.tile(cos, (1, 2)), jnp.tile(sin, (1, 2))


def kernel(x_prompt, x_sample, meta_tokens, ln_in_g, ln_in_b, t5_table, w_in, q_gain, k_gain, sink,
           na_rpb, na_meta_bias, w_branch, w_out, ln1_g, ln1_b, w_router, router_bias,
           w_expert_gate_up, w_expert_down, w_shared_gate_up, w_shared_down, ln2_g, ln2_b):
    depth = w_in.shape[0]
    alpha = (2 * depth) ** 0.25
    B0, S0, D = x_prompt.shape
    B1, S1, _ = x_sample.shape
    assert D == D_MODEL
    real = B0 * S0 + B1 * S1
    n_meta_rows = -(-(B0 + B1) * N_META // TM) * TM
    R = real + n_meta_rows
    n_meta_blocks = n_meta_rows // N_META
    g0 = _Group(B0, S0, 0, 0, real, B0)
    g1 = _Group(B1, S1, B0 * S0, B0, real, n_meta_blocks - B0)
    s_max = max(S0, S1)

    x = jnp.concatenate([x_prompt.reshape(B0 * S0, D), x_sample.reshape(B1 * S1, D),
                         jnp.tile(meta_tokens, (n_meta_blocks, 1))], axis=0)
    h = _embed_ln(x, ln_in_g.reshape(1, D), ln_in_b.reshape(1, D))

    cos_tab, sin_tab = _rope_tables(s_max, TM)
    n0, n1 = B0 * S0 // TM, real // TM
    p0, p1, pm = S0 // TM, S1 // TM, s_max // TM

    def pos_block(i):
        return jnp.where(i < n0, i % p0, jnp.where(i < n1, (i - n0) % p1, pm))

    bband, bmeta, bq_meta, bq_blk = (t * LOG2E for t in _t5_tables(t5_table, s_max))
    ones_bd = jnp.asarray(np.kron(np.eye(N_HEADS), np.ones((HEAD_DIM, HEAD_DIM))), BF16)

    qs, ks, vs = [], [], []
    for n in range(N_MIXERS):
        off = n * QKV_WIDTH
        qs.append(w_in[:, :, off:off + MIX_WIDTH])
        ks.append(w_in[:, :, off + MIX_WIDTH:off + MIX_WIDTH + KV_WIDTH])
        vs.append(w_in[:, :, off + MIX_WIDTH + KV_WIDTH:off + QKV_WIDTH])
    wr_t = jnp.swapaxes(w_router, 1, 2)
    wr_hi = wr_t.astype(BF16)
    layers = dict(
        w_qkv=jnp.concatenate(qs + ks + vs, axis=2).astype(BF16),
        w_gate=w_in[:, :, N_MIXERS * QKV_WIDTH:].astype(BF16),
        q_gain=jnp.tile(q_gain, (1, N_HEADS)).reshape(depth, 1, MIX_WIDTH),
        k_gain=jnp.tile(k_gain, (1, N_KV_HEADS)).reshape(depth, 1, KV_WIDTH),
        sink=sink.astype(F32) * LOG2E,
        na_bias=jax.vmap(_na_bias_cases)(na_rpb) * LOG2E,
        na_mbias=na_meta_bias.astype(F32) * LOG2E,
        w_branch=w_branch.astype(BF16),
        w_out=w_out.astype(BF16),
        ln1_g=ln1_g.reshape(depth, 1, D), ln1_b=ln1_b.reshape(depth, 1, D),
        wr_hi=wr_hi, wr_lo=(wr_t - wr_hi.astype(F32)).astype(BF16),
        rbias=router_bias.astype(F32).reshape(depth, N_EXPERTS, 1),
        wgu=w_expert_gate_up.astype(BF16), wd=w_expert_down.astype(BF16),
        wsgu=w_shared_gate_up.astype(BF16), wsd=w_shared_down.astype(BF16),
        ln2_g=ln2_g.reshape(depth, 1, D), ln2_b=ln2_b.reshape(depth, 1, D),
    )

    def layer(h, p):
        q_all, k_all, v_all, v_ones = _inproj(h, p["w_qkv"], cos_tab, sin_tab, p["q_gain"], p["k_gain"],
                                      ones_bd, pos_block)
        o = None
        for grp in (g0, g1):
            o = _global_attn(grp, q_all, k_all, v_ones, o)
            o = _window_attn(grp, q_all, k_all, v_all, p["sink"], bband, bmeta[:, :grp.S], o)
            o = _na_attn(grp, q_all, k_all, v_ones, p["na_bias"], p["na_mbias"], o)
        for grp in (g0, g1):
            o = _meta_attn(grp, q_all, k_all, v_all, p["sink"], bq_meta, bq_blk, p["na_mbias"], o)
        h1, gates = _merge(h, o, p["w_gate"], p["w_branch"], p["w_out"], p["ln1_g"], p["ln1_b"],
                           p["wr_hi"], p["wr_lo"], p["rbias"], alpha)
        h2 = _moe(h1, gates, p["wgu"], p["wd"], p["wsgu"], p["wsd"], p["ln2_g"], p["ln2_b"],
                  alpha)
        return h2, None

    h, _ = lax.scan(layer, h, layers)
    y_prompt = h[:B0 * S0].reshape(B0, S0, D)
    y_sample = h[B0 * S0:real].reshape(B1, S1, D)
    return (y_prompt, y_sample)
```

```python
import functools
import math

import numpy as np
import jax
import jax.numpy as jnp
from jax import lax
from jax.experimental import pallas as pl
from jax.experimental.pallas import tpu as pltpu

F32 = jnp.float32
BF16 = jnp.bfloat16

D_MODEL = 1024
HEAD_DIM = 64
N_HEADS = 8
N_KV_HEADS = 2
GROUP = N_HEADS // N_KV_HEADS
MIX_WIDTH = N_HEADS * HEAD_DIM
KV_WIDTH = N_KV_HEADS * HEAD_DIM
N_MIXERS = 3
QKV_WIDTH = MIX_WIDTH + 2 * KV_WIDTH
N_META = 16
GRID_W = 64
BLOCK = 128
WINDOW = 128
NA_ROWS = 8
NA_COLS = 16
T5_BUCKETS = 32
T5_MAX_DIST = 128
ROPE_THETA = 10000.0
N_EXPERTS = 64
TOP_K = 8
N_EXPERT_GROUPS = 8
TOPK_GROUPS = 4
D_EXPERT = 256
ROUTED_SCALE = 2.5
NEG_INF = -1e30
LOG2E = math.log2(math.e)
LANES = 128

TM = 512
TQ_GLOBAL = 256
TK_GLOBAL = 512
NA_QROWS = 8
VMEM_LIMIT = 56 * 1024 * 1024


def _cparams(sem):
    return pltpu.CompilerParams(dimension_semantics=sem, vmem_limit_bytes=VMEM_LIMIT)


def _dot(a, b):
    return jnp.dot(a, b, preferred_element_type=F32)


def _dot_nt(a, b):
    return lax.dot_general(a, b, (((1,), (1,)), ((), ())), preferred_element_type=F32)


def _split_bf16(x):
    hi = x.astype(BF16)
    lo = (x - hi.astype(F32)).astype(BF16)
    return hi, lo


def _layer_norm(x, g, b):
    mu = jnp.mean(x, axis=-1, keepdims=True)
    xc = x - mu
    var = jnp.mean(xc * xc, axis=-1, keepdims=True)
    return xc * lax.rsqrt(var + 1e-5) * g + b


def _embed_ln_kernel(x_ref, g_ref, b_ref, o_ref):
    o_ref[...] = _layer_norm(x_ref[...], g_ref[...], b_ref[...])


def _embed_ln(x, g, b):
    R = x.shape[0]
    return pl.pallas_call(
        _embed_ln_kernel,
        grid=(R // TM,),
        in_specs=[pl.BlockSpec((TM, D_MODEL), lambda i: (i, 0)),
                  pl.BlockSpec((1, D_MODEL), lambda i: (0, 0)),
                  pl.BlockSpec((1, D_MODEL), lambda i: (0, 0))],
        out_specs=pl.BlockSpec((TM, D_MODEL), lambda i: (i, 0)),
        out_shape=jax.ShapeDtypeStruct((R, D_MODEL), F32),
        compiler_params=_cparams(("parallel",)),
        name="embed_ln",
    )(x, g, b)


def _rope_slot(x, cos, sin_signed, first_half):
    fwd = pltpu.roll(x, LANES - 16, 1)
    bwd = pltpu.roll(x, 16, 1)
    return x * cos + jnp.where(first_half, fwd, bwd) * sin_signed


def _head_rms(x, ones_bd, gain):
    hi, lo = _split_bf16(x * x)
    ss = _dot(hi, ones_bd) + _dot(lo, ones_bd)
    return x * lax.rsqrt(ss * (1.0 / HEAD_DIM) + 1e-6) * gain


def _inproj_kernel(h_ref, w_ref, cos_ref, sin_ref, qg_ref, kg_ref, ones_ref, q_ref, k_ref, v_ref, v1_ref):
    x = h_ref[...].astype(BF16)
    cos = cos_ref[...]
    sin = sin_ref[...]
    lane = lax.broadcasted_iota(jnp.int32, cos.shape, 1)
    first_half = (lane % 32) < 16
    scale = HEAD_DIM ** -0.5
    qw = N_MIXERS * MIX_WIDTH
    qa = _head_rms(_dot(x, w_ref[:, 0:MIX_WIDTH]), ones_ref[...], qg_ref[...])
    for s in range(MIX_WIDTH // LANES):
        sl = slice(s * LANES, (s + 1) * LANES)
        q_ref[:, sl] = (_rope_slot(qa[:, sl], cos, sin, first_half) * (scale * LOG2E)).astype(BF16)
    ka = _head_rms(_dot(x, w_ref[:, qw:qw + KV_WIDTH]), ones_ref[0:LANES, 0:LANES], kg_ref[...])
    k_ref[:, 0:KV_WIDTH] = _rope_slot(ka, cos, sin, first_half).astype(BF16)
    for n in range(1, N_MIXERS):
        q_ref[:, n * MIX_WIDTH:(n + 1) * MIX_WIDTH] = (
            _dot(x, w_ref[:, n * MIX_WIDTH:(n + 1) * MIX_WIDTH]) * (scale * LOG2E)).astype(BF16)
        k_ref[:, n * KV_WIDTH:(n + 1) * KV_WIDTH] = _dot(
            x, w_ref[:, qw + n * KV_WIDTH:qw + (n + 1) * KV_WIDTH]).astype(BF16)
    vw = qw + N_MIXERS * KV_WIDTH
    v = _dot(x, w_ref[:, vw:vw + N_MIXERS * KV_WIDTH])
    v_ref[...] = v.astype(BF16)
    va = v[:, 0:KV_WIDTH]
    lo = lane < HEAD_DIM
    v1_ref[:, 0:LANES] = jnp.where(lo, va, 1.0).astype(BF16)
    v1_ref[:, LANES:2 * LANES] = jnp.where(lo, pltpu.roll(va, HEAD_DIM, 1), 1.0).astype(BF16)


def _inproj(h, w_qkv, cos_tab, sin_tab, q_gain, k_gain, ones_bd, pos_block, tm):
    R = h.shape[0]
    const = lambda i: (0, 0)
    return pl.pallas_call(
        _inproj_kernel,
        grid=(R // tm,),
        in_specs=[pl.BlockSpec((tm, D_MODEL), lambda i: (i, 0)),
                  pl.BlockSpec(w_qkv.shape, const),
                  pl.BlockSpec((tm, LANES), lambda i: (pos_block(i), 0)),
                  pl.BlockSpec((tm, LANES), lambda i: (pos_block(i), 0)),
                  pl.BlockSpec((1, MIX_WIDTH), const),
                  pl.BlockSpec((1, KV_WIDTH), const),
                  pl.BlockSpec((MIX_WIDTH, MIX_WIDTH), const)],
        out_specs=[pl.BlockSpec((tm, N_MIXERS * MIX_WIDTH), lambda i: (i, 0)),
                   pl.BlockSpec((tm, N_MIXERS * KV_WIDTH), lambda i: (i, 0)),
                   pl.BlockSpec((tm, N_MIXERS * KV_WIDTH), lambda i: (i, 0)),
                   pl.BlockSpec((tm, N_KV_HEADS * LANES), lambda i: (i, 0))],
        out_shape=[jax.ShapeDtypeStruct((R, N_MIXERS * MIX_WIDTH), BF16),
                   jax.ShapeDtypeStruct((R, N_MIXERS * KV_WIDTH), BF16),
                   jax.ShapeDtypeStruct((R, N_MIXERS * KV_WIDTH), BF16),
                   jax.ShapeDtypeStruct((R, N_KV_HEADS * LANES), BF16)],
        compiler_params=_cparams(("parallel",)),
        name="inproj",
    )(h, w_qkv, cos_tab, sin_tab, q_gain, k_gain, ones_bd)


def _group_queries(q, j):
    lane = lax.broadcasted_iota(jnp.int32, (q.shape[0], LANES), 1)
    keep = (lane < HEAD_DIM) if j == 0 else (lane >= HEAD_DIM)
    parts = []
    for hh in range(GROUP):
        h = GROUP * j + hh
        slot = q[:, (h // 2) * LANES:(h // 2 + 1) * LANES]
        if h % 2 != j:
            slot = pltpu.roll(slot, HEAD_DIM, 1)
        parts.append(jnp.where(keep, slot, 0.0))
    return jnp.concatenate(parts, axis=0).astype(BF16)


def _ungroup_outputs(out, j, T):
    lane = lax.broadcasted_iota(jnp.int32, (T, LANES), 1)
    lo = lane < HEAD_DIM
    slots = []
    for s in range(2):
        even = out[(2 * s) * T:(2 * s + 1) * T]
        odd = out[(2 * s + 1) * T:(2 * s + 2) * T]
        if j == 0:
            slots.append(jnp.where(lo, even, pltpu.roll(odd, HEAD_DIM, 1)))
        else:
            slots.append(jnp.where(lo, pltpu.roll(even, HEAD_DIM, 1), odd))
    return jnp.concatenate(slots, axis=1)


def _head_rows(vals, T):
    return jnp.concatenate([jnp.broadcast_to(v, (T, v.shape[-1])) for v in vals], axis=0)


def _global_kernel(q_ref, k_ref, v_ref, km_ref, vm_ref, *rest, S):
    o_ref = rest[-1]
    T = q_ref.shape[0]
    q = q_ref[...].astype(F32)
    km = km_ref[...]
    qs, state = [], []
    for j in range(N_KV_HEADS):
        qj = _group_queries(q, j)
        s_m = _dot_nt(qj, km)
        m0 = jnp.max(s_m, axis=-1, keepdims=True)
        p_m = jnp.exp2((s_m - m0).astype(BF16))
        qs.append(qj)
        state.append((m0, _dot(p_m, vm_ref[:, j * LANES:(j + 1) * LANES])))
    for c in range(S // TK_GLOBAL):
        kc = k_ref[c * TK_GLOBAL:(c + 1) * TK_GLOBAL, :]
        for j in range(N_KV_HEADS):
            m, acc = state[j]
            s = _dot_nt(qs[j], kc)
            m_new = jnp.maximum(m, jnp.max(s, axis=-1, keepdims=True))
            p = jnp.exp2((s - m_new).astype(BF16))
            vc = v_ref[c * TK_GLOBAL:(c + 1) * TK_GLOBAL, j * LANES:(j + 1) * LANES]
            state[j] = (m_new, jnp.exp2(m - m_new) * acc + _dot(p, vc))
    for j in range(N_KV_HEADS):
        acc = state[j][1]
        out = acc / pltpu.roll(acc, HEAD_DIM, 1)
        o_ref[:, j * 2 * LANES:(j + 1) * 2 * LANES] = _ungroup_outputs(out, 0, T).astype(BF16)


def _window_kernel(sink_ref, q_ref, kp_ref, kc_ref, kn_ref, vp_ref, vc_ref, vn_ref, km_ref, vm_ref,
                   bband_ref, bmeta_ref, *rest, nb):
    o_ref = rest[-1]
    i = pl.program_id(1)
    T = q_ref.shape[0]
    q = q_ref[...].astype(F32)
    kband = jnp.concatenate([kp_ref[...], kc_ref[...], kn_ref[...]], axis=0)
    vband = jnp.concatenate([vp_ref[...], vc_ref[...], vn_ref[...]], axis=0)
    col = lax.broadcasted_iota(jnp.int32, (1, 3 * BLOCK), 1)
    in_range = ((col >= BLOCK) | (i > 0)) & ((col < 2 * BLOCK) | (i < nb - 1))
    for j in range(N_KV_HEADS):
        qj = _group_queries(q, j)
        hs = slice(GROUP * j, GROUP * (j + 1))
        s_b = _dot_nt(qj, kband) + bband_ref[hs].reshape(GROUP * T, 3 * BLOCK)
        s_b = jnp.where(in_range, s_b, NEG_INF)
        s_m = _dot_nt(qj, km_ref[...]) + bmeta_ref[hs].reshape(GROUP * T, N_META)
        sink = jnp.concatenate([jnp.full((T, 1), sink_ref[GROUP * j + hh], F32) for hh in range(GROUP)], axis=0)
        m = jnp.maximum(jnp.maximum(jnp.max(s_b, axis=-1, keepdims=True),
                                    jnp.max(s_m, axis=-1, keepdims=True)), sink)
        e_b = jnp.exp2(s_b - m)
        e_m = jnp.exp2(s_m - m)
        denom = (jnp.sum(e_b, axis=-1, keepdims=True) + jnp.sum(e_m, axis=-1, keepdims=True)
                 + jnp.exp2(sink - m))
        acc = _dot(e_b.astype(BF16), vband) + _dot(e_m.astype(BF16), vm_ref[...])
        o_ref[:, j * 2 * LANES:(j + 1) * 2 * LANES] = _ungroup_outputs(acc / denom, j, T).astype(BF16)


def _na_kernel(q_ref, k_ref, v_ref, km_ref, vm_ref, bias_ref, mbias_ref, *rest, rows):
    o_ref = rest[-1]
    blk = pl.program_id(1)
    W = GRID_W
    nkeys = NA_ROWS * W
    km = km_ref[...]
    vm = vm_ref[...]
    for j in range(N_KV_HEADS):
        mb = _head_rows([mbias_ref[GROUP * j + hh:GROUP * j + hh + 1, :] for hh in range(GROUP)], W)

        def row_body(rr, carry, j=j, mb=mb):
            r = blk * NA_QROWS + rr
            rs = jnp.clip(r - NA_ROWS // 2, 0, rows - NA_ROWS)
            delta = r - rs
            qoff = pl.multiple_of(rr * W, W)
            koff = pl.multiple_of(rs * W, W)
            qj = _group_queries(q_ref[pl.ds(qoff, W), :].astype(F32), j)
            kw = k_ref[pl.ds(koff, nkeys), :]
            vw = v_ref[pl.ds(koff, nkeys), :]
            s_w = _dot_nt(qj, kw) + bias_ref[delta, pl.ds(GROUP * j, GROUP)].reshape(GROUP * W, nkeys)
            s_m = _dot_nt(qj, km) + mb
            m = jnp.maximum(jnp.max(s_w, axis=-1, keepdims=True), jnp.max(s_m, axis=-1, keepdims=True))
            e_w = jnp.exp2(s_w - m)
            e_m = jnp.exp2(s_m - m)
            denom = jnp.sum(e_w, axis=-1, keepdims=True) + jnp.sum(e_m, axis=-1, keepdims=True)
            acc = _dot(e_w.astype(BF16), vw) + _dot(e_m.astype(BF16), vm)
            o_ref[pl.ds(qoff, W), j * 2 * LANES:(j + 1) * 2 * LANES] = (
                _ungroup_outputs(acc / denom, j, W).astype(BF16))
            return carry

        lax.fori_loop(0, NA_QROWS, row_body, 0, unroll=True)


def _meta_kernel(sink_ref, q_ref, ka_ref, va_ref, kb_ref, vb_ref, km_ref, vm_ref,
                 bq_meta_ref, bq_blk_ref, mbias_ref, o_in_ref, o_ref):
    del o_in_ref
    T = N_META
    q = q_ref[...].astype(F32)
    km_all = km_ref[...]
    vm_all = vm_ref[...]

    def finish(n, j, acc, denom):
        lo = n * MIX_WIDTH + j * 2 * LANES
        o_ref[:, lo:lo + 2 * LANES] = _ungroup_outputs(acc / denom, j, T).astype(BF16)

    for j in range(N_KV_HEADS):
        hs = slice(GROUP * j, GROUP * (j + 1))
        qj = _group_queries(q[:, 0:MIX_WIDTH], j)
        km, vm = km_all[:, 0:KV_WIDTH], vm_all[:, 0:KV_WIDTH]
        s_r = _dot_nt(qj, ka_ref[...])
        s_m = _dot_nt(qj, km)
        m = jnp.maximum(jnp.max(s_r, axis=-1, keepdims=True), jnp.max(s_m, axis=-1, keepdims=True))
        e_r = jnp.exp2(s_r - m)
        e_m = jnp.exp2(s_m - m)
        denom = jnp.sum(e_r, axis=-1, keepdims=True) + jnp.sum(e_m, axis=-1, keepdims=True)
        finish(0, j, _dot(e_r.astype(BF16), va_ref[...]) + _dot(e_m.astype(BF16), vm), denom)
        qj = _group_queries(q[:, MIX_WIDTH:2 * MIX_WIDTH], j)
        km, vm = km_all[:, KV_WIDTH:2 * KV_WIDTH], vm_all[:, KV_WIDTH:2 * KV_WIDTH]
        s_r = _dot_nt(qj, kb_ref[...]) + bq_blk_ref[hs].reshape(GROUP * T, BLOCK)
        s_m = _dot_nt(qj, km) + bq_meta_ref[hs].reshape(GROUP * T, N_META)
        sink = jnp.concatenate([jnp.full((T, 1), sink_ref[GROUP * j + hh], F32) for hh in range(GROUP)], axis=0)
        m = jnp.maximum(jnp.maximum(jnp.max(s_r, axis=-1, keepdims=True),
                                    jnp.max(s_m, axis=-1, keepdims=True)), sink)
        e_r = jnp.exp2(s_r - m)
        e_m = jnp.exp2(s_m - m)
        denom = (jnp.sum(e_r, axis=-1, keepdims=True) + jnp.sum(e_m, axis=-1, keepdims=True)
                 + jnp.exp2(sink - m))
        finish(1, j, _dot(e_r.astype(BF16), vb_ref[...]) + _dot(e_m.astype(BF16), vm), denom)
        qj = _group_queries(q[:, 2 * MIX_WIDTH:3 * MIX_WIDTH], j)
        km, vm = km_all[:, 2 * KV_WIDTH:3 * KV_WIDTH], vm_all[:, 2 * KV_WIDTH:3 * KV_WIDTH]
        mb = _head_rows([mbias_ref[GROUP * j + hh:GROUP * j + hh + 1, :] for hh in range(GROUP)], T)
        s_m = _dot_nt(qj, km) + mb
        m = jnp.max(s_m, axis=-1, keepdims=True)
        e_m = jnp.exp2(s_m - m)
        finish(2, j, _dot(e_m.astype(BF16), vm), jnp.sum(e_m, axis=-1, keepdims=True))


class _Group:
    def __init__(self, B, S, real_base, meta_batch_base, meta_base, n_meta_blocks):
        self.B, self.S = B, S
        self.real_base = real_base
        self.meta_blk0 = meta_base // N_META + meta_batch_base
        self.n_meta_blocks = n_meta_blocks
        assert real_base % S == 0 and S % TM == 0 and meta_base % N_META == 0


def _alias_args(o_prev, n_inputs):
    if o_prev is None:
        return [], [], {}
    return [o_prev], [pl.BlockSpec(memory_space=pl.ANY)], {n_inputs: 0}


def _global_attn(grp, q_all, k_all, v_ones, o_prev):
    B, S = grp.B, grp.S
    nq = S // TQ_GLOBAL
    qb0 = grp.real_base // TQ_GLOBAL
    sb0 = grp.real_base // S
    mb0 = grp.meta_blk0
    in_specs = [pl.BlockSpec((TQ_GLOBAL, MIX_WIDTH), lambda b, i: (qb0 + b * nq + i, 0)),
                pl.BlockSpec((S, KV_WIDTH), lambda b, i: (sb0 + b, 0)),
                pl.BlockSpec((S, N_KV_HEADS * LANES), lambda b, i: (sb0 + b, 0)),
                pl.BlockSpec((N_META, KV_WIDTH), lambda b, i: (mb0 + b, 0)),
                pl.BlockSpec((N_META, N_KV_HEADS * LANES), lambda b, i: (mb0 + b, 0))]
    extra, extra_specs, aliases = _alias_args(o_prev, len(in_specs))
    return pl.pallas_call(
        functools.partial(_global_kernel, S=S),
        grid=(B, nq),
        in_specs=in_specs + extra_specs,
        out_specs=pl.BlockSpec((TQ_GLOBAL, MIX_WIDTH), lambda b, i: (qb0 + b * nq + i, 0)),
        out_shape=jax.ShapeDtypeStruct((q_all.shape[0], N_MIXERS * MIX_WIDTH), BF16),
        input_output_aliases=aliases,
        compiler_params=_cparams(("parallel", "arbitrary")),
        name="mixer_global",
    )(q_all, k_all, v_ones, k_all, v_ones, *extra)


def _window_attn(grp, q_all, k_all, v_all, sink, bband, bmeta, o_prev):
    B, S = grp.B, grp.S
    nb = S // BLOCK
    qb0 = grp.real_base // BLOCK
    mb0 = grp.meta_blk0
    cur = lambda b, i, sink: (qb0 + b * nb + i, 1)
    prv = lambda b, i, sink: (qb0 + b * nb + jnp.maximum(i - 1, 0), 1)
    nxt = lambda b, i, sink: (qb0 + b * nb + jnp.minimum(i + 1, nb - 1), 1)
    met = lambda b, i, sink: (mb0 + b, 1)
    kv = lambda im: pl.BlockSpec((BLOCK, KV_WIDTH), im)
    in_specs = [pl.BlockSpec((BLOCK, MIX_WIDTH), cur),
                kv(prv), kv(cur), kv(nxt), kv(prv), kv(cur), kv(nxt),
                pl.BlockSpec((N_META, KV_WIDTH), met), pl.BlockSpec((N_META, KV_WIDTH), met),
                pl.BlockSpec((N_HEADS, BLOCK, 3 * BLOCK), lambda b, i, sink: (0, 0, 0)),
                pl.BlockSpec((N_HEADS, BLOCK, N_META), lambda b, i, sink: (0, i, 0))]
    extra, extra_specs, aliases = _alias_args(o_prev, len(in_specs) + 1)
    return pl.pallas_call(
        functools.partial(_window_kernel, nb=nb),
        grid_spec=pltpu.PrefetchScalarGridSpec(
            num_scalar_prefetch=1,
            grid=(B, nb),
            in_specs=in_specs + extra_specs,
            out_specs=pl.BlockSpec((BLOCK, MIX_WIDTH), cur)),
        out_shape=jax.ShapeDtypeStruct((q_all.shape[0], N_MIXERS * MIX_WIDTH), BF16),
        input_output_aliases=aliases,
        compiler_params=_cparams(("parallel", "arbitrary")),
        name="mixer_window",
    )(sink, q_all, k_all, k_all, k_all, v_all, v_all, v_all, k_all, v_all, bband, bmeta, *extra)


def _na_attn(grp, q_all, k_all, v_all, na_bias, na_mbias, o_prev):
    B, S = grp.B, grp.S
    rows = S // GRID_W
    tq = NA_QROWS * GRID_W
    nq = S // tq
    qb0 = grp.real_base // tq
    sb0 = grp.real_base // S
    mb0 = grp.meta_blk0
    in_specs = [pl.BlockSpec((tq, MIX_WIDTH), lambda b, i: (qb0 + b * nq + i, 2)),
                pl.BlockSpec((S, KV_WIDTH), lambda b, i: (sb0 + b, 2)),
                pl.BlockSpec((S, KV_WIDTH), lambda b, i: (sb0 + b, 2)),
                pl.BlockSpec((N_META, KV_WIDTH), lambda b, i: (mb0 + b, 2)),
                pl.BlockSpec((N_META, KV_WIDTH), lambda b, i: (mb0 + b, 2)),
                pl.BlockSpec(na_bias.shape, lambda b, i: (0, 0, 0, 0)),
                pl.BlockSpec(na_mbias.shape, lambda b, i: (0, 0))]
    extra, extra_specs, aliases = _alias_args(o_prev, len(in_specs))
    return pl.pallas_call(
        functools.partial(_na_kernel, rows=rows),
        grid=(B, nq),
        in_specs=in_specs + extra_specs,
        out_specs=pl.BlockSpec((tq, MIX_WIDTH), lambda b, i: (qb0 + b * nq + i, 2)),
        out_shape=jax.ShapeDtypeStruct((q_all.shape[0], N_MIXERS * MIX_WIDTH), BF16),
        input_output_aliases=aliases,
        compiler_params=_cparams(("parallel", "arbitrary")),
        name="mixer_neighbourhood",
    )(q_all, k_all, v_all, k_all, v_all, na_bias, na_mbias, *extra)


def _meta_attn(grp, q_all, k_all, v_all, sink, bq_meta, bq_blk, na_mbias, o_prev):
    B, S = grp.B, grp.S
    sb0 = grp.real_base // S
    bb0 = grp.real_base // BLOCK
    nb = S // BLOCK
    mb0 = grp.meta_blk0
    clamp = lambda b: jnp.minimum(b, B - 1)
    mrow = lambda b, sink: (mb0 + clamp(b), 0)
    in_specs = [pl.BlockSpec((N_META, N_MIXERS * MIX_WIDTH), mrow),
                pl.BlockSpec((S, KV_WIDTH), lambda b, sink: (sb0 + clamp(b), 0)),
                pl.BlockSpec((S, KV_WIDTH), lambda b, sink: (sb0 + clamp(b), 0)),
                pl.BlockSpec((BLOCK, KV_WIDTH), lambda b, sink: (bb0 + clamp(b) * nb, 1)),
                pl.BlockSpec((BLOCK, KV_WIDTH), lambda b, sink: (bb0 + clamp(b) * nb, 1)),
                pl.BlockSpec((N_META, N_MIXERS * KV_WIDTH), mrow),
                pl.BlockSpec((N_META, N_MIXERS * KV_WIDTH), mrow),
                pl.BlockSpec(bq_meta.shape, lambda b, sink: (0, 0, 0)),
                pl.BlockSpec(bq_blk.shape, lambda b, sink: (0, 0, 0)),
                pl.BlockSpec(na_mbias.shape, lambda b, sink: (0, 0)),
                pl.BlockSpec(memory_space=pl.ANY)]
    return pl.pallas_call(
        _meta_kernel,
        grid_spec=pltpu.PrefetchScalarGridSpec(
            num_scalar_prefetch=1,
            grid=(grp.n_meta_blocks,),
            in_specs=in_specs,
            out_specs=pl.BlockSpec((N_META, N_MIXERS * MIX_WIDTH), lambda b, sink: (mb0 + b, 0))),
        out_shape=jax.ShapeDtypeStruct((q_all.shape[0], N_MIXERS * MIX_WIDTH), BF16),
        input_output_aliases={len(in_specs): 0},
        compiler_params=_cparams(("arbitrary",)),
        name="mixer_meta_queries",
    )(sink, q_all, k_all, v_all, k_all, v_all, k_all, v_all, bq_meta, bq_blk, na_mbias, o_prev)


def _route(h1, wr_hi, wr_lo, rbias):
    T = h1.shape[0]
    x_hi, x_lo = _split_bf16(h1)
    logits = _dot_nt(wr_hi, x_hi) + _dot_nt(wr_hi, x_lo) + _dot_nt(wr_lo, x_hi)
    scores = 1.0 / (1.0 + jnp.exp(-logits))
    sel = scores + rbias
    per_group = N_EXPERTS // N_EXPERT_GROUPS
    sel3 = sel.reshape(N_EXPERT_GROUPS, per_group, T)
    idx3 = lax.broadcasted_iota(jnp.int32, sel3.shape, 1).astype(F32)
    m1 = jnp.max(sel3, axis=1, keepdims=True)
    first = jnp.min(jnp.where(sel3 == m1, idx3, float(per_group)), axis=1, keepdims=True)
    m2 = jnp.max(jnp.where(idx3 == first, -jnp.inf, sel3), axis=1, keepdims=True)
    gscore = (m1 + m2).reshape(N_EXPERT_GROUPS, T)

    def rank_of(vals):
        idx = lax.broadcasted_iota(jnp.int32, vals.shape, 0)
        rank = jnp.zeros(vals.shape, F32)
        for r in range(vals.shape[0]):
            row = vals[r:r + 1, :]
            ge = jnp.where(row >= vals, 1.0, 0.0)
            gt = jnp.where(row > vals, 1.0, 0.0)
            rank = rank + jnp.where(idx > r, ge, gt)
        return rank

    gkeep = jnp.where(rank_of(gscore) < TOPK_GROUPS, 1.0, 0.0)
    ekeep = jnp.broadcast_to(gkeep.reshape(N_EXPERT_GROUPS, 1, T), sel3.shape).reshape(N_EXPERTS, T)
    masked = jnp.where(ekeep > 0.5, sel, NEG_INF)
    eidx = lax.broadcasted_iota(jnp.int32, masked.shape, 0).astype(F32)
    chosen = jnp.zeros(masked.shape, F32)
    for _ in range(TOP_K):
        best = jnp.max(masked, axis=0, keepdims=True)
        first = jnp.min(jnp.where(masked == best, eidx, float(N_EXPERTS)), axis=0, keepdims=True)
        hit = eidx == first
        chosen = jnp.where(hit, 1.0, chosen)
        masked = jnp.where(hit, -jnp.inf, masked)
    w = jnp.where(chosen > 0.5, scores, 0.0)
    return w / jnp.sum(w, axis=0, keepdims=True) * ROUTED_SCALE


def _merge_kernel(h_ref, o_ref, wg_ref, wb_ref, wo_ref, g_ref, b_ref, wrh_ref, wrl_ref, rb_ref,
                  h1_ref, gates_ref, *, alpha):
    h = h_ref[...]
    x = h.astype(BF16)
    merged = None
    for n in range(N_MIXERS):
        logit = _dot(x, wg_ref[:, n * D_MODEL:(n + 1) * D_MODEL])
        branch = _dot(o_ref[:, n * MIX_WIDTH:(n + 1) * MIX_WIDTH], wb_ref[n])
        term = branch / (1.0 + jnp.exp(-logit))
        merged = term if merged is None else merged + term
    mix = _dot(merged.astype(BF16), wo_ref[...])
    h1 = _layer_norm(alpha * h + mix, g_ref[...], b_ref[...])
    h1_ref[...] = h1
    gates_t = _route(h1, wrh_ref[...], wrl_ref[...], rb_ref[...])
    pad = jnp.zeros((LANES - N_EXPERTS, gates_t.shape[1]), F32)
    gates_ref[...] = jnp.concatenate([gates_t, pad], axis=0).T


def _merge(h, o_all, w_gate, w_branch, w_out, ln_g, ln_b, wr_hi, wr_lo, rbias, alpha, tm):
    R = h.shape[0]
    c2 = lambda i: (0, 0)
    once = pl.Buffered(1)
    return pl.pallas_call(
        functools.partial(_merge_kernel, alpha=alpha),
        grid=(R // tm,),
        in_specs=[pl.BlockSpec((tm, D_MODEL), lambda i: (i, 0)),
                  pl.BlockSpec((tm, N_MIXERS * MIX_WIDTH), lambda i: (i, 0)),
                  pl.BlockSpec(w_gate.shape, c2, pipeline_mode=once),
                  pl.BlockSpec(w_branch.shape, lambda i: (0, 0, 0), pipeline_mode=once),
                  pl.BlockSpec(w_out.shape, c2, pipeline_mode=once),
                  pl.BlockSpec((1, D_MODEL), c2),
                  pl.BlockSpec((1, D_MODEL), c2),
                  pl.BlockSpec(wr_hi.shape, c2),
                  pl.BlockSpec(wr_lo.shape, c2),
                  pl.BlockSpec(rbias.shape, c2)],
        out_specs=[pl.BlockSpec((tm, D_MODEL), lambda i: (i, 0)),
                   pl.BlockSpec((tm, LANES), lambda i: (i, 0))],
        out_shape=[jax.ShapeDtypeStruct((R, D_MODEL), F32),
                   jax.ShapeDtypeStruct((R, LANES), F32)],
        compiler_params=_cparams(("parallel",)),
        name="merge_ln_route",
    )(h, o_all, w_gate, w_branch, w_out, ln_g, ln_b, wr_hi, wr_lo, rbias)


TD = 256
CH = 16
SLOTS = 3072
NCH = SLOTS // CH
MT = 512
ME = 1024
CPM = ME // CH
XBUFS = 3


def _swiglu_act(gu):
    g = gu[:, :D_EXPERT]
    return g / (1.0 + jnp.exp(-g)) * gu[:, D_EXPERT:]


def _slot_of_token(gates, lo_row):
    routed = gates > 0.0
    r = lax.broadcasted_iota(jnp.int32, (TD, TD), 0)
    c = lax.broadcasted_iota(jnp.int32, (TD, TD), 1)
    earlier = jnp.where(c < r, 1.0, 0.0).astype(BF16)
    rank = _dot(earlier, jnp.where(routed, 1.0, 0.0).astype(BF16))
    return jnp.where(routed, lo_row + rank + 1.0, 0.0)


def _split64(x):
    hi = 64.0 * jnp.floor(x * (1.0 / 64.0))
    return jnp.concatenate([hi.astype(BF16), (x - hi).astype(BF16)], axis=0)


def _dispatch_kernel(h_ref, g_ref, lohi_ref, x_ref, w_ref):
    lohi = lohi_ref[0]
    lo_row, hi_row = lohi[0:1], lohi[1:2]
    gates = g_ref[...]
    slot_t = _split64(_slot_of_token(gates, lo_row).T)
    gates_t = gates.T.astype(BF16)
    x = h_ref[...].astype(BF16)
    for blk in range(SLOTS // MT):
        s = (lax.broadcasted_iota(jnp.int32, (MT, LANES), 0) + blk * MT).astype(F32)
        owner = jnp.where(s >= lo_row, jnp.where(s < hi_row, 1.0, 0.0), 0.0).astype(BF16)
        want = _dot(jnp.concatenate([owner, owner], axis=1), slot_t)
        s1 = (lax.broadcasted_iota(jnp.int32, (MT, TD), 0) + (blk * MT + 1)).astype(F32)
        hit = want == s1
        x_ref[blk * MT:(blk + 1) * MT, :] = _dot(jnp.where(hit, 1.0, 0.0).astype(BF16), x).astype(BF16)
        weight = jnp.where(hit, _dot(owner, gates_t), 0.0)
        w_ref[:, blk * MT:(blk + 1) * MT] = weight.T.astype(BF16)


def _dispatch(h1, gates, lohi):
    n = h1.shape[0] // TD
    return pl.pallas_call(
        _dispatch_kernel,
        grid=(n,),
        in_specs=[pl.BlockSpec((TD, D_MODEL), lambda i: (i, 0)),
                  pl.BlockSpec((TD, LANES), lambda i: (i, 0)),
                  pl.BlockSpec((1, 8, LANES), lambda i: (i, 0, 0))],
        out_specs=[pl.BlockSpec((SLOTS, D_MODEL), lambda i: (i, 0)),
                   pl.BlockSpec((TD, SLOTS), lambda i: (i, 0))],
        out_shape=[jax.ShapeDtypeStruct((n * SLOTS, D_MODEL), BF16),
                   jax.ShapeDtypeStruct((n * TD, SLOTS), BF16)],
        compiler_params=_cparams(("parallel",)),
        name="moe_dispatch",
    )(h1, gates, lohi)


def _chunk_gather(table_ref, first, n_chunks, src_hbm, buf, sem):
    copies = []
    for c in range(n_chunks):
        row = pl.multiple_of(table_ref[first + c] * CH, CH)
        copies.append(pltpu.make_async_copy(src_hbm.at[pl.ds(row, CH)], buf.at[pl.ds(c * CH, CH)], sem))
    return copies


def _chunk_wait(n_chunks, src_hbm, buf, sem):
    for c in range(n_chunks):
        pltpu.make_async_copy(src_hbm.at[pl.ds(0, CH)], buf.at[pl.ds(c * CH, CH)], sem).wait()


def _expert_kernel(te_ref, src_ref, nu_ref, x_hbm, wgu_ref, wd_ref, y_ref, xbuf, sem):
    del te_ref
    m = pl.program_id(0)
    n_used = nu_ref[0]

    def start(step):
        slot = step % XBUFS
        for cp in _chunk_gather(src_ref, step * CPM, CPM, x_hbm, xbuf.at[slot], sem.at[slot]):
            cp.start()

    for ahead in range(XBUFS - 1):
        @pl.when((m == 0) & (ahead < n_used))
        def _(ahead=ahead):
            start(ahead)

    @pl.when(m + (XBUFS - 1) < n_used)
    def _():
        start(m + (XBUFS - 1))

    @pl.when(m < n_used)
    def _():
        slot = m % XBUFS
        _chunk_wait(CPM, x_hbm, xbuf.at[slot], sem.at[slot])
        act = _swiglu_act(_dot(xbuf[slot], wgu_ref[0]))
        y_ref[...] = _dot(act.astype(BF16), wd_ref[0]).astype(BF16)


def _experts(x_disp, wgu, wd, tile_expert, src_chunk, n_used):
    n_steps = tile_expert.shape[0]
    return pl.pallas_call(
        _expert_kernel,
        grid_spec=pltpu.PrefetchScalarGridSpec(
            num_scalar_prefetch=3,
            grid=(n_steps,),
            in_specs=[pl.BlockSpec(memory_space=pl.ANY),
                      pl.BlockSpec((1, D_MODEL, 2 * D_EXPERT), lambda m, te, src, nu: (te[m], 0, 0)),
                      pl.BlockSpec((1, D_EXPERT, D_MODEL), lambda m, te, src, nu: (te[m], 0, 0))],
            out_specs=pl.BlockSpec((ME, D_MODEL), lambda m, te, src, nu: (jnp.minimum(m, nu[0] - 1), 0)),
            scratch_shapes=[pltpu.VMEM((XBUFS, ME, D_MODEL), BF16), pltpu.SemaphoreType.DMA((XBUFS,))]),
        out_shape=jax.ShapeDtypeStruct((n_steps * ME, D_MODEL), BF16),
        compiler_params=_cparams(("arbitrary",)),
        name="moe_experts",
    )(tile_expert, src_chunk, n_used, x_disp, wgu, wd)


def _combine_kernel(dst_ref, y_hbm, w_ref, h_ref, wsgu_ref, wsd_ref, lg_ref, lb_ref, o_ref, ybuf, sem, *, alpha):
    i = pl.program_id(0)

    def start(tile):
        slot = tile % 2
        for cp in _chunk_gather(dst_ref, tile * NCH, NCH, y_hbm, ybuf.at[slot], sem.at[slot]):
            cp.start()

    @pl.when(i == 0)
    def _():
        start(i)

    @pl.when(i + 1 < pl.num_programs(0))
    def _():
        start(i + 1)

    h = h_ref[...]
    shared = _dot(_swiglu_act(_dot(h.astype(BF16), wsgu_ref[...])).astype(BF16), wsd_ref[...])
    slot = i % 2
    _chunk_wait(NCH, y_hbm, ybuf.at[slot], sem.at[slot])
    routed = _dot(w_ref[...], ybuf[slot])
    o_ref[...] = _layer_norm(alpha * h + shared + routed, lg_ref[...], lb_ref[...])


def _combine(y_sorted, w_t, h1, dst_chunk, wsgu, wsd, ln_g, ln_b, alpha):
    n = h1.shape[0] // TD
    c2 = lambda i, dst: (0, 0)
    return pl.pallas_call(
        functools.partial(_combine_kernel, alpha=alpha),
        grid_spec=pltpu.PrefetchScalarGridSpec(
            num_scalar_prefetch=1,
            grid=(n,),
            in_specs=[pl.BlockSpec(memory_space=pl.ANY),
                      pl.BlockSpec((TD, SLOTS), lambda i, dst: (i, 0)),
                      pl.BlockSpec((TD, D_MODEL), lambda i, dst: (i, 0)),
                      pl.BlockSpec(wsgu.shape, c2),
                      pl.BlockSpec(wsd.shape, c2),
                      pl.BlockSpec((1, D_MODEL), c2),
                      pl.BlockSpec((1, D_MODEL), c2)],
            out_specs=pl.BlockSpec((TD, D_MODEL), lambda i, dst: (i, 0)),
            scratch_shapes=[pltpu.VMEM((2, SLOTS, D_MODEL), BF16), pltpu.SemaphoreType.DMA((2,))]),
        out_shape=jax.ShapeDtypeStruct((h1.shape[0], D_MODEL), F32),
        compiler_params=_cparams(("arbitrary",)),
        name="moe_combine_ln",
    )(dst_chunk, y_sorted, w_t, h1, wsgu, wsd, ln_g, ln_b)


def _routing_tables(gates):
    n = gates.shape[0] // TD
    cnt = jnp.sum((gates[:, :N_EXPERTS] > 0.0).reshape(n, TD, N_EXPERTS), axis=1, dtype=jnp.int32)
    nch = (cnt + (CH - 1)) // CH
    hi16 = jnp.cumsum(nch, axis=1)
    lo16 = hi16 - nch
    nct = hi16[:, -1:]
    pad = jnp.broadcast_to(nct, (n, LANES - N_EXPERTS))
    lohi = jnp.zeros((n, 8, LANES), F32)
    lohi = lohi.at[:, 0, :].set((jnp.concatenate([lo16, pad], axis=1) * CH).astype(F32))
    lohi = lohi.at[:, 1, :].set((jnp.concatenate([hi16, pad], axis=1) * CH).astype(F32))
    tot = jnp.sum(nch, axis=0)
    seg_len = (tot + (CPM - 1)) // CPM * CPM
    seg_end = jnp.cumsum(seg_len)
    seg_start = seg_end - seg_len
    gpos = seg_start[None, :] + jnp.cumsum(nch, axis=0) - nch
    n_steps = (n * NCH + N_EXPERTS * CPM) // CPM
    n_used = (seg_end[-1] // CPM).astype(jnp.int32).reshape(1)
    step = jnp.arange(n_steps, dtype=jnp.int32)
    tile_expert = jnp.sum(seg_end[None, :] // CPM <= jnp.minimum(step, n_used - 1)[:, None], axis=1, dtype=jnp.int32)
    tile_expert = jnp.minimum(tile_expert, N_EXPERTS - 1)
    exact = functools.partial(jnp.dot, precision=lax.Precision.HIGHEST)
    experts = jnp.arange(N_EXPERTS, dtype=jnp.int32)
    g = jnp.arange(n_steps * CPM, dtype=jnp.int32)
    e_of_g = jnp.minimum(jnp.sum(seg_end[None, :] <= g[:, None], axis=1, dtype=jnp.int32), N_EXPERTS - 1)
    pick_e = (e_of_g[:, None] == experts[None, :]).astype(F32)
    first = exact(pick_e, gpos.T.astype(F32))
    count = exact(pick_e, nch.T.astype(F32))
    base = exact(pick_e, (jnp.arange(n, dtype=jnp.int32)[:, None] * NCH + lo16).T.astype(F32))
    gf = g.astype(F32)[:, None]
    inside = (first <= gf) & (gf < first + count)
    src_chunk = jnp.sum(jnp.where(inside, base + gf - first, 0.0), axis=1).astype(jnp.int32)
    k = jnp.arange(NCH, dtype=jnp.int32)
    e_of_k = jnp.minimum(jnp.sum(hi16[:, None, :] <= k[None, :, None], axis=2, dtype=jnp.int32), N_EXPERTS - 1)
    pick_k = e_of_k[:, :, None] == experts[None, None, :]
    pos = jnp.sum(jnp.where(pick_k, (gpos - lo16)[:, None, :], 0), axis=2) + k[None, :]
    dst_chunk = jnp.where(k[None, :] < nct, pos, 0).astype(jnp.int32).reshape(-1)
    return lohi, tile_expert, src_chunk, n_used, dst_chunk


def _moe(h1, gates, wgu, wd, wsgu, wsd, ln_g, ln_b, alpha):
    lohi, tile_expert, src_chunk, n_used, dst_chunk = _routing_tables(gates)
    x_disp, w_t = _dispatch(h1, gates, lohi)
    y_sorted = _experts(x_disp, wgu, wd, tile_expert, src_chunk, n_used)
    return _combine(y_sorted, w_t, h1, dst_chunk, wsgu, wsd, ln_g, ln_b, alpha)


def _t5_bucket(rel):
    half = T5_BUCKETS // 2
    max_exact = half // 2
    n = np.abs(rel)
    ratio = np.log(np.maximum(n, 1).astype(np.float32) / np.float32(max_exact))
    ratio = ratio / np.float32(math.log(T5_MAX_DIST / max_exact)) * np.float32(half - max_exact)
    large = np.minimum(max_exact + ratio.astype(np.int32), half - 1)
    return np.where(rel > 0, half, 0) + np.where(n < max_exact, n, large)


def _t5_tables(t5_table, s_max):
    def bias(rel, valid):
        b = t5_table[_t5_bucket(rel)].astype(F32)
        return jnp.where(jnp.asarray(valid)[None], jnp.transpose(b, (2, 0, 1)), NEG_INF)

    ii = np.arange(BLOCK)[:, None]
    jj = np.arange(3 * BLOCK)[None, :]
    rel = jj - ii - BLOCK
    bband = bias(rel, np.abs(rel) <= WINDOW)
    t = np.arange(s_max)[:, None]
    m = np.arange(N_META)[None, :]
    bmeta = bias(m - (N_META + t), np.ones((s_max, N_META), bool))
    mpos = np.arange(N_META)[:, None]
    kpos = np.arange(N_META + BLOCK)[None, :]
    relq = kpos - mpos
    bq = bias(relq, (kpos < N_META) | (np.abs(relq) <= WINDOW))
    return bband, bmeta, bq[:, :, :N_META], bq[:, :, N_META:]


def _na_bias_cases(rpb):
    W = GRID_W
    delta = np.arange(NA_ROWS)[:, None, None, None]
    i = np.arange(NA_ROWS)[None, :, None, None]
    c = np.arange(W)[None, None, :, None]
    kc = np.arange(W)[None, None, None, :]
    cs = np.clip(c - NA_COLS // 2, 0, W - NA_COLS)
    valid = (kc >= cs) & (kc < cs + NA_COLS)
    dc = np.clip(kc - c + (NA_COLS - 1), 0, 2 * NA_COLS - 2)[0, 0]
    t = jnp.where(jnp.asarray(valid[0, 0]), rpb.astype(F32)[:, :, dc], NEG_INF)
    cases = [jnp.transpose(t[:, NA_ROWS - 1 - d:2 * NA_ROWS - 1 - d], (0, 2, 1, 3)) for d in range(NA_ROWS)]
    return jnp.stack(cases, axis=0).reshape(NA_ROWS, N_HEADS, W, NA_ROWS * W)


def _rope_tables(s_max, n_meta_rows):
    half = HEAD_DIM // 4
    freq = ROPE_THETA ** (-jnp.arange(half, dtype=F32) / half)
    t = np.arange(s_max)
    mp = np.tile(np.arange(N_META) - N_META, n_meta_rows // N_META)
    pos_row = jnp.asarray(np.concatenate([t // GRID_W, mp]), jnp.int32).astype(F32)
    pos_col = jnp.asarray(np.concatenate([t % GRID_W, mp]), jnp.int32).astype(F32)
    ar = pos_row[:, None] * freq
    ac = pos_col[:, None] * freq
    cos = jnp.concatenate([jnp.cos(ar), jnp.cos(ar), jnp.cos(ac), jnp.cos(ac)], axis=1)
    sin = jnp.concatenate([-jnp.sin(ar), jnp.sin(ar), -jnp.sin(ac), jnp.sin(ac)], axis=1)
    return jnp.tile(cos, (1, 2)), jnp.tile(sin, (1, 2))


def kernel(x_prompt, x_sample, meta_tokens, ln_in_g, ln_in_b, t5_table, w_in, q_gain, k_gain, sink,
           na_rpb, na_meta_bias, w_branch, w_out, ln1_g, ln1_b, w_router, router_bias,
           w_expert_gate_up, w_expert_down, w_shared_gate_up, w_shared_down, ln2_g, ln2_b):
    depth = w_in.shape[0]
    alpha = (2 * depth) ** 0.25
    B0, S0, D = x_prompt.shape
    B1, S1, _ = x_sample.shape
    assert D == D_MODEL
    real = B0 * S0 + B1 * S1
    n_meta_rows = -(-(B0 + B1) * N_META // TM) * TM
    R = real + n_meta_rows
    n_meta_blocks = n_meta_rows // N_META
    g0 = _Group(B0, S0, 0, 0, real, B0)
    g1 = _Group(B1, S1, B0 * S0, B0, real, n_meta_blocks - B0)
    s_max = max(S0, S1)

    x = jnp.concatenate([x_prompt.reshape(B0 * S0, D), x_sample.reshape(B1 * S1, D),
                         jnp.tile(meta_tokens, (n_meta_blocks, 1))], axis=0)
    h = _embed_ln(x, ln_in_g.reshape(1, D), ln_in_b.reshape(1, D))

    tm = 2 * TM if all(v % (2 * TM) == 0 for v in (S0, S1, n_meta_rows)) else TM
    cos_tab, sin_tab = _rope_tables(s_max, tm)
    n0, n1 = B0 * S0 // tm, real // tm
    p0, p1, pm = S0 // tm, S1 // tm, s_max // tm

    def pos_block(i):
        return jnp.where(i < n0, i % p0, jnp.where(i < n1, (i - n0) % p1, pm))

    bband, bmeta, bq_meta, bq_blk = (t * LOG2E for t in _t5_tables(t5_table, s_max))
    ones_bd = jnp.asarray(np.kron(np.eye(N_HEADS), np.ones((HEAD_DIM, HEAD_DIM))), BF16)

    qs, ks, vs = [], [], []
    for n in range(N_MIXERS):
        off = n * QKV_WIDTH
        qs.append(w_in[:, :, off:off + MIX_WIDTH])
        ks.append(w_in[:, :, off + MIX_WIDTH:off + MIX_WIDTH + KV_WIDTH])
        vs.append(w_in[:, :, off + MIX_WIDTH + KV_WIDTH:off + QKV_WIDTH])
    wr_t = jnp.swapaxes(w_router, 1, 2)
    wr_hi = wr_t.astype(BF16)
    layers = dict(
        w_qkv=jnp.concatenate(qs + ks + vs, axis=2).astype(BF16),
        w_gate=w_in[:, :, N_MIXERS * QKV_WIDTH:].astype(BF16),
        q_gain=jnp.tile(q_gain, (1, N_HEADS)).reshape(depth, 1, MIX_WIDTH),
        k_gain=jnp.tile(k_gain, (1, N_KV_HEADS)).reshape(depth, 1, KV_WIDTH),
        sink=sink.astype(F32) * LOG2E,
        na_bias=jax.vmap(_na_bias_cases)(na_rpb) * LOG2E,
        na_mbias=na_meta_bias.astype(F32) * LOG2E,
        w_branch=w_branch.astype(BF16),
        w_out=w_out.astype(BF16),
        ln1_g=ln1_g.reshape(depth, 1, D), ln1_b=ln1_b.reshape(depth, 1, D),
        wr_hi=wr_hi, wr_lo=(wr_t - wr_hi.astype(F32)).astype(BF16),
        rbias=router_bias.astype(F32).reshape(depth, N_EXPERTS, 1),
        wgu=w_expert_gate_up.astype(BF16), wd=w_expert_down.astype(BF16),
        wsgu=w_shared_gate_up.astype(BF16), wsd=w_shared_down.astype(BF16),
        ln2_g=ln2_g.reshape(depth, 1, D), ln2_b=ln2_b.reshape(depth, 1, D),
    )

    def layer(h, p):
        q_all, k_all, v_all, v_ones = _inproj(h, p["w_qkv"], cos_tab, sin_tab, p["q_gain"], p["k_gain"],
                                      ones_bd, pos_block, tm)
        o = None
        for grp in (g0, g1):
            o = _global_attn(grp, q_all, k_all, v_ones, o)
            o = _window_attn(grp, q_all, k_all, v_all, p["sink"], bband, bmeta[:, :grp.S], o)
            o = _na_attn(grp, q_all, k_all, v_all, p["na_bias"], p["na_mbias"], o)
        for grp in (g0, g1):
            o = _meta_attn(grp, q_all, k_all, v_all, p["sink"], bq_meta, bq_blk, p["na_mbias"], o)
        h1, gates = _merge(h, o, p["w_gate"], p["w_branch"], p["w_out"], p["ln1_g"], p["ln1_b"],
                           p["wr_hi"], p["wr_lo"], p["rbias"], alpha, tm)
        h2 = _moe(h1, gates, p["wgu"], p["wd"], p["wsgu"], p["wsd"], p["ln2_g"], p["ln2_b"],
                  alpha)
        return h2, None

    h, _ = lax.scan(layer, h, layers)
    y_prompt = h[:B0 * S0].reshape(B0, S0, D)
    y_sample = h[B0 * S0:real].reshape(B1, S1, D)
    return (y_prompt, y_sample)
```

```python
import functools
import math

import numpy as np
import jax
import jax.numpy as jnp
from jax import lax
from jax.experimental import pallas as pl
from jax.experimental.pallas import tpu as pltpu

F32 = jnp.float32
BF16 = jnp.bfloat16

D_MODEL = 1024
HEAD_DIM = 64
N_HEADS = 8
N_KV_HEADS = 2
GROUP = N_HEADS // N_KV_HEADS
MIX_WIDTH = N_HEADS * HEAD_DIM
KV_WIDTH = N_KV_HEADS * HEAD_DIM
N_MIXERS = 3
QKV_WIDTH = MIX_WIDTH + 2 * KV_WIDTH
N_META = 16
GRID_W = 64
BLOCK = 128
WINDOW = 128
NA_ROWS = 8
NA_COLS = 16
T5_BUCKETS = 32
T5_MAX_DIST = 128
ROPE_THETA = 10000.0
N_EXPERTS = 64
TOP_K = 8
N_EXPERT_GROUPS = 8
TOPK_GROUPS = 4
D_EXPERT = 256
ROUTED_SCALE = 2.5
NEG_INF = -1e30
LOG2E = math.log2(math.e)
LANES = 128

TM = 512
TQ_GLOBAL = 256
TK_GLOBAL = 512
NA_QROWS = 8
WIN_QBLOCKS = 4
VMEM_LIMIT = 56 * 1024 * 1024


def _cparams(sem):
    return pltpu.CompilerParams(dimension_semantics=sem, vmem_limit_bytes=VMEM_LIMIT)


def _dot(a, b):
    return jnp.dot(a, b, preferred_element_type=F32)


def _dot_nt(a, b):
    return lax.dot_general(a, b, (((1,), (1,)), ((), ())), preferred_element_type=F32)


def _split_bf16(x):
    hi = x.astype(BF16)
    lo = (x - hi.astype(F32)).astype(BF16)
    return hi, lo


def _layer_norm(x, g, b):
    mu = jnp.mean(x, axis=-1, keepdims=True)
    xc = x - mu
    var = jnp.mean(xc * xc, axis=-1, keepdims=True)
    return xc * lax.rsqrt(var + 1e-5) * g + b


def _embed_ln_kernel(x_ref, g_ref, b_ref, o_ref):
    o_ref[...] = _layer_norm(x_ref[...], g_ref[...], b_ref[...])


def _embed_ln(x, g, b):
    R = x.shape[0]
    return pl.pallas_call(
        _embed_ln_kernel,
        grid=(R // TM,),
        in_specs=[pl.BlockSpec((TM, D_MODEL), lambda i: (i, 0)),
                  pl.BlockSpec((1, D_MODEL), lambda i: (0, 0)),
                  pl.BlockSpec((1, D_MODEL), lambda i: (0, 0))],
        out_specs=pl.BlockSpec((TM, D_MODEL), lambda i: (i, 0)),
        out_shape=jax.ShapeDtypeStruct((R, D_MODEL), F32),
        compiler_params=_cparams(("parallel",)),
        name="embed_ln",
    )(x, g, b)


def _rope_slot(x, cos, sin_signed, first_half):
    fwd = pltpu.roll(x, LANES - 16, 1)
    bwd = pltpu.roll(x, 16, 1)
    return x * cos + jnp.where(first_half, fwd, bwd) * sin_signed


def _head_rms(x, ones_bd, gain):
    hi, lo = _split_bf16(x * x)
    ss = _dot(hi, ones_bd) + _dot(lo, ones_bd)
    return x * lax.rsqrt(ss * (1.0 / HEAD_DIM) + 1e-6) * gain


def _inproj_kernel(h_ref, w_ref, cos_ref, sin_ref, qg_ref, kg_ref, ones_ref, q_ref, k_ref, v_ref, v1_ref):
    x = h_ref[...].astype(BF16)
    cos = cos_ref[...]
    sin = sin_ref[...]
    lane = lax.broadcasted_iota(jnp.int32, cos.shape, 1)
    first_half = (lane % 32) < 16
    scale = HEAD_DIM ** -0.5
    qw = N_MIXERS * MIX_WIDTH
    qa = _head_rms(_dot(x, w_ref[:, 0:MIX_WIDTH]), ones_ref[...], qg_ref[...])
    for s in range(MIX_WIDTH // LANES):
        sl = slice(s * LANES, (s + 1) * LANES)
        q_ref[:, sl] = (_rope_slot(qa[:, sl], cos, sin, first_half) * (scale * LOG2E)).astype(BF16)
    ka = _head_rms(_dot(x, w_ref[:, qw:qw + KV_WIDTH]), ones_ref[0:LANES, 0:LANES], kg_ref[...])
    k_ref[:, 0:KV_WIDTH] = _rope_slot(ka, cos, sin, first_half).astype(BF16)
    for n in range(1, N_MIXERS):
        q_ref[:, n * MIX_WIDTH:(n + 1) * MIX_WIDTH] = (
            _dot(x, w_ref[:, n * MIX_WIDTH:(n + 1) * MIX_WIDTH]) * (scale * LOG2E)).astype(BF16)
        k_ref[:, n * KV_WIDTH:(n + 1) * KV_WIDTH] = _dot(
            x, w_ref[:, qw + n * KV_WIDTH:qw + (n + 1) * KV_WIDTH]).astype(BF16)
    vw = qw + N_MIXERS * KV_WIDTH
    v = _dot(x, w_ref[:, vw:vw + N_MIXERS * KV_WIDTH])
    v_ref[...] = v.astype(BF16)
    va = v[:, 0:KV_WIDTH]
    lo = lane < HEAD_DIM
    v1_ref[:, 0:LANES] = jnp.where(lo, va, 1.0).astype(BF16)
    v1_ref[:, LANES:2 * LANES] = jnp.where(lo, pltpu.roll(va, HEAD_DIM, 1), 1.0).astype(BF16)


def _inproj(h, w_qkv, cos_tab, sin_tab, q_gain, k_gain, ones_bd, pos_block, tm):
    R = h.shape[0]
    const = lambda i: (0, 0)
    return pl.pallas_call(
        _inproj_kernel,
        grid=(R // tm,),
        in_specs=[pl.BlockSpec((tm, D_MODEL), lambda i: (i, 0)),
                  pl.BlockSpec(w_qkv.shape, const),
                  pl.BlockSpec((tm, LANES), lambda i: (pos_block(i), 0)),
                  pl.BlockSpec((tm, LANES), lambda i: (pos_block(i), 0)),
                  pl.BlockSpec((1, MIX_WIDTH), const),
                  pl.BlockSpec((1, KV_WIDTH), const),
                  pl.BlockSpec((MIX_WIDTH, MIX_WIDTH), const)],
        out_specs=[pl.BlockSpec((tm, N_MIXERS * MIX_WIDTH), lambda i: (i, 0)),
                   pl.BlockSpec((tm, N_MIXERS * KV_WIDTH), lambda i: (i, 0)),
                   pl.BlockSpec((tm, N_MIXERS * KV_WIDTH), lambda i: (i, 0)),
                   pl.BlockSpec((tm, N_KV_HEADS * LANES), lambda i: (i, 0))],
        out_shape=[jax.ShapeDtypeStruct((R, N_MIXERS * MIX_WIDTH), BF16),
                   jax.ShapeDtypeStruct((R, N_MIXERS * KV_WIDTH), BF16),
                   jax.ShapeDtypeStruct((R, N_MIXERS * KV_WIDTH), BF16),
                   jax.ShapeDtypeStruct((R, N_KV_HEADS * LANES), BF16)],
        compiler_params=_cparams(("parallel",)),
        name="inproj",
    )(h, w_qkv, cos_tab, sin_tab, q_gain, k_gain, ones_bd)


def _group_queries(q, j):
    lane = lax.broadcasted_iota(jnp.int32, (q.shape[0], LANES), 1)
    keep = (lane < HEAD_DIM) if j == 0 else (lane >= HEAD_DIM)
    parts = []
    for hh in range(GROUP):
        h = GROUP * j + hh
        slot = q[:, (h // 2) * LANES:(h // 2 + 1) * LANES]
        if h % 2 != j:
            slot = pltpu.roll(slot, HEAD_DIM, 1)
        parts.append(jnp.where(keep, slot, 0.0))
    return jnp.concatenate(parts, axis=0).astype(BF16)


def _ungroup_outputs(out, j, T):
    lane = lax.broadcasted_iota(jnp.int32, (T, LANES), 1)
    lo = lane < HEAD_DIM
    slots = []
    for s in range(2):
        even = out[(2 * s) * T:(2 * s + 1) * T]
        odd = out[(2 * s + 1) * T:(2 * s + 2) * T]
        if j == 0:
            slots.append(jnp.where(lo, even, pltpu.roll(odd, HEAD_DIM, 1)))
        else:
            slots.append(jnp.where(lo, pltpu.roll(even, HEAD_DIM, 1), odd))
    return jnp.concatenate(slots, axis=1)


def _fold_lanes(op, *parts):
    cols = [p[:, c * LANES:(c + 1) * LANES] for p in parts for c in range(p.shape[1] // LANES)]
    return functools.reduce(op, cols)


def _row_max(*parts):
    return jnp.max(_fold_lanes(jnp.maximum, *parts), axis=-1, keepdims=True)


def _row_sum(*parts):
    return jnp.sum(_fold_lanes(jnp.add, *parts), axis=-1, keepdims=True)


def _pad_meta_rows(x):
    return jnp.concatenate([x, jnp.zeros((LANES - N_META, x.shape[1]), x.dtype)], axis=0)


def _head_rows(vals, T):
    return jnp.concatenate([jnp.broadcast_to(v, (T, v.shape[-1])) for v in vals], axis=0)


def _global_kernel(q_ref, k_ref, v_ref, km_ref, vm_ref, *rest, S):
    o_ref = rest[-1]
    T = q_ref.shape[0]
    q = q_ref[...].astype(F32)
    km = km_ref[...]
    qs, state = [], []
    for j in range(N_KV_HEADS):
        qj = _group_queries(q, j)
        s_m = _dot_nt(qj, km)
        m0 = jnp.max(s_m, axis=-1, keepdims=True)
        p_m = jnp.exp2((s_m - m0).astype(BF16))
        qs.append(qj)
        state.append((m0, _dot(p_m, vm_ref[:, j * LANES:(j + 1) * LANES])))
    for c in range(S // TK_GLOBAL):
        kc = k_ref[c * TK_GLOBAL:(c + 1) * TK_GLOBAL, :]
        for j in range(N_KV_HEADS):
            m, acc = state[j]
            s = _dot_nt(qs[j], kc)
            m_new = jnp.maximum(m, jnp.max(s, axis=-1, keepdims=True))
            p = jnp.exp2((s - m_new).astype(BF16))
            vc = v_ref[c * TK_GLOBAL:(c + 1) * TK_GLOBAL, j * LANES:(j + 1) * LANES]
            state[j] = (m_new, jnp.exp2(m - m_new) * acc + _dot(p, vc))
    for j in range(N_KV_HEADS):
        acc = state[j][1]
        out = acc / pltpu.roll(acc, HEAD_DIM, 1)
        o_ref[:, j * 2 * LANES:(j + 1) * 2 * LANES] = _ungroup_outputs(out, 0, T).astype(BF16)


def _window_kernel(sink_ref, q_ref, kp_ref, kc_ref, kn_ref, vp_ref, vc_ref, vn_ref, km_ref, vm_ref,
                   bband_ref, bmeta_ref, *rest, nb):
    o_ref = rest[-1]
    i = pl.program_id(1)
    T = BLOCK
    k_span = jnp.concatenate([kp_ref[...], kc_ref[...], kn_ref[...]], axis=0)
    v_span = jnp.concatenate([vp_ref[...], vc_ref[...], vn_ref[...]], axis=0)
    col = lax.broadcasted_iota(jnp.int32, (1, 3 * BLOCK), 1)
    km = _pad_meta_rows(km_ref[...])
    vm = _pad_meta_rows(vm_ref[...])
    for u in range(WIN_QBLOCKS):
        q = q_ref[u * BLOCK:(u + 1) * BLOCK, :].astype(F32)
        kband = k_span[u * BLOCK:(u + 3) * BLOCK]
        vband = v_span[u * BLOCK:(u + 3) * BLOCK]
        in_range = None
        if u == 0:
            in_range = (col >= BLOCK) | (i > 0)
        if u == WIN_QBLOCKS - 1:
            after = (col < 2 * BLOCK) | (i < nb - 1)
            in_range = after if in_range is None else in_range & after
        for j in range(N_KV_HEADS):
            qj = _group_queries(q, j)
            hs = slice(GROUP * j, GROUP * (j + 1))
            s_b = _dot_nt(qj, kband) + bband_ref[hs].reshape(GROUP * T, 3 * BLOCK)
            if in_range is not None:
                s_b = jnp.where(in_range, s_b, NEG_INF)
            s_m = _dot_nt(qj, km) + bmeta_ref[hs, u * BLOCK:(u + 1) * BLOCK].reshape(GROUP * T, LANES)
            sink = jnp.concatenate([jnp.full((T, 1), sink_ref[GROUP * j + hh], F32) for hh in range(GROUP)], axis=0)
            m = jnp.maximum(_row_max(s_b, s_m), sink)
            e_b = jnp.exp2(s_b - m)
            e_m = jnp.exp2(s_m - m)
            denom = _row_sum(e_b, e_m) + jnp.exp2(sink - m)
            acc = _dot(e_b.astype(BF16), vband) + _dot(e_m.astype(BF16), vm)
            o_ref[u * BLOCK:(u + 1) * BLOCK, j * 2 * LANES:(j + 1) * 2 * LANES] = (
                _ungroup_outputs(acc / denom, j, T).astype(BF16))


def _na_kernel(q_ref, k_ref, v_ref, km_ref, vm_ref, bias_ref, mbias_ref, *rest, rows):
    o_ref = rest[-1]
    blk = pl.program_id(1)
    W = GRID_W
    nkeys = NA_ROWS * W
    km = _pad_meta_rows(km_ref[...])
    vm = _pad_meta_rows(vm_ref[...])
    mb = _head_rows([mbias_ref[h:h + 1, :] for h in range(N_HEADS)], W)
    half = GROUP * W
    for rr in range(NA_QROWS):
        r = blk * NA_QROWS + rr
        rs = jnp.clip(r - NA_ROWS // 2, 0, rows - NA_ROWS)
        delta = r - rs
        koff = pl.multiple_of(rs * W, W)
        q = q_ref[rr * W:(rr + 1) * W, :].astype(F32)
        q8 = jnp.concatenate([_group_queries(q, j) for j in range(N_KV_HEADS)], axis=0)
        kw = k_ref[pl.ds(koff, nkeys), :]
        vw = v_ref[pl.ds(koff, nkeys), :]
        s_w = _dot_nt(q8, kw) + bias_ref[delta].reshape(N_HEADS * W, nkeys)
        s_m = _dot_nt(q8, km) + mb
        m = _row_max(s_w, s_m)
        e_w = jnp.exp2(s_w - m)
        e_m = jnp.exp2(s_m - m)
        out = (_dot(e_w.astype(BF16), vw) + _dot(e_m.astype(BF16), vm)) / _row_sum(e_w, e_m)
        for j in range(N_KV_HEADS):
            o_ref[rr * W:(rr + 1) * W, j * 2 * LANES:(j + 1) * 2 * LANES] = (
                _ungroup_outputs(out[j * half:(j + 1) * half], j, W).astype(BF16))


def _meta_kernel(sink_ref, q_ref, ka_ref, va_ref, kb_ref, vb_ref, km_ref, vm_ref,
                 bq_meta_ref, bq_blk_ref, mbias_ref, o_in_ref, o_ref):
    del o_in_ref
    T = N_META
    q = q_ref[...].astype(F32)
    km_all = km_ref[...]
    vm_all = vm_ref[...]

    def finish(n, j, acc, denom):
        lo = n * MIX_WIDTH + j * 2 * LANES
        o_ref[:, lo:lo + 2 * LANES] = _ungroup_outputs(acc / denom, j, T).astype(BF16)

    for j in range(N_KV_HEADS):
        hs = slice(GROUP * j, GROUP * (j + 1))
        qj = _group_queries(q[:, 0:MIX_WIDTH], j)
        km, vm = km_all[:, 0:KV_WIDTH], vm_all[:, 0:KV_WIDTH]
        s_r = _dot_nt(qj, ka_ref[...])
        s_m = _dot_nt(qj, km)
        m = jnp.maximum(jnp.max(s_r, axis=-1, keepdims=True), jnp.max(s_m, axis=-1, keepdims=True))
        e_r = jnp.exp2(s_r - m)
        e_m = jnp.exp2(s_m - m)
        denom = jnp.sum(e_r, axis=-1, keepdims=True) + jnp.sum(e_m, axis=-1, keepdims=True)
        finish(0, j, _dot(e_r.astype(BF16), va_ref[...]) + _dot(e_m.astype(BF16), vm), denom)
        qj = _group_queries(q[:, MIX_WIDTH:2 * MIX_WIDTH], j)
        km, vm = km_all[:, KV_WIDTH:2 * KV_WIDTH], vm_all[:, KV_WIDTH:2 * KV_WIDTH]
        s_r = _dot_nt(qj, kb_ref[...]) + bq_blk_ref[hs].reshape(GROUP * T, BLOCK)
        s_m = _dot_nt(qj, km) + bq_meta_ref[hs].reshape(GROUP * T, N_META)
        sink = jnp.concatenate([jnp.full((T, 1), sink_ref[GROUP * j + hh], F32) for hh in range(GROUP)], axis=0)
        m = jnp.maximum(jnp.maximum(jnp.max(s_r, axis=-1, keepdims=True),
                                    jnp.max(s_m, axis=-1, keepdims=True)), sink)
        e_r = jnp.exp2(s_r - m)
        e_m = jnp.exp2(s_m - m)
        denom = (jnp.sum(e_r, axis=-1, keepdims=True) + jnp.sum(e_m, axis=-1, keepdims=True)
                 + jnp.exp2(sink - m))
        finish(1, j, _dot(e_r.astype(BF16), vb_ref[...]) + _dot(e_m.astype(BF16), vm), denom)
        qj = _group_queries(q[:, 2 * MIX_WIDTH:3 * MIX_WIDTH], j)
        km, vm = km_all[:, 2 * KV_WIDTH:3 * KV_WIDTH], vm_all[:, 2 * KV_WIDTH:3 * KV_WIDTH]
        mb = _head_rows([mbias_ref[GROUP * j + hh:GROUP * j + hh + 1, :] for hh in range(GROUP)], T)
        s_m = _dot_nt(qj, km) + mb
        m = jnp.max(s_m, axis=-1, keepdims=True)
        e_m = jnp.exp2(s_m - m)
        finish(2, j, _dot(e_m.astype(BF16), vm), jnp.sum(e_m, axis=-1, keepdims=True))


class _Group:
    def __init__(self, B, S, real_base, meta_batch_base, meta_base, n_meta_blocks):
        self.B, self.S = B, S
        self.real_base = real_base
        self.meta_blk0 = meta_base // N_META + meta_batch_base
        self.n_meta_blocks = n_meta_blocks
        assert real_base % S == 0 and S % TM == 0 and meta_base % N_META == 0


def _alias_args(o_prev, n_inputs):
    if o_prev is None:
        return [], [], {}
    return [o_prev], [pl.BlockSpec(memory_space=pl.ANY)], {n_inputs: 0}


def _global_attn(grp, q_all, k_all, v_ones, o_prev):
    B, S = grp.B, grp.S
    nq = S // TQ_GLOBAL
    qb0 = grp.real_base // TQ_GLOBAL
    sb0 = grp.real_base // S
    mb0 = grp.meta_blk0
    in_specs = [pl.BlockSpec((TQ_GLOBAL, MIX_WIDTH), lambda b, i: (qb0 + b * nq + i, 0)),
                pl.BlockSpec((S, KV_WIDTH), lambda b, i: (sb0 + b, 0)),
                pl.BlockSpec((S, N_KV_HEADS * LANES), lambda b, i: (sb0 + b, 0)),
                pl.BlockSpec((N_META, KV_WIDTH), lambda b, i: (mb0 + b, 0)),
                pl.BlockSpec((N_META, N_KV_HEADS * LANES), lambda b, i: (mb0 + b, 0))]
    extra, extra_specs, aliases = _alias_args(o_prev, len(in_specs))
    return pl.pallas_call(
        functools.partial(_global_kernel, S=S),
        grid=(B, nq),
        in_specs=in_specs + extra_specs,
        out_specs=pl.BlockSpec((TQ_GLOBAL, MIX_WIDTH), lambda b, i: (qb0 + b * nq + i, 0)),
        out_shape=jax.ShapeDtypeStruct((q_all.shape[0], N_MIXERS * MIX_WIDTH), BF16),
        input_output_aliases=aliases,
        compiler_params=_cparams(("parallel", "arbitrary")),
        name="mixer_global",
    )(q_all, k_all, v_ones, k_all, v_ones, *extra)


def _window_attn(grp, q_all, k_all, v_all, sink, bband, bmeta, o_prev):
    B, S = grp.B, grp.S
    span = WIN_QBLOCKS * BLOCK
    nb, ns = S // BLOCK, S // span
    qb0 = grp.real_base // BLOCK
    sp0 = grp.real_base // span
    mb0 = grp.meta_blk0
    cur = lambda b, i, sink: (sp0 + b * ns + i, 1)
    prv = lambda b, i, sink: (qb0 + b * nb + jnp.maximum(i * WIN_QBLOCKS - 1, 0), 1)
    nxt = lambda b, i, sink: (qb0 + b * nb + jnp.minimum((i + 1) * WIN_QBLOCKS, nb - 1), 1)
    met = lambda b, i, sink: (mb0 + b, 1)
    edge = lambda im: pl.BlockSpec((BLOCK, KV_WIDTH), im)
    mid = pl.BlockSpec((span, KV_WIDTH), cur)
    in_specs = [pl.BlockSpec((span, MIX_WIDTH), cur),
                edge(prv), mid, edge(nxt), edge(prv), mid, edge(nxt),
                pl.BlockSpec((N_META, KV_WIDTH), met), pl.BlockSpec((N_META, KV_WIDTH), met),
                pl.BlockSpec((N_HEADS, BLOCK, 3 * BLOCK), lambda b, i, sink: (0, 0, 0)),
                pl.BlockSpec((N_HEADS, span, LANES), lambda b, i, sink: (0, i, 0))]
    extra, extra_specs, aliases = _alias_args(o_prev, len(in_specs) + 1)
    return pl.pallas_call(
        functools.partial(_window_kernel, nb=ns),
        grid_spec=pltpu.PrefetchScalarGridSpec(
            num_scalar_prefetch=1,
            grid=(B, ns),
            in_specs=in_specs + extra_specs,
            out_specs=pl.BlockSpec((span, MIX_WIDTH), cur)),
        out_shape=jax.ShapeDtypeStruct((q_all.shape[0], N_MIXERS * MIX_WIDTH), BF16),
        input_output_aliases=aliases,
        compiler_params=_cparams(("parallel", "arbitrary")),
        name="mixer_window",
    )(sink, q_all, k_all, k_all, k_all, v_all, v_all, v_all, k_all, v_all, bband, bmeta, *extra)


def _na_attn(grp, q_all, k_all, v_all, na_bias, na_mbias, o_prev):
    B, S = grp.B, grp.S
    rows = S // GRID_W
    tq = NA_QROWS * GRID_W
    nq = S // tq
    qb0 = grp.real_base // tq
    sb0 = grp.real_base // S
    mb0 = grp.meta_blk0
    in_specs = [pl.BlockSpec((tq, MIX_WIDTH), lambda b, i: (qb0 + b * nq + i, 2)),
                pl.BlockSpec((S, KV_WIDTH), lambda b, i: (sb0 + b, 2)),
                pl.BlockSpec((S, KV_WIDTH), lambda b, i: (sb0 + b, 2)),
                pl.BlockSpec((N_META, KV_WIDTH), lambda b, i: (mb0 + b, 2)),
                pl.BlockSpec((N_META, KV_WIDTH), lambda b, i: (mb0 + b, 2)),
                pl.BlockSpec(na_bias.shape, lambda b, i: (0, 0, 0, 0)),
                pl.BlockSpec(na_mbias.shape, lambda b, i: (0, 0))]
    extra, extra_specs, aliases = _alias_args(o_prev, len(in_specs))
    return pl.pallas_call(
        functools.partial(_na_kernel, rows=rows),
        grid=(B, nq),
        in_specs=in_specs + extra_specs,
        out_specs=pl.BlockSpec((tq, MIX_WIDTH), lambda b, i: (qb0 + b * nq + i, 2)),
        out_shape=jax.ShapeDtypeStruct((q_all.shape[0], N_MIXERS * MIX_WIDTH), BF16),
        input_output_aliases=aliases,
        compiler_params=_cparams(("parallel", "arbitrary")),
        name="mixer_neighbourhood",
    )(q_all, k_all, v_all, k_all, v_all, na_bias, na_mbias, *extra)


def _meta_attn(grp, q_all, k_all, v_all, sink, bq_meta, bq_blk, na_mbias, o_prev):
    B, S = grp.B, grp.S
    sb0 = grp.real_base // S
    bb0 = grp.real_base // BLOCK
    nb = S // BLOCK
    mb0 = grp.meta_blk0
    clamp = lambda b: jnp.minimum(b, B - 1)
    mrow = lambda b, sink: (mb0 + clamp(b), 0)
    in_specs = [pl.BlockSpec((N_META, N_MIXERS * MIX_WIDTH), mrow),
                pl.BlockSpec((S, KV_WIDTH), lambda b, sink: (sb0 + clamp(b), 0)),
                pl.BlockSpec((S, KV_WIDTH), lambda b, sink: (sb0 + clamp(b), 0)),
                pl.BlockSpec((BLOCK, KV_WIDTH), lambda b, sink: (bb0 + clamp(b) * nb, 1)),
                pl.BlockSpec((BLOCK, KV_WIDTH), lambda b, sink: (bb0 + clamp(b) * nb, 1)),
                pl.BlockSpec((N_META, N_MIXERS * KV_WIDTH), mrow),
                pl.BlockSpec((N_META, N_MIXERS * KV_WIDTH), mrow),
                pl.BlockSpec(bq_meta.shape, lambda b, sink: (0, 0, 0)),
                pl.BlockSpec(bq_blk.shape, lambda b, sink: (0, 0, 0)),
                pl.BlockSpec(na_mbias.shape, lambda b, sink: (0, 0)),
                pl.BlockSpec(memory_space=pl.ANY)]
    return pl.pallas_call(
        _meta_kernel,
        grid_spec=pltpu.PrefetchScalarGridSpec(
            num_scalar_prefetch=1,
            grid=(grp.n_meta_blocks,),
            in_specs=in_specs,
            out_specs=pl.BlockSpec((N_META, N_MIXERS * MIX_WIDTH), lambda b, sink: (mb0 + b, 0))),
        out_shape=jax.ShapeDtypeStruct((q_all.shape[0], N_MIXERS * MIX_WIDTH), BF16),
        input_output_aliases={len(in_specs): 0},
        compiler_params=_cparams(("arbitrary",)),
        name="mixer_meta_queries",
    )(sink, q_all, k_all, v_all, k_all, v_all, k_all, v_all, bq_meta, bq_blk, na_mbias, o_prev)


def _route(h1, wr_hi, wr_lo, rbias):
    T = h1.shape[0]
    x_hi, x_lo = _split_bf16(h1)
    logits = _dot_nt(wr_hi, x_hi) + _dot_nt(wr_hi, x_lo) + _dot_nt(wr_lo, x_hi)
    scores = 1.0 / (1.0 + jnp.exp(-logits))
    sel = scores + rbias
    per_group = N_EXPERTS // N_EXPERT_GROUPS
    sel3 = sel.reshape(N_EXPERT_GROUPS, per_group, T)
    idx3 = lax.broadcasted_iota(jnp.int32, sel3.shape, 1).astype(F32)
    m1 = jnp.max(sel3, axis=1, keepdims=True)
    first = jnp.min(jnp.where(sel3 == m1, idx3, float(per_group)), axis=1, keepdims=True)
    m2 = jnp.max(jnp.where(idx3 == first, -jnp.inf, sel3), axis=1, keepdims=True)
    gscore = (m1 + m2).reshape(N_EXPERT_GROUPS, T)

    def rank_of(vals):
        idx = lax.broadcasted_iota(jnp.int32, vals.shape, 0)
        rank = jnp.zeros(vals.shape, F32)
        for r in range(vals.shape[0]):
            row = vals[r:r + 1, :]
            ge = jnp.where(row >= vals, 1.0, 0.0)
            gt = jnp.where(row > vals, 1.0, 0.0)
            rank = rank + jnp.where(idx > r, ge, gt)
        return rank

    gkeep = jnp.where(rank_of(gscore) < TOPK_GROUPS, 1.0, 0.0)
    ekeep = jnp.broadcast_to(gkeep.reshape(N_EXPERT_GROUPS, 1, T), sel3.shape).reshape(N_EXPERTS, T)
    masked = jnp.where(ekeep > 0.5, sel, NEG_INF)
    eidx = lax.broadcasted_iota(jnp.int32, masked.shape, 0).astype(F32)
    chosen = jnp.zeros(masked.shape, F32)
    for _ in range(TOP_K):
        best = jnp.max(masked, axis=0, keepdims=True)
        first = jnp.min(jnp.where(masked == best, eidx, float(N_EXPERTS)), axis=0, keepdims=True)
        hit = eidx == first
        chosen = jnp.where(hit, 1.0, chosen)
        masked = jnp.where(hit, -jnp.inf, masked)
    w = jnp.where(chosen > 0.5, scores, 0.0)
    return w / jnp.sum(w, axis=0, keepdims=True) * ROUTED_SCALE


def _merge_kernel(h_ref, o_ref, wg_ref, wb_ref, wo_ref, g_ref, b_ref, wrh_ref, wrl_ref, rb_ref,
                  h1_ref, gates_ref, *, alpha):
    h = h_ref[...]
    x = h.astype(BF16)
    merged = None
    for n in range(N_MIXERS):
        logit = _dot(x, wg_ref[:, n * D_MODEL:(n + 1) * D_MODEL])
        branch = _dot(o_ref[:, n * MIX_WIDTH:(n + 1) * MIX_WIDTH], wb_ref[n])
        term = branch / (1.0 + jnp.exp(-logit))
        merged = term if merged is None else merged + term
    mix = _dot(merged.astype(BF16), wo_ref[...])
    h1 = _layer_norm(alpha * h + mix, g_ref[...], b_ref[...])
    h1_ref[...] = h1
    gates_t = _route(h1, wrh_ref[...], wrl_ref[...], rb_ref[...])
    pad = jnp.zeros((LANES - N_EXPERTS, gates_t.shape[1]), F32)
    gates_ref[...] = jnp.concatenate([gates_t, pad], axis=0).T


def _merge(h, o_all, w_gate, w_branch, w_out, ln_g, ln_b, wr_hi, wr_lo, rbias, alpha, tm):
    R = h.shape[0]
    c2 = lambda i: (0, 0)
    once = pl.Buffered(1)
    return pl.pallas_call(
        functools.partial(_merge_kernel, alpha=alpha),
        grid=(R // tm,),
        in_specs=[pl.BlockSpec((tm, D_MODEL), lambda i: (i, 0)),
                  pl.BlockSpec((tm, N_MIXERS * MIX_WIDTH), lambda i: (i, 0)),
                  pl.BlockSpec(w_gate.shape, c2, pipeline_mode=once),
                  pl.BlockSpec(w_branch.shape, lambda i: (0, 0, 0), pipeline_mode=once),
                  pl.BlockSpec(w_out.shape, c2, pipeline_mode=once),
                  pl.BlockSpec((1, D_MODEL), c2),
                  pl.BlockSpec((1, D_MODEL), c2),
                  pl.BlockSpec(wr_hi.shape, c2),
                  pl.BlockSpec(wr_lo.shape, c2),
                  pl.BlockSpec(rbias.shape, c2)],
        out_specs=[pl.BlockSpec((tm, D_MODEL), lambda i: (i, 0)),
                   pl.BlockSpec((tm, LANES), lambda i: (i, 0))],
        out_shape=[jax.ShapeDtypeStruct((R, D_MODEL), F32),
                   jax.ShapeDtypeStruct((R, LANES), F32)],
        compiler_params=_cparams(("parallel",)),
        name="merge_ln_route",
    )(h, o_all, w_gate, w_branch, w_out, ln_g, ln_b, wr_hi, wr_lo, rbias)


TD = 256
CH = 16
SLOTS = 3072
NCH = SLOTS // CH
MT = 512
ME = 1024
CPM = ME // CH
XBUFS = 3


def _swiglu_act(gu):
    g = gu[:, :D_EXPERT]
    return g / (1.0 + jnp.exp(-g)) * gu[:, D_EXPERT:]


def _slot_of_token(gates, lo_row):
    routed = gates > 0.0
    r = lax.broadcasted_iota(jnp.int32, (TD, TD), 0)
    c = lax.broadcasted_iota(jnp.int32, (TD, TD), 1)
    earlier = jnp.where(c < r, 1.0, 0.0).astype(BF16)
    rank = _dot(earlier, jnp.where(routed, 1.0, 0.0).astype(BF16))
    return jnp.where(routed, lo_row + rank + 1.0, 0.0)


def _split64(x):
    hi = 64.0 * jnp.floor(x * (1.0 / 64.0))
    return jnp.concatenate([hi.astype(BF16), (x - hi).astype(BF16)], axis=0)


def _dispatch_kernel(h_ref, g_ref, lohi_ref, x_ref, w_ref):
    lohi = lohi_ref[0]
    lo_row, hi_row = lohi[0:1], lohi[1:2]
    gates = g_ref[...]
    slot_t = _split64(_slot_of_token(gates, lo_row).T)
    gates_t = gates.T.astype(BF16)
    x = h_ref[...].astype(BF16)
    for blk in range(SLOTS // MT):
        s = (lax.broadcasted_iota(jnp.int32, (MT, LANES), 0) + blk * MT).astype(F32)
        owner = jnp.where(s >= lo_row, jnp.where(s < hi_row, 1.0, 0.0), 0.0).astype(BF16)
        want = _dot(jnp.concatenate([owner, owner], axis=1), slot_t)
        s1 = (lax.broadcasted_iota(jnp.int32, (MT, TD), 0) + (blk * MT + 1)).astype(F32)
        hit = want == s1
        x_ref[blk * MT:(blk + 1) * MT, :] = _dot(jnp.where(hit, 1.0, 0.0).astype(BF16), x).astype(BF16)
        weight = jnp.where(hit, _dot(owner, gates_t), 0.0)
        w_ref[:, blk * MT:(blk + 1) * MT] = weight.T.astype(BF16)


def _dispatch(h1, gates, lohi):
    n = h1.shape[0] // TD
    return pl.pallas_call(
        _dispatch_kernel,
        grid=(n,),
        in_specs=[pl.BlockSpec((TD, D_MODEL), lambda i: (i, 0)),
                  pl.BlockSpec((TD, LANES), lambda i: (i, 0)),
                  pl.BlockSpec((1, 8, LANES), lambda i: (i, 0, 0))],
        out_specs=[pl.BlockSpec((SLOTS, D_MODEL), lambda i: (i, 0)),
                   pl.BlockSpec((TD, SLOTS), lambda i: (i, 0))],
        out_shape=[jax.ShapeDtypeStruct((n * SLOTS, D_MODEL), BF16),
                   jax.ShapeDtypeStruct((n * TD, SLOTS), BF16)],
        compiler_params=_cparams(("parallel",)),
        name="moe_dispatch",
    )(h1, gates, lohi)


def _chunk_gather(table_ref, first, n_chunks, src_hbm, buf, sem):
    copies = []
    for c in range(n_chunks):
        row = pl.multiple_of(table_ref[first + c] * CH, CH)
        copies.append(pltpu.make_async_copy(src_hbm.at[pl.ds(row, CH)], buf.at[pl.ds(c * CH, CH)], sem))
    return copies


def _chunk_wait(n_chunks, src_hbm, buf, sem):
    for c in range(n_chunks):
        pltpu.make_async_copy(src_hbm.at[pl.ds(0, CH)], buf.at[pl.ds(c * CH, CH)], sem).wait()


def _expert_kernel(te_ref, src_ref, nu_ref, x_hbm, wgu_ref, wd_ref, y_ref, xbuf, sem):
    del te_ref
    m = pl.program_id(0)
    n_used = nu_ref[0]

    def start(step):
        slot = step % XBUFS
        for cp in _chunk_gather(src_ref, step * CPM, CPM, x_hbm, xbuf.at[slot], sem.at[slot]):
            cp.start()

    for ahead in range(XBUFS - 1):
        @pl.when((m == 0) & (ahead < n_used))
        def _(ahead=ahead):
            start(ahead)

    @pl.when(m + (XBUFS - 1) < n_used)
    def _():
        start(m + (XBUFS - 1))

    @pl.when(m < n_used)
    def _():
        slot = m % XBUFS
        _chunk_wait(CPM, x_hbm, xbuf.at[slot], sem.at[slot])
        act = _swiglu_act(_dot(xbuf[slot], wgu_ref[0]))
        y_ref[...] = _dot(act.astype(BF16), wd_ref[0]).astype(BF16)


def _experts(x_disp, wgu, wd, tile_expert, src_chunk, n_used):
    n_steps = tile_expert.shape[0]
    return pl.pallas_call(
        _expert_kernel,
        grid_spec=pltpu.PrefetchScalarGridSpec(
            num_scalar_prefetch=3,
            grid=(n_steps,),
            in_specs=[pl.BlockSpec(memory_space=pl.ANY),
                      pl.BlockSpec((1, D_MODEL, 2 * D_EXPERT), lambda m, te, src, nu: (te[m], 0, 0)),
                      pl.BlockSpec((1, D_EXPERT, D_MODEL), lambda m, te, src, nu: (te[m], 0, 0))],
            out_specs=pl.BlockSpec((ME, D_MODEL), lambda m, te, src, nu: (jnp.minimum(m, nu[0] - 1), 0)),
            scratch_shapes=[pltpu.VMEM((XBUFS, ME, D_MODEL), BF16), pltpu.SemaphoreType.DMA((XBUFS,))]),
        out_shape=jax.ShapeDtypeStruct((n_steps * ME, D_MODEL), BF16),
        compiler_params=_cparams(("arbitrary",)),
        name="moe_experts",
    )(tile_expert, src_chunk, n_used, x_disp, wgu, wd)


def _combine_kernel(dst_ref, y_hbm, w_ref, h_ref, wsgu_ref, wsd_ref, lg_ref, lb_ref, o_ref, ybuf, sem, *, alpha):
    i = pl.program_id(0)

    def start(tile):
        slot = tile % 2
        for cp in _chunk_gather(dst_ref, tile * NCH, NCH, y_hbm, ybuf.at[slot], sem.at[slot]):
            cp.start()

    @pl.when(i == 0)
    def _():
        start(i)

    @pl.when(i + 1 < pl.num_programs(0))
    def _():
        start(i + 1)

    h = h_ref[...]
    shared = _dot(_swiglu_act(_dot(h.astype(BF16), wsgu_ref[...])).astype(BF16), wsd_ref[...])
    slot = i % 2
    _chunk_wait(NCH, y_hbm, ybuf.at[slot], sem.at[slot])
    routed = _dot(w_ref[...], ybuf[slot])
    o_ref[...] = _layer_norm(alpha * h + shared + routed, lg_ref[...], lb_ref[...])


def _combine(y_sorted, w_t, h1, dst_chunk, wsgu, wsd, ln_g, ln_b, alpha):
    n = h1.shape[0] // TD
    c2 = lambda i, dst: (0, 0)
    return pl.pallas_call(
        functools.partial(_combine_kernel, alpha=alpha),
        grid_spec=pltpu.PrefetchScalarGridSpec(
            num_scalar_prefetch=1,
            grid=(n,),
            in_specs=[pl.BlockSpec(memory_space=pl.ANY),
                      pl.BlockSpec((TD, SLOTS), lambda i, dst: (i, 0)),
                      pl.BlockSpec((TD, D_MODEL), lambda i, dst: (i, 0)),
                      pl.BlockSpec(wsgu.shape, c2),
                      pl.BlockSpec(wsd.shape, c2),
                      pl.BlockSpec((1, D_MODEL), c2),
                      pl.BlockSpec((1, D_MODEL), c2)],
            out_specs=pl.BlockSpec((TD, D_MODEL), lambda i, dst: (i, 0)),
            scratch_shapes=[pltpu.VMEM((2, SLOTS, D_MODEL), BF16), pltpu.SemaphoreType.DMA((2,))]),
        out_shape=jax.ShapeDtypeStruct((h1.shape[0], D_MODEL), F32),
        compiler_params=_cparams(("arbitrary",)),
        name="moe_combine_ln",
    )(dst_chunk, y_sorted, w_t, h1, wsgu, wsd, ln_g, ln_b)


def _routing_tables(gates):
    n = gates.shape[0] // TD
    cnt = jnp.sum((gates[:, :N_EXPERTS] > 0.0).reshape(n, TD, N_EXPERTS), axis=1, dtype=jnp.int32)
    nch = (cnt + (CH - 1)) // CH
    hi16 = jnp.cumsum(nch, axis=1)
    lo16 = hi16 - nch
    nct = hi16[:, -1:]
    pad = jnp.broadcast_to(nct, (n, LANES - N_EXPERTS))
    lohi = jnp.zeros((n, 8, LANES), F32)
    lohi = lohi.at[:, 0, :].set((jnp.concatenate([lo16, pad], axis=1) * CH).astype(F32))
    lohi = lohi.at[:, 1, :].set((jnp.concatenate([hi16, pad], axis=1) * CH).astype(F32))
    tot = jnp.sum(nch, axis=0)
    seg_len = (tot + (CPM - 1)) // CPM * CPM
    seg_end = jnp.cumsum(seg_len)
    seg_start = seg_end - seg_len
    gpos = seg_start[None, :] + jnp.cumsum(nch, axis=0) - nch
    n_steps = (n * NCH + N_EXPERTS * CPM) // CPM
    n_used = (seg_end[-1] // CPM).astype(jnp.int32).reshape(1)
    step = jnp.arange(n_steps, dtype=jnp.int32)
    tile_expert = jnp.sum(seg_end[None, :] // CPM <= jnp.minimum(step, n_used - 1)[:, None], axis=1, dtype=jnp.int32)
    tile_expert = jnp.minimum(tile_expert, N_EXPERTS - 1)
    exact = functools.partial(jnp.dot, precision=lax.Precision.HIGHEST)
    experts = jnp.arange(N_EXPERTS, dtype=jnp.int32)
    g = jnp.arange(n_steps * CPM, dtype=jnp.int32)
    e_of_g = jnp.minimum(jnp.sum(seg_end[None, :] <= g[:, None], axis=1, dtype=jnp.int32), N_EXPERTS - 1)
    pick_e = (e_of_g[:, None] == experts[None, :]).astype(F32)
    first = exact(pick_e, gpos.T.astype(F32))
    count = exact(pick_e, nch.T.astype(F32))
    base = exact(pick_e, (jnp.arange(n, dtype=jnp.int32)[:, None] * NCH + lo16).T.astype(F32))
    gf = g.astype(F32)[:, None]
    inside = (first <= gf) & (gf < first + count)
    src_chunk = jnp.sum(jnp.where(inside, base + gf - first, 0.0), axis=1).astype(jnp.int32)
    k = jnp.arange(NCH, dtype=jnp.int32)
    e_of_k = jnp.minimum(jnp.sum(hi16[:, None, :] <= k[None, :, None], axis=2, dtype=jnp.int32), N_EXPERTS - 1)
    pick_k = e_of_k[:, :, None] == experts[None, None, :]
    pos = jnp.sum(jnp.where(pick_k, (gpos - lo16)[:, None, :], 0), axis=2) + k[None, :]
    dst_chunk = jnp.where(k[None, :] < nct, pos, 0).astype(jnp.int32).reshape(-1)
    return lohi, tile_expert, src_chunk, n_used, dst_chunk


def _moe(h1, gates, wgu, wd, wsgu, wsd, ln_g, ln_b, alpha):
    lohi, tile_expert, src_chunk, n_used, dst_chunk = _routing_tables(gates)
    x_disp, w_t = _dispatch(h1, gates, lohi)
    y_sorted = _experts(x_disp, wgu, wd, tile_expert, src_chunk, n_used)
    return _combine(y_sorted, w_t, h1, dst_chunk, wsgu, wsd, ln_g, ln_b, alpha)


def _t5_bucket(rel):
    half = T5_BUCKETS // 2
    max_exact = half // 2
    n = np.abs(rel)
    ratio = np.log(np.maximum(n, 1).astype(np.float32) / np.float32(max_exact))
    ratio = ratio / np.float32(math.log(T5_MAX_DIST / max_exact)) * np.float32(half - max_exact)
    large = np.minimum(max_exact + ratio.astype(np.int32), half - 1)
    return np.where(rel > 0, half, 0) + np.where(n < max_exact, n, large)


def _t5_tables(t5_table, s_max):
    def bias(rel, valid):
        b = t5_table[_t5_bucket(rel)].astype(F32)
        return jnp.where(jnp.asarray(valid)[None], jnp.transpose(b, (2, 0, 1)), NEG_INF)

    ii = np.arange(BLOCK)[:, None]
    jj = np.arange(3 * BLOCK)[None, :]
    rel = jj - ii - BLOCK
    bband = bias(rel, np.abs(rel) <= WINDOW)
    t = np.arange(s_max)[:, None]
    m = np.arange(N_META)[None, :]
    bmeta = bias(m - (N_META + t), np.ones((s_max, N_META), bool))
    mpos = np.arange(N_META)[:, None]
    kpos = np.arange(N_META + BLOCK)[None, :]
    relq = kpos - mpos
    bq = bias(relq, (kpos < N_META) | (np.abs(relq) <= WINDOW))
    return bband, bmeta, bq[:, :, :N_META], bq[:, :, N_META:]


def _na_bias_cases(rpb):
    W = GRID_W
    delta = np.arange(NA_ROWS)[:, None, None, None]
    i = np.arange(NA_ROWS)[None, :, None, None]
    c = np.arange(W)[None, None, :, None]
    kc = np.arange(W)[None, None, None, :]
    cs = np.clip(c - NA_COLS // 2, 0, W - NA_COLS)
    valid = (kc >= cs) & (kc < cs + NA_COLS)
    dc = np.clip(kc - c + (NA_COLS - 1), 0, 2 * NA_COLS - 2)[0, 0]
    t = jnp.where(jnp.asarray(valid[0, 0]), rpb.astype(F32)[:, :, dc], NEG_INF)
    cases = [jnp.transpose(t[:, NA_ROWS - 1 - d:2 * NA_ROWS - 1 - d], (0, 2, 1, 3)) for d in range(NA_ROWS)]
    return jnp.stack(cases, axis=0).reshape(NA_ROWS, N_HEADS, W, NA_ROWS * W)


def _rope_tables(s_max, n_meta_rows):
    half = HEAD_DIM // 4
    freq = ROPE_THETA ** (-jnp.arange(half, dtype=F32) / half)
    t = np.arange(s_max)
    mp = np.tile(np.arange(N_META) - N_META, n_meta_rows // N_META)
    pos_row = jnp.asarray(np.concatenate([t // GRID_W, mp]), jnp.int32).astype(F32)
    pos_col = jnp.asarray(np.concatenate([t % GRID_W, mp]), jnp.int32).astype(F32)
    ar = pos_row[:, None] * freq
    ac = pos_col[:, None] * freq
    cos = jnp.concatenate([jnp.cos(ar), jnp.cos(ar), jnp.cos(ac), jnp.cos(ac)], axis=1)
    sin = jnp.concatenate([-jnp.sin(ar), jnp.sin(ar), -jnp.sin(ac), jnp.sin(ac)], axis=1)
    return jnp.tile(cos, (1, 2)), jnp.tile(sin, (1, 2))


def kernel(x_prompt, x_sample, meta_tokens, ln_in_g, ln_in_b, t5_table, w_in, q_gain, k_gain, sink,
           na_rpb, na_meta_bias, w_branch, w_out, ln1_g, ln1_b, w_router, router_bias,
           w_expert_gate_up, w_expert_down, w_shared_gate_up, w_shared_down, ln2_g, ln2_b):
    depth = w_in.shape[0]
    alpha = (2 * depth) ** 0.25
    B0, S0, D = x_prompt.shape
    B1, S1, _ = x_sample.shape
    assert D == D_MODEL
    real = B0 * S0 + B1 * S1
    n_meta_rows = -(-(B0 + B1) * N_META // TM) * TM
    R = real + n_meta_rows
    n_meta_blocks = n_meta_rows // N_META
    g0 = _Group(B0, S0, 0, 0, real, B0)
    g1 = _Group(B1, S1, B0 * S0, B0, real, n_meta_blocks - B0)
    s_max = max(S0, S1)

    x = jnp.concatenate([x_prompt.reshape(B0 * S0, D), x_sample.reshape(B1 * S1, D),
                         jnp.tile(meta_tokens, (n_meta_blocks, 1))], axis=0)
    h = _embed_ln(x, ln_in_g.reshape(1, D), ln_in_b.reshape(1, D))

    tm = 2 * TM if all(v % (2 * TM) == 0 for v in (S0, S1, n_meta_rows)) else TM
    cos_tab, sin_tab = _rope_tables(s_max, tm)
    n0, n1 = B0 * S0 // tm, real // tm
    p0, p1, pm = S0 // tm, S1 // tm, s_max // tm

    def pos_block(i):
        return jnp.where(i < n0, i % p0, jnp.where(i < n1, (i - n0) % p1, pm))

    bband, bmeta, bq_meta, bq_blk = (t * LOG2E for t in _t5_tables(t5_table, s_max))
    bmeta_wide = jnp.pad(bmeta, ((0, 0), (0, 0), (0, LANES - N_META)), constant_values=NEG_INF)
    ones_bd = jnp.asarray(np.kron(np.eye(N_HEADS), np.ones((HEAD_DIM, HEAD_DIM))), BF16)

    qs, ks, vs = [], [], []
    for n in range(N_MIXERS):
        off = n * QKV_WIDTH
        qs.append(w_in[:, :, off:off + MIX_WIDTH])
        ks.append(w_in[:, :, off + MIX_WIDTH:off + MIX_WIDTH + KV_WIDTH])
        vs.append(w_in[:, :, off + MIX_WIDTH + KV_WIDTH:off + QKV_WIDTH])
    wr_t = jnp.swapaxes(w_router, 1, 2)
    wr_hi = wr_t.astype(BF16)
    layers = dict(
        w_qkv=jnp.concatenate(qs + ks + vs, axis=2).astype(BF16),
        w_gate=w_in[:, :, N_MIXERS * QKV_WIDTH:].astype(BF16),
        q_gain=jnp.tile(q_gain, (1, N_HEADS)).reshape(depth, 1, MIX_WIDTH),
        k_gain=jnp.tile(k_gain, (1, N_KV_HEADS)).reshape(depth, 1, KV_WIDTH),
        sink=sink.astype(F32) * LOG2E,
        na_bias=jax.vmap(_na_bias_cases)(na_rpb) * LOG2E,
        na_mbias=na_meta_bias.astype(F32) * LOG2E,
        na_mbias_wide=jnp.pad(na_meta_bias.astype(F32) * LOG2E, ((0, 0), (0, 0), (0, LANES - N_META)),
                              constant_values=NEG_INF),
        w_branch=w_branch.astype(BF16),
        w_out=w_out.astype(BF16),
        ln1_g=ln1_g.reshape(depth, 1, D), ln1_b=ln1_b.reshape(depth, 1, D),
        wr_hi=wr_hi, wr_lo=(wr_t - wr_hi.astype(F32)).astype(BF16),
        rbias=router_bias.astype(F32).reshape(depth, N_EXPERTS, 1),
        wgu=w_expert_gate_up.astype(BF16), wd=w_expert_down.astype(BF16),
        wsgu=w_shared_gate_up.astype(BF16), wsd=w_shared_down.astype(BF16),
        ln2_g=ln2_g.reshape(depth, 1, D), ln2_b=ln2_b.reshape(depth, 1, D),
    )

    def layer(h, p):
        q_all, k_all, v_all, v_ones = _inproj(h, p["w_qkv"], cos_tab, sin_tab, p["q_gain"], p["k_gain"],
                                      ones_bd, pos_block, tm)
        o = None
        for grp in (g0, g1):
            o = _global_attn(grp, q_all, k_all, v_ones, o)
            o = _window_attn(grp, q_all, k_all, v_all, p["sink"], bband, bmeta_wide[:, :grp.S], o)
            o = _na_attn(grp, q_all, k_all, v_all, p["na_bias"], p["na_mbias_wide"], o)
        for grp in (g0, g1):
            o = _meta_attn(grp, q_all, k_all, v_all, p["sink"], bq_meta, bq_blk, p["na_mbias"], o)
        h1, gates = _merge(h, o, p["w_gate"], p["w_branch"], p["w_out"], p["ln1_g"], p["ln1_b"],
                           p["wr_hi"], p["wr_lo"], p["rbias"], alpha, tm)
        h2 = _moe(h1, gates, p["wgu"], p["wd"], p["wsgu"], p["wsd"], p["ln2_g"], p["ln2_b"],
                  alpha)
        return h2, None

    h, _ = lax.scan(layer, h, layers)
    y_prompt = h[:B0 * S0].reshape(B0, S0, D)
    y_sample = h[B0 * S0:real].reshape(B1, S1, D)
    return (y_prompt, y_sample)
```

```python
import functools
import math

import numpy as np
import jax
import jax.numpy as jnp
from jax import lax
from jax.experimental import pallas as pl
from jax.experimental.pallas import tpu as pltpu

F32 = jnp.float32
BF16 = jnp.bfloat16

D_MODEL = 1024
HEAD_DIM = 64
N_HEADS = 8
N_KV_HEADS = 2
GROUP = N_HEADS // N_KV_HEADS
MIX_WIDTH = N_HEADS * HEAD_DIM
KV_WIDTH = N_KV_HEADS * HEAD_DIM
N_MIXERS = 3
QKV_WIDTH = MIX_WIDTH + 2 * KV_WIDTH
N_META = 16
GRID_W = 64
BLOCK = 128
WINDOW = 128
NA_ROWS = 8
NA_COLS = 16
T5_BUCKETS = 32
T5_MAX_DIST = 128
ROPE_THETA = 10000.0
N_EXPERTS = 64
TOP_K = 8
N_EXPERT_GROUPS = 8
TOPK_GROUPS = 4
D_EXPERT = 256
ROUTED_SCALE = 2.5
NEG_INF = -1e30
LOG2E = math.log2(math.e)
LANES = 128

TM = 512
TQ_GLOBAL = 256
TK_GLOBAL = 512
NA_QROWS = 8
WIN_QBLOCKS = 4
VMEM_LIMIT = 56 * 1024 * 1024


def _cparams(sem):
    return pltpu.CompilerParams(dimension_semantics=sem, vmem_limit_bytes=VMEM_LIMIT)


def _dot(a, b):
    return jnp.dot(a, b, preferred_element_type=F32)


def _dot_nt(a, b):
    return lax.dot_general(a, b, (((1,), (1,)), ((), ())), preferred_element_type=F32)


def _split_bf16(x):
    hi = x.astype(BF16)
    lo = (x - hi.astype(F32)).astype(BF16)
    return hi, lo


def _layer_norm(x, g, b):
    mu = jnp.mean(x, axis=-1, keepdims=True)
    xc = x - mu
    var = jnp.mean(xc * xc, axis=-1, keepdims=True)
    return xc * lax.rsqrt(var + 1e-5) * g + b


def _embed_ln_kernel(x0_ref, x1_ref, xm_ref, g_ref, b_ref, o_ref, *, n0, n1):
    i = pl.program_id(0)
    for src, pred in ((x0_ref, i < n0), (x1_ref, (i >= n0) & (i < n0 + n1)), (xm_ref, i >= n0 + n1)):
        @pl.when(pred)
        def _(src=src):
            o_ref[...] = _layer_norm(src[...], g_ref[...], b_ref[...])


def _embed_ln(x0, x1, meta_tile, g, b, n_rows):
    n0, n1 = x0.shape[0] // TM, x1.shape[0] // TM
    return pl.pallas_call(
        functools.partial(_embed_ln_kernel, n0=n0, n1=n1),
        grid=(n_rows // TM,),
        in_specs=[pl.BlockSpec((TM, D_MODEL), lambda i: (jnp.minimum(i, n0 - 1), 0)),
                  pl.BlockSpec((TM, D_MODEL), lambda i: (jnp.clip(i - n0, 0, n1 - 1), 0)),
                  pl.BlockSpec((TM, D_MODEL), lambda i: (0, 0)),
                  pl.BlockSpec((1, D_MODEL), lambda i: (0, 0)),
                  pl.BlockSpec((1, D_MODEL), lambda i: (0, 0))],
        out_specs=pl.BlockSpec((TM, D_MODEL), lambda i: (i, 0)),
        out_shape=jax.ShapeDtypeStruct((n_rows, D_MODEL), F32),
        compiler_params=_cparams(("arbitrary",)),
        name="embed_ln",
    )(x0, x1, meta_tile, g, b)


def _rope_slot(x, cos, sin_signed, first_half):
    fwd = pltpu.roll(x, LANES - 16, 1)
    bwd = pltpu.roll(x, 16, 1)
    return x * cos + jnp.where(first_half, fwd, bwd) * sin_signed


def _head_rms(x, ones_bd, gain):
    hi, lo = _split_bf16(x * x)
    ss = _dot(hi, ones_bd) + _dot(lo, ones_bd)
    return x * lax.rsqrt(ss * (1.0 / HEAD_DIM) + 1e-6) * gain


def _inproj_kernel(h_ref, w_ref, cos_ref, sin_ref, qg_ref, kg_ref, ones_ref, q_ref, k_ref, v_ref, v1_ref):
    x = h_ref[...].astype(BF16)
    cos = cos_ref[...]
    sin = sin_ref[...]
    lane = lax.broadcasted_iota(jnp.int32, cos.shape, 1)
    first_half = (lane % 32) < 16
    scale = HEAD_DIM ** -0.5
    qw = N_MIXERS * MIX_WIDTH
    qa = _head_rms(_dot(x, w_ref[:, 0:MIX_WIDTH]), ones_ref[...], qg_ref[...])
    for s in range(MIX_WIDTH // LANES):
        sl = slice(s * LANES, (s + 1) * LANES)
        q_ref[:, sl] = (_rope_slot(qa[:, sl], cos, sin, first_half) * (scale * LOG2E)).astype(BF16)
    ka = _head_rms(_dot(x, w_ref[:, qw:qw + KV_WIDTH]), ones_ref[0:LANES, 0:LANES], kg_ref[...])
    k_ref[:, 0:KV_WIDTH] = _rope_slot(ka, cos, sin, first_half).astype(BF16)
    for n in range(1, N_MIXERS):
        q_ref[:, n * MIX_WIDTH:(n + 1) * MIX_WIDTH] = (
            _dot(x, w_ref[:, n * MIX_WIDTH:(n + 1) * MIX_WIDTH]) * (scale * LOG2E)).astype(BF16)
        k_ref[:, n * KV_WIDTH:(n + 1) * KV_WIDTH] = _dot(
            x, w_ref[:, qw + n * KV_WIDTH:qw + (n + 1) * KV_WIDTH]).astype(BF16)
    vw = qw + N_MIXERS * KV_WIDTH
    v = _dot(x, w_ref[:, vw:vw + N_MIXERS * KV_WIDTH])
    v_ref[...] = v.astype(BF16)
    va = v[:, 0:KV_WIDTH]
    lo = lane < HEAD_DIM
    v1_ref[:, 0:LANES] = jnp.where(lo, va, 1.0).astype(BF16)
    v1_ref[:, LANES:2 * LANES] = jnp.where(lo, pltpu.roll(va, HEAD_DIM, 1), 1.0).astype(BF16)


def _inproj(h, w_qkv, cos_tab, sin_tab, q_gain, k_gain, ones_bd, pos_block, tm):
    R = h.shape[0]
    const = lambda i: (0, 0)
    return pl.pallas_call(
        _inproj_kernel,
        grid=(R // tm,),
        in_specs=[pl.BlockSpec((tm, D_MODEL), lambda i: (i, 0)),
                  pl.BlockSpec(w_qkv.shape, const),
                  pl.BlockSpec((tm, LANES), lambda i: (pos_block(i), 0)),
                  pl.BlockSpec((tm, LANES), lambda i: (pos_block(i), 0)),
                  pl.BlockSpec((1, MIX_WIDTH), const),
                  pl.BlockSpec((1, KV_WIDTH), const),
                  pl.BlockSpec((MIX_WIDTH, MIX_WIDTH), const)],
        out_specs=[pl.BlockSpec((tm, N_MIXERS * MIX_WIDTH), lambda i: (i, 0)),
                   pl.BlockSpec((tm, N_MIXERS * KV_WIDTH), lambda i: (i, 0)),
                   pl.BlockSpec((tm, N_MIXERS * KV_WIDTH), lambda i: (i, 0)),
                   pl.BlockSpec((tm, N_KV_HEADS * LANES), lambda i: (i, 0))],
        out_shape=[jax.ShapeDtypeStruct((R, N_MIXERS * MIX_WIDTH), BF16),
                   jax.ShapeDtypeStruct((R, N_MIXERS * KV_WIDTH), BF16),
                   jax.ShapeDtypeStruct((R, N_MIXERS * KV_WIDTH), BF16),
                   jax.ShapeDtypeStruct((R, N_KV_HEADS * LANES), BF16)],
        compiler_params=_cparams(("parallel",)),
        name="inproj",
    )(h, w_qkv, cos_tab, sin_tab, q_gain, k_gain, ones_bd)


def _group_queries(q, j):
    lane = lax.broadcasted_iota(jnp.int32, (q.shape[0], LANES), 1)
    keep = (lane < HEAD_DIM) if j == 0 else (lane >= HEAD_DIM)
    parts = []
    for hh in range(GROUP):
        h = GROUP * j + hh
        slot = q[:, (h // 2) * LANES:(h // 2 + 1) * LANES]
        if h % 2 != j:
            slot = pltpu.roll(slot, HEAD_DIM, 1)
        parts.append(jnp.where(keep, slot, 0.0))
    return jnp.concatenate(parts, axis=0).astype(BF16)


def _ungroup_outputs(out, j, T):
    lane = lax.broadcasted_iota(jnp.int32, (T, LANES), 1)
    lo = lane < HEAD_DIM
    slots = []
    for s in range(2):
        even = out[(2 * s) * T:(2 * s + 1) * T]
        odd = out[(2 * s + 1) * T:(2 * s + 2) * T]
        if j == 0:
            slots.append(jnp.where(lo, even, pltpu.roll(odd, HEAD_DIM, 1)))
        else:
            slots.append(jnp.where(lo, pltpu.roll(even, HEAD_DIM, 1), odd))
    return jnp.concatenate(slots, axis=1)


def _fold_lanes(op, *parts):
    cols = [p[:, c * LANES:(c + 1) * LANES] for p in parts for c in range(p.shape[1] // LANES)]
    return functools.reduce(op, cols)


def _row_max(*parts):
    return jnp.max(_fold_lanes(jnp.maximum, *parts), axis=-1, keepdims=True)


def _row_sum(*parts):
    return jnp.sum(_fold_lanes(jnp.add, *parts), axis=-1, keepdims=True)


def _pad_meta_rows(x):
    return jnp.concatenate([x, jnp.zeros((LANES - N_META, x.shape[1]), x.dtype)], axis=0)


def _head_rows(vals, T):
    return jnp.concatenate([jnp.broadcast_to(v, (T, v.shape[-1])) for v in vals], axis=0)


def _global_kernel(q_ref, k_ref, v_ref, km_ref, vm_ref, *rest, S):
    o_ref = rest[-1]
    T = q_ref.shape[0]
    q = q_ref[...].astype(F32)
    km = km_ref[...]
    qs, state = [], []
    for j in range(N_KV_HEADS):
        qj = _group_queries(q, j)
        s_m = _dot_nt(qj, km)
        m0 = jnp.max(s_m, axis=-1, keepdims=True)
        p_m = jnp.exp2((s_m - m0).astype(BF16))
        qs.append(qj)
        state.append((m0, _dot(p_m, vm_ref[:, j * LANES:(j + 1) * LANES])))
    for c in range(S // TK_GLOBAL):
        kc = k_ref[c * TK_GLOBAL:(c + 1) * TK_GLOBAL, :]
        for j in range(N_KV_HEADS):
            m, acc = state[j]
            s = _dot_nt(qs[j], kc)
            m_new = jnp.maximum(m, jnp.max(s, axis=-1, keepdims=True))
            p = jnp.exp2((s - m_new).astype(BF16))
            vc = v_ref[c * TK_GLOBAL:(c + 1) * TK_GLOBAL, j * LANES:(j + 1) * LANES]
            state[j] = (m_new, jnp.exp2(m - m_new) * acc + _dot(p, vc))
    for j in range(N_KV_HEADS):
        acc = state[j][1]
        out = acc / pltpu.roll(acc, HEAD_DIM, 1)
        o_ref[:, j * 2 * LANES:(j + 1) * 2 * LANES] = _ungroup_outputs(out, 0, T).astype(BF16)


def _window_kernel(sink_ref, q_ref, kp_ref, kc_ref, kn_ref, vp_ref, vc_ref, vn_ref, km_ref, vm_ref,
                   bband_ref, bmeta_ref, *rest, nb):
    o_ref = rest[-1]
    i = pl.program_id(1)
    T = BLOCK
    k_span = jnp.concatenate([kp_ref[...], kc_ref[...], kn_ref[...]], axis=0)
    v_span = jnp.concatenate([vp_ref[...], vc_ref[...], vn_ref[...]], axis=0)
    col = lax.broadcasted_iota(jnp.int32, (1, 3 * BLOCK), 1)
    km = _pad_meta_rows(km_ref[...])
    vm = _pad_meta_rows(vm_ref[...])
    for u in range(WIN_QBLOCKS):
        q = q_ref[u * BLOCK:(u + 1) * BLOCK, :].astype(F32)
        kband = k_span[u * BLOCK:(u + 3) * BLOCK]
        vband = v_span[u * BLOCK:(u + 3) * BLOCK]
        in_range = None
        if u == 0:
            in_range = (col >= BLOCK) | (i > 0)
        if u == WIN_QBLOCKS - 1:
            after = (col < 2 * BLOCK) | (i < nb - 1)
            in_range = after if in_range is None else in_range & after
        for j in range(N_KV_HEADS):
            qj = _group_queries(q, j)
            hs = slice(GROUP * j, GROUP * (j + 1))
            s_b = _dot_nt(qj, kband) + bband_ref[hs].reshape(GROUP * T, 3 * BLOCK)
            if in_range is not None:
                s_b = jnp.where(in_range, s_b, NEG_INF)
            s_m = _dot_nt(qj, km) + bmeta_ref[hs, u * BLOCK:(u + 1) * BLOCK].reshape(GROUP * T, LANES)
            sink = jnp.concatenate([jnp.full((T, 1), sink_ref[GROUP * j + hh], F32) for hh in range(GROUP)], axis=0)
            m = jnp.maximum(_row_max(s_b, s_m), sink)
            e_b = jnp.exp2(s_b - m)
            e_m = jnp.exp2(s_m - m)
            denom = _row_sum(e_b, e_m) + jnp.exp2(sink - m)
            acc = _dot(e_b.astype(BF16), vband) + _dot(e_m.astype(BF16), vm)
            o_ref[u * BLOCK:(u + 1) * BLOCK, j * 2 * LANES:(j + 1) * 2 * LANES] = (
                _ungroup_outputs(acc / denom, j, T).astype(BF16))


def _na_kernel(q_ref, k_ref, v_ref, km_ref, vm_ref, bias_ref, mbias_ref, *rest, rows):
    o_ref = rest[-1]
    blk = pl.program_id(1)
    W = GRID_W
    nkeys = NA_ROWS * W
    km = _pad_meta_rows(km_ref[...])
    vm = _pad_meta_rows(vm_ref[...])
    mb = _head_rows([mbias_ref[h:h + 1, :] for h in range(N_HEADS)], W)
    half = GROUP * W
    for rr in range(NA_QROWS):
        r = blk * NA_QROWS + rr
        rs = jnp.clip(r - NA_ROWS // 2, 0, rows - NA_ROWS)
        delta = r - rs
        koff = pl.multiple_of(rs * W, W)
        q = q_ref[rr * W:(rr + 1) * W, :].astype(F32)
        q8 = jnp.concatenate([_group_queries(q, j) for j in range(N_KV_HEADS)], axis=0)
        kw = k_ref[pl.ds(koff, nkeys), :]
        vw = v_ref[pl.ds(koff, nkeys), :]
        s_w = _dot_nt(q8, kw) + bias_ref[delta].reshape(N_HEADS * W, nkeys)
        s_m = _dot_nt(q8, km) + mb
        m = _row_max(s_w, s_m)
        e_w = jnp.exp2(s_w - m)
        e_m = jnp.exp2(s_m - m)
        out = (_dot(e_w.astype(BF16), vw) + _dot(e_m.astype(BF16), vm)) / _row_sum(e_w, e_m)
        for j in range(N_KV_HEADS):
            o_ref[rr * W:(rr + 1) * W, j * 2 * LANES:(j + 1) * 2 * LANES] = (
                _ungroup_outputs(out[j * half:(j + 1) * half], j, W).astype(BF16))


def _meta_kernel(sink_ref, q_ref, ka_ref, va_ref, kb_ref, vb_ref, km_ref, vm_ref,
                 bq_meta_ref, bq_blk_ref, mbias_ref, o_in_ref, o_ref):
    del o_in_ref
    T = N_META
    q = q_ref[...].astype(F32)
    km_all = km_ref[...]
    vm_all = vm_ref[...]

    def finish(n, j, acc, denom):
        lo = n * MIX_WIDTH + j * 2 * LANES
        o_ref[:, lo:lo + 2 * LANES] = _ungroup_outputs(acc / denom, j, T).astype(BF16)

    for j in range(N_KV_HEADS):
        hs = slice(GROUP * j, GROUP * (j + 1))
        qj = _group_queries(q[:, 0:MIX_WIDTH], j)
        km, vm = km_all[:, 0:KV_WIDTH], vm_all[:, 0:KV_WIDTH]
        s_r = _dot_nt(qj, ka_ref[...])
        s_m = _dot_nt(qj, km)
        m = jnp.maximum(jnp.max(s_r, axis=-1, keepdims=True), jnp.max(s_m, axis=-1, keepdims=True))
        e_r = jnp.exp2(s_r - m)
        e_m = jnp.exp2(s_m - m)
        denom = jnp.sum(e_r, axis=-1, keepdims=True) + jnp.sum(e_m, axis=-1, keepdims=True)
        finish(0, j, _dot(e_r.astype(BF16), va_ref[...]) + _dot(e_m.astype(BF16), vm), denom)
        qj = _group_queries(q[:, MIX_WIDTH:2 * MIX_WIDTH], j)
        km, vm = km_all[:, KV_WIDTH:2 * KV_WIDTH], vm_all[:, KV_WIDTH:2 * KV_WIDTH]
        s_r = _dot_nt(qj, kb_ref[...]) + bq_blk_ref[hs].reshape(GROUP * T, BLOCK)
        s_m = _dot_nt(qj, km) + bq_meta_ref[hs].reshape(GROUP * T, N_META)
        sink = jnp.concatenate([jnp.full((T, 1), sink_ref[GROUP * j + hh], F32) for hh in range(GROUP)], axis=0)
        m = jnp.maximum(jnp.maximum(jnp.max(s_r, axis=-1, keepdims=True),
                                    jnp.max(s_m, axis=-1, keepdims=True)), sink)
        e_r = jnp.exp2(s_r - m)
        e_m = jnp.exp2(s_m - m)
        denom = (jnp.sum(e_r, axis=-1, keepdims=True) + jnp.sum(e_m, axis=-1, keepdims=True)
                 + jnp.exp2(sink - m))
        finish(1, j, _dot(e_r.astype(BF16), vb_ref[...]) + _dot(e_m.astype(BF16), vm), denom)
        qj = _group_queries(q[:, 2 * MIX_WIDTH:3 * MIX_WIDTH], j)
        km, vm = km_all[:, 2 * KV_WIDTH:3 * KV_WIDTH], vm_all[:, 2 * KV_WIDTH:3 * KV_WIDTH]
        mb = _head_rows([mbias_ref[GROUP * j + hh:GROUP * j + hh + 1, :] for hh in range(GROUP)], T)
        s_m = _dot_nt(qj, km) + mb
        m = jnp.max(s_m, axis=-1, keepdims=True)
        e_m = jnp.exp2(s_m - m)
        finish(2, j, _dot(e_m.astype(BF16), vm), jnp.sum(e_m, axis=-1, keepdims=True))


class _Group:
    def __init__(self, B, S, real_base, meta_batch_base, meta_base, n_meta_blocks):
        self.B, self.S = B, S
        self.real_base = real_base
        self.meta_blk0 = meta_base // N_META + meta_batch_base
        self.n_meta_blocks = n_meta_blocks
        assert real_base % S == 0 and S % TM == 0 and meta_base % N_META == 0


def _alias_args(o_prev, n_inputs):
    if o_prev is None:
        return [], [], {}
    return [o_prev], [pl.BlockSpec(memory_space=pl.ANY)], {n_inputs: 0}


def _global_attn(grp, q_all, k_all, v_ones, o_prev):
    B, S = grp.B, grp.S
    nq = S // TQ_GLOBAL
    qb0 = grp.real_base // TQ_GLOBAL
    sb0 = grp.real_base // S
    mb0 = grp.meta_blk0
    in_specs = [pl.BlockSpec((TQ_GLOBAL, MIX_WIDTH), lambda b, i: (qb0 + b * nq + i, 0)),
                pl.BlockSpec((S, KV_WIDTH), lambda b, i: (sb0 + b, 0)),
                pl.BlockSpec((S, N_KV_HEADS * LANES), lambda b, i: (sb0 + b, 0)),
                pl.BlockSpec((N_META, KV_WIDTH), lambda b, i: (mb0 + b, 0)),
                pl.BlockSpec((N_META, N_KV_HEADS * LANES), lambda b, i: (mb0 + b, 0))]
    extra, extra_specs, aliases = _alias_args(o_prev, len(in_specs))
    return pl.pallas_call(
        functools.partial(_global_kernel, S=S),
        grid=(B, nq),
        in_specs=in_specs + extra_specs,
        out_specs=pl.BlockSpec((TQ_GLOBAL, MIX_WIDTH), lambda b, i: (qb0 + b * nq + i, 0)),
        out_shape=jax.ShapeDtypeStruct((q_all.shape[0], N_MIXERS * MIX_WIDTH), BF16),
        input_output_aliases=aliases,
        compiler_params=_cparams(("parallel", "arbitrary")),
        name="mixer_global",
    )(q_all, k_all, v_ones, k_all, v_ones, *extra)


def _window_attn(grp, q_all, k_all, v_all, sink, bband, bmeta, o_prev):
    B, S = grp.B, grp.S
    span = WIN_QBLOCKS * BLOCK
    nb, ns = S // BLOCK, S // span
    qb0 = grp.real_base // BLOCK
    sp0 = grp.real_base // span
    mb0 = grp.meta_blk0
    cur = lambda b, i, sink: (sp0 + b * ns + i, 1)
    prv = lambda b, i, sink: (qb0 + b * nb + jnp.maximum(i * WIN_QBLOCKS - 1, 0), 1)
    nxt = lambda b, i, sink: (qb0 + b * nb + jnp.minimum((i + 1) * WIN_QBLOCKS, nb - 1), 1)
    met = lambda b, i, sink: (mb0 + b, 1)
    edge = lambda im: pl.BlockSpec((BLOCK, KV_WIDTH), im)
    mid = pl.BlockSpec((span, KV_WIDTH), cur)
    in_specs = [pl.BlockSpec((span, MIX_WIDTH), cur),
                edge(prv), mid, edge(nxt), edge(prv), mid, edge(nxt),
                pl.BlockSpec((N_META, KV_WIDTH), met), pl.BlockSpec((N_META, KV_WIDTH), met),
                pl.BlockSpec((N_HEADS, BLOCK, 3 * BLOCK), lambda b, i, sink: (0, 0, 0)),
                pl.BlockSpec((N_HEADS, span, LANES), lambda b, i, sink: (0, i, 0))]
    extra, extra_specs, aliases = _alias_args(o_prev, len(in_specs) + 1)
    return pl.pallas_call(
        functools.partial(_window_kernel, nb=ns),
        grid_spec=pltpu.PrefetchScalarGridSpec(
            num_scalar_prefetch=1,
            grid=(B, ns),
            in_specs=in_specs + extra_specs,
            out_specs=pl.BlockSpec((span, MIX_WIDTH), cur)),
        out_shape=jax.ShapeDtypeStruct((q_all.shape[0], N_MIXERS * MIX_WIDTH), BF16),
        input_output_aliases=aliases,
        compiler_params=_cparams(("parallel", "arbitrary")),
        name="mixer_window",
    )(sink, q_all, k_all, k_all, k_all, v_all, v_all, v_all, k_all, v_all, bband, bmeta, *extra)


def _na_attn(grp, q_all, k_all, v_all, na_bias, na_mbias, o_prev):
    B, S = grp.B, grp.S
    rows = S // GRID_W
    tq = NA_QROWS * GRID_W
    nq = S // tq
    qb0 = grp.real_base // tq
    sb0 = grp.real_base // S
    mb0 = grp.meta_blk0
    in_specs = [pl.BlockSpec((tq, MIX_WIDTH), lambda b, i: (qb0 + b * nq + i, 2)),
                pl.BlockSpec((S, KV_WIDTH), lambda b, i: (sb0 + b, 2)),
                pl.BlockSpec((S, KV_WIDTH), lambda b, i: (sb0 + b, 2)),
                pl.BlockSpec((N_META, KV_WIDTH), lambda b, i: (mb0 + b, 2)),
                pl.BlockSpec((N_META, KV_WIDTH), lambda b, i: (mb0 + b, 2)),
                pl.BlockSpec(na_bias.shape, lambda b, i: (0, 0, 0, 0)),
                pl.BlockSpec(na_mbias.shape, lambda b, i: (0, 0))]
    extra, extra_specs, aliases = _alias_args(o_prev, len(in_specs))
    return pl.pallas_call(
        functools.partial(_na_kernel, rows=rows),
        grid=(B, nq),
        in_specs=in_specs + extra_specs,
        out_specs=pl.BlockSpec((tq, MIX_WIDTH), lambda b, i: (qb0 + b * nq + i, 2)),
        out_shape=jax.ShapeDtypeStruct((q_all.shape[0], N_MIXERS * MIX_WIDTH), BF16),
        input_output_aliases=aliases,
        compiler_params=_cparams(("parallel", "arbitrary")),
        name="mixer_neighbourhood",
    )(q_all, k_all, v_all, k_all, v_all, na_bias, na_mbias, *extra)


def _meta_attn(grp, q_all, k_all, v_all, sink, bq_meta, bq_blk, na_mbias, o_prev):
    B, S = grp.B, grp.S
    sb0 = grp.real_base // S
    bb0 = grp.real_base // BLOCK
    nb = S // BLOCK
    mb0 = grp.meta_blk0
    clamp = lambda b: jnp.minimum(b, B - 1)
    mrow = lambda b, sink: (mb0 + clamp(b), 0)
    in_specs = [pl.BlockSpec((N_META, N_MIXERS * MIX_WIDTH), mrow),
                pl.BlockSpec((S, KV_WIDTH), lambda b, sink: (sb0 + clamp(b), 0)),
                pl.BlockSpec((S, KV_WIDTH), lambda b, sink: (sb0 + clamp(b), 0)),
                pl.BlockSpec((BLOCK, KV_WIDTH), lambda b, sink: (bb0 + clamp(b) * nb, 1)),
                pl.BlockSpec((BLOCK, KV_WIDTH), lambda b, sink: (bb0 + clamp(b) * nb, 1)),
                pl.BlockSpec((N_META, N_MIXERS * KV_WIDTH), mrow),
                pl.BlockSpec((N_META, N_MIXERS * KV_WIDTH), mrow),
                pl.BlockSpec(bq_meta.shape, lambda b, sink: (0, 0, 0)),
                pl.BlockSpec(bq_blk.shape, lambda b, sink: (0, 0, 0)),
                pl.BlockSpec(na_mbias.shape, lambda b, sink: (0, 0)),
                pl.BlockSpec(memory_space=pl.ANY)]
    return pl.pallas_call(
        _meta_kernel,
        grid_spec=pltpu.PrefetchScalarGridSpec(
            num_scalar_prefetch=1,
            grid=(grp.n_meta_blocks,),
            in_specs=in_specs,
            out_specs=pl.BlockSpec((N_META, N_MIXERS * MIX_WIDTH), lambda b, sink: (mb0 + b, 0))),
        out_shape=jax.ShapeDtypeStruct((q_all.shape[0], N_MIXERS * MIX_WIDTH), BF16),
        input_output_aliases={len(in_specs): 0},
        compiler_params=_cparams(("arbitrary",)),
        name="mixer_meta_queries",
    )(sink, q_all, k_all, v_all, k_all, v_all, k_all, v_all, bq_meta, bq_blk, na_mbias, o_prev)


def _route(h1, wr_hi, wr_lo, rbias):
    T = h1.shape[0]
    x_hi, x_lo = _split_bf16(h1)
    logits = _dot_nt(wr_hi, x_hi) + _dot_nt(wr_hi, x_lo) + _dot_nt(wr_lo, x_hi)
    scores = 1.0 / (1.0 + jnp.exp(-logits))
    sel = scores + rbias
    per_group = N_EXPERTS // N_EXPERT_GROUPS
    sel3 = sel.reshape(N_EXPERT_GROUPS, per_group, T)
    idx3 = lax.broadcasted_iota(jnp.int32, sel3.shape, 1).astype(F32)
    m1 = jnp.max(sel3, axis=1, keepdims=True)
    first = jnp.min(jnp.where(sel3 == m1, idx3, float(per_group)), axis=1, keepdims=True)
    m2 = jnp.max(jnp.where(idx3 == first, -jnp.inf, sel3), axis=1, keepdims=True)
    gscore = (m1 + m2).reshape(N_EXPERT_GROUPS, T)

    def rank_of(vals):
        idx = lax.broadcasted_iota(jnp.int32, vals.shape, 0)
        rank = jnp.zeros(vals.shape, F32)
        for r in range(vals.shape[0]):
            row = vals[r:r + 1, :]
            ge = jnp.where(row >= vals, 1.0, 0.0)
            gt = jnp.where(row > vals, 1.0, 0.0)
            rank = rank + jnp.where(idx > r, ge, gt)
        return rank

    gkeep = jnp.where(rank_of(gscore) < TOPK_GROUPS, 1.0, 0.0)
    ekeep = jnp.broadcast_to(gkeep.reshape(N_EXPERT_GROUPS, 1, T), sel3.shape).reshape(N_EXPERTS, T)
    masked = jnp.where(ekeep > 0.5, sel, NEG_INF)
    eidx = lax.broadcasted_iota(jnp.int32, masked.shape, 0).astype(F32)
    chosen = jnp.zeros(masked.shape, F32)
    for _ in range(TOP_K):
        best = jnp.max(masked, axis=0, keepdims=True)
        first = jnp.min(jnp.where(masked == best, eidx, float(N_EXPERTS)), axis=0, keepdims=True)
        hit = eidx == first
        chosen = jnp.where(hit, 1.0, chosen)
        masked = jnp.where(hit, -jnp.inf, masked)
    w = jnp.where(chosen > 0.5, scores, 0.0)
    return w / jnp.sum(w, axis=0, keepdims=True) * ROUTED_SCALE


def _merge_kernel(h_ref, o_ref, wg_ref, wb_ref, wo_ref, g_ref, b_ref, wrh_ref, wrl_ref, rb_ref,
                  h1_ref, gates_ref, *, alpha):
    h = h_ref[...]
    x = h.astype(BF16)
    merged = None
    for n in range(N_MIXERS):
        logit = _dot(x, wg_ref[:, n * D_MODEL:(n + 1) * D_MODEL])
        branch = _dot(o_ref[:, n * MIX_WIDTH:(n + 1) * MIX_WIDTH], wb_ref[n])
        term = branch / (1.0 + jnp.exp(-logit))
        merged = term if merged is None else merged + term
    mix = _dot(merged.astype(BF16), wo_ref[...])
    h1 = _layer_norm(alpha * h + mix, g_ref[...], b_ref[...])
    h1_ref[...] = h1
    gates_t = _route(h1, wrh_ref[...], wrl_ref[...], rb_ref[...])
    pad = jnp.zeros((LANES - N_EXPERTS, gates_t.shape[1]), F32)
    gates_ref[...] = jnp.concatenate([gates_t, pad], axis=0).T


def _merge(h, o_all, w_gate, w_branch, w_out, ln_g, ln_b, wr_hi, wr_lo, rbias, alpha, tm):
    R = h.shape[0]
    c2 = lambda i: (0, 0)
    once = pl.Buffered(1)
    return pl.pallas_call(
        functools.partial(_merge_kernel, alpha=alpha),
        grid=(R // tm,),
        in_specs=[pl.BlockSpec((tm, D_MODEL), lambda i: (i, 0)),
                  pl.BlockSpec((tm, N_MIXERS * MIX_WIDTH), lambda i: (i, 0)),
                  pl.BlockSpec(w_gate.shape, c2, pipeline_mode=once),
                  pl.BlockSpec(w_branch.shape, lambda i: (0, 0, 0), pipeline_mode=once),
                  pl.BlockSpec(w_out.shape, c2, pipeline_mode=once),
                  pl.BlockSpec((1, D_MODEL), c2),
                  pl.BlockSpec((1, D_MODEL), c2),
                  pl.BlockSpec(wr_hi.shape, c2),
                  pl.BlockSpec(wr_lo.shape, c2),
                  pl.BlockSpec(rbias.shape, c2)],
        out_specs=[pl.BlockSpec((tm, D_MODEL), lambda i: (i, 0)),
                   pl.BlockSpec((tm, LANES), lambda i: (i, 0))],
        out_shape=[jax.ShapeDtypeStruct((R, D_MODEL), F32),
                   jax.ShapeDtypeStruct((R, LANES), F32)],
        compiler_params=_cparams(("parallel",)),
        name="merge_ln_route",
    )(h, o_all, w_gate, w_branch, w_out, ln_g, ln_b, wr_hi, wr_lo, rbias)


TD = 256
CH = 16
SLOTS = 3072
NCH = SLOTS // CH
MT = 512
ME = 1024
CPM = ME // CH
XBUFS = 3


def _swiglu_act(gu):
    g = gu[:, :D_EXPERT]
    return g / (1.0 + jnp.exp(-g)) * gu[:, D_EXPERT:]


def _slot_of_token(gates, lo_row):
    routed = gates > 0.0
    r = lax.broadcasted_iota(jnp.int32, (TD, TD), 0)
    c = lax.broadcasted_iota(jnp.int32, (TD, TD), 1)
    earlier = jnp.where(c < r, 1.0, 0.0).astype(BF16)
    rank = _dot(earlier, jnp.where(routed, 1.0, 0.0).astype(BF16))
    return jnp.where(routed, lo_row + rank + 1.0, 0.0)


def _split64(x):
    hi = 64.0 * jnp.floor(x * (1.0 / 64.0))
    return jnp.concatenate([hi.astype(BF16), (x - hi).astype(BF16)], axis=0)


def _dispatch_kernel(h_ref, g_ref, lohi_ref, x_ref, w_ref):
    lohi = lohi_ref[0]
    lo_row, hi_row = lohi[0:1], lohi[1:2]
    gates = g_ref[...]
    slot_t = _split64(_slot_of_token(gates, lo_row).T)
    gates_t = gates.T.astype(BF16)
    x = h_ref[...].astype(BF16)
    for blk in range(SLOTS // MT):
        s = (lax.broadcasted_iota(jnp.int32, (MT, LANES), 0) + blk * MT).astype(F32)
        owner = jnp.where(s >= lo_row, jnp.where(s < hi_row, 1.0, 0.0), 0.0).astype(BF16)
        want = _dot(jnp.concatenate([owner, owner], axis=1), slot_t)
        s1 = (lax.broadcasted_iota(jnp.int32, (MT, TD), 0) + (blk * MT + 1)).astype(F32)
        hit = want == s1
        x_ref[blk * MT:(blk + 1) * MT, :] = _dot(jnp.where(hit, 1.0, 0.0).astype(BF16), x).astype(BF16)
        weight = jnp.where(hit, _dot(owner, gates_t), 0.0)
        w_ref[:, blk * MT:(blk + 1) * MT] = weight.T.astype(BF16)


def _dispatch(h1, gates, lohi):
    n = h1.shape[0] // TD
    return pl.pallas_call(
        _dispatch_kernel,
        grid=(n,),
        in_specs=[pl.BlockSpec((TD, D_MODEL), lambda i: (i, 0)),
                  pl.BlockSpec((TD, LANES), lambda i: (i, 0)),
                  pl.BlockSpec((1, 8, LANES), lambda i: (i, 0, 0))],
        out_specs=[pl.BlockSpec((SLOTS, D_MODEL), lambda i: (i, 0)),
                   pl.BlockSpec((TD, SLOTS), lambda i: (i, 0))],
        out_shape=[jax.ShapeDtypeStruct((n * SLOTS, D_MODEL), BF16),
                   jax.ShapeDtypeStruct((n * TD, SLOTS), BF16)],
        compiler_params=_cparams(("parallel",)),
        name="moe_dispatch",
    )(h1, gates, lohi)


def _chunk_gather(table_ref, first, n_chunks, src_hbm, buf, sem):
    copies = []
    for c in range(n_chunks):
        row = pl.multiple_of(table_ref[first + c] * CH, CH)
        copies.append(pltpu.make_async_copy(src_hbm.at[pl.ds(row, CH)], buf.at[pl.ds(c * CH, CH)], sem))
    return copies


def _chunk_wait(n_chunks, src_hbm, buf, sem):
    for c in range(n_chunks):
        pltpu.make_async_copy(src_hbm.at[pl.ds(0, CH)], buf.at[pl.ds(c * CH, CH)], sem).wait()


def _expert_kernel(te_ref, src_ref, nu_ref, x_hbm, wgu_ref, wd_ref, y_ref, xbuf, sem):
    del te_ref
    m = pl.program_id(0)
    n_used = nu_ref[0]

    def start(step):
        slot = step % XBUFS
        for cp in _chunk_gather(src_ref, step * CPM, CPM, x_hbm, xbuf.at[slot], sem.at[slot]):
            cp.start()

    for ahead in range(XBUFS - 1):
        @pl.when((m == 0) & (ahead < n_used))
        def _(ahead=ahead):
            start(ahead)

    @pl.when(m + (XBUFS - 1) < n_used)
    def _():
        start(m + (XBUFS - 1))

    @pl.when(m < n_used)
    def _():
        slot = m % XBUFS
        _chunk_wait(CPM, x_hbm, xbuf.at[slot], sem.at[slot])
        act = _swiglu_act(_dot(xbuf[slot], wgu_ref[0]))
        y_ref[...] = _dot(act.astype(BF16), wd_ref[0]).astype(BF16)


def _experts(x_disp, wgu, wd, tile_expert, src_chunk, n_used):
    n_steps = tile_expert.shape[0]
    return pl.pallas_call(
        _expert_kernel,
        grid_spec=pltpu.PrefetchScalarGridSpec(
            num_scalar_prefetch=3,
            grid=(n_steps,),
            in_specs=[pl.BlockSpec(memory_space=pl.ANY),
                      pl.BlockSpec((1, D_MODEL, 2 * D_EXPERT), lambda m, te, src, nu: (te[m], 0, 0)),
                      pl.BlockSpec((1, D_EXPERT, D_MODEL), lambda m, te, src, nu: (te[m], 0, 0))],
            out_specs=pl.BlockSpec((ME, D_MODEL), lambda m, te, src, nu: (jnp.minimum(m, nu[0] - 1), 0)),
            scratch_shapes=[pltpu.VMEM((XBUFS, ME, D_MODEL), BF16), pltpu.SemaphoreType.DMA((XBUFS,))]),
        out_shape=jax.ShapeDtypeStruct((n_steps * ME, D_MODEL), BF16),
        compiler_params=_cparams(("arbitrary",)),
        name="moe_experts",
    )(tile_expert, src_chunk, n_used, x_disp, wgu, wd)


def _combine_kernel(dst_ref, y_hbm, w_ref, h_ref, wsgu_ref, wsd_ref, lg_ref, lb_ref, o_ref, ybuf, sem, *, alpha):
    i = pl.program_id(0)

    def start(tile):
        slot = tile % 2
        for cp in _chunk_gather(dst_ref, tile * NCH, NCH, y_hbm, ybuf.at[slot], sem.at[slot]):
            cp.start()

    @pl.when(i == 0)
    def _():
        start(i)

    @pl.when(i + 1 < pl.num_programs(0))
    def _():
        start(i + 1)

    h = h_ref[...]
    shared = _dot(_swiglu_act(_dot(h.astype(BF16), wsgu_ref[...])).astype(BF16), wsd_ref[...])
    slot = i % 2
    _chunk_wait(NCH, y_hbm, ybuf.at[slot], sem.at[slot])
    routed = _dot(w_ref[...], ybuf[slot])
    o_ref[...] = _layer_norm(alpha * h + shared + routed, lg_ref[...], lb_ref[...])


def _combine(y_sorted, w_t, h1, dst_chunk, wsgu, wsd, ln_g, ln_b, alpha):
    n = h1.shape[0] // TD
    c2 = lambda i, dst: (0, 0)
    return pl.pallas_call(
        functools.partial(_combine_kernel, alpha=alpha),
        grid_spec=pltpu.PrefetchScalarGridSpec(
            num_scalar_prefetch=1,
            grid=(n,),
            in_specs=[pl.BlockSpec(memory_space=pl.ANY),
                      pl.BlockSpec((TD, SLOTS), lambda i, dst: (i, 0)),
                      pl.BlockSpec((TD, D_MODEL), lambda i, dst: (i, 0)),
                      pl.BlockSpec(wsgu.shape, c2),
                      pl.BlockSpec(wsd.shape, c2),
                      pl.BlockSpec((1, D_MODEL), c2),
                      pl.BlockSpec((1, D_MODEL), c2)],
            out_specs=pl.BlockSpec((TD, D_MODEL), lambda i, dst: (i, 0)),
            scratch_shapes=[pltpu.VMEM((2, SLOTS, D_MODEL), BF16), pltpu.SemaphoreType.DMA((2,))]),
        out_shape=jax.ShapeDtypeStruct((h1.shape[0], D_MODEL), F32),
        compiler_params=_cparams(("arbitrary",)),
        name="moe_combine_ln",
    )(dst_chunk, y_sorted, w_t, h1, wsgu, wsd, ln_g, ln_b)


def _routing_tables(gates):
    n = gates.shape[0] // TD
    cnt = jnp.sum((gates[:, :N_EXPERTS] > 0.0).reshape(n, TD, N_EXPERTS), axis=1, dtype=jnp.int32)
    nch = (cnt + (CH - 1)) // CH
    hi16 = jnp.cumsum(nch, axis=1)
    lo16 = hi16 - nch
    nct = hi16[:, -1:]
    pad = jnp.broadcast_to(nct, (n, LANES - N_EXPERTS))
    lohi = jnp.zeros((n, 8, LANES), F32)
    lohi = lohi.at[:, 0, :].set((jnp.concatenate([lo16, pad], axis=1) * CH).astype(F32))
    lohi = lohi.at[:, 1, :].set((jnp.concatenate([hi16, pad], axis=1) * CH).astype(F32))
    tot = jnp.sum(nch, axis=0)
    seg_len = (tot + (CPM - 1)) // CPM * CPM
    seg_end = jnp.cumsum(seg_len)
    seg_start = seg_end - seg_len
    gpos = seg_start[None, :] + jnp.cumsum(nch, axis=0) - nch
    n_steps = (n * NCH + N_EXPERTS * CPM) // CPM
    n_used = (seg_end[-1] // CPM).astype(jnp.int32).reshape(1)
    step = jnp.arange(n_steps, dtype=jnp.int32)
    tile_expert = jnp.sum(seg_end[None, :] // CPM <= jnp.minimum(step, n_used - 1)[:, None], axis=1, dtype=jnp.int32)
    tile_expert = jnp.minimum(tile_expert, N_EXPERTS - 1)
    exact = functools.partial(jnp.dot, precision=lax.Precision.HIGHEST)
    experts = jnp.arange(N_EXPERTS, dtype=jnp.int32)
    g = jnp.arange(n_steps * CPM, dtype=jnp.int32)
    e_of_g = jnp.minimum(jnp.sum(seg_end[None, :] <= g[:, None], axis=1, dtype=jnp.int32), N_EXPERTS - 1)
    pick_e = (e_of_g[:, None] == experts[None, :]).astype(F32)
    first = exact(pick_e, gpos.T.astype(F32))
    count = exact(pick_e, nch.T.astype(F32))
    base = exact(pick_e, (jnp.arange(n, dtype=jnp.int32)[:, None] * NCH + lo16).T.astype(F32))
    gf = g.astype(F32)[:, None]
    inside = (first <= gf) & (gf < first + count)
    src_chunk = jnp.sum(jnp.where(inside, base + gf - first, 0.0), axis=1).astype(jnp.int32)
    k = jnp.arange(NCH, dtype=jnp.int32)
    e_of_k = jnp.minimum(jnp.sum(hi16[:, None, :] <= k[None, :, None], axis=2, dtype=jnp.int32), N_EXPERTS - 1)
    pick_k = e_of_k[:, :, None] == experts[None, None, :]
    pos = jnp.sum(jnp.where(pick_k, (gpos - lo16)[:, None, :], 0), axis=2) + k[None, :]
    dst_chunk = jnp.where(k[None, :] < nct, pos, 0).astype(jnp.int32).reshape(-1)
    return lohi, tile_expert, src_chunk, n_used, dst_chunk


def _moe(h1, gates, wgu, wd, wsgu, wsd, ln_g, ln_b, alpha):
    lohi, tile_expert, src_chunk, n_used, dst_chunk = _routing_tables(gates)
    x_disp, w_t = _dispatch(h1, gates, lohi)
    y_sorted = _experts(x_disp, wgu, wd, tile_expert, src_chunk, n_used)
    return _combine(y_sorted, w_t, h1, dst_chunk, wsgu, wsd, ln_g, ln_b, alpha)


def _t5_bucket(rel):
    half = T5_BUCKETS // 2
    max_exact = half // 2
    n = np.abs(rel)
    ratio = np.log(np.maximum(n, 1).astype(np.float32) / np.float32(max_exact))
    ratio = ratio / np.float32(math.log(T5_MAX_DIST / max_exact)) * np.float32(half - max_exact)
    large = np.minimum(max_exact + ratio.astype(np.int32), half - 1)
    return np.where(rel > 0, half, 0) + np.where(n < max_exact, n, large)


def _t5_tables(t5_table, s_max):
    def bias(rel, valid):
        onehot = np.eye(T5_BUCKETS, dtype=np.float32)[_t5_bucket(rel)]
        b = jnp.einsum("qkb,bh->hqk", jnp.asarray(onehot), t5_table.astype(F32), precision=lax.Precision.HIGHEST)
        return jnp.where(jnp.asarray(valid)[None], b, NEG_INF)

    ii = np.arange(BLOCK)[:, None]
    jj = np.arange(3 * BLOCK)[None, :]
    rel = jj - ii - BLOCK
    bband = bias(rel, np.abs(rel) <= WINDOW)
    t = np.arange(s_max)[:, None]
    m = np.arange(N_META)[None, :]
    bmeta = bias(m - (N_META + t), np.ones((s_max, N_META), bool))
    mpos = np.arange(N_META)[:, None]
    kpos = np.arange(N_META + BLOCK)[None, :]
    relq = kpos - mpos
    bq = bias(relq, (kpos < N_META) | (np.abs(relq) <= WINDOW))
    return bband, bmeta, bq[:, :, :N_META], bq[:, :, N_META:]


def _na_bias_cases(rpb):
    W = GRID_W
    delta = np.arange(NA_ROWS)[:, None, None, None]
    i = np.arange(NA_ROWS)[None, :, None, None]
    c = np.arange(W)[None, None, :, None]
    kc = np.arange(W)[None, None, None, :]
    cs = np.clip(c - NA_COLS // 2, 0, W - NA_COLS)
    valid = (kc >= cs) & (kc < cs + NA_COLS)
    dc = np.clip(kc - c + (NA_COLS - 1), 0, 2 * NA_COLS - 2)[0, 0]
    t = jnp.where(jnp.asarray(valid[0, 0]), rpb.astype(F32)[:, :, dc], NEG_INF)
    cases = [jnp.transpose(t[:, NA_ROWS - 1 - d:2 * NA_ROWS - 1 - d], (0, 2, 1, 3)) for d in range(NA_ROWS)]
    return jnp.stack(cases, axis=0).reshape(NA_ROWS, N_HEADS, W, NA_ROWS * W)


def _rope_tables(s_max, n_meta_rows):
    half = HEAD_DIM // 4
    freq = ROPE_THETA ** (-jnp.arange(half, dtype=F32) / half)
    t = np.arange(s_max)
    mp = np.tile(np.arange(N_META) - N_META, n_meta_rows // N_META)
    pos_row = jnp.asarray(np.concatenate([t // GRID_W, mp]), jnp.int32).astype(F32)
    pos_col = jnp.asarray(np.concatenate([t % GRID_W, mp]), jnp.int32).astype(F32)
    ar = pos_row[:, None] * freq
    ac = pos_col[:, None] * freq
    cos = jnp.concatenate([jnp.cos(ar), jnp.cos(ar), jnp.cos(ac), jnp.cos(ac)], axis=1)
    sin = jnp.concatenate([-jnp.sin(ar), jnp.sin(ar), -jnp.sin(ac), jnp.sin(ac)], axis=1)
    return jnp.tile(cos, (1, 2)), jnp.tile(sin, (1, 2))


def kernel(x_prompt, x_sample, meta_tokens, ln_in_g, ln_in_b, t5_table, w_in, q_gain, k_gain, sink,
           na_rpb, na_meta_bias, w_branch, w_out, ln1_g, ln1_b, w_router, router_bias,
           w_expert_gate_up, w_expert_down, w_shared_gate_up, w_shared_down, ln2_g, ln2_b):
    depth = w_in.shape[0]
    alpha = (2 * depth) ** 0.25
    B0, S0, D = x_prompt.shape
    B1, S1, _ = x_sample.shape
    assert D == D_MODEL
    real = B0 * S0 + B1 * S1
    n_meta_rows = -(-(B0 + B1) * N_META // TM) * TM
    R = real + n_meta_rows
    n_meta_blocks = n_meta_rows // N_META
    g0 = _Group(B0, S0, 0, 0, real, B0)
    g1 = _Group(B1, S1, B0 * S0, B0, real, n_meta_blocks - B0)
    s_max = max(S0, S1)

    h = _embed_ln(x_prompt.reshape(B0 * S0, D), x_sample.reshape(B1 * S1, D),
                  jnp.tile(meta_tokens, (TM // N_META, 1)), ln_in_g.reshape(1, D), ln_in_b.reshape(1, D), R)

    tm = 2 * TM if all(v % (2 * TM) == 0 for v in (S0, S1, n_meta_rows)) else TM
    cos_tab, sin_tab = _rope_tables(s_max, tm)
    n0, n1 = B0 * S0 // tm, real // tm
    p0, p1, pm = S0 // tm, S1 // tm, s_max // tm

    def pos_block(i):
        return jnp.where(i < n0, i % p0, jnp.where(i < n1, (i - n0) % p1, pm))

    bband, bmeta, bq_meta, bq_blk = (t * LOG2E for t in _t5_tables(t5_table, s_max))
    bmeta_wide = jnp.pad(bmeta, ((0, 0), (0, 0), (0, LANES - N_META)), constant_values=NEG_INF)
    ones_bd = jnp.asarray(np.kron(np.eye(N_HEADS), np.ones((HEAD_DIM, HEAD_DIM))), BF16)

    qs, ks, vs = [], [], []
    for n in range(N_MIXERS):
        off = n * QKV_WIDTH
        qs.append(w_in[:, :, off:off + MIX_WIDTH])
        ks.append(w_in[:, :, off + MIX_WIDTH:off + MIX_WIDTH + KV_WIDTH])
        vs.append(w_in[:, :, off + MIX_WIDTH + KV_WIDTH:off + QKV_WIDTH])
    wr_t = jnp.swapaxes(w_router, 1, 2)
    wr_hi = wr_t.astype(BF16)
    layers = dict(
        w_qkv=jnp.concatenate(qs + ks + vs, axis=2).astype(BF16),
        w_gate=w_in[:, :, N_MIXERS * QKV_WIDTH:].astype(BF16),
        q_gain=jnp.tile(q_gain, (1, N_HEADS)).reshape(depth, 1, MIX_WIDTH),
        k_gain=jnp.tile(k_gain, (1, N_KV_HEADS)).reshape(depth, 1, KV_WIDTH),
        sink=sink.astype(F32) * LOG2E,
        na_bias=jax.vmap(_na_bias_cases)(na_rpb) * LOG2E,
        na_mbias=na_meta_bias.astype(F32) * LOG2E,
        na_mbias_wide=jnp.pad(na_meta_bias.astype(F32) * LOG2E, ((0, 0), (0, 0), (0, LANES - N_META)),
                              constant_values=NEG_INF),
        w_branch=w_branch.astype(BF16),
        w_out=w_out.astype(BF16),
        ln1_g=ln1_g.reshape(depth, 1, D), ln1_b=ln1_b.reshape(depth, 1, D),
        wr_hi=wr_hi, wr_lo=(wr_t - wr_hi.astype(F32)).astype(BF16),
        rbias=router_bias.astype(F32).reshape(depth, N_EXPERTS, 1),
        wgu=w_expert_gate_up.astype(BF16), wd=w_expert_down.astype(BF16),
        wsgu=w_shared_gate_up.astype(BF16), wsd=w_shared_down.astype(BF16),
        ln2_g=ln2_g.reshape(depth, 1, D), ln2_b=ln2_b.reshape(depth, 1, D),
    )

    def layer(h, p):
        q_all, k_all, v_all, v_ones = _inproj(h, p["w_qkv"], cos_tab, sin_tab, p["q_gain"], p["k_gain"],
                                      ones_bd, pos_block, tm)
        o = None
        for grp in (g0, g1):
            o = _global_attn(grp, q_all, k_all, v_ones, o)
            o = _window_attn(grp, q_all, k_all, v_all, p["sink"], bband, bmeta_wide[:, :grp.S], o)
            o = _na_attn(grp, q_all, k_all, v_all, p["na_bias"], p["na_mbias_wide"], o)
        for grp in (g0, g1):
            o = _meta_attn(grp, q_all, k_all, v_all, p["sink"], bq_meta, bq_blk, p["na_mbias"], o)
        h1, gates = _merge(h, o, p["w_gate"], p["w_branch"], p["w_out"], p["ln1_g"], p["ln1_b"],
                           p["wr_hi"], p["wr_lo"], p["rbias"], alpha, tm)
        h2 = _moe(h1, gates, p["wgu"], p["wd"], p["wsgu"], p["wsd"], p["ln2_g"], p["ln2_b"],
                  alpha)
        return h2, None

    h, _ = lax.scan(layer, h, layers)
    y_prompt = h[:B0 * S0].reshape(B0, S0, D)
    y_sample = h[B0 * S0:real].reshape(B1, S1, D)
    return (y_prompt, y_sample)
```

```python
import functools
import math

import numpy as np
import jax
import jax.numpy as jnp
from jax import lax
from jax.experimental import pallas as pl
from jax.experimental.pallas import tpu as pltpu

F32 = jnp.float32
BF16 = jnp.bfloat16

D_MODEL = 1024
HEAD_DIM = 64
N_HEADS = 8
N_KV_HEADS = 2
GROUP = N_HEADS // N_KV_HEADS
MIX_WIDTH = N_HEADS * HEAD_DIM
KV_WIDTH = N_KV_HEADS * HEAD_DIM
N_MIXERS = 3
QKV_WIDTH = MIX_WIDTH + 2 * KV_WIDTH
N_META = 16
GRID_W = 64
BLOCK = 128
WINDOW = 128
NA_ROWS = 8
NA_COLS = 16
T5_BUCKETS = 32
T5_MAX_DIST = 128
ROPE_THETA = 10000.0
N_EXPERTS = 64
TOP_K = 8
N_EXPERT_GROUPS = 8
TOPK_GROUPS = 4
D_EXPERT = 256
ROUTED_SCALE = 2.5
NEG_INF = -1e30
LOG2E = math.log2(math.e)
LANES = 128

TM = 512
TQ_GLOBAL = 256
TK_GLOBAL = 512
NA_QROWS = 8
WIN_QBLOCKS = 4
VMEM_LIMIT = 56 * 1024 * 1024


def _cparams(sem):
    return pltpu.CompilerParams(dimension_semantics=sem, vmem_limit_bytes=VMEM_LIMIT)


def _dot(a, b):
    return jnp.dot(a, b, preferred_element_type=F32)


def _dot_nt(a, b):
    return lax.dot_general(a, b, (((1,), (1,)), ((), ())), preferred_element_type=F32)


def _split_bf16(x):
    hi = x.astype(BF16)
    lo = (x - hi.astype(F32)).astype(BF16)
    return hi, lo


def _layer_norm(x, g, b):
    mu = jnp.mean(x, axis=-1, keepdims=True)
    xc = x - mu
    var = jnp.mean(xc * xc, axis=-1, keepdims=True)
    return xc * lax.rsqrt(var + 1e-5) * g + b


def _embed_ln_kernel(x0_ref, x1_ref, xm_ref, g_ref, b_ref, o_ref, *, n0, n1):
    i = pl.program_id(0)
    for src, pred in ((x0_ref, i < n0), (x1_ref, (i >= n0) & (i < n0 + n1)), (xm_ref, i >= n0 + n1)):
        @pl.when(pred)
        def _(src=src):
            o_ref[...] = _layer_norm(src[...], g_ref[...], b_ref[...])


def _embed_ln(x0, x1, meta_tile, g, b, n_rows):
    n0, n1 = x0.shape[0] // TM, x1.shape[0] // TM
    return pl.pallas_call(
        functools.partial(_embed_ln_kernel, n0=n0, n1=n1),
        grid=(n_rows // TM,),
        in_specs=[pl.BlockSpec((TM, D_MODEL), lambda i: (jnp.minimum(i, n0 - 1), 0)),
                  pl.BlockSpec((TM, D_MODEL), lambda i: (jnp.clip(i - n0, 0, n1 - 1), 0)),
                  pl.BlockSpec((TM, D_MODEL), lambda i: (0, 0)),
                  pl.BlockSpec((1, D_MODEL), lambda i: (0, 0)),
                  pl.BlockSpec((1, D_MODEL), lambda i: (0, 0))],
        out_specs=pl.BlockSpec((TM, D_MODEL), lambda i: (i, 0)),
        out_shape=jax.ShapeDtypeStruct((n_rows, D_MODEL), F32),
        compiler_params=_cparams(("arbitrary",)),
        name="embed_ln",
    )(x0, x1, meta_tile, g, b)


def _rope_slot(x, cos, sin_signed, first_half):
    fwd = pltpu.roll(x, LANES - 16, 1)
    bwd = pltpu.roll(x, 16, 1)
    return x * cos + jnp.where(first_half, fwd, bwd) * sin_signed


def _head_rms(x, ones_bd, gain):
    hi, lo = _split_bf16(x * x)
    ss = _dot(hi, ones_bd) + _dot(lo, ones_bd)
    return x * lax.rsqrt(ss * (1.0 / HEAD_DIM) + 1e-6) * gain


def _inproj_kernel(h_ref, w_ref, cos_ref, sin_ref, qg_ref, kg_ref, ones_ref, q_ref, k_ref, v_ref, v1_ref):
    x = h_ref[...].astype(BF16)
    cos = cos_ref[...]
    sin = sin_ref[...]
    lane = lax.broadcasted_iota(jnp.int32, cos.shape, 1)
    first_half = (lane % 32) < 16
    scale = HEAD_DIM ** -0.5
    qw = N_MIXERS * MIX_WIDTH
    qa = _head_rms(_dot(x, w_ref[:, 0:MIX_WIDTH]), ones_ref[...], qg_ref[...])
    for s in range(MIX_WIDTH // LANES):
        sl = slice(s * LANES, (s + 1) * LANES)
        q_ref[:, sl] = (_rope_slot(qa[:, sl], cos, sin, first_half) * (scale * LOG2E)).astype(BF16)
    ka = _head_rms(_dot(x, w_ref[:, qw:qw + KV_WIDTH]), ones_ref[0:LANES, 0:LANES], kg_ref[...])
    k_ref[:, 0:KV_WIDTH] = _rope_slot(ka, cos, sin, first_half).astype(BF16)
    for n in range(1, N_MIXERS):
        q_ref[:, n * MIX_WIDTH:(n + 1) * MIX_WIDTH] = (
            _dot(x, w_ref[:, n * MIX_WIDTH:(n + 1) * MIX_WIDTH]) * (scale * LOG2E)).astype(BF16)
        k_ref[:, n * KV_WIDTH:(n + 1) * KV_WIDTH] = _dot(
            x, w_ref[:, qw + n * KV_WIDTH:qw + (n + 1) * KV_WIDTH]).astype(BF16)
    vw = qw + N_MIXERS * KV_WIDTH
    v = _dot(x, w_ref[:, vw:vw + N_MIXERS * KV_WIDTH])
    v_ref[...] = v.astype(BF16)
    va = v[:, 0:KV_WIDTH]
    lo = lane < HEAD_DIM
    v1_ref[:, 0:LANES] = jnp.where(lo, va, 1.0).astype(BF16)
    v1_ref[:, LANES:2 * LANES] = jnp.where(lo, pltpu.roll(va, HEAD_DIM, 1), 1.0).astype(BF16)


def _inproj(h, w_qkv, cos_tab, sin_tab, q_gain, k_gain, ones_bd, pos_block, tm):
    R = h.shape[0]
    const = lambda i: (0, 0)
    return pl.pallas_call(
        _inproj_kernel,
        grid=(R // tm,),
        in_specs=[pl.BlockSpec((tm, D_MODEL), lambda i: (i, 0)),
                  pl.BlockSpec(w_qkv.shape, const),
                  pl.BlockSpec((tm, LANES), lambda i: (pos_block(i), 0)),
                  pl.BlockSpec((tm, LANES), lambda i: (pos_block(i), 0)),
                  pl.BlockSpec((1, MIX_WIDTH), const),
                  pl.BlockSpec((1, KV_WIDTH), const),
                  pl.BlockSpec((MIX_WIDTH, MIX_WIDTH), const)],
        out_specs=[pl.BlockSpec((tm, N_MIXERS * MIX_WIDTH), lambda i: (i, 0)),
                   pl.BlockSpec((tm, N_MIXERS * KV_WIDTH), lambda i: (i, 0)),
                   pl.BlockSpec((tm, N_MIXERS * KV_WIDTH), lambda i: (i, 0)),
                   pl.BlockSpec((tm, N_KV_HEADS * LANES), lambda i: (i, 0))],
        out_shape=[jax.ShapeDtypeStruct((R, N_MIXERS * MIX_WIDTH), BF16),
                   jax.ShapeDtypeStruct((R, N_MIXERS * KV_WIDTH), BF16),
                   jax.ShapeDtypeStruct((R, N_MIXERS * KV_WIDTH), BF16),
                   jax.ShapeDtypeStruct((R, N_KV_HEADS * LANES), BF16)],
        compiler_params=_cparams(("parallel",)),
        name="inproj",
    )(h, w_qkv, cos_tab, sin_tab, q_gain, k_gain, ones_bd)


def _group_queries(q, j):
    lane = lax.broadcasted_iota(jnp.int32, (q.shape[0], LANES), 1)
    keep = (lane < HEAD_DIM) if j == 0 else (lane >= HEAD_DIM)
    parts = []
    for hh in range(GROUP):
        h = GROUP * j + hh
        slot = q[:, (h // 2) * LANES:(h // 2 + 1) * LANES]
        if h % 2 != j:
            slot = pltpu.roll(slot, HEAD_DIM, 1)
        parts.append(jnp.where(keep, slot, 0.0))
    return jnp.concatenate(parts, axis=0).astype(BF16)


def _ungroup_outputs(out, j, T):
    lane = lax.broadcasted_iota(jnp.int32, (T, LANES), 1)
    lo = lane < HEAD_DIM
    slots = []
    for s in range(2):
        even = out[(2 * s) * T:(2 * s + 1) * T]
        odd = out[(2 * s + 1) * T:(2 * s + 2) * T]
        if j == 0:
            slots.append(jnp.where(lo, even, pltpu.roll(odd, HEAD_DIM, 1)))
        else:
            slots.append(jnp.where(lo, pltpu.roll(even, HEAD_DIM, 1), odd))
    return jnp.concatenate(slots, axis=1)


def _fold_lanes(op, *parts):
    cols = [p[:, c * LANES:(c + 1) * LANES] for p in parts for c in range(p.shape[1] // LANES)]
    return functools.reduce(op, cols)


def _row_max(*parts):
    return jnp.max(_fold_lanes(jnp.maximum, *parts), axis=-1, keepdims=True)


def _row_sum(*parts):
    return jnp.sum(_fold_lanes(jnp.add, *parts), axis=-1, keepdims=True)


def _pad_meta_rows(x):
    return jnp.concatenate([x, jnp.zeros((LANES - N_META, x.shape[1]), x.dtype)], axis=0)


def _head_rows(vals, T):
    return jnp.concatenate([jnp.broadcast_to(v, (T, v.shape[-1])) for v in vals], axis=0)


def _global_kernel(q_ref, k_ref, v_ref, km_ref, vm_ref, *rest, S):
    o_ref = rest[-1]
    T = q_ref.shape[0]
    q = q_ref[...].astype(F32)
    km = km_ref[...]
    qs, state = [], []
    for j in range(N_KV_HEADS):
        qj = _group_queries(q, j)
        s_m = _dot_nt(qj, km)
        m0 = jnp.max(s_m, axis=-1, keepdims=True)
        p_m = jnp.exp2((s_m - m0).astype(BF16))
        qs.append(qj)
        state.append((m0, _dot(p_m, vm_ref[:, j * LANES:(j + 1) * LANES])))
    for c in range(S // TK_GLOBAL):
        kc = k_ref[c * TK_GLOBAL:(c + 1) * TK_GLOBAL, :]
        for j in range(N_KV_HEADS):
            m, acc = state[j]
            s = _dot_nt(qs[j], kc)
            m_new = jnp.maximum(m, jnp.max(s, axis=-1, keepdims=True))
            p = jnp.exp2((s - m_new).astype(BF16))
            vc = v_ref[c * TK_GLOBAL:(c + 1) * TK_GLOBAL, j * LANES:(j + 1) * LANES]
            state[j] = (m_new, jnp.exp2(m - m_new) * acc + _dot(p, vc))
    for j in range(N_KV_HEADS):
        acc = state[j][1]
        out = acc / pltpu.roll(acc, HEAD_DIM, 1)
        o_ref[:, j * 2 * LANES:(j + 1) * 2 * LANES] = _ungroup_outputs(out, 0, T).astype(BF16)


def _window_kernel(sink_ref, q_ref, kp_ref, kc_ref, kn_ref, vp_ref, vc_ref, vn_ref, km_ref, vm_ref,
                   bband_ref, bmeta_ref, *rest, nb):
    o_ref = rest[-1]
    i = pl.program_id(1)
    T = BLOCK
    k_span = jnp.concatenate([kp_ref[...], kc_ref[...], kn_ref[...]], axis=0)
    v_span = jnp.concatenate([vp_ref[...], vc_ref[...], vn_ref[...]], axis=0)
    col = lax.broadcasted_iota(jnp.int32, (1, 3 * BLOCK), 1)
    km = _pad_meta_rows(km_ref[...])
    vm = _pad_meta_rows(vm_ref[...])
    for u in range(WIN_QBLOCKS):
        q = q_ref[u * BLOCK:(u + 1) * BLOCK, :].astype(F32)
        kband = k_span[u * BLOCK:(u + 3) * BLOCK]
        vband = v_span[u * BLOCK:(u + 3) * BLOCK]
        in_range = None
        if u == 0:
            in_range = (col >= BLOCK) | (i > 0)
        if u == WIN_QBLOCKS - 1:
            after = (col < 2 * BLOCK) | (i < nb - 1)
            in_range = after if in_range is None else in_range & after
        for j in range(N_KV_HEADS):
            qj = _group_queries(q, j)
            hs = slice(GROUP * j, GROUP * (j + 1))
            s_b = _dot_nt(qj, kband) + bband_ref[hs].reshape(GROUP * T, 3 * BLOCK)
            if in_range is not None:
                s_b = jnp.where(in_range, s_b, NEG_INF)
            s_m = _dot_nt(qj, km) + bmeta_ref[hs, u * BLOCK:(u + 1) * BLOCK].reshape(GROUP * T, LANES)
            sink = jnp.concatenate([jnp.full((T, 1), sink_ref[GROUP * j + hh], F32) for hh in range(GROUP)], axis=0)
            m = jnp.maximum(_row_max(s_b, s_m), sink)
            e_b = jnp.exp2(s_b - m)
            e_m = jnp.exp2(s_m - m)
            denom = _row_sum(e_b, e_m) + jnp.exp2(sink - m)
            acc = _dot(e_b.astype(BF16), vband) + _dot(e_m.astype(BF16), vm)
            o_ref[u * BLOCK:(u + 1) * BLOCK, j * 2 * LANES:(j + 1) * 2 * LANES] = (
                _ungroup_outputs(acc / denom, j, T).astype(BF16))


def _na_kernel(q_ref, k_ref, v_ref, km_ref, vm_ref, bias_ref, mbias_ref, *rest, rows):
    o_ref = rest[-1]
    blk = pl.program_id(1)
    W = GRID_W
    nkeys = NA_ROWS * W
    km = _pad_meta_rows(km_ref[...])
    vm = _pad_meta_rows(vm_ref[...])
    mb = _head_rows([mbias_ref[h:h + 1, :] for h in range(N_HEADS)], W)
    half = GROUP * W
    for rr in range(NA_QROWS):
        r = blk * NA_QROWS + rr
        rs = jnp.clip(r - NA_ROWS // 2, 0, rows - NA_ROWS)
        delta = r - rs
        koff = pl.multiple_of(rs * W, W)
        q = q_ref[rr * W:(rr + 1) * W, :].astype(F32)
        q8 = jnp.concatenate([_group_queries(q, j) for j in range(N_KV_HEADS)], axis=0)
        kw = k_ref[pl.ds(koff, nkeys), :]
        vw = v_ref[pl.ds(koff, nkeys), :]
        s_w = _dot_nt(q8, kw) + bias_ref[delta].reshape(N_HEADS * W, nkeys)
        s_m = _dot_nt(q8, km) + mb
        m = _row_max(s_w, s_m)
        e_w = jnp.exp2(s_w - m)
        e_m = jnp.exp2(s_m - m)
        out = (_dot(e_w.astype(BF16), vw) + _dot(e_m.astype(BF16), vm)) / _row_sum(e_w, e_m)
        for j in range(N_KV_HEADS):
            o_ref[rr * W:(rr + 1) * W, j * 2 * LANES:(j + 1) * 2 * LANES] = (
                _ungroup_outputs(out[j * half:(j + 1) * half], j, W).astype(BF16))


def _meta_kernel(sink_ref, q_ref, ka_ref, va_ref, kb_ref, vb_ref, km_ref, vm_ref,
                 bq_meta_ref, bq_blk_ref, mbias_ref, o_in_ref, o_ref):
    del o_in_ref
    T = N_META
    q = q_ref[...].astype(F32)
    km_all = km_ref[...]
    vm_all = vm_ref[...]

    def finish(n, j, acc, denom):
        lo = n * MIX_WIDTH + j * 2 * LANES
        o_ref[:, lo:lo + 2 * LANES] = _ungroup_outputs(acc / denom, j, T).astype(BF16)

    for j in range(N_KV_HEADS):
        hs = slice(GROUP * j, GROUP * (j + 1))
        qj = _group_queries(q[:, 0:MIX_WIDTH], j)
        km, vm = km_all[:, 0:KV_WIDTH], vm_all[:, 0:KV_WIDTH]
        s_r = _dot_nt(qj, ka_ref[...])
        s_m = _dot_nt(qj, km)
        m = jnp.maximum(jnp.max(s_r, axis=-1, keepdims=True), jnp.max(s_m, axis=-1, keepdims=True))
        e_r = jnp.exp2(s_r - m)
        e_m = jnp.exp2(s_m - m)
        denom = jnp.sum(e_r, axis=-1, keepdims=True) + jnp.sum(e_m, axis=-1, keepdims=True)
        finish(0, j, _dot(e_r.astype(BF16), va_ref[...]) + _dot(e_m.astype(BF16), vm), denom)
        qj = _group_queries(q[:, MIX_WIDTH:2 * MIX_WIDTH], j)
        km, vm = km_all[:, KV_WIDTH:2 * KV_WIDTH], vm_all[:, KV_WIDTH:2 * KV_WIDTH]
        s_r = _dot_nt(qj, kb_ref[...]) + bq_blk_ref[hs].reshape(GROUP * T, BLOCK)
        s_m = _dot_nt(qj, km) + bq_meta_ref[hs].reshape(GROUP * T, N_META)
        sink = jnp.concatenate([jnp.full((T, 1), sink_ref[GROUP * j + hh], F32) for hh in range(GROUP)], axis=0)
        m = jnp.maximum(jnp.maximum(jnp.max(s_r, axis=-1, keepdims=True),
                                    jnp.max(s_m, axis=-1, keepdims=True)), sink)
        e_r = jnp.exp2(s_r - m)
        e_m = jnp.exp2(s_m - m)
        denom = (jnp.sum(e_r, axis=-1, keepdims=True) + jnp.sum(e_m, axis=-1, keepdims=True)
                 + jnp.exp2(sink - m))
        finish(1, j, _dot(e_r.astype(BF16), vb_ref[...]) + _dot(e_m.astype(BF16), vm), denom)
        qj = _group_queries(q[:, 2 * MIX_WIDTH:3 * MIX_WIDTH], j)
        km, vm = km_all[:, 2 * KV_WIDTH:3 * KV_WIDTH], vm_all[:, 2 * KV_WIDTH:3 * KV_WIDTH]
        mb = _head_rows([mbias_ref[GROUP * j + hh:GROUP * j + hh + 1, :] for hh in range(GROUP)], T)
        s_m = _dot_nt(qj, km) + mb
        m = jnp.max(s_m, axis=-1, keepdims=True)
        e_m = jnp.exp2(s_m - m)
        finish(2, j, _dot(e_m.astype(BF16), vm), jnp.sum(e_m, axis=-1, keepdims=True))


class _Group:
    def __init__(self, B, S, real_base, meta_batch_base, meta_base, n_meta_blocks):
        self.B, self.S = B, S
        self.real_base = real_base
        self.meta_blk0 = meta_base // N_META + meta_batch_base
        self.n_meta_blocks = n_meta_blocks
        assert real_base % S == 0 and S % TM == 0 and meta_base % N_META == 0


def _alias_args(o_prev, n_inputs):
    if o_prev is None:
        return [], [], {}
    return [o_prev], [pl.BlockSpec(memory_space=pl.ANY)], {n_inputs: 0}


def _global_attn(grp, q_all, k_all, v_ones, o_prev):
    B, S = grp.B, grp.S
    nq = S // TQ_GLOBAL
    qb0 = grp.real_base // TQ_GLOBAL
    sb0 = grp.real_base // S
    mb0 = grp.meta_blk0
    in_specs = [pl.BlockSpec((TQ_GLOBAL, MIX_WIDTH), lambda b, i: (qb0 + b * nq + i, 0)),
                pl.BlockSpec((S, KV_WIDTH), lambda b, i: (sb0 + b, 0)),
                pl.BlockSpec((S, N_KV_HEADS * LANES), lambda b, i: (sb0 + b, 0)),
                pl.BlockSpec((N_META, KV_WIDTH), lambda b, i: (mb0 + b, 0)),
                pl.BlockSpec((N_META, N_KV_HEADS * LANES), lambda b, i: (mb0 + b, 0))]
    extra, extra_specs, aliases = _alias_args(o_prev, len(in_specs))
    return pl.pallas_call(
        functools.partial(_global_kernel, S=S),
        grid=(B, nq),
        in_specs=in_specs + extra_specs,
        out_specs=pl.BlockSpec((TQ_GLOBAL, MIX_WIDTH), lambda b, i: (qb0 + b * nq + i, 0)),
        out_shape=jax.ShapeDtypeStruct((q_all.shape[0], N_MIXERS * MIX_WIDTH), BF16),
        input_output_aliases=aliases,
        compiler_params=_cparams(("parallel", "arbitrary")),
        name="mixer_global",
    )(q_all, k_all, v_ones, k_all, v_ones, *extra)


def _window_attn(grp, q_all, k_all, v_all, sink, bband, bmeta, o_prev):
    B, S = grp.B, grp.S
    span = WIN_QBLOCKS * BLOCK
    nb, ns = S // BLOCK, S // span
    qb0 = grp.real_base // BLOCK
    sp0 = grp.real_base // span
    mb0 = grp.meta_blk0
    cur = lambda b, i, sink: (sp0 + b * ns + i, 1)
    prv = lambda b, i, sink: (qb0 + b * nb + jnp.maximum(i * WIN_QBLOCKS - 1, 0), 1)
    nxt = lambda b, i, sink: (qb0 + b * nb + jnp.minimum((i + 1) * WIN_QBLOCKS, nb - 1), 1)
    met = lambda b, i, sink: (mb0 + b, 1)
    edge = lambda im: pl.BlockSpec((BLOCK, KV_WIDTH), im)
    mid = pl.BlockSpec((span, KV_WIDTH), cur)
    in_specs = [pl.BlockSpec((span, MIX_WIDTH), cur),
                edge(prv), mid, edge(nxt), edge(prv), mid, edge(nxt),
                pl.BlockSpec((N_META, KV_WIDTH), met), pl.BlockSpec((N_META, KV_WIDTH), met),
                pl.BlockSpec((N_HEADS, BLOCK, 3 * BLOCK), lambda b, i, sink: (0, 0, 0)),
                pl.BlockSpec((N_HEADS, span, LANES), lambda b, i, sink: (0, i, 0))]
    extra, extra_specs, aliases = _alias_args(o_prev, len(in_specs) + 1)
    return pl.pallas_call(
        functools.partial(_window_kernel, nb=ns),
        grid_spec=pltpu.PrefetchScalarGridSpec(
            num_scalar_prefetch=1,
            grid=(B, ns),
            in_specs=in_specs + extra_specs,
            out_specs=pl.BlockSpec((span, MIX_WIDTH), cur)),
        out_shape=jax.ShapeDtypeStruct((q_all.shape[0], N_MIXERS * MIX_WIDTH), BF16),
        input_output_aliases=aliases,
        compiler_params=_cparams(("parallel", "arbitrary")),
        name="mixer_window",
    )(sink, q_all, k_all, k_all, k_all, v_all, v_all, v_all, k_all, v_all, bband, bmeta, *extra)


def _na_attn(grp, q_all, k_all, v_all, na_bias, na_mbias, o_prev):
    B, S = grp.B, grp.S
    rows = S // GRID_W
    tq = NA_QROWS * GRID_W
    nq = S // tq
    qb0 = grp.real_base // tq
    sb0 = grp.real_base // S
    mb0 = grp.meta_blk0
    in_specs = [pl.BlockSpec((tq, MIX_WIDTH), lambda b, i: (qb0 + b * nq + i, 2)),
                pl.BlockSpec((S, KV_WIDTH), lambda b, i: (sb0 + b, 2)),
                pl.BlockSpec((S, KV_WIDTH), lambda b, i: (sb0 + b, 2)),
                pl.BlockSpec((N_META, KV_WIDTH), lambda b, i: (mb0 + b, 2)),
                pl.BlockSpec((N_META, KV_WIDTH), lambda b, i: (mb0 + b, 2)),
                pl.BlockSpec(na_bias.shape, lambda b, i: (0, 0, 0, 0)),
                pl.BlockSpec(na_mbias.shape, lambda b, i: (0, 0))]
    extra, extra_specs, aliases = _alias_args(o_prev, len(in_specs))
    return pl.pallas_call(
        functools.partial(_na_kernel, rows=rows),
        grid=(B, nq),
        in_specs=in_specs + extra_specs,
        out_specs=pl.BlockSpec((tq, MIX_WIDTH), lambda b, i: (qb0 + b * nq + i, 2)),
        out_shape=jax.ShapeDtypeStruct((q_all.shape[0], N_MIXERS * MIX_WIDTH), BF16),
        input_output_aliases=aliases,
        compiler_params=_cparams(("parallel", "arbitrary")),
        name="mixer_neighbourhood",
    )(q_all, k_all, v_all, k_all, v_all, na_bias, na_mbias, *extra)


def _meta_attn(grp, q_all, k_all, v_all, sink, bq_meta, bq_blk, na_mbias, o_prev):
    B, S = grp.B, grp.S
    sb0 = grp.real_base // S
    bb0 = grp.real_base // BLOCK
    nb = S // BLOCK
    mb0 = grp.meta_blk0
    clamp = lambda b: jnp.minimum(b, B - 1)
    mrow = lambda b, sink: (mb0 + clamp(b), 0)
    in_specs = [pl.BlockSpec((N_META, N_MIXERS * MIX_WIDTH), mrow),
                pl.BlockSpec((S, KV_WIDTH), lambda b, sink: (sb0 + clamp(b), 0)),
                pl.BlockSpec((S, KV_WIDTH), lambda b, sink: (sb0 + clamp(b), 0)),
                pl.BlockSpec((BLOCK, KV_WIDTH), lambda b, sink: (bb0 + clamp(b) * nb, 1)),
                pl.BlockSpec((BLOCK, KV_WIDTH), lambda b, sink: (bb0 + clamp(b) * nb, 1)),
                pl.BlockSpec((N_META, N_MIXERS * KV_WIDTH), mrow),
                pl.BlockSpec((N_META, N_MIXERS * KV_WIDTH), mrow),
                pl.BlockSpec(bq_meta.shape, lambda b, sink: (0, 0, 0)),
                pl.BlockSpec(bq_blk.shape, lambda b, sink: (0, 0, 0)),
                pl.BlockSpec(na_mbias.shape, lambda b, sink: (0, 0)),
                pl.BlockSpec(memory_space=pl.ANY)]
    return pl.pallas_call(
        _meta_kernel,
        grid_spec=pltpu.PrefetchScalarGridSpec(
            num_scalar_prefetch=1,
            grid=(grp.n_meta_blocks,),
            in_specs=in_specs,
            out_specs=pl.BlockSpec((N_META, N_MIXERS * MIX_WIDTH), lambda b, sink: (mb0 + b, 0))),
        out_shape=jax.ShapeDtypeStruct((q_all.shape[0], N_MIXERS * MIX_WIDTH), BF16),
        input_output_aliases={len(in_specs): 0},
        compiler_params=_cparams(("arbitrary",)),
        name="mixer_meta_queries",
    )(sink, q_all, k_all, v_all, k_all, v_all, k_all, v_all, bq_meta, bq_blk, na_mbias, o_prev)


def _route(h1, wr_hi, wr_lo, rbias):
    T = h1.shape[0]
    x_hi, x_lo = _split_bf16(h1)
    logits = _dot_nt(wr_hi, x_hi) + _dot_nt(wr_hi, x_lo) + _dot_nt(wr_lo, x_hi)
    scores = 1.0 / (1.0 + jnp.exp(-logits))
    sel = scores + rbias
    per_group = N_EXPERTS // N_EXPERT_GROUPS
    sel3 = sel.reshape(N_EXPERT_GROUPS, per_group, T)
    idx3 = lax.broadcasted_iota(jnp.int32, sel3.shape, 1).astype(F32)
    m1 = jnp.max(sel3, axis=1, keepdims=True)
    first = jnp.min(jnp.where(sel3 == m1, idx3, float(per_group)), axis=1, keepdims=True)
    m2 = jnp.max(jnp.where(idx3 == first, -jnp.inf, sel3), axis=1, keepdims=True)
    gscore = (m1 + m2).reshape(N_EXPERT_GROUPS, T)

    def rank_of(vals):
        idx = lax.broadcasted_iota(jnp.int32, vals.shape, 0)
        rank = jnp.zeros(vals.shape, F32)
        for r in range(vals.shape[0]):
            row = vals[r:r + 1, :]
            ge = jnp.where(row >= vals, 1.0, 0.0)
            gt = jnp.where(row > vals, 1.0, 0.0)
            rank = rank + jnp.where(idx > r, ge, gt)
        return rank

    gkeep = jnp.where(rank_of(gscore) < TOPK_GROUPS, 1.0, 0.0)
    ekeep = jnp.broadcast_to(gkeep.reshape(N_EXPERT_GROUPS, 1, T), sel3.shape).reshape(N_EXPERTS, T)
    masked = jnp.where(ekeep > 0.5, sel, NEG_INF)
    eidx = lax.broadcasted_iota(jnp.int32, masked.shape, 0).astype(F32)
    chosen = jnp.zeros(masked.shape, F32)
    for _ in range(TOP_K):
        best = jnp.max(masked, axis=0, keepdims=True)
        first = jnp.min(jnp.where(masked == best, eidx, float(N_EXPERTS)), axis=0, keepdims=True)
        hit = eidx == first
        chosen = jnp.where(hit, 1.0, chosen)
        masked = jnp.where(hit, -jnp.inf, masked)
    w = jnp.where(chosen > 0.5, scores, 0.0)
    return w / jnp.sum(w, axis=0, keepdims=True) * ROUTED_SCALE


def _merge_kernel(h_ref, o_ref, wg_ref, wb_ref, wo_ref, g_ref, b_ref, wrh_ref, wrl_ref, rb_ref,
                  h1_ref, gates_ref, *, alpha):
    h = h_ref[...]
    x = h.astype(BF16)
    merged = None
    for n in range(N_MIXERS):
        logit = _dot(x, wg_ref[:, n * D_MODEL:(n + 1) * D_MODEL])
        branch = _dot(o_ref[:, n * MIX_WIDTH:(n + 1) * MIX_WIDTH], wb_ref[n])
        term = branch / (1.0 + jnp.exp(-logit))
        merged = term if merged is None else merged + term
    mix = _dot(merged.astype(BF16), wo_ref[...])
    h1 = _layer_norm(alpha * h + mix, g_ref[...], b_ref[...])
    h1_ref[...] = h1
    gates_t = _route(h1, wrh_ref[...], wrl_ref[...], rb_ref[...])
    pad = jnp.zeros((LANES - N_EXPERTS, gates_t.shape[1]), F32)
    gates_ref[...] = jnp.concatenate([gates_t, pad], axis=0).T


def _merge(h, o_all, w_gate, w_branch, w_out, ln_g, ln_b, wr_hi, wr_lo, rbias, alpha, tm):
    R = h.shape[0]
    c2 = lambda i: (0, 0)
    once = pl.Buffered(1)
    return pl.pallas_call(
        functools.partial(_merge_kernel, alpha=alpha),
        grid=(R // tm,),
        in_specs=[pl.BlockSpec((tm, D_MODEL), lambda i: (i, 0)),
                  pl.BlockSpec((tm, N_MIXERS * MIX_WIDTH), lambda i: (i, 0)),
                  pl.BlockSpec(w_gate.shape, c2, pipeline_mode=once),
                  pl.BlockSpec(w_branch.shape, lambda i: (0, 0, 0), pipeline_mode=once),
                  pl.BlockSpec(w_out.shape, c2, pipeline_mode=once),
                  pl.BlockSpec((1, D_MODEL), c2),
                  pl.BlockSpec((1, D_MODEL), c2),
                  pl.BlockSpec(wr_hi.shape, c2),
                  pl.BlockSpec(wr_lo.shape, c2),
                  pl.BlockSpec(rbias.shape, c2)],
        out_specs=[pl.BlockSpec((tm, D_MODEL), lambda i: (i, 0)),
                   pl.BlockSpec((tm, LANES), lambda i: (i, 0))],
        out_shape=[jax.ShapeDtypeStruct((R, D_MODEL), F32),
                   jax.ShapeDtypeStruct((R, LANES), F32)],
        compiler_params=_cparams(("parallel",)),
        name="merge_ln_route",
    )(h, o_all, w_gate, w_branch, w_out, ln_g, ln_b, wr_hi, wr_lo, rbias)


TD = 256
CH = 16
SLOTS = 3072
NCH = SLOTS // CH
MT = 512
ME = 2048
CPM = ME // CH
XBUFS = 3


def _swiglu_act(gu):
    g = gu[:, :D_EXPERT]
    return g / (1.0 + jnp.exp(-g)) * gu[:, D_EXPERT:]


def _slot_of_token(gates, lo_row):
    routed = gates > 0.0
    r = lax.broadcasted_iota(jnp.int32, (TD, TD), 0)
    c = lax.broadcasted_iota(jnp.int32, (TD, TD), 1)
    earlier = jnp.where(c < r, 1.0, 0.0).astype(BF16)
    rank = _dot(earlier, jnp.where(routed, 1.0, 0.0).astype(BF16))
    return jnp.where(routed, lo_row + rank + 1.0, 0.0)


def _split64(x):
    hi = 64.0 * jnp.floor(x * (1.0 / 64.0))
    return jnp.concatenate([hi.astype(BF16), (x - hi).astype(BF16)], axis=0)


def _dispatch_kernel(h_ref, g_ref, lohi_ref, x_ref, w_ref):
    lohi = lohi_ref[0]
    lo_row, hi_row = lohi[0:1], lohi[1:2]
    gates = g_ref[...]
    slot_t = _split64(_slot_of_token(gates, lo_row).T)
    gates_t = gates.T.astype(BF16)
    x = h_ref[...].astype(BF16)
    for blk in range(SLOTS // MT):
        s = (lax.broadcasted_iota(jnp.int32, (MT, LANES), 0) + blk * MT).astype(F32)
        owner = jnp.where(s >= lo_row, jnp.where(s < hi_row, 1.0, 0.0), 0.0).astype(BF16)
        want = _dot(jnp.concatenate([owner, owner], axis=1), slot_t)
        s1 = (lax.broadcasted_iota(jnp.int32, (MT, TD), 0) + (blk * MT + 1)).astype(F32)
        hit = want == s1
        x_ref[blk * MT:(blk + 1) * MT, :] = _dot(jnp.where(hit, 1.0, 0.0).astype(BF16), x).astype(BF16)
        weight = jnp.where(hit, _dot(owner, gates_t), 0.0)
        w_ref[:, blk * MT:(blk + 1) * MT] = weight.T.astype(BF16)


def _dispatch(h1, gates, lohi):
    n = h1.shape[0] // TD
    return pl.pallas_call(
        _dispatch_kernel,
        grid=(n,),
        in_specs=[pl.BlockSpec((TD, D_MODEL), lambda i: (i, 0)),
                  pl.BlockSpec((TD, LANES), lambda i: (i, 0)),
                  pl.BlockSpec((1, 8, LANES), lambda i: (i, 0, 0))],
        out_specs=[pl.BlockSpec((SLOTS, D_MODEL), lambda i: (i, 0)),
                   pl.BlockSpec((TD, SLOTS), lambda i: (i, 0))],
        out_shape=[jax.ShapeDtypeStruct((n * SLOTS, D_MODEL), BF16),
                   jax.ShapeDtypeStruct((n * TD, SLOTS), BF16)],
        compiler_params=_cparams(("parallel",)),
        name="moe_dispatch",
    )(h1, gates, lohi)


def _chunk_gather(table_ref, first, n_chunks, src_hbm, buf, sem):
    copies = []
    for c in range(n_chunks):
        row = pl.multiple_of(table_ref[first + c] * CH, CH)
        copies.append(pltpu.make_async_copy(src_hbm.at[pl.ds(row, CH)], buf.at[pl.ds(c * CH, CH)], sem))
    return copies


def _chunk_wait(n_chunks, src_hbm, buf, sem):
    for c in range(n_chunks):
        pltpu.make_async_copy(src_hbm.at[pl.ds(0, CH)], buf.at[pl.ds(c * CH, CH)], sem).wait()


def _expert_kernel(te_ref, src_ref, nu_ref, x_hbm, wgu_ref, wd_ref, y_ref, xbuf, sem):
    del te_ref
    m = pl.program_id(0)
    n_used = nu_ref[0]

    def start(step):
        slot = step % XBUFS
        for cp in _chunk_gather(src_ref, step * CPM, CPM, x_hbm, xbuf.at[slot], sem.at[slot]):
            cp.start()

    for ahead in range(XBUFS - 1):
        @pl.when((m == 0) & (ahead < n_used))
        def _(ahead=ahead):
            start(ahead)

    @pl.when(m + (XBUFS - 1) < n_used)
    def _():
        start(m + (XBUFS - 1))

    @pl.when(m < n_used)
    def _():
        slot = m % XBUFS
        _chunk_wait(CPM, x_hbm, xbuf.at[slot], sem.at[slot])
        act = _swiglu_act(_dot(xbuf[slot], wgu_ref[0]))
        y_ref[...] = _dot(act.astype(BF16), wd_ref[0]).astype(BF16)


def _experts(x_disp, wgu, wd, tile_expert, src_chunk, n_used):
    n_steps = tile_expert.shape[0]
    return pl.pallas_call(
        _expert_kernel,
        grid_spec=pltpu.PrefetchScalarGridSpec(
            num_scalar_prefetch=3,
            grid=(n_steps,),
            in_specs=[pl.BlockSpec(memory_space=pl.ANY),
                      pl.BlockSpec((1, D_MODEL, 2 * D_EXPERT), lambda m, te, src, nu: (te[m], 0, 0)),
                      pl.BlockSpec((1, D_EXPERT, D_MODEL), lambda m, te, src, nu: (te[m], 0, 0))],
            out_specs=pl.BlockSpec((ME, D_MODEL), lambda m, te, src, nu: (jnp.minimum(m, nu[0] - 1), 0)),
            scratch_shapes=[pltpu.VMEM((XBUFS, ME, D_MODEL), BF16), pltpu.SemaphoreType.DMA((XBUFS,))]),
        out_shape=jax.ShapeDtypeStruct((n_steps * ME, D_MODEL), BF16),
        compiler_params=_cparams(("arbitrary",)),
        name="moe_experts",
    )(tile_expert, src_chunk, n_used, x_disp, wgu, wd)


def _combine_kernel(dst_ref, y_hbm, w_ref, h_ref, wsgu_ref, wsd_ref, lg_ref, lb_ref, o_ref, ybuf, sem, *, alpha):
    i = pl.program_id(0)

    def start(tile):
        slot = tile % 2
        for cp in _chunk_gather(dst_ref, tile * NCH, NCH, y_hbm, ybuf.at[slot], sem.at[slot]):
            cp.start()

    @pl.when(i == 0)
    def _():
        start(i)

    @pl.when(i + 1 < pl.num_programs(0))
    def _():
        start(i + 1)

    h = h_ref[...]
    shared = _dot(_swiglu_act(_dot(h.astype(BF16), wsgu_ref[...])).astype(BF16), wsd_ref[...])
    slot = i % 2
    _chunk_wait(NCH, y_hbm, ybuf.at[slot], sem.at[slot])
    routed = _dot(w_ref[...], ybuf[slot])
    o_ref[...] = _layer_norm(alpha * h + shared + routed, lg_ref[...], lb_ref[...])


def _combine(y_sorted, w_t, h1, dst_chunk, wsgu, wsd, ln_g, ln_b, alpha):
    n = h1.shape[0] // TD
    c2 = lambda i, dst: (0, 0)
    return pl.pallas_call(
        functools.partial(_combine_kernel, alpha=alpha),
        grid_spec=pltpu.PrefetchScalarGridSpec(
            num_scalar_prefetch=1,
            grid=(n,),
            in_specs=[pl.BlockSpec(memory_space=pl.ANY),
                      pl.BlockSpec((TD, SLOTS), lambda i, dst: (i, 0)),
                      pl.BlockSpec((TD, D_MODEL), lambda i, dst: (i, 0)),
                      pl.BlockSpec(wsgu.shape, c2),
                      pl.BlockSpec(wsd.shape, c2),
                      pl.BlockSpec((1, D_MODEL), c2),
                      pl.BlockSpec((1, D_MODEL), c2)],
            out_specs=pl.BlockSpec((TD, D_MODEL), lambda i, dst: (i, 0)),
            scratch_shapes=[pltpu.VMEM((2, SLOTS, D_MODEL), BF16), pltpu.SemaphoreType.DMA((2,))]),
        out_shape=jax.ShapeDtypeStruct((h1.shape[0], D_MODEL), F32),
        compiler_params=_cparams(("arbitrary",)),
        name="moe_combine_ln",
    )(dst_chunk, y_sorted, w_t, h1, wsgu, wsd, ln_g, ln_b)


def _routing_tables(gates):
    n = gates.shape[0] // TD
    cnt = jnp.sum((gates[:, :N_EXPERTS] > 0.0).reshape(n, TD, N_EXPERTS), axis=1, dtype=jnp.int32)
    nch = (cnt + (CH - 1)) // CH
    hi16 = jnp.cumsum(nch, axis=1)
    lo16 = hi16 - nch
    nct = hi16[:, -1:]
    pad = jnp.broadcast_to(nct, (n, LANES - N_EXPERTS))
    lohi = jnp.zeros((n, 8, LANES), F32)
    lohi = lohi.at[:, 0, :].set((jnp.concatenate([lo16, pad], axis=1) * CH).astype(F32))
    lohi = lohi.at[:, 1, :].set((jnp.concatenate([hi16, pad], axis=1) * CH).astype(F32))
    tot = jnp.sum(nch, axis=0)
    seg_len = (tot + (CPM - 1)) // CPM * CPM
    seg_end = jnp.cumsum(seg_len)
    seg_start = seg_end - seg_len
    gpos = seg_start[None, :] + jnp.cumsum(nch, axis=0) - nch
    assert (n * NCH) % CPM == 0
    n_steps = (n * NCH + N_EXPERTS * CPM) // CPM
    n_used = (seg_end[-1] // CPM).astype(jnp.int32).reshape(1)
    step = jnp.arange(n_steps, dtype=jnp.int32)
    tile_expert = jnp.sum(seg_end[None, :] // CPM <= jnp.minimum(step, n_used - 1)[:, None], axis=1, dtype=jnp.int32)
    tile_expert = jnp.minimum(tile_expert, N_EXPERTS - 1)
    exact = functools.partial(jnp.dot, precision=lax.Precision.HIGHEST)
    experts = jnp.arange(N_EXPERTS, dtype=jnp.int32)
    g = jnp.arange(n_steps * CPM, dtype=jnp.int32)
    e_of_g = jnp.minimum(jnp.sum(seg_end[None, :] <= g[:, None], axis=1, dtype=jnp.int32), N_EXPERTS - 1)
    pick_e = (e_of_g[:, None] == experts[None, :]).astype(F32)
    first = exact(pick_e, gpos.T.astype(F32))
    count = exact(pick_e, nch.T.astype(F32))
    base = exact(pick_e, (jnp.arange(n, dtype=jnp.int32)[:, None] * NCH + lo16).T.astype(F32))
    gf = g.astype(F32)[:, None]
    inside = (first <= gf) & (gf < first + count)
    src_chunk = jnp.sum(jnp.where(inside, base + gf - first, 0.0), axis=1).astype(jnp.int32)
    k = jnp.arange(NCH, dtype=jnp.int32)
    e_of_k = jnp.minimum(jnp.sum(hi16[:, None, :] <= k[None, :, None], axis=2, dtype=jnp.int32), N_EXPERTS - 1)
    pick_k = e_of_k[:, :, None] == experts[None, None, :]
    pos = jnp.sum(jnp.where(pick_k, (gpos - lo16)[:, None, :], 0), axis=2) + k[None, :]
    dst_chunk = jnp.where(k[None, :] < nct, pos, 0).astype(jnp.int32).reshape(-1)
    return lohi, tile_expert, src_chunk, n_used, dst_chunk


def _moe(h1, gates, wgu, wd, wsgu, wsd, ln_g, ln_b, alpha):
    lohi, tile_expert, src_chunk, n_used, dst_chunk = _routing_tables(gates)
    x_disp, w_t = _dispatch(h1, gates, lohi)
    y_sorted = _experts(x_disp, wgu, wd, tile_expert, src_chunk, n_used)
    return _combine(y_sorted, w_t, h1, dst_chunk, wsgu, wsd, ln_g, ln_b, alpha)


def _t5_bucket(rel):
    half = T5_BUCKETS // 2
    max_exact = half // 2
    n = np.abs(rel)
    ratio = np.log(np.maximum(n, 1).astype(np.float32) / np.float32(max_exact))
    ratio = ratio / np.float32(math.log(T5_MAX_DIST / max_exact)) * np.float32(half - max_exact)
    large = np.minimum(max_exact + ratio.astype(np.int32), half - 1)
    return np.where(rel > 0, half, 0) + np.where(n < max_exact, n, large)


def _t5_tables(t5_table, s_max):
    def bias(rel, valid):
        onehot = np.eye(T5_BUCKETS, dtype=np.float32)[_t5_bucket(rel)]
        b = jnp.einsum("qkb,bh->hqk", jnp.asarray(onehot), t5_table.astype(F32), precision=lax.Precision.HIGHEST)
        return jnp.where(jnp.asarray(valid)[None], b, NEG_INF)

    ii = np.arange(BLOCK)[:, None]
    jj = np.arange(3 * BLOCK)[None, :]
    rel = jj - ii - BLOCK
    bband = bias(rel, np.abs(rel) <= WINDOW)
    t = np.arange(s_max)[:, None]
    m = np.arange(N_META)[None, :]
    bmeta = bias(m - (N_META + t), np.ones((s_max, N_META), bool))
    mpos = np.arange(N_META)[:, None]
    kpos = np.arange(N_META + BLOCK)[None, :]
    relq = kpos - mpos
    bq = bias(relq, (kpos < N_META) | (np.abs(relq) <= WINDOW))
    return bband, bmeta, bq[:, :, :N_META], bq[:, :, N_META:]


def _na_bias_cases(rpb):
    W = GRID_W
    c = np.arange(W)[:, None]
    kc = np.arange(W)[None, :]
    cs = np.clip(c - NA_COLS // 2, 0, W - NA_COLS)
    valid = (kc >= cs) & (kc < cs + NA_COLS)
    dc = np.clip(kc - c + (NA_COLS - 1), 0, 2 * NA_COLS - 2)
    onehot = np.eye(2 * NA_COLS - 1, dtype=np.float32)[dc]
    t = jnp.einsum("hrd,ckd->hrck", rpb.astype(F32), jnp.asarray(onehot), precision=lax.Precision.HIGHEST)
    t = jnp.where(jnp.asarray(valid), t, NEG_INF)
    cases = [jnp.transpose(t[:, NA_ROWS - 1 - d:2 * NA_ROWS - 1 - d], (0, 2, 1, 3)) for d in range(NA_ROWS)]
    return jnp.stack(cases, axis=0).reshape(NA_ROWS, N_HEADS, W, NA_ROWS * W)


def _rope_tables(s_max, n_meta_rows):
    half = HEAD_DIM // 4
    freq = ROPE_THETA ** (-jnp.arange(half, dtype=F32) / half)
    t = np.arange(s_max)
    mp = np.tile(np.arange(N_META) - N_META, n_meta_rows // N_META)
    pos_row = jnp.asarray(np.concatenate([t // GRID_W, mp]), jnp.int32).astype(F32)
    pos_col = jnp.asarray(np.concatenate([t % GRID_W, mp]), jnp.int32).astype(F32)
    ar = pos_row[:, None] * freq
    ac = pos_col[:, None] * freq
    cos = jnp.concatenate([jnp.cos(ar), jnp.cos(ar), jnp.cos(ac), jnp.cos(ac)], axis=1)
    sin = jnp.concatenate([-jnp.sin(ar), jnp.sin(ar), -jnp.sin(ac), jnp.sin(ac)], axis=1)
    return jnp.tile(cos, (1, 2)), jnp.tile(sin, (1, 2))


def kernel(x_prompt, x_sample, meta_tokens, ln_in_g, ln_in_b, t5_table, w_in, q_gain, k_gain, sink,
           na_rpb, na_meta_bias, w_branch, w_out, ln1_g, ln1_b, w_router, router_bias,
           w_expert_gate_up, w_expert_down, w_shared_gate_up, w_shared_down, ln2_g, ln2_b):
    depth = w_in.shape[0]
    alpha = (2 * depth) ** 0.25
    B0, S0, D = x_prompt.shape
    B1, S1, _ = x_sample.shape
    assert D == D_MODEL
    real = B0 * S0 + B1 * S1
    n_meta_rows = -(-(B0 + B1) * N_META // TM) * TM
    R = real + n_meta_rows
    n_meta_blocks = n_meta_rows // N_META
    g0 = _Group(B0, S0, 0, 0, real, B0)
    g1 = _Group(B1, S1, B0 * S0, B0, real, n_meta_blocks - B0)
    s_max = max(S0, S1)

    h = _embed_ln(x_prompt.reshape(B0 * S0, D), x_sample.reshape(B1 * S1, D),
                  jnp.tile(meta_tokens, (TM // N_META, 1)), ln_in_g.reshape(1, D), ln_in_b.reshape(1, D), R)

    tm = 2 * TM if all(v % (2 * TM) == 0 for v in (S0, S1, n_meta_rows)) else TM
    cos_tab, sin_tab = _rope_tables(s_max, tm)
    n0, n1 = B0 * S0 // tm, real // tm
    p0, p1, pm = S0 // tm, S1 // tm, s_max // tm

    def pos_block(i):
        return jnp.where(i < n0, i % p0, jnp.where(i < n1, (i - n0) % p1, pm))

    bband, bmeta, bq_meta, bq_blk = (t * LOG2E for t in _t5_tables(t5_table, s_max))
    bmeta_wide = jnp.pad(bmeta, ((0, 0), (0, 0), (0, LANES - N_META)), constant_values=NEG_INF)
    ones_bd = jnp.asarray(np.kron(np.eye(N_HEADS), np.ones((HEAD_DIM, HEAD_DIM))), BF16)

    qs, ks, vs = [], [], []
    for n in range(N_MIXERS):
        off = n * QKV_WIDTH
        qs.append(w_in[:, :, off:off + MIX_WIDTH])
        ks.append(w_in[:, :, off + MIX_WIDTH:off + MIX_WIDTH + KV_WIDTH])
        vs.append(w_in[:, :, off + MIX_WIDTH + KV_WIDTH:off + QKV_WIDTH])
    wr_t = jnp.swapaxes(w_router, 1, 2)
    wr_hi = wr_t.astype(BF16)
    layers = dict(
        w_qkv=jnp.concatenate(qs + ks + vs, axis=2).astype(BF16),
        w_gate=w_in[:, :, N_MIXERS * QKV_WIDTH:].astype(BF16),
        q_gain=jnp.tile(q_gain, (1, N_HEADS)).reshape(depth, 1, MIX_WIDTH),
        k_gain=jnp.tile(k_gain, (1, N_KV_HEADS)).reshape(depth, 1, KV_WIDTH),
        sink=sink.astype(F32) * LOG2E,
        na_bias=jax.vmap(_na_bias_cases)(na_rpb) * LOG2E,
        na_mbias=na_meta_bias.astype(F32) * LOG2E,
        na_mbias_wide=jnp.pad(na_meta_bias.astype(F32) * LOG2E, ((0, 0), (0, 0), (0, LANES - N_META)),
                              constant_values=NEG_INF),
        w_branch=w_branch.astype(BF16),
        w_out=w_out.astype(BF16),
        ln1_g=ln1_g.reshape(depth, 1, D), ln1_b=ln1_b.reshape(depth, 1, D),
        wr_hi=wr_hi, wr_lo=(wr_t - wr_hi.astype(F32)).astype(BF16),
        rbias=router_bias.astype(F32).reshape(depth, N_EXPERTS, 1),
        wgu=w_expert_gate_up.astype(BF16), wd=w_expert_down.astype(BF16),
        wsgu=w_shared_gate_up.astype(BF16), wsd=w_shared_down.astype(BF16),
        ln2_g=ln2_g.reshape(depth, 1, D), ln2_b=ln2_b.reshape(depth, 1, D),
    )

    def layer(h, p):
        q_all, k_all, v_all, v_ones = _inproj(h, p["w_qkv"], cos_tab, sin_tab, p["q_gain"], p["k_gain"],
                                      ones_bd, pos_block, tm)
        o = None
        for grp in (g0, g1):
            o = _global_attn(grp, q_all, k_all, v_ones, o)
            o = _window_attn(grp, q_all, k_all, v_all, p["sink"], bband, bmeta_wide[:, :grp.S], o)
            o = _na_attn(grp, q_all, k_all, v_all, p["na_bias"], p["na_mbias_wide"], o)
        for grp in (g0, g1):
            o = _meta_attn(grp, q_all, k_all, v_all, p["sink"], bq_meta, bq_blk, p["na_mbias"], o)
        h1, gates = _merge(h, o, p["w_gate"], p["w_branch"], p["w_out"], p["ln1_g"], p["ln1_b"],
                           p["wr_hi"], p["wr_lo"], p["rbias"], alpha, tm)
        h2 = _moe(h1, gates, p["wgu"], p["wd"], p["wsgu"], p["wsd"], p["ln2_g"], p["ln2_b"],
                  alpha)
        return h2, None

    h, _ = lax.scan(layer, h, layers)
    y_prompt = h[:B0 * S0].reshape(B0, S0, D)
    y_sample = h[B0 * S0:real].reshape(B1, S1, D)
    return (y_prompt, y_sample)
```

```python
import functools
import math

import numpy as np
import jax
import jax.numpy as jnp
from jax import lax
from jax.experimental import pallas as pl
from jax.experimental.pallas import tpu as pltpu

F32 = jnp.float32
BF16 = jnp.bfloat16

D_MODEL = 1024
HEAD_DIM = 64
N_HEADS = 8
N_KV_HEADS = 2
GROUP = N_HEADS // N_KV_HEADS
MIX_WIDTH = N_HEADS * HEAD_DIM
KV_WIDTH = N_KV_HEADS * HEAD_DIM
N_MIXERS = 3
QKV_WIDTH = MIX_WIDTH + 2 * KV_WIDTH
N_META = 16
GRID_W = 64
BLOCK = 128
WINDOW = 128
NA_ROWS = 8
NA_COLS = 16
T5_BUCKETS = 32
T5_MAX_DIST = 128
ROPE_THETA = 10000.0
N_EXPERTS = 64
TOP_K = 8
N_EXPERT_GROUPS = 8
TOPK_GROUPS = 4
D_EXPERT = 256
ROUTED_SCALE = 2.5
NEG_INF = -1e30
LOG2E = math.log2(math.e)
LANES = 128

TM = 512
TQ_GLOBAL = 256
TK_GLOBAL = 512
NA_QROWS = 8
WIN_QBLOCKS = 4
VMEM_LIMIT = 56 * 1024 * 1024


def _cparams(sem):
    return pltpu.CompilerParams(dimension_semantics=sem, vmem_limit_bytes=VMEM_LIMIT)


def _dot(a, b):
    return jnp.dot(a, b, preferred_element_type=F32)


def _dot_nt(a, b):
    return lax.dot_general(a, b, (((1,), (1,)), ((), ())), preferred_element_type=F32)


def _split_bf16(x):
    hi = x.astype(BF16)
    lo = (x - hi.astype(F32)).astype(BF16)
    return hi, lo


def _layer_norm(x, g, b):
    mu = jnp.mean(x, axis=-1, keepdims=True)
    xc = x - mu
    var = jnp.mean(xc * xc, axis=-1, keepdims=True)
    return xc * lax.rsqrt(var + 1e-5) * g + b


def _embed_ln_kernel(x0_ref, x1_ref, xm_ref, g_ref, b_ref, o_ref, *, n0, n1):
    i = pl.program_id(0)
    for src, pred in ((x0_ref, i < n0), (x1_ref, (i >= n0) & (i < n0 + n1)), (xm_ref, i >= n0 + n1)):
        @pl.when(pred)
        def _(src=src):
            o_ref[...] = _layer_norm(src[...], g_ref[...], b_ref[...])


def _embed_ln(x0, x1, meta_tile, g, b, n_rows):
    n0, n1 = x0.shape[0] // TM, x1.shape[0] // TM
    return pl.pallas_call(
        functools.partial(_embed_ln_kernel, n0=n0, n1=n1),
        grid=(n_rows // TM,),
        in_specs=[pl.BlockSpec((TM, D_MODEL), lambda i: (jnp.minimum(i, n0 - 1), 0)),
                  pl.BlockSpec((TM, D_MODEL), lambda i: (jnp.clip(i - n0, 0, n1 - 1), 0)),
                  pl.BlockSpec((TM, D_MODEL), lambda i: (0, 0)),
                  pl.BlockSpec((1, D_MODEL), lambda i: (0, 0)),
                  pl.BlockSpec((1, D_MODEL), lambda i: (0, 0))],
        out_specs=pl.BlockSpec((TM, D_MODEL), lambda i: (i, 0)),
        out_shape=jax.ShapeDtypeStruct((n_rows, D_MODEL), F32),
        compiler_params=_cparams(("arbitrary",)),
        name="embed_ln",
    )(x0, x1, meta_tile, g, b)


def _rope_slot(x, cos, sin_signed, first_half):
    fwd = pltpu.roll(x, LANES - 16, 1)
    bwd = pltpu.roll(x, 16, 1)
    return x * cos + jnp.where(first_half, fwd, bwd) * sin_signed


def _head_rms(x, ones_bd, gain):
    hi, lo = _split_bf16(x * x)
    ss = _dot(hi, ones_bd) + _dot(lo, ones_bd)
    return x * lax.rsqrt(ss * (1.0 / HEAD_DIM) + 1e-6) * gain


def _inproj_kernel(h_ref, w_ref, cos_ref, sin_ref, qg_ref, kg_ref, ones_ref, q_ref, k_ref, v_ref, v1_ref):
    x = h_ref[...].astype(BF16)
    cos = cos_ref[...]
    sin = sin_ref[...]
    lane = lax.broadcasted_iota(jnp.int32, cos.shape, 1)
    first_half = (lane % 32) < 16
    scale = HEAD_DIM ** -0.5
    qw = N_MIXERS * MIX_WIDTH
    qa = _head_rms(_dot(x, w_ref[:, 0:MIX_WIDTH]), ones_ref[...], qg_ref[...])
    for s in range(MIX_WIDTH // LANES):
        sl = slice(s * LANES, (s + 1) * LANES)
        q_ref[:, sl] = (_rope_slot(qa[:, sl], cos, sin, first_half) * (scale * LOG2E)).astype(BF16)
    ka = _head_rms(_dot(x, w_ref[:, qw:qw + KV_WIDTH]), ones_ref[0:LANES, 0:LANES], kg_ref[...])
    k_ref[:, 0:KV_WIDTH] = _rope_slot(ka, cos, sin, first_half).astype(BF16)
    for n in range(1, N_MIXERS):
        q_ref[:, n * MIX_WIDTH:(n + 1) * MIX_WIDTH] = (
            _dot(x, w_ref[:, n * MIX_WIDTH:(n + 1) * MIX_WIDTH]) * (scale * LOG2E)).astype(BF16)
        k_ref[:, n * KV_WIDTH:(n + 1) * KV_WIDTH] = _dot(
            x, w_ref[:, qw + n * KV_WIDTH:qw + (n + 1) * KV_WIDTH]).astype(BF16)
    vw = qw + N_MIXERS * KV_WIDTH
    v = _dot(x, w_ref[:, vw:vw + N_MIXERS * KV_WIDTH])
    v_ref[...] = v.astype(BF16)
    va = v[:, 0:KV_WIDTH]
    lo = lane < HEAD_DIM
    v1_ref[:, 0:LANES] = jnp.where(lo, va, 1.0).astype(BF16)
    v1_ref[:, LANES:2 * LANES] = jnp.where(lo, pltpu.roll(va, HEAD_DIM, 1), 1.0).astype(BF16)


def _inproj(h, w_qkv, cos_tab, sin_tab, q_gain, k_gain, ones_bd, pos_block, tm):
    R = h.shape[0]
    const = lambda i: (0, 0)
    return pl.pallas_call(
        _inproj_kernel,
        grid=(R // tm,),
        in_specs=[pl.BlockSpec((tm, D_MODEL), lambda i: (i, 0)),
                  pl.BlockSpec(w_qkv.shape, const),
                  pl.BlockSpec((tm, LANES), lambda i: (pos_block(i), 0)),
                  pl.BlockSpec((tm, LANES), lambda i: (pos_block(i), 0)),
                  pl.BlockSpec((1, MIX_WIDTH), const),
                  pl.BlockSpec((1, KV_WIDTH), const),
                  pl.BlockSpec((MIX_WIDTH, MIX_WIDTH), const)],
        out_specs=[pl.BlockSpec((tm, N_MIXERS * MIX_WIDTH), lambda i: (i, 0)),
                   pl.BlockSpec((tm, N_MIXERS * KV_WIDTH), lambda i: (i, 0)),
                   pl.BlockSpec((tm, N_MIXERS * KV_WIDTH), lambda i: (i, 0)),
                   pl.BlockSpec((tm, N_KV_HEADS * LANES), lambda i: (i, 0))],
        out_shape=[jax.ShapeDtypeStruct((R, N_MIXERS * MIX_WIDTH), BF16),
                   jax.ShapeDtypeStruct((R, N_MIXERS * KV_WIDTH), BF16),
                   jax.ShapeDtypeStruct((R, N_MIXERS * KV_WIDTH), BF16),
                   jax.ShapeDtypeStruct((R, N_KV_HEADS * LANES), BF16)],
        compiler_params=_cparams(("parallel",)),
        name="inproj",
    )(h, w_qkv, cos_tab, sin_tab, q_gain, k_gain, ones_bd)


def _group_queries(q, j):
    lane = lax.broadcasted_iota(jnp.int32, (q.shape[0], LANES), 1)
    keep = (lane < HEAD_DIM) if j == 0 else (lane >= HEAD_DIM)
    parts = []
    for hh in range(GROUP):
        h = GROUP * j + hh
        slot = q[:, (h // 2) * LANES:(h // 2 + 1) * LANES]
        if h % 2 != j:
            slot = pltpu.roll(slot, HEAD_DIM, 1)
        parts.append(jnp.where(keep, slot, 0.0))
    return jnp.concatenate(parts, axis=0).astype(BF16)


def _ungroup_outputs(out, j, T):
    lane = lax.broadcasted_iota(jnp.int32, (T, LANES), 1)
    lo = lane < HEAD_DIM
    slots = []
    for s in range(2):
        even = out[(2 * s) * T:(2 * s + 1) * T]
        odd = out[(2 * s + 1) * T:(2 * s + 2) * T]
        if j == 0:
            slots.append(jnp.where(lo, even, pltpu.roll(odd, HEAD_DIM, 1)))
        else:
            slots.append(jnp.where(lo, pltpu.roll(even, HEAD_DIM, 1), odd))
    return jnp.concatenate(slots, axis=1)


def _fold_lanes(op, *parts):
    cols = [p[:, c * LANES:(c + 1) * LANES] for p in parts for c in range(p.shape[1] // LANES)]
    return functools.reduce(op, cols)


def _row_max(*parts):
    return jnp.max(_fold_lanes(jnp.maximum, *parts), axis=-1, keepdims=True)


def _row_sum(*parts):
    return jnp.sum(_fold_lanes(jnp.add, *parts), axis=-1, keepdims=True)


def _pad_meta_rows(x):
    return jnp.concatenate([x, jnp.zeros((LANES - N_META, x.shape[1]), x.dtype)], axis=0)


def _head_rows(vals, T):
    return jnp.concatenate([jnp.broadcast_to(v, (T, v.shape[-1])) for v in vals], axis=0)


def _global_kernel(q_ref, k_ref, v_ref, km_ref, vm_ref, *rest, S):
    o_ref = rest[-1]
    T = q_ref.shape[0]
    q = q_ref[...].astype(F32)
    km = km_ref[...]
    qs, state = [], []
    for j in range(N_KV_HEADS):
        qj = _group_queries(q, j)
        s_m = _dot_nt(qj, km)
        m0 = jnp.max(s_m, axis=-1, keepdims=True)
        p_m = jnp.exp2((s_m - m0).astype(BF16))
        qs.append(qj)
        state.append((m0, _dot(p_m, vm_ref[:, j * LANES:(j + 1) * LANES])))
    for c in range(S // TK_GLOBAL):
        kc = k_ref[c * TK_GLOBAL:(c + 1) * TK_GLOBAL, :]
        for j in range(N_KV_HEADS):
            m, acc = state[j]
            s = _dot_nt(qs[j], kc)
            m_new = jnp.maximum(m, jnp.max(s, axis=-1, keepdims=True))
            p = jnp.exp2((s - m_new).astype(BF16))
            vc = v_ref[c * TK_GLOBAL:(c + 1) * TK_GLOBAL, j * LANES:(j + 1) * LANES]
            state[j] = (m_new, jnp.exp2(m - m_new) * acc + _dot(p, vc))
    for j in range(N_KV_HEADS):
        acc = state[j][1]
        out = acc / pltpu.roll(acc, HEAD_DIM, 1)
        o_ref[:, j * 2 * LANES:(j + 1) * 2 * LANES] = _ungroup_outputs(out, 0, T).astype(BF16)


def _window_kernel(sink_ref, q_ref, kp_ref, kc_ref, kn_ref, vp_ref, vc_ref, vn_ref, km_ref, vm_ref,
                   bband_ref, bmeta_ref, *rest, nb):
    o_ref = rest[-1]
    i = pl.program_id(1)
    T = BLOCK
    k_span = jnp.concatenate([kp_ref[...], kc_ref[...], kn_ref[...]], axis=0)
    v_span = jnp.concatenate([vp_ref[...], vc_ref[...], vn_ref[...]], axis=0)
    col = lax.broadcasted_iota(jnp.int32, (1, 3 * BLOCK), 1)
    km = _pad_meta_rows(km_ref[...])
    vm = _pad_meta_rows(vm_ref[...])
    for u in range(WIN_QBLOCKS):
        q = q_ref[u * BLOCK:(u + 1) * BLOCK, :].astype(F32)
        kband = k_span[u * BLOCK:(u + 3) * BLOCK]
        vband = v_span[u * BLOCK:(u + 3) * BLOCK]
        in_range = None
        if u == 0:
            in_range = (col >= BLOCK) | (i > 0)
        if u == WIN_QBLOCKS - 1:
            after = (col < 2 * BLOCK) | (i < nb - 1)
            in_range = after if in_range is None else in_range & after
        for j in range(N_KV_HEADS):
            qj = _group_queries(q, j)
            hs = slice(GROUP * j, GROUP * (j + 1))
            s_b = _dot_nt(qj, kband) + bband_ref[hs].reshape(GROUP * T, 3 * BLOCK)
            if in_range is not None:
                s_b = jnp.where(in_range, s_b, NEG_INF)
            s_m = _dot_nt(qj, km) + bmeta_ref[hs, u * BLOCK:(u + 1) * BLOCK].reshape(GROUP * T, LANES)
            sink = jnp.concatenate([jnp.full((T, 1), sink_ref[GROUP * j + hh], F32) for hh in range(GROUP)], axis=0)
            m = jnp.maximum(_row_max(s_b, s_m), sink)
            e_b = jnp.exp2(s_b - m)
            e_m = jnp.exp2(s_m - m)
            denom = _row_sum(e_b, e_m) + jnp.exp2(sink - m)
            acc = _dot(e_b.astype(BF16), vband) + _dot(e_m.astype(BF16), vm)
            o_ref[u * BLOCK:(u + 1) * BLOCK, j * 2 * LANES:(j + 1) * 2 * LANES] = (
                _ungroup_outputs(acc / denom, j, T).astype(BF16))


def _na_kernel(q_ref, k_ref, v_ref, km_ref, vm_ref, bias_ref, mbias_ref, *rest, rows):
    o_ref = rest[-1]
    blk = pl.program_id(1)
    W = GRID_W
    nkeys = NA_ROWS * W
    km = _pad_meta_rows(km_ref[...])
    vm = _pad_meta_rows(vm_ref[...])
    mb = _head_rows([mbias_ref[h:h + 1, :] for h in range(N_HEADS)], W)
    half = GROUP * W
    for rr in range(NA_QROWS):
        r = blk * NA_QROWS + rr
        rs = jnp.clip(r - NA_ROWS // 2, 0, rows - NA_ROWS)
        delta = r - rs
        koff = pl.multiple_of(rs * W, W)
        q = q_ref[rr * W:(rr + 1) * W, :].astype(F32)
        q8 = jnp.concatenate([_group_queries(q, j) for j in range(N_KV_HEADS)], axis=0)
        kw = k_ref[pl.ds(koff, nkeys), :]
        vw = v_ref[pl.ds(koff, nkeys), :]
        s_w = _dot_nt(q8, kw) + bias_ref[delta].reshape(N_HEADS * W, nkeys)
        s_m = _dot_nt(q8, km) + mb
        m = _row_max(s_w, s_m)
        e_w = jnp.exp2(s_w - m)
        e_m = jnp.exp2(s_m - m)
        out = (_dot(e_w.astype(BF16), vw) + _dot(e_m.astype(BF16), vm)) / _row_sum(e_w, e_m)
        for j in range(N_KV_HEADS):
            o_ref[rr * W:(rr + 1) * W, j * 2 * LANES:(j + 1) * 2 * LANES] = (
                _ungroup_outputs(out[j * half:(j + 1) * half], j, W).astype(BF16))


def _meta_kernel(sink_ref, q_ref, ka_ref, va_ref, kb_ref, vb_ref, km_ref, vm_ref,
                 bq_meta_ref, bq_blk_ref, mbias_ref, o_in_ref, o_ref):
    del o_in_ref
    T = N_META
    q = q_ref[...].astype(F32)
    km_all = km_ref[...]
    vm_all = vm_ref[...]

    def finish(n, j, acc, denom):
        lo = n * MIX_WIDTH + j * 2 * LANES
        o_ref[:, lo:lo + 2 * LANES] = _ungroup_outputs(acc / denom, j, T).astype(BF16)

    for j in range(N_KV_HEADS):
        hs = slice(GROUP * j, GROUP * (j + 1))
        qj = _group_queries(q[:, 0:MIX_WIDTH], j)
        km, vm = km_all[:, 0:KV_WIDTH], vm_all[:, 0:KV_WIDTH]
        s_r = _dot_nt(qj, ka_ref[...])
        s_m = _dot_nt(qj, km)
        m = jnp.maximum(jnp.max(s_r, axis=-1, keepdims=True), jnp.max(s_m, axis=-1, keepdims=True))
        e_r = jnp.exp2(s_r - m)
        e_m = jnp.exp2(s_m - m)
        denom = jnp.sum(e_r, axis=-1, keepdims=True) + jnp.sum(e_m, axis=-1, keepdims=True)
        finish(0, j, _dot(e_r.astype(BF16), va_ref[...]) + _dot(e_m.astype(BF16), vm), denom)
        qj = _group_queries(q[:, MIX_WIDTH:2 * MIX_WIDTH], j)
        km, vm = km_all[:, KV_WIDTH:2 * KV_WIDTH], vm_all[:, KV_WIDTH:2 * KV_WIDTH]
        s_r = _dot_nt(qj, kb_ref[...]) + bq_blk_ref[hs].reshape(GROUP * T, BLOCK)
        s_m = _dot_nt(qj, km) + bq_meta_ref[hs].reshape(GROUP * T, N_META)
        sink = jnp.concatenate([jnp.full((T, 1), sink_ref[GROUP * j + hh], F32) for hh in range(GROUP)], axis=0)
        m = jnp.maximum(jnp.maximum(jnp.max(s_r, axis=-1, keepdims=True),
                                    jnp.max(s_m, axis=-1, keepdims=True)), sink)
        e_r = jnp.exp2(s_r - m)
        e_m = jnp.exp2(s_m - m)
        denom = (jnp.sum(e_r, axis=-1, keepdims=True) + jnp.sum(e_m, axis=-1, keepdims=True)
                 + jnp.exp2(sink - m))
        finish(1, j, _dot(e_r.astype(BF16), vb_ref[...]) + _dot(e_m.astype(BF16), vm), denom)
        qj = _group_queries(q[:, 2 * MIX_WIDTH:3 * MIX_WIDTH], j)
        km, vm = km_all[:, 2 * KV_WIDTH:3 * KV_WIDTH], vm_all[:, 2 * KV_WIDTH:3 * KV_WIDTH]
        mb = _head_rows([mbias_ref[GROUP * j + hh:GROUP * j + hh + 1, :] for hh in range(GROUP)], T)
        s_m = _dot_nt(qj, km) + mb
        m = jnp.max(s_m, axis=-1, keepdims=True)
        e_m = jnp.exp2(s_m - m)
        finish(2, j, _dot(e_m.astype(BF16), vm), jnp.sum(e_m, axis=-1, keepdims=True))


class _Group:
    def __init__(self, B, S, real_base, meta_batch_base, meta_base, n_meta_blocks):
        self.B, self.S = B, S
        self.real_base = real_base
        self.meta_blk0 = meta_base // N_META + meta_batch_base
        self.n_meta_blocks = n_meta_blocks
        assert real_base % S == 0 and S % TM == 0 and meta_base % N_META == 0


def _alias_args(o_prev, n_inputs):
    if o_prev is None:
        return [], [], {}
    return [o_prev], [pl.BlockSpec(memory_space=pl.ANY)], {n_inputs: 0}


def _global_attn(grp, q_all, k_all, v_ones, o_prev):
    B, S = grp.B, grp.S
    nq = S // TQ_GLOBAL
    qb0 = grp.real_base // TQ_GLOBAL
    sb0 = grp.real_base // S
    mb0 = grp.meta_blk0
    in_specs = [pl.BlockSpec((TQ_GLOBAL, MIX_WIDTH), lambda b, i: (qb0 + b * nq + i, 0)),
                pl.BlockSpec((S, KV_WIDTH), lambda b, i: (sb0 + b, 0)),
                pl.BlockSpec((S, N_KV_HEADS * LANES), lambda b, i: (sb0 + b, 0)),
                pl.BlockSpec((N_META, KV_WIDTH), lambda b, i: (mb0 + b, 0)),
                pl.BlockSpec((N_META, N_KV_HEADS * LANES), lambda b, i: (mb0 + b, 0))]
    extra, extra_specs, aliases = _alias_args(o_prev, len(in_specs))
    return pl.pallas_call(
        functools.partial(_global_kernel, S=S),
        grid=(B, nq),
        in_specs=in_specs + extra_specs,
        out_specs=pl.BlockSpec((TQ_GLOBAL, MIX_WIDTH), lambda b, i: (qb0 + b * nq + i, 0)),
        out_shape=jax.ShapeDtypeStruct((q_all.shape[0], N_MIXERS * MIX_WIDTH), BF16),
        input_output_aliases=aliases,
        compiler_params=_cparams(("parallel", "arbitrary")),
        name="mixer_global",
    )(q_all, k_all, v_ones, k_all, v_ones, *extra)


def _window_attn(grp, q_all, k_all, v_all, sink, bband, bmeta, o_prev):
    B, S = grp.B, grp.S
    span = WIN_QBLOCKS * BLOCK
    nb, ns = S // BLOCK, S // span
    qb0 = grp.real_base // BLOCK
    sp0 = grp.real_base // span
    mb0 = grp.meta_blk0
    cur = lambda b, i, sink: (sp0 + b * ns + i, 1)
    prv = lambda b, i, sink: (qb0 + b * nb + jnp.maximum(i * WIN_QBLOCKS - 1, 0), 1)
    nxt = lambda b, i, sink: (qb0 + b * nb + jnp.minimum((i + 1) * WIN_QBLOCKS, nb - 1), 1)
    met = lambda b, i, sink: (mb0 + b, 1)
    edge = lambda im: pl.BlockSpec((BLOCK, KV_WIDTH), im)
    mid = pl.BlockSpec((span, KV_WIDTH), cur)
    in_specs = [pl.BlockSpec((span, MIX_WIDTH), cur),
                edge(prv), mid, edge(nxt), edge(prv), mid, edge(nxt),
                pl.BlockSpec((N_META, KV_WIDTH), met), pl.BlockSpec((N_META, KV_WIDTH), met),
                pl.BlockSpec((N_HEADS, BLOCK, 3 * BLOCK), lambda b, i, sink: (0, 0, 0)),
                pl.BlockSpec((N_HEADS, span, LANES), lambda b, i, sink: (0, i, 0))]
    extra, extra_specs, aliases = _alias_args(o_prev, len(in_specs) + 1)
    return pl.pallas_call(
        functools.partial(_window_kernel, nb=ns),
        grid_spec=pltpu.PrefetchScalarGridSpec(
            num_scalar_prefetch=1,
            grid=(B, ns),
            in_specs=in_specs + extra_specs,
            out_specs=pl.BlockSpec((span, MIX_WIDTH), cur)),
        out_shape=jax.ShapeDtypeStruct((q_all.shape[0], N_MIXERS * MIX_WIDTH), BF16),
        input_output_aliases=aliases,
        compiler_params=_cparams(("parallel", "arbitrary")),
        name="mixer_window",
    )(sink, q_all, k_all, k_all, k_all, v_all, v_all, v_all, k_all, v_all, bband, bmeta, *extra)


def _na_attn(grp, q_all, k_all, v_all, na_bias, na_mbias, o_prev):
    B, S = grp.B, grp.S
    rows = S // GRID_W
    tq = NA_QROWS * GRID_W
    nq = S // tq
    qb0 = grp.real_base // tq
    sb0 = grp.real_base // S
    mb0 = grp.meta_blk0
    in_specs = [pl.BlockSpec((tq, MIX_WIDTH), lambda b, i: (qb0 + b * nq + i, 2)),
                pl.BlockSpec((S, KV_WIDTH), lambda b, i: (sb0 + b, 2)),
                pl.BlockSpec((S, KV_WIDTH), lambda b, i: (sb0 + b, 2)),
                pl.BlockSpec((N_META, KV_WIDTH), lambda b, i: (mb0 + b, 2)),
                pl.BlockSpec((N_META, KV_WIDTH), lambda b, i: (mb0 + b, 2)),
                pl.BlockSpec(na_bias.shape, lambda b, i: (0, 0, 0, 0)),
                pl.BlockSpec(na_mbias.shape, lambda b, i: (0, 0))]
    extra, extra_specs, aliases = _alias_args(o_prev, len(in_specs))
    return pl.pallas_call(
        functools.partial(_na_kernel, rows=rows),
        grid=(B, nq),
        in_specs=in_specs + extra_specs,
        out_specs=pl.BlockSpec((tq, MIX_WIDTH), lambda b, i: (qb0 + b * nq + i, 2)),
        out_shape=jax.ShapeDtypeStruct((q_all.shape[0], N_MIXERS * MIX_WIDTH), BF16),
        input_output_aliases=aliases,
        compiler_params=_cparams(("parallel", "arbitrary")),
        name="mixer_neighbourhood",
    )(q_all, k_all, v_all, k_all, v_all, na_bias, na_mbias, *extra)


def _meta_attn(grp, q_all, k_all, v_all, sink, bq_meta, bq_blk, na_mbias, o_prev):
    B, S = grp.B, grp.S
    sb0 = grp.real_base // S
    bb0 = grp.real_base // BLOCK
    nb = S // BLOCK
    mb0 = grp.meta_blk0
    clamp = lambda b: jnp.minimum(b, B - 1)
    mrow = lambda b, sink: (mb0 + clamp(b), 0)
    in_specs = [pl.BlockSpec((N_META, N_MIXERS * MIX_WIDTH), mrow),
                pl.BlockSpec((S, KV_WIDTH), lambda b, sink: (sb0 + clamp(b), 0)),
                pl.BlockSpec((S, KV_WIDTH), lambda b, sink: (sb0 + clamp(b), 0)),
                pl.BlockSpec((BLOCK, KV_WIDTH), lambda b, sink: (bb0 + clamp(b) * nb, 1)),
                pl.BlockSpec((BLOCK, KV_WIDTH), lambda b, sink: (bb0 + clamp(b) * nb, 1)),
                pl.BlockSpec((N_META, N_MIXERS * KV_WIDTH), mrow),
                pl.BlockSpec((N_META, N_MIXERS * KV_WIDTH), mrow),
                pl.BlockSpec(bq_meta.shape, lambda b, sink: (0, 0, 0)),
                pl.BlockSpec(bq_blk.shape, lambda b, sink: (0, 0, 0)),
                pl.BlockSpec(na_mbias.shape, lambda b, sink: (0, 0)),
                pl.BlockSpec(memory_space=pl.ANY)]
    return pl.pallas_call(
        _meta_kernel,
        grid_spec=pltpu.PrefetchScalarGridSpec(
            num_scalar_prefetch=1,
            grid=(grp.n_meta_blocks,),
            in_specs=in_specs,
            out_specs=pl.BlockSpec((N_META, N_MIXERS * MIX_WIDTH), lambda b, sink: (mb0 + b, 0))),
        out_shape=jax.ShapeDtypeStruct((q_all.shape[0], N_MIXERS * MIX_WIDTH), BF16),
        input_output_aliases={len(in_specs): 0},
        compiler_params=_cparams(("arbitrary",)),
        name="mixer_meta_queries",
    )(sink, q_all, k_all, v_all, k_all, v_all, k_all, v_all, bq_meta, bq_blk, na_mbias, o_prev)


def _route(h1, wr_hi, wr_lo, rbias):
    T = h1.shape[0]
    x_hi, x_lo = _split_bf16(h1)
    logits = _dot_nt(wr_hi, x_hi) + _dot_nt(wr_hi, x_lo) + _dot_nt(wr_lo, x_hi)
    scores = 1.0 / (1.0 + jnp.exp(-logits))
    sel = scores + rbias
    per_group = N_EXPERTS // N_EXPERT_GROUPS
    sel3 = sel.reshape(N_EXPERT_GROUPS, per_group, T)
    idx3 = lax.broadcasted_iota(jnp.int32, sel3.shape, 1).astype(F32)
    m1 = jnp.max(sel3, axis=1, keepdims=True)
    first = jnp.min(jnp.where(sel3 == m1, idx3, float(per_group)), axis=1, keepdims=True)
    m2 = jnp.max(jnp.where(idx3 == first, -jnp.inf, sel3), axis=1, keepdims=True)
    gscore = (m1 + m2).reshape(N_EXPERT_GROUPS, T)

    def rank_of(vals):
        idx = lax.broadcasted_iota(jnp.int32, vals.shape, 0)
        rank = jnp.zeros(vals.shape, F32)
        for r in range(vals.shape[0]):
            row = vals[r:r + 1, :]
            ge = jnp.where(row >= vals, 1.0, 0.0)
            gt = jnp.where(row > vals, 1.0, 0.0)
            rank = rank + jnp.where(idx > r, ge, gt)
        return rank

    gkeep = jnp.where(rank_of(gscore) < TOPK_GROUPS, 1.0, 0.0)
    ekeep = jnp.broadcast_to(gkeep.reshape(N_EXPERT_GROUPS, 1, T), sel3.shape).reshape(N_EXPERTS, T)
    masked = jnp.where(ekeep > 0.5, sel, NEG_INF)
    eidx = lax.broadcasted_iota(jnp.int32, masked.shape, 0).astype(F32)
    chosen = jnp.zeros(masked.shape, F32)
    for _ in range(TOP_K):
        best = jnp.max(masked, axis=0, keepdims=True)
        first = jnp.min(jnp.where(masked == best, eidx, float(N_EXPERTS)), axis=0, keepdims=True)
        hit = eidx == first
        chosen = jnp.where(hit, 1.0, chosen)
        masked = jnp.where(hit, -jnp.inf, masked)
    w = jnp.where(chosen > 0.5, scores, 0.0)
    return w / jnp.sum(w, axis=0, keepdims=True) * ROUTED_SCALE


def _merge_kernel(h_ref, o_ref, wg_ref, wb_ref, wo_ref, g_ref, b_ref, wrh_ref, wrl_ref, rb_ref,
                  h1_ref, gates_ref, *, alpha):
    h = h_ref[...]
    x = h.astype(BF16)
    merged = None
    for n in range(N_MIXERS):
        logit = _dot(x, wg_ref[:, n * D_MODEL:(n + 1) * D_MODEL])
        branch = _dot(o_ref[:, n * MIX_WIDTH:(n + 1) * MIX_WIDTH], wb_ref[n])
        term = branch / (1.0 + jnp.exp(-logit))
        merged = term if merged is None else merged + term
    mix = _dot(merged.astype(BF16), wo_ref[...])
    h1 = _layer_norm(alpha * h + mix, g_ref[...], b_ref[...])
    h1_ref[...] = h1
    gates_t = _route(h1, wrh_ref[...], wrl_ref[...], rb_ref[...])
    pad = jnp.zeros((LANES - N_EXPERTS, gates_t.shape[1]), F32)
    gates_ref[...] = jnp.concatenate([gates_t, pad], axis=0).T


def _merge(h, o_all, w_gate, w_branch, w_out, ln_g, ln_b, wr_hi, wr_lo, rbias, alpha, tm):
    R = h.shape[0]
    c2 = lambda i: (0, 0)
    once = pl.Buffered(1)
    return pl.pallas_call(
        functools.partial(_merge_kernel, alpha=alpha),
        grid=(R // tm,),
        in_specs=[pl.BlockSpec((tm, D_MODEL), lambda i: (i, 0)),
                  pl.BlockSpec((tm, N_MIXERS * MIX_WIDTH), lambda i: (i, 0)),
                  pl.BlockSpec(w_gate.shape, c2, pipeline_mode=once),
                  pl.BlockSpec(w_branch.shape, lambda i: (0, 0, 0), pipeline_mode=once),
                  pl.BlockSpec(w_out.shape, c2, pipeline_mode=once),
                  pl.BlockSpec((1, D_MODEL), c2),
                  pl.BlockSpec((1, D_MODEL), c2),
                  pl.BlockSpec(wr_hi.shape, c2),
                  pl.BlockSpec(wr_lo.shape, c2),
                  pl.BlockSpec(rbias.shape, c2)],
        out_specs=[pl.BlockSpec((tm, D_MODEL), lambda i: (i, 0)),
                   pl.BlockSpec((tm, LANES), lambda i: (i, 0))],
        out_shape=[jax.ShapeDtypeStruct((R, D_MODEL), F32),
                   jax.ShapeDtypeStruct((R, LANES), F32)],
        compiler_params=_cparams(("parallel",)),
        name="merge_ln_route",
    )(h, o_all, w_gate, w_branch, w_out, ln_g, ln_b, wr_hi, wr_lo, rbias)


TD = 256
CH = 16
SLOTS = 3072
NCH = SLOTS // CH
MT = 512
ME = 1024
CPM = ME // CH
XBUFS = 3


def _swiglu_act(gu):
    g = gu[:, :D_EXPERT]
    return g / (1.0 + jnp.exp(-g)) * gu[:, D_EXPERT:]


def _slot_of_token(gates, lo_row):
    routed = gates > 0.0
    r = lax.broadcasted_iota(jnp.int32, (TD, TD), 0)
    c = lax.broadcasted_iota(jnp.int32, (TD, TD), 1)
    earlier = jnp.where(c < r, 1.0, 0.0).astype(BF16)
    rank = _dot(earlier, jnp.where(routed, 1.0, 0.0).astype(BF16))
    return jnp.where(routed, lo_row + rank + 1.0, 0.0)


def _split64(x):
    hi = 64.0 * jnp.floor(x * (1.0 / 64.0))
    return jnp.concatenate([hi.astype(BF16), (x - hi).astype(BF16)], axis=0)


def _dispatch_kernel(h_ref, g_ref, lohi_ref, x_ref, w_ref):
    lohi = lohi_ref[0]
    lo_row, hi_row = lohi[0:1], lohi[1:2]
    gates = g_ref[...]
    slot_t = _split64(_slot_of_token(gates, lo_row).T)
    gates_t = gates.T.astype(BF16)
    x = h_ref[...].astype(BF16)
    for blk in range(SLOTS // MT):
        s = (lax.broadcasted_iota(jnp.int32, (MT, LANES), 0) + blk * MT).astype(F32)
        owner = jnp.where(s >= lo_row, jnp.where(s < hi_row, 1.0, 0.0), 0.0).astype(BF16)
        want = _dot(jnp.concatenate([owner, owner], axis=1), slot_t)
        s1 = (lax.broadcasted_iota(jnp.int32, (MT, TD), 0) + (blk * MT + 1)).astype(F32)
        hit = want == s1
        x_ref[blk * MT:(blk + 1) * MT, :] = _dot(jnp.where(hit, 1.0, 0.0).astype(BF16), x).astype(BF16)
        weight = jnp.where(hit, _dot(owner, gates_t), 0.0)
        w_ref[:, blk * MT:(blk + 1) * MT] = weight.T.astype(BF16)


def _dispatch(h1, gates, lohi):
    n = h1.shape[0] // TD
    return pl.pallas_call(
        _dispatch_kernel,
        grid=(n,),
        in_specs=[pl.BlockSpec((TD, D_MODEL), lambda i: (i, 0)),
                  pl.BlockSpec((TD, LANES), lambda i: (i, 0)),
                  pl.BlockSpec((1, 8, LANES), lambda i: (i, 0, 0))],
        out_specs=[pl.BlockSpec((SLOTS, D_MODEL), lambda i: (i, 0)),
                   pl.BlockSpec((TD, SLOTS), lambda i: (i, 0))],
        out_shape=[jax.ShapeDtypeStruct((n * SLOTS, D_MODEL), BF16),
                   jax.ShapeDtypeStruct((n * TD, SLOTS), BF16)],
        compiler_params=_cparams(("parallel",)),
        name="moe_dispatch",
    )(h1, gates, lohi)


def _chunk_gather(table_ref, first, n_chunks, src_hbm, buf, sem):
    copies = []
    for c in range(n_chunks):
        row = pl.multiple_of(table_ref[first + c] * CH, CH)
        copies.append(pltpu.make_async_copy(src_hbm.at[pl.ds(row, CH)], buf.at[pl.ds(c * CH, CH)], sem))
    return copies


def _chunk_wait(n_chunks, src_hbm, buf, sem):
    for c in range(n_chunks):
        pltpu.make_async_copy(src_hbm.at[pl.ds(0, CH)], buf.at[pl.ds(c * CH, CH)], sem).wait()


def _expert_kernel(te_ref, src_ref, nu_ref, x_hbm, wgu_ref, wd_ref, y_ref, xbuf, sem):
    del te_ref
    m = pl.program_id(0)
    n_used = nu_ref[0]

    def start(step):
        slot = step % XBUFS
        for cp in _chunk_gather(src_ref, step * CPM, CPM, x_hbm, xbuf.at[slot], sem.at[slot]):
            cp.start()

    for ahead in range(XBUFS - 1):
        @pl.when((m == 0) & (ahead < n_used))
        def _(ahead=ahead):
            start(ahead)

    @pl.when(m + (XBUFS - 1) < n_used)
    def _():
        start(m + (XBUFS - 1))

    @pl.when(m < n_used)
    def _():
        slot = m % XBUFS
        _chunk_wait(CPM, x_hbm, xbuf.at[slot], sem.at[slot])
        act = _swiglu_act(_dot(xbuf[slot], wgu_ref[0]))
        y_ref[...] = _dot(act.astype(BF16), wd_ref[0]).astype(BF16)


def _experts(x_disp, wgu, wd, tile_expert, src_chunk, n_used):
    n_steps = tile_expert.shape[0]
    return pl.pallas_call(
        _expert_kernel,
        grid_spec=pltpu.PrefetchScalarGridSpec(
            num_scalar_prefetch=3,
            grid=(n_steps,),
            in_specs=[pl.BlockSpec(memory_space=pl.ANY),
                      pl.BlockSpec((1, D_MODEL, 2 * D_EXPERT), lambda m, te, src, nu: (te[m], 0, 0)),
                      pl.BlockSpec((1, D_EXPERT, D_MODEL), lambda m, te, src, nu: (te[m], 0, 0))],
            out_specs=pl.BlockSpec((ME, D_MODEL), lambda m, te, src, nu: (jnp.minimum(m, nu[0] - 1), 0)),
            scratch_shapes=[pltpu.VMEM((XBUFS, ME, D_MODEL), BF16), pltpu.SemaphoreType.DMA((XBUFS,))]),
        out_shape=jax.ShapeDtypeStruct((n_steps * ME, D_MODEL), BF16),
        compiler_params=_cparams(("arbitrary",)),
        name="moe_experts",
    )(tile_expert, src_chunk, n_used, x_disp, wgu, wd)


def _combine_kernel(dst_ref, y_hbm, w_ref, h_ref, wsgu_ref, wsd_ref, lg_ref, lb_ref, o_ref, ybuf, sem, *, alpha):
    i = pl.program_id(0)

    def start(tile):
        slot = tile % 2
        for cp in _chunk_gather(dst_ref, tile * NCH, NCH, y_hbm, ybuf.at[slot], sem.at[slot]):
            cp.start()

    @pl.when(i == 0)
    def _():
        start(i)

    @pl.when(i + 1 < pl.num_programs(0))
    def _():
        start(i + 1)

    h = h_ref[...]
    shared = _dot(_swiglu_act(_dot(h.astype(BF16), wsgu_ref[...])).astype(BF16), wsd_ref[...])
    slot = i % 2
    _chunk_wait(NCH, y_hbm, ybuf.at[slot], sem.at[slot])
    routed = _dot(w_ref[...], ybuf[slot])
    o_ref[...] = _layer_norm(alpha * h + shared + routed, lg_ref[...], lb_ref[...])


def _combine(y_sorted, w_t, h1, dst_chunk, wsgu, wsd, ln_g, ln_b, alpha):
    n = h1.shape[0] // TD
    c2 = lambda i, dst: (0, 0)
    return pl.pallas_call(
        functools.partial(_combine_kernel, alpha=alpha),
        grid_spec=pltpu.PrefetchScalarGridSpec(
            num_scalar_prefetch=1,
            grid=(n,),
            in_specs=[pl.BlockSpec(memory_space=pl.ANY),
                      pl.BlockSpec((TD, SLOTS), lambda i, dst: (i, 0)),
                      pl.BlockSpec((TD, D_MODEL), lambda i, dst: (i, 0)),
                      pl.BlockSpec(wsgu.shape, c2),
                      pl.BlockSpec(wsd.shape, c2),
                      pl.BlockSpec((1, D_MODEL), c2),
                      pl.BlockSpec((1, D_MODEL), c2)],
            out_specs=pl.BlockSpec((TD, D_MODEL), lambda i, dst: (i, 0)),
            scratch_shapes=[pltpu.VMEM((2, SLOTS, D_MODEL), BF16), pltpu.SemaphoreType.DMA((2,))]),
        out_shape=jax.ShapeDtypeStruct((h1.shape[0], D_MODEL), F32),
        compiler_params=_cparams(("arbitrary",)),
        name="moe_combine_ln",
    )(dst_chunk, y_sorted, w_t, h1, wsgu, wsd, ln_g, ln_b)


def _routing_tables(gates):
    n = gates.shape[0] // TD
    cnt = jnp.sum((gates[:, :N_EXPERTS] > 0.0).reshape(n, TD, N_EXPERTS), axis=1, dtype=jnp.int32)
    nch = (cnt + (CH - 1)) // CH
    hi16 = jnp.cumsum(nch, axis=1)
    lo16 = hi16 - nch
    nct = hi16[:, -1:]
    pad = jnp.broadcast_to(nct, (n, LANES - N_EXPERTS))
    lohi = jnp.zeros((n, 8, LANES), F32)
    lohi = lohi.at[:, 0, :].set((jnp.concatenate([lo16, pad], axis=1) * CH).astype(F32))
    lohi = lohi.at[:, 1, :].set((jnp.concatenate([hi16, pad], axis=1) * CH).astype(F32))
    tot = jnp.sum(nch, axis=0)
    seg_len = (tot + (CPM - 1)) // CPM * CPM
    seg_end = jnp.cumsum(seg_len)
    seg_start = seg_end - seg_len
    gpos = seg_start[None, :] + jnp.cumsum(nch, axis=0) - nch
    assert (n * NCH) % CPM == 0
    n_steps = (n * NCH + N_EXPERTS * CPM) // CPM
    n_used = (seg_end[-1] // CPM).astype(jnp.int32).reshape(1)
    step = jnp.arange(n_steps, dtype=jnp.int32)
    tile_expert = jnp.sum(seg_end[None, :] // CPM <= jnp.minimum(step, n_used - 1)[:, None], axis=1, dtype=jnp.int32)
    tile_expert = jnp.minimum(tile_expert, N_EXPERTS - 1)
    exact = functools.partial(jnp.dot, precision=lax.Precision.HIGHEST)
    experts = jnp.arange(N_EXPERTS, dtype=jnp.int32)
    g = jnp.arange(n_steps * CPM, dtype=jnp.int32)
    e_of_g = jnp.minimum(jnp.sum(seg_end[None, :] <= g[:, None], axis=1, dtype=jnp.int32), N_EXPERTS - 1)
    pick_e = (e_of_g[:, None] == experts[None, :]).astype(F32)
    first = exact(pick_e, gpos.T.astype(F32))
    count = exact(pick_e, nch.T.astype(F32))
    base = exact(pick_e, (jnp.arange(n, dtype=jnp.int32)[:, None] * NCH + lo16).T.astype(F32))
    gf = g.astype(F32)[:, None]
    inside = (first <= gf) & (gf < first + count)
    src_chunk = jnp.sum(jnp.where(inside, base + gf - first, 0.0), axis=1).astype(jnp.int32)
    k = jnp.arange(NCH, dtype=jnp.int32)
    e_of_k = jnp.minimum(jnp.sum(hi16[:, None, :] <= k[None, :, None], axis=2, dtype=jnp.int32), N_EXPERTS - 1)
    pick_k = e_of_k[:, :, None] == experts[None, None, :]
    pos = jnp.sum(jnp.where(pick_k, (gpos - lo16)[:, None, :], 0), axis=2) + k[None, :]
    dst_chunk = jnp.where(k[None, :] < nct, pos, 0).astype(jnp.int32).reshape(-1)
    return lohi, tile_expert, src_chunk, n_used, dst_chunk


def _moe(h1, gates, wgu, wd, wsgu, wsd, ln_g, ln_b, alpha):
    lohi, tile_expert, src_chunk, n_used, dst_chunk = _routing_tables(gates)
    x_disp, w_t = _dispatch(h1, gates, lohi)
    y_sorted = _experts(x_disp, wgu, wd, tile_expert, src_chunk, n_used)
    return _combine(y_sorted, w_t, h1, dst_chunk, wsgu, wsd, ln_g, ln_b, alpha)


def _t5_bucket(rel):
    half = T5_BUCKETS // 2
    max_exact = half // 2
    n = np.abs(rel)
    ratio = np.log(np.maximum(n, 1).astype(np.float32) / np.float32(max_exact))
    ratio = ratio / np.float32(math.log(T5_MAX_DIST / max_exact)) * np.float32(half - max_exact)
    large = np.minimum(max_exact + ratio.astype(np.int32), half - 1)
    return np.where(rel > 0, half, 0) + np.where(n < max_exact, n, large)


def _t5_tables(t5_table, s_max):
    def bias(rel, valid):
        onehot = np.eye(T5_BUCKETS, dtype=np.float32)[_t5_bucket(rel)]
        b = jnp.einsum("qkb,bh->hqk", jnp.asarray(onehot), t5_table.astype(F32), precision=lax.Precision.HIGHEST)
        return jnp.where(jnp.asarray(valid)[None], b, NEG_INF)

    ii = np.arange(BLOCK)[:, None]
    jj = np.arange(3 * BLOCK)[None, :]
    rel = jj - ii - BLOCK
    bband = bias(rel, np.abs(rel) <= WINDOW)
    t = np.arange(s_max)[:, None]
    m = np.arange(N_META)[None, :]
    bmeta = bias(m - (N_META + t), np.ones((s_max, N_META), bool))
    mpos = np.arange(N_META)[:, None]
    kpos = np.arange(N_META + BLOCK)[None, :]
    relq = kpos - mpos
    bq = bias(relq, (kpos < N_META) | (np.abs(relq) <= WINDOW))
    return bband, bmeta, bq[:, :, :N_META], bq[:, :, N_META:]


def _na_bias_cases(rpb):
    W = GRID_W
    c = np.arange(W)[:, None]
    kc = np.arange(W)[None, :]
    cs = np.clip(c - NA_COLS // 2, 0, W - NA_COLS)
    valid = (kc >= cs) & (kc < cs + NA_COLS)
    dc = np.clip(kc - c + (NA_COLS - 1), 0, 2 * NA_COLS - 2)
    onehot = np.eye(2 * NA_COLS - 1, dtype=np.float32)[dc]
    t = jnp.einsum("hrd,ckd->hrck", rpb.astype(F32), jnp.asarray(onehot), precision=lax.Precision.HIGHEST)
    t = jnp.where(jnp.asarray(valid), t, NEG_INF)
    cases = [jnp.transpose(t[:, NA_ROWS - 1 - d:2 * NA_ROWS - 1 - d], (0, 2, 1, 3)) for d in range(NA_ROWS)]
    return jnp.stack(cases, axis=0).reshape(NA_ROWS, N_HEADS, W, NA_ROWS * W)


def _rope_tables(s_max, n_meta_rows):
    half = HEAD_DIM // 4
    freq = ROPE_THETA ** (-jnp.arange(half, dtype=F32) / half)
    t = np.arange(s_max)
    mp = np.tile(np.arange(N_META) - N_META, n_meta_rows // N_META)
    pos_row = jnp.asarray(np.concatenate([t // GRID_W, mp]), jnp.int32).astype(F32)
    pos_col = jnp.asarray(np.concatenate([t % GRID_W, mp]), jnp.int32).astype(F32)
    ar = pos_row[:, None] * freq
    ac = pos_col[:, None] * freq
    cos = jnp.concatenate([jnp.cos(ar), jnp.cos(ar), jnp.cos(ac), jnp.cos(ac)], axis=1)
    sin = jnp.concatenate([-jnp.sin(ar), jnp.sin(ar), -jnp.sin(ac), jnp.sin(ac)], axis=1)
    return jnp.tile(cos, (1, 2)), jnp.tile(sin, (1, 2))


def kernel(x_prompt, x_sample, meta_tokens, ln_in_g, ln_in_b, t5_table, w_in, q_gain, k_gain, sink,
           na_rpb, na_meta_bias, w_branch, w_out, ln1_g, ln1_b, w_router, router_bias,
           w_expert_gate_up, w_expert_down, w_shared_gate_up, w_shared_down, ln2_g, ln2_b):
    depth = w_in.shape[0]
    alpha = (2 * depth) ** 0.25
    B0, S0, D = x_prompt.shape
    B1, S1, _ = x_sample.shape
    assert D == D_MODEL
    real = B0 * S0 + B1 * S1
    n_meta_rows = -(-(B0 + B1) * N_META // TM) * TM
    R = real + n_meta_rows
    n_meta_blocks = n_meta_rows // N_META
    g0 = _Group(B0, S0, 0, 0, real, B0)
    g1 = _Group(B1, S1, B0 * S0, B0, real, n_meta_blocks - B0)
    s_max = max(S0, S1)

    h = _embed_ln(x_prompt.reshape(B0 * S0, D), x_sample.reshape(B1 * S1, D),
                  jnp.tile(meta_tokens, (TM // N_META, 1)), ln_in_g.reshape(1, D), ln_in_b.reshape(1, D), R)

    tm = 2 * TM if all(v % (2 * TM) == 0 for v in (S0, S1, n_meta_rows)) else TM
    cos_tab, sin_tab = _rope_tables(s_max, tm)
    n0, n1 = B0 * S0 // tm, real // tm
    p0, p1, pm = S0 // tm, S1 // tm, s_max // tm

    def pos_block(i):
        return jnp.where(i < n0, i % p0, jnp.where(i < n1, (i - n0) % p1, pm))

    bband, bmeta, bq_meta, bq_blk = (t * LOG2E for t in _t5_tables(t5_table, s_max))
    bmeta_wide = jnp.pad(bmeta, ((0, 0), (0, 0), (0, LANES - N_META)), constant_values=NEG_INF)
    ones_bd = jnp.asarray(np.kron(np.eye(N_HEADS), np.ones((HEAD_DIM, HEAD_DIM))), BF16)

    qs, ks, vs = [], [], []
    for n in range(N_MIXERS):
        off = n * QKV_WIDTH
        qs.append(w_in[:, :, off:off + MIX_WIDTH])
        ks.append(w_in[:, :, off + MIX_WIDTH:off + MIX_WIDTH + KV_WIDTH])
        vs.append(w_in[:, :, off + MIX_WIDTH + KV_WIDTH:off + QKV_WIDTH])
    wr_t = jnp.swapaxes(w_router, 1, 2)
    wr_hi = wr_t.astype(BF16)
    layers = dict(
        w_qkv=jnp.concatenate(qs + ks + vs, axis=2).astype(BF16),
        w_gate=w_in[:, :, N_MIXERS * QKV_WIDTH:].astype(BF16),
        q_gain=jnp.tile(q_gain, (1, N_HEADS)).reshape(depth, 1, MIX_WIDTH),
        k_gain=jnp.tile(k_gain, (1, N_KV_HEADS)).reshape(depth, 1, KV_WIDTH),
        sink=sink.astype(F32) * LOG2E,
        na_bias=jax.vmap(_na_bias_cases)(na_rpb) * LOG2E,
        na_mbias=na_meta_bias.astype(F32) * LOG2E,
        na_mbias_wide=jnp.pad(na_meta_bias.astype(F32) * LOG2E, ((0, 0), (0, 0), (0, LANES - N_META)),
                              constant_values=NEG_INF),
        w_branch=w_branch.astype(BF16),
        w_out=w_out.astype(BF16),
        ln1_g=ln1_g.reshape(depth, 1, D), ln1_b=ln1_b.reshape(depth, 1, D),
        wr_hi=wr_hi, wr_lo=(wr_t - wr_hi.astype(F32)).astype(BF16),
        rbias=router_bias.astype(F32).reshape(depth, N_EXPERTS, 1),
        wgu=w_expert_gate_up.astype(BF16), wd=w_expert_down.astype(BF16),
        wsgu=w_shared_gate_up.astype(BF16), wsd=w_shared_down.astype(BF16),
        ln2_g=ln2_g.reshape(depth, 1, D), ln2_b=ln2_b.reshape(depth, 1, D),
    )

    def layer(h, p):
        q_all, k_all, v_all, v_ones = _inproj(h, p["w_qkv"], cos_tab, sin_tab, p["q_gain"], p["k_gain"],
                                      ones_bd, pos_block, tm)
        o = None
        for grp in (g0, g1):
            o = _global_attn(grp, q_all, k_all, v_ones, o)
            o = _window_attn(grp, q_all, k_all, v_all, p["sink"], bband, bmeta_wide[:, :grp.S], o)
            o = _na_attn(grp, q_all, k_all, v_all, p["na_bias"], p["na_mbias_wide"], o)
        for grp in (g0, g1):
            o = _meta_attn(grp, q_all, k_all, v_all, p["sink"], bq_meta, bq_blk, p["na_mbias"], o)
        h1, gates = _merge(h, o, p["w_gate"], p["w_branch"], p["w_out"], p["ln1_g"], p["ln1_b"],
                           p["wr_hi"], p["wr_lo"], p["rbias"], alpha, tm)
        h2 = _moe(h1, gates, p["wgu"], p["wd"], p["wsgu"], p["wsd"], p["ln2_g"], p["ln2_b"],
                  alpha)
        return h2, None

    h, _ = lax.scan(layer, h, layers)
    y_prompt = h[:B0 * S0].reshape(B0, S0, D)
    y_sample = h[B0 * S0:real].reshape(B1, S1, D)
    return (y_prompt, y_sample)
```

```python
import functools
import math

import numpy as np
import jax
import jax.numpy as jnp
from jax import lax
from jax.experimental import pallas as pl
from jax.experimental.pallas import tpu as pltpu

F32 = jnp.float32
BF16 = jnp.bfloat16

D_MODEL = 1024
HEAD_DIM = 64
N_HEADS = 8
N_KV_HEADS = 2
GROUP = N_HEADS // N_KV_HEADS
MIX_WIDTH = N_HEADS * HEAD_DIM
KV_WIDTH = N_KV_HEADS * HEAD_DIM
N_MIXERS = 3
QKV_WIDTH = MIX_WIDTH + 2 * KV_WIDTH
N_META = 16
GRID_W = 64
BLOCK = 128
WINDOW = 128
NA_ROWS = 8
NA_COLS = 16
T5_BUCKETS = 32
T5_MAX_DIST = 128
ROPE_THETA = 10000.0
N_EXPERTS = 64
TOP_K = 8
N_EXPERT_GROUPS = 8
TOPK_GROUPS = 4
D_EXPERT = 256
ROUTED_SCALE = 2.5
NEG_INF = -1e30
LOG2E = math.log2(math.e)
LANES = 128

TM = 512
TQ_GLOBAL = 256
TK_GLOBAL = 512
NA_QROWS = 8
WIN_QBLOCKS = 4
VMEM_LIMIT = 56 * 1024 * 1024


def _cparams(sem):
    return pltpu.CompilerParams(dimension_semantics=sem, vmem_limit_bytes=VMEM_LIMIT)


def _dot(a, b):
    return jnp.dot(a, b, preferred_element_type=F32)


def _dot_nt(a, b):
    return lax.dot_general(a, b, (((1,), (1,)), ((), ())), preferred_element_type=F32)


def _split_bf16(x):
    hi = x.astype(BF16)
    lo = (x - hi.astype(F32)).astype(BF16)
    return hi, lo


def _layer_norm(x, g, b):
    mu = jnp.mean(x, axis=-1, keepdims=True)
    xc = x - mu
    var = jnp.mean(xc * xc, axis=-1, keepdims=True)
    return xc * lax.rsqrt(var + 1e-5) * g + b


def _embed_ln_kernel(x0_ref, x1_ref, xm_ref, g_ref, b_ref, o_ref, *, n0, n1):
    i = pl.program_id(0)
    for src, pred in ((x0_ref, i < n0), (x1_ref, (i >= n0) & (i < n0 + n1)), (xm_ref, i >= n0 + n1)):
        @pl.when(pred)
        def _(src=src):
            o_ref[...] = _layer_norm(src[...], g_ref[...], b_ref[...])


def _embed_ln(x0, x1, meta_tile, g, b, n_rows):
    n0, n1 = x0.shape[0] // TM, x1.shape[0] // TM
    return pl.pallas_call(
        functools.partial(_embed_ln_kernel, n0=n0, n1=n1),
        grid=(n_rows // TM,),
        in_specs=[pl.BlockSpec((TM, D_MODEL), lambda i: (jnp.minimum(i, n0 - 1), 0)),
                  pl.BlockSpec((TM, D_MODEL), lambda i: (jnp.clip(i - n0, 0, n1 - 1), 0)),
                  pl.BlockSpec((TM, D_MODEL), lambda i: (0, 0)),
                  pl.BlockSpec((1, D_MODEL), lambda i: (0, 0)),
                  pl.BlockSpec((1, D_MODEL), lambda i: (0, 0))],
        out_specs=pl.BlockSpec((TM, D_MODEL), lambda i: (i, 0)),
        out_shape=jax.ShapeDtypeStruct((n_rows, D_MODEL), F32),
        compiler_params=_cparams(("arbitrary",)),
        name="embed_ln",
    )(x0, x1, meta_tile, g, b)


def _rope_slot(x, cos, sin_signed, first_half):
    fwd = pltpu.roll(x, LANES - 16, 1)
    bwd = pltpu.roll(x, 16, 1)
    return x * cos + jnp.where(first_half, fwd, bwd) * sin_signed


def _head_rms(x, ones_bd, gain):
    hi, lo = _split_bf16(x * x)
    ss = _dot(hi, ones_bd) + _dot(lo, ones_bd)
    return x * lax.rsqrt(ss * (1.0 / HEAD_DIM) + 1e-6) * gain


def _inproj_kernel(h_ref, w_ref, cos_ref, sin_ref, qg_ref, kg_ref, ones_ref, q_ref, k_ref, v_ref, v1_ref):
    x = h_ref[...].astype(BF16)
    cos = cos_ref[...]
    sin = sin_ref[...]
    lane = lax.broadcasted_iota(jnp.int32, cos.shape, 1)
    first_half = (lane % 32) < 16
    scale = HEAD_DIM ** -0.5
    qw = N_MIXERS * MIX_WIDTH
    qa = _head_rms(_dot(x, w_ref[:, 0:MIX_WIDTH]), ones_ref[...], qg_ref[...])
    for s in range(MIX_WIDTH // LANES):
        sl = slice(s * LANES, (s + 1) * LANES)
        q_ref[:, sl] = (_rope_slot(qa[:, sl], cos, sin, first_half) * (scale * LOG2E)).astype(BF16)
    ka = _head_rms(_dot(x, w_ref[:, qw:qw + KV_WIDTH]), ones_ref[0:LANES, 0:LANES], kg_ref[...])
    k_ref[:, 0:KV_WIDTH] = _rope_slot(ka, cos, sin, first_half).astype(BF16)
    for n in range(1, N_MIXERS):
        q_ref[:, n * MIX_WIDTH:(n + 1) * MIX_WIDTH] = (
            _dot(x, w_ref[:, n * MIX_WIDTH:(n + 1) * MIX_WIDTH]) * (scale * LOG2E)).astype(BF16)
        k_ref[:, n * KV_WIDTH:(n + 1) * KV_WIDTH] = _dot(
            x, w_ref[:, qw + n * KV_WIDTH:qw + (n + 1) * KV_WIDTH]).astype(BF16)
    vw = qw + N_MIXERS * KV_WIDTH
    v = _dot(x, w_ref[:, vw:vw + N_MIXERS * KV_WIDTH])
    v_ref[...] = v.astype(BF16)
    va = v[:, 0:KV_WIDTH]
    lo = lane < HEAD_DIM
    v1_ref[:, 0:LANES] = jnp.where(lo, va, 1.0).astype(BF16)
    v1_ref[:, LANES:2 * LANES] = jnp.where(lo, pltpu.roll(va, HEAD_DIM, 1), 1.0).astype(BF16)


def _inproj(h, w_qkv, cos_tab, sin_tab, q_gain, k_gain, ones_bd, pos_block, tm):
    R = h.shape[0]
    const = lambda i: (0, 0)
    return pl.pallas_call(
        _inproj_kernel,
        grid=(R // tm,),
        in_specs=[pl.BlockSpec((tm, D_MODEL), lambda i: (i, 0)),
                  pl.BlockSpec(w_qkv.shape, const),
                  pl.BlockSpec((tm, LANES), lambda i: (pos_block(i), 0)),
                  pl.BlockSpec((tm, LANES), lambda i: (pos_block(i), 0)),
                  pl.BlockSpec((1, MIX_WIDTH), const),
                  pl.BlockSpec((1, KV_WIDTH), const),
                  pl.BlockSpec((MIX_WIDTH, MIX_WIDTH), const)],
        out_specs=[pl.BlockSpec((tm, N_MIXERS * MIX_WIDTH), lambda i: (i, 0)),
                   pl.BlockSpec((tm, N_MIXERS * KV_WIDTH), lambda i: (i, 0)),
                   pl.BlockSpec((tm, N_MIXERS * KV_WIDTH), lambda i: (i, 0)),
                   pl.BlockSpec((tm, N_KV_HEADS * LANES), lambda i: (i, 0))],
        out_shape=[jax.ShapeDtypeStruct((R, N_MIXERS * MIX_WIDTH), BF16),
                   jax.ShapeDtypeStruct((R, N_MIXERS * KV_WIDTH), BF16),
                   jax.ShapeDtypeStruct((R, N_MIXERS * KV_WIDTH), BF16),
                   jax.ShapeDtypeStruct((R, N_KV_HEADS * LANES), BF16)],
        compiler_params=_cparams(("parallel",)),
        name="inproj",
    )(h, w_qkv, cos_tab, sin_tab, q_gain, k_gain, ones_bd)


def _group_queries(q, j):
    lane = lax.broadcasted_iota(jnp.int32, (q.shape[0], LANES), 1)
    keep = (lane < HEAD_DIM) if j == 0 else (lane >= HEAD_DIM)
    parts = []
    for hh in range(GROUP):
        h = GROUP * j + hh
        slot = q[:, (h // 2) * LANES:(h // 2 + 1) * LANES]
        if h % 2 != j:
            slot = pltpu.roll(slot, HEAD_DIM, 1)
        parts.append(jnp.where(keep, slot, 0.0))
    return jnp.concatenate(parts, axis=0).astype(BF16)


def _ungroup_outputs(out, j, T):
    lane = lax.broadcasted_iota(jnp.int32, (T, LANES), 1)
    lo = lane < HEAD_DIM
    slots = []
    for s in range(2):
        even = out[(2 * s) * T:(2 * s + 1) * T]
        odd = out[(2 * s + 1) * T:(2 * s + 2) * T]
        if j == 0:
            slots.append(jnp.where(lo, even, pltpu.roll(odd, HEAD_DIM, 1)))
        else:
            slots.append(jnp.where(lo, pltpu.roll(even, HEAD_DIM, 1), odd))
    return jnp.concatenate(slots, axis=1)


def _fold_lanes(op, *parts):
    cols = [p[:, c * LANES:(c + 1) * LANES] for p in parts for c in range(p.shape[1] // LANES)]
    return functools.reduce(op, cols)


def _row_max(*parts):
    return jnp.max(_fold_lanes(jnp.maximum, *parts), axis=-1, keepdims=True)


def _row_sum(*parts):
    return jnp.sum(_fold_lanes(jnp.add, *parts), axis=-1, keepdims=True)


def _pad_meta_rows(x):
    return jnp.concatenate([x, jnp.zeros((LANES - N_META, x.shape[1]), x.dtype)], axis=0)


def _head_rows(vals, T):
    return jnp.concatenate([jnp.broadcast_to(v, (T, v.shape[-1])) for v in vals], axis=0)


def _global_kernel(q_ref, k_ref, v_ref, km_ref, vm_ref, *rest, S):
    o_ref = rest[-1]
    T = q_ref.shape[0]
    q = q_ref[...].astype(F32)
    km = km_ref[...]
    qs, state = [], []
    for j in range(N_KV_HEADS):
        qj = _group_queries(q, j)
        s_m = _dot_nt(qj, km)
        m0 = jnp.max(s_m, axis=-1, keepdims=True)
        p_m = jnp.exp2((s_m - m0).astype(BF16))
        qs.append(qj)
        state.append((m0, _dot(p_m, vm_ref[:, j * LANES:(j + 1) * LANES])))
    for c in range(S // TK_GLOBAL):
        kc = k_ref[c * TK_GLOBAL:(c + 1) * TK_GLOBAL, :]
        for j in range(N_KV_HEADS):
            m, acc = state[j]
            s = _dot_nt(qs[j], kc)
            m_new = jnp.maximum(m, jnp.max(s, axis=-1, keepdims=True))
            p = jnp.exp2((s - m_new).astype(BF16))
            vc = v_ref[c * TK_GLOBAL:(c + 1) * TK_GLOBAL, j * LANES:(j + 1) * LANES]
            state[j] = (m_new, jnp.exp2(m - m_new) * acc + _dot(p, vc))
    for j in range(N_KV_HEADS):
        acc = state[j][1]
        out = acc / pltpu.roll(acc, HEAD_DIM, 1)
        o_ref[:, j * 2 * LANES:(j + 1) * 2 * LANES] = _ungroup_outputs(out, 0, T).astype(BF16)


def _window_kernel(sink_ref, q_ref, kp_ref, kc_ref, kn_ref, vp_ref, vc_ref, vn_ref, km_ref, vm_ref,
                   bband_ref, bmeta_ref, *rest, nb):
    o_ref = rest[-1]
    i = pl.program_id(1)
    T = BLOCK
    k_span = jnp.concatenate([kp_ref[...], kc_ref[...], kn_ref[...]], axis=0)
    v_span = jnp.concatenate([vp_ref[...], vc_ref[...], vn_ref[...]], axis=0)
    col = lax.broadcasted_iota(jnp.int32, (1, 3 * BLOCK), 1)
    km = _pad_meta_rows(km_ref[...])
    vm = _pad_meta_rows(vm_ref[...])
    for u in range(WIN_QBLOCKS):
        q = q_ref[u * BLOCK:(u + 1) * BLOCK, :].astype(F32)
        kband = k_span[u * BLOCK:(u + 3) * BLOCK]
        vband = v_span[u * BLOCK:(u + 3) * BLOCK]
        in_range = None
        if u == 0:
            in_range = (col >= BLOCK) | (i > 0)
        if u == WIN_QBLOCKS - 1:
            after = (col < 2 * BLOCK) | (i < nb - 1)
            in_range = after if in_range is None else in_range & after
        for j in range(N_KV_HEADS):
            qj = _group_queries(q, j)
            hs = slice(GROUP * j, GROUP * (j + 1))
            s_b = _dot_nt(qj, kband) + bband_ref[hs].reshape(GROUP * T, 3 * BLOCK)
            if in_range is not None:
                s_b = jnp.where(in_range, s_b, NEG_INF)
            s_m = _dot_nt(qj, km) + bmeta_ref[hs, u * BLOCK:(u + 1) * BLOCK].reshape(GROUP * T, LANES)
            sink = jnp.concatenate([jnp.full((T, 1), sink_ref[GROUP * j + hh], F32) for hh in range(GROUP)], axis=0)
            m = jnp.maximum(_row_max(s_b, s_m), sink)
            e_b = jnp.exp2(s_b - m)
            e_m = jnp.exp2(s_m - m)
            denom = _row_sum(e_b, e_m) + jnp.exp2(sink - m)
            acc = _dot(e_b.astype(BF16), vband) + _dot(e_m.astype(BF16), vm)
            o_ref[u * BLOCK:(u + 1) * BLOCK, j * 2 * LANES:(j + 1) * 2 * LANES] = (
                _ungroup_outputs(acc / denom, j, T).astype(BF16))


def _na_kernel(q_ref, k_ref, v_ref, km_ref, vm_ref, bias_ref, mbias_ref, *rest, rows):
    o_ref = rest[-1]
    blk = pl.program_id(1)
    W = GRID_W
    nkeys = NA_ROWS * W
    km = _pad_meta_rows(km_ref[...])
    vm = _pad_meta_rows(vm_ref[...])
    mb = _head_rows([mbias_ref[h:h + 1, :] for h in range(N_HEADS)], W)
    half = GROUP * W
    for rr in range(NA_QROWS):
        r = blk * NA_QROWS + rr
        rs = jnp.clip(r - NA_ROWS // 2, 0, rows - NA_ROWS)
        delta = r - rs
        koff = pl.multiple_of(rs * W, W)
        q = q_ref[rr * W:(rr + 1) * W, :].astype(F32)
        q8 = jnp.concatenate([_group_queries(q, j) for j in range(N_KV_HEADS)], axis=0)
        kw = k_ref[pl.ds(koff, nkeys), :]
        vw = v_ref[pl.ds(koff, nkeys), :]
        s_w = _dot_nt(q8, kw) + bias_ref[delta].reshape(N_HEADS * W, nkeys)
        s_m = _dot_nt(q8, km) + mb
        m = _row_max(s_w, s_m)
        e_w = jnp.exp2(s_w - m)
        e_m = jnp.exp2(s_m - m)
        out = (_dot(e_w.astype(BF16), vw) + _dot(e_m.astype(BF16), vm)) / _row_sum(e_w, e_m)
        for j in range(N_KV_HEADS):
            o_ref[rr * W:(rr + 1) * W, j * 2 * LANES:(j + 1) * 2 * LANES] = (
                _ungroup_outputs(out[j * half:(j + 1) * half], j, W).astype(BF16))


def _meta_kernel(sink_ref, q_ref, ka_ref, va_ref, kb_ref, vb_ref, km_ref, vm_ref,
                 bq_meta_ref, bq_blk_ref, mbias_ref, o_in_ref, o_ref):
    del o_in_ref
    T = N_META
    q = q_ref[...].astype(F32)
    km_all = km_ref[...]
    vm_all = vm_ref[...]

    def finish(n, j, acc, denom):
        lo = n * MIX_WIDTH + j * 2 * LANES
        o_ref[:, lo:lo + 2 * LANES] = _ungroup_outputs(acc / denom, j, T).astype(BF16)

    for j in range(N_KV_HEADS):
        hs = slice(GROUP * j, GROUP * (j + 1))
        qj = _group_queries(q[:, 0:MIX_WIDTH], j)
        km, vm = km_all[:, 0:KV_WIDTH], vm_all[:, 0:KV_WIDTH]
        s_r = _dot_nt(qj, ka_ref[...])
        s_m = _dot_nt(qj, km)
        m = jnp.maximum(jnp.max(s_r, axis=-1, keepdims=True), jnp.max(s_m, axis=-1, keepdims=True))
        e_r = jnp.exp2(s_r - m)
        e_m = jnp.exp2(s_m - m)
        denom = jnp.sum(e_r, axis=-1, keepdims=True) + jnp.sum(e_m, axis=-1, keepdims=True)
        finish(0, j, _dot(e_r.astype(BF16), va_ref[...]) + _dot(e_m.astype(BF16), vm), denom)
        qj = _group_queries(q[:, MIX_WIDTH:2 * MIX_WIDTH], j)
        km, vm = km_all[:, KV_WIDTH:2 * KV_WIDTH], vm_all[:, KV_WIDTH:2 * KV_WIDTH]
        s_r = _dot_nt(qj, kb_ref[...]) + bq_blk_ref[hs].reshape(GROUP * T, BLOCK)
        s_m = _dot_nt(qj, km) + bq_meta_ref[hs].reshape(GROUP * T, N_META)
        sink = jnp.concatenate([jnp.full((T, 1), sink_ref[GROUP * j + hh], F32) for hh in range(GROUP)], axis=0)
        m = jnp.maximum(jnp.maximum(jnp.max(s_r, axis=-1, keepdims=True),
                                    jnp.max(s_m, axis=-1, keepdims=True)), sink)
        e_r = jnp.exp2(s_r - m)
        e_m = jnp.exp2(s_m - m)
        denom = (jnp.sum(e_r, axis=-1, keepdims=True) + jnp.sum(e_m, axis=-1, keepdims=True)
                 + jnp.exp2(sink - m))
        finish(1, j, _dot(e_r.astype(BF16), vb_ref[...]) + _dot(e_m.astype(BF16), vm), denom)
        qj = _group_queries(q[:, 2 * MIX_WIDTH:3 * MIX_WIDTH], j)
        km, vm = km_all[:, 2 * KV_WIDTH:3 * KV_WIDTH], vm_all[:, 2 * KV_WIDTH:3 * KV_WIDTH]
        mb = _head_rows([mbias_ref[GROUP * j + hh:GROUP * j + hh + 1, :] for hh in range(GROUP)], T)
        s_m = _dot_nt(qj, km) + mb
        m = jnp.max(s_m, axis=-1, keepdims=True)
        e_m = jnp.exp2(s_m - m)
        finish(2, j, _dot(e_m.astype(BF16), vm), jnp.sum(e_m, axis=-1, keepdims=True))


class _Group:
    def __init__(self, B, S, real_base, meta_batch_base, meta_base, n_meta_blocks):
        self.B, self.S = B, S
        self.real_base = real_base
        self.meta_blk0 = meta_base // N_META + meta_batch_base
        self.n_meta_blocks = n_meta_blocks
        assert real_base % S == 0 and S % TM == 0 and meta_base % N_META == 0


def _alias_args(o_prev, n_inputs):
    if o_prev is None:
        return [], [], {}
    return [o_prev], [pl.BlockSpec(memory_space=pl.ANY)], {n_inputs: 0}


def _global_attn(grp, q_all, k_all, v_ones, o_prev):
    B, S = grp.B, grp.S
    nq = S // TQ_GLOBAL
    qb0 = grp.real_base // TQ_GLOBAL
    sb0 = grp.real_base // S
    mb0 = grp.meta_blk0
    in_specs = [pl.BlockSpec((TQ_GLOBAL, MIX_WIDTH), lambda b, i: (qb0 + b * nq + i, 0)),
                pl.BlockSpec((S, KV_WIDTH), lambda b, i: (sb0 + b, 0)),
                pl.BlockSpec((S, N_KV_HEADS * LANES), lambda b, i: (sb0 + b, 0)),
                pl.BlockSpec((N_META, KV_WIDTH), lambda b, i: (mb0 + b, 0)),
                pl.BlockSpec((N_META, N_KV_HEADS * LANES), lambda b, i: (mb0 + b, 0))]
    extra, extra_specs, aliases = _alias_args(o_prev, len(in_specs))
    return pl.pallas_call(
        functools.partial(_global_kernel, S=S),
        grid=(B, nq),
        in_specs=in_specs + extra_specs,
        out_specs=pl.BlockSpec((TQ_GLOBAL, MIX_WIDTH), lambda b, i: (qb0 + b * nq + i, 0)),
        out_shape=jax.ShapeDtypeStruct((q_all.shape[0], N_MIXERS * MIX_WIDTH), BF16),
        input_output_aliases=aliases,
        compiler_params=_cparams(("parallel", "arbitrary")),
        name="mixer_global",
    )(q_all, k_all, v_ones, k_all, v_ones, *extra)


def _window_attn(grp, q_all, k_all, v_all, sink, bband, bmeta, o_prev):
    B, S = grp.B, grp.S
    span = WIN_QBLOCKS * BLOCK
    nb, ns = S // BLOCK, S // span
    qb0 = grp.real_base // BLOCK
    sp0 = grp.real_base // span
    mb0 = grp.meta_blk0
    cur = lambda b, i, sink: (sp0 + b * ns + i, 1)
    prv = lambda b, i, sink: (qb0 + b * nb + jnp.maximum(i * WIN_QBLOCKS - 1, 0), 1)
    nxt = lambda b, i, sink: (qb0 + b * nb + jnp.minimum((i + 1) * WIN_QBLOCKS, nb - 1), 1)
    met = lambda b, i, sink: (mb0 + b, 1)
    edge = lambda im: pl.BlockSpec((BLOCK, KV_WIDTH), im)
    mid = pl.BlockSpec((span, KV_WIDTH), cur)
    in_specs = [pl.BlockSpec((span, MIX_WIDTH), cur),
                edge(prv), mid, edge(nxt), edge(prv), mid, edge(nxt),
                pl.BlockSpec((N_META, KV_WIDTH), met), pl.BlockSpec((N_META, KV_WIDTH), met),
                pl.BlockSpec((N_HEADS, BLOCK, 3 * BLOCK), lambda b, i, sink: (0, 0, 0)),
                pl.BlockSpec((N_HEADS, span, LANES), lambda b, i, sink: (0, i, 0))]
    extra, extra_specs, aliases = _alias_args(o_prev, len(in_specs) + 1)
    return pl.pallas_call(
        functools.partial(_window_kernel, nb=ns),
        grid_spec=pltpu.PrefetchScalarGridSpec(
            num_scalar_prefetch=1,
            grid=(B, ns),
            in_specs=in_specs + extra_specs,
            out_specs=pl.BlockSpec((span, MIX_WIDTH), cur)),
        out_shape=jax.ShapeDtypeStruct((q_all.shape[0], N_MIXERS * MIX_WIDTH), BF16),
        input_output_aliases=aliases,
        compiler_params=_cparams(("parallel", "arbitrary")),
        name="mixer_window",
    )(sink, q_all, k_all, k_all, k_all, v_all, v_all, v_all, k_all, v_all, bband, bmeta, *extra)


def _na_attn(grp, q_all, k_all, v_all, na_bias, na_mbias, o_prev):
    B, S = grp.B, grp.S
    rows = S // GRID_W
    tq = NA_QROWS * GRID_W
    nq = S // tq
    qb0 = grp.real_base // tq
    sb0 = grp.real_base // S
    mb0 = grp.meta_blk0
    in_specs = [pl.BlockSpec((tq, MIX_WIDTH), lambda b, i: (qb0 + b * nq + i, 2)),
                pl.BlockSpec((S, KV_WIDTH), lambda b, i: (sb0 + b, 2)),
                pl.BlockSpec((S, KV_WIDTH), lambda b, i: (sb0 + b, 2)),
                pl.BlockSpec((N_META, KV_WIDTH), lambda b, i: (mb0 + b, 2)),
                pl.BlockSpec((N_META, KV_WIDTH), lambda b, i: (mb0 + b, 2)),
                pl.BlockSpec(na_bias.shape, lambda b, i: (0, 0, 0, 0)),
                pl.BlockSpec(na_mbias.shape, lambda b, i: (0, 0))]
    extra, extra_specs, aliases = _alias_args(o_prev, len(in_specs))
    return pl.pallas_call(
        functools.partial(_na_kernel, rows=rows),
        grid=(B, nq),
        in_specs=in_specs + extra_specs,
        out_specs=pl.BlockSpec((tq, MIX_WIDTH), lambda b, i: (qb0 + b * nq + i, 2)),
        out_shape=jax.ShapeDtypeStruct((q_all.shape[0], N_MIXERS * MIX_WIDTH), BF16),
        input_output_aliases=aliases,
        compiler_params=_cparams(("parallel", "arbitrary")),
        name="mixer_neighbourhood",
    )(q_all, k_all, v_all, k_all, v_all, na_bias, na_mbias, *extra)


def _meta_attn(grp, q_all, k_all, v_all, sink, bq_meta, bq_blk, na_mbias, o_prev):
    B, S = grp.B, grp.S
    sb0 = grp.real_base // S
    bb0 = grp.real_base // BLOCK
    nb = S // BLOCK
    mb0 = grp.meta_blk0
    clamp = lambda b: jnp.minimum(b, B - 1)
    mrow = lambda b, sink: (mb0 + clamp(b), 0)
    in_specs = [pl.BlockSpec((N_META, N_MIXERS * MIX_WIDTH), mrow),
                pl.BlockSpec((S, KV_WIDTH), lambda b, sink: (sb0 + clamp(b), 0)),
                pl.BlockSpec((S, KV_WIDTH), lambda b, sink: (sb0 + clamp(b), 0)),
                pl.BlockSpec((BLOCK, KV_WIDTH), lambda b, sink: (bb0 + clamp(b) * nb, 1)),
                pl.BlockSpec((BLOCK, KV_WIDTH), lambda b, sink: (bb0 + clamp(b) * nb, 1)),
                pl.BlockSpec((N_META, N_MIXERS * KV_WIDTH), mrow),
                pl.BlockSpec((N_META, N_MIXERS * KV_WIDTH), mrow),
                pl.BlockSpec(bq_meta.shape, lambda b, sink: (0, 0, 0)),
                pl.BlockSpec(bq_blk.shape, lambda b, sink: (0, 0, 0)),
                pl.BlockSpec(na_mbias.shape, lambda b, sink: (0, 0)),
                pl.BlockSpec(memory_space=pl.ANY)]
    return pl.pallas_call(
        _meta_kernel,
        grid_spec=pltpu.PrefetchScalarGridSpec(
            num_scalar_prefetch=1,
            grid=(grp.n_meta_blocks,),
            in_specs=in_specs,
            out_specs=pl.BlockSpec((N_META, N_MIXERS * MIX_WIDTH), lambda b, sink: (mb0 + b, 0))),
        out_shape=jax.ShapeDtypeStruct((q_all.shape[0], N_MIXERS * MIX_WIDTH), BF16),
        input_output_aliases={len(in_specs): 0},
        compiler_params=_cparams(("arbitrary",)),
        name="mixer_meta_queries",
    )(sink, q_all, k_all, v_all, k_all, v_all, k_all, v_all, bq_meta, bq_blk, na_mbias, o_prev)


def _route(h1, wr_hi, wr_lo, rbias):
    T = h1.shape[0]
    x_hi, x_lo = _split_bf16(h1)
    logits = _dot_nt(wr_hi, x_hi) + _dot_nt(wr_hi, x_lo) + _dot_nt(wr_lo, x_hi)
    scores = 1.0 / (1.0 + jnp.exp(-logits))
    sel = scores + rbias
    per_group = N_EXPERTS // N_EXPERT_GROUPS
    sel3 = sel.reshape(N_EXPERT_GROUPS, per_group, T)
    idx3 = lax.broadcasted_iota(jnp.int32, sel3.shape, 1).astype(F32)
    m1 = jnp.max(sel3, axis=1, keepdims=True)
    first = jnp.min(jnp.where(sel3 == m1, idx3, float(per_group)), axis=1, keepdims=True)
    m2 = jnp.max(jnp.where(idx3 == first, -jnp.inf, sel3), axis=1, keepdims=True)
    gscore = (m1 + m2).reshape(N_EXPERT_GROUPS, T)

    def rank_of(vals):
        idx = lax.broadcasted_iota(jnp.int32, vals.shape, 0)
        rank = jnp.zeros(vals.shape, F32)
        for r in range(vals.shape[0]):
            row = vals[r:r + 1, :]
            ge = jnp.where(row >= vals, 1.0, 0.0)
            gt = jnp.where(row > vals, 1.0, 0.0)
            rank = rank + jnp.where(idx > r, ge, gt)
        return rank

    gkeep = jnp.where(rank_of(gscore) < TOPK_GROUPS, 1.0, 0.0)
    ekeep = jnp.broadcast_to(gkeep.reshape(N_EXPERT_GROUPS, 1, T), sel3.shape).reshape(N_EXPERTS, T)
    masked = jnp.where(ekeep > 0.5, sel, NEG_INF)
    eidx = lax.broadcasted_iota(jnp.int32, masked.shape, 0).astype(F32)
    chosen = jnp.zeros(masked.shape, F32)
    for _ in range(TOP_K):
        best = jnp.max(masked, axis=0, keepdims=True)
        first = jnp.min(jnp.where(masked == best, eidx, float(N_EXPERTS)), axis=0, keepdims=True)
        hit = eidx == first
        chosen = jnp.where(hit, 1.0, chosen)
        masked = jnp.where(hit, -jnp.inf, masked)
    w = jnp.where(chosen > 0.5, scores, 0.0)
    return w / jnp.sum(w, axis=0, keepdims=True) * ROUTED_SCALE


def _merge_kernel(h_ref, o_ref, wg_ref, wb_ref, wo_ref, g_ref, b_ref, wrh_ref, wrl_ref, rb_ref,
                  h1_ref, gates_ref, *, alpha):
    h = h_ref[...]
    x = h.astype(BF16)
    merged = None
    for n in range(N_MIXERS):
        logit = _dot(x, wg_ref[:, n * D_MODEL:(n + 1) * D_MODEL])
        branch = _dot(o_ref[:, n * MIX_WIDTH:(n + 1) * MIX_WIDTH], wb_ref[n])
        term = branch / (1.0 + jnp.exp(-logit))
        merged = term if merged is None else merged + term
    mix = _dot(merged.astype(BF16), wo_ref[...])
    h1 = _layer_norm(alpha * h + mix, g_ref[...], b_ref[...])
    h1_ref[...] = h1
    gates_t = _route(h1, wrh_ref[...], wrl_ref[...], rb_ref[...])
    pad = jnp.zeros((LANES - N_EXPERTS, gates_t.shape[1]), F32)
    gates_ref[...] = jnp.concatenate([gates_t, pad], axis=0).T


def _merge(h, o_all, w_gate, w_branch, w_out, ln_g, ln_b, wr_hi, wr_lo, rbias, alpha, tm):
    R = h.shape[0]
    c2 = lambda i: (0, 0)
    once = pl.Buffered(1)
    return pl.pallas_call(
        functools.partial(_merge_kernel, alpha=alpha),
        grid=(R // tm,),
        in_specs=[pl.BlockSpec((tm, D_MODEL), lambda i: (i, 0)),
                  pl.BlockSpec((tm, N_MIXERS * MIX_WIDTH), lambda i: (i, 0)),
                  pl.BlockSpec(w_gate.shape, c2, pipeline_mode=once),
                  pl.BlockSpec(w_branch.shape, lambda i: (0, 0, 0), pipeline_mode=once),
                  pl.BlockSpec(w_out.shape, c2, pipeline_mode=once),
                  pl.BlockSpec((1, D_MODEL), c2),
                  pl.BlockSpec((1, D_MODEL), c2),
                  pl.BlockSpec(wr_hi.shape, c2),
                  pl.BlockSpec(wr_lo.shape, c2),
                  pl.BlockSpec(rbias.shape, c2)],
        out_specs=[pl.BlockSpec((tm, D_MODEL), lambda i: (i, 0)),
                   pl.BlockSpec((tm, LANES), lambda i: (i, 0))],
        out_shape=[jax.ShapeDtypeStruct((R, D_MODEL), F32),
                   jax.ShapeDtypeStruct((R, LANES), F32)],
        compiler_params=_cparams(("parallel",)),
        name="merge_ln_route",
    )(h, o_all, w_gate, w_branch, w_out, ln_g, ln_b, wr_hi, wr_lo, rbias)


TD = 256
CH = 16
SLOTS = 3072
NCH = SLOTS // CH
MT = 512
ME = 1024
CPM = ME // CH
XBUFS = 3
CBUFS = 3
CORE_CHUNKS = TOP_K * TD // CH
TAIL_GROUP = 8


def _swiglu_act(gu):
    g = gu[:, :D_EXPERT]
    return g / (1.0 + jnp.exp(-g)) * gu[:, D_EXPERT:]


def _slot_of_token(gates, lo_row):
    routed = gates > 0.0
    r = lax.broadcasted_iota(jnp.int32, (TD, TD), 0)
    c = lax.broadcasted_iota(jnp.int32, (TD, TD), 1)
    earlier = jnp.where(c < r, 1.0, 0.0).astype(BF16)
    rank = _dot(earlier, jnp.where(routed, 1.0, 0.0).astype(BF16))
    return jnp.where(routed, lo_row + rank + 1.0, 0.0)


def _split64(x):
    hi = 64.0 * jnp.floor(x * (1.0 / 64.0))
    return jnp.concatenate([hi.astype(BF16), (x - hi).astype(BF16)], axis=0)


def _dispatch_kernel(h_ref, g_ref, lohi_ref, x_ref, w_ref):
    lohi = lohi_ref[0]
    lo_row, hi_row = lohi[0:1], lohi[1:2]
    gates = g_ref[...]
    slot_t = _split64(_slot_of_token(gates, lo_row).T)
    gates_t = gates.T.astype(BF16)
    x = h_ref[...].astype(BF16)
    for blk in range(SLOTS // MT):
        s = (lax.broadcasted_iota(jnp.int32, (MT, LANES), 0) + blk * MT).astype(F32)
        owner = jnp.where(s >= lo_row, jnp.where(s < hi_row, 1.0, 0.0), 0.0).astype(BF16)
        want = _dot(jnp.concatenate([owner, owner], axis=1), slot_t)
        s1 = (lax.broadcasted_iota(jnp.int32, (MT, TD), 0) + (blk * MT + 1)).astype(F32)
        hit = want == s1
        x_ref[blk * MT:(blk + 1) * MT, :] = _dot(jnp.where(hit, 1.0, 0.0).astype(BF16), x).astype(BF16)
        weight = jnp.where(hit, _dot(owner, gates_t), 0.0)
        w_ref[:, blk * MT:(blk + 1) * MT] = weight.T.astype(BF16)


def _dispatch(h1, gates, lohi):
    n = h1.shape[0] // TD
    return pl.pallas_call(
        _dispatch_kernel,
        grid=(n,),
        in_specs=[pl.BlockSpec((TD, D_MODEL), lambda i: (i, 0)),
                  pl.BlockSpec((TD, LANES), lambda i: (i, 0)),
                  pl.BlockSpec((1, 8, LANES), lambda i: (i, 0, 0))],
        out_specs=[pl.BlockSpec((SLOTS, D_MODEL), lambda i: (i, 0)),
                   pl.BlockSpec((TD, SLOTS), lambda i: (i, 0))],
        out_shape=[jax.ShapeDtypeStruct((n * SLOTS, D_MODEL), BF16),
                   jax.ShapeDtypeStruct((n * TD, SLOTS), BF16)],
        compiler_params=_cparams(("parallel",)),
        name="moe_dispatch",
    )(h1, gates, lohi)


def _chunk_gather(table_ref, first, n_chunks, src_hbm, buf, sem):
    copies = []
    for c in range(n_chunks):
        row = pl.multiple_of(table_ref[first + c] * CH, CH)
        copies.append(pltpu.make_async_copy(src_hbm.at[pl.ds(row, CH)], buf.at[pl.ds(c * CH, CH)], sem))
    return copies


def _chunk_wait(n_chunks, src_hbm, buf, sem):
    for c in range(n_chunks):
        pltpu.make_async_copy(src_hbm.at[pl.ds(0, CH)], buf.at[pl.ds(c * CH, CH)], sem).wait()


def _expert_kernel(te_ref, src_ref, nu_ref, x_hbm, wgu_ref, wd_ref, y_ref, xbuf, sem):
    del te_ref
    m = pl.program_id(0)
    n_used = nu_ref[0]

    def start(step):
        slot = step % XBUFS
        for cp in _chunk_gather(src_ref, step * CPM, CPM, x_hbm, xbuf.at[slot], sem.at[slot]):
            cp.start()

    for ahead in range(XBUFS - 1):
        @pl.when((m == 0) & (ahead < n_used))
        def _(ahead=ahead):
            start(ahead)

    @pl.when(m + (XBUFS - 1) < n_used)
    def _():
        start(m + (XBUFS - 1))

    @pl.when(m < n_used)
    def _():
        slot = m % XBUFS
        _chunk_wait(CPM, x_hbm, xbuf.at[slot], sem.at[slot])
        act = _swiglu_act(_dot(xbuf[slot], wgu_ref[0]))
        y_ref[...] = _dot(act.astype(BF16), wd_ref[0]).astype(BF16)


def _experts(x_disp, wgu, wd, tile_expert, src_chunk, n_used):
    n_steps = tile_expert.shape[0]
    return pl.pallas_call(
        _expert_kernel,
        grid_spec=pltpu.PrefetchScalarGridSpec(
            num_scalar_prefetch=3,
            grid=(n_steps,),
            in_specs=[pl.BlockSpec(memory_space=pl.ANY),
                      pl.BlockSpec((1, D_MODEL, 2 * D_EXPERT), lambda m, te, src, nu: (te[m], 0, 0)),
                      pl.BlockSpec((1, D_EXPERT, D_MODEL), lambda m, te, src, nu: (te[m], 0, 0))],
            out_specs=pl.BlockSpec((ME, D_MODEL), lambda m, te, src, nu: (jnp.minimum(m, nu[0] - 1), 0)),
            scratch_shapes=[pltpu.VMEM((XBUFS, ME, D_MODEL), BF16), pltpu.SemaphoreType.DMA((XBUFS,))]),
        out_shape=jax.ShapeDtypeStruct((n_steps * ME, D_MODEL), BF16),
        compiler_params=_cparams(("arbitrary",)),
        name="moe_experts",
    )(tile_expert, src_chunk, n_used, x_disp, wgu, wd)


def _combine_kernel(dst_ref, nct_ref, y_hbm, w_ref, h_ref, wsgu_ref, wsd_ref, lg_ref, lb_ref, o_ref, ybuf, sem,
                    *, alpha):
    i = pl.program_id(0)
    n_tiles = pl.num_programs(0)

    def transfer(tile, wait):
        slot = tile % CBUFS
        buf, s = ybuf.at[slot], sem.at[slot]
        used = nct_ref[tile]

        def run(first, count):
            if wait:
                _chunk_wait(count, y_hbm, buf.at[pl.ds(first * CH, count * CH)], s)
            else:
                for cp in _chunk_gather(dst_ref, tile * NCH + first, count, y_hbm,
                                        buf.at[pl.ds(first * CH, count * CH)], s):
                    cp.start()

        run(0, CORE_CHUNKS)
        for first in range(CORE_CHUNKS, NCH, TAIL_GROUP):
            pl.when(used > first)(functools.partial(run, first, TAIL_GROUP))

    @pl.when(i == 0)
    def _():
        for slot in range(CBUFS):
            ybuf[slot] = jnp.zeros((SLOTS, D_MODEL), BF16)
        for ahead in range(CBUFS - 1):
            pl.when(ahead < n_tiles)(functools.partial(transfer, ahead, False))

    @pl.when(i + (CBUFS - 1) < n_tiles)
    def _():
        transfer(i + (CBUFS - 1), False)

    h = h_ref[...]
    shared = _dot(_swiglu_act(_dot(h.astype(BF16), wsgu_ref[...])).astype(BF16), wsd_ref[...])
    transfer(i, True)
    routed = _dot(w_ref[...], ybuf[i % CBUFS])
    o_ref[...] = _layer_norm(alpha * h + shared + routed, lg_ref[...], lb_ref[...])


def _combine(y_sorted, w_t, h1, dst_chunk, tile_chunks, wsgu, wsd, ln_g, ln_b, alpha):
    n = h1.shape[0] // TD
    c2 = lambda i, dst, nct: (0, 0)
    return pl.pallas_call(
        functools.partial(_combine_kernel, alpha=alpha),
        grid_spec=pltpu.PrefetchScalarGridSpec(
            num_scalar_prefetch=2,
            grid=(n,),
            in_specs=[pl.BlockSpec(memory_space=pl.ANY),
                      pl.BlockSpec((TD, SLOTS), lambda i, dst, nct: (i, 0)),
                      pl.BlockSpec((TD, D_MODEL), lambda i, dst, nct: (i, 0)),
                      pl.BlockSpec(wsgu.shape, c2),
                      pl.BlockSpec(wsd.shape, c2),
                      pl.BlockSpec((1, D_MODEL), c2),
                      pl.BlockSpec((1, D_MODEL), c2)],
            out_specs=pl.BlockSpec((TD, D_MODEL), lambda i, dst, nct: (i, 0)),
            scratch_shapes=[pltpu.VMEM((CBUFS, SLOTS, D_MODEL), BF16), pltpu.SemaphoreType.DMA((CBUFS,))]),
        out_shape=jax.ShapeDtypeStruct((h1.shape[0], D_MODEL), F32),
        compiler_params=_cparams(("arbitrary",)),
        name="moe_combine_ln",
    )(dst_chunk, tile_chunks, y_sorted, w_t, h1, wsgu, wsd, ln_g, ln_b)


def _routing_tables(gates):
    n = gates.shape[0] // TD
    cnt = jnp.sum((gates[:, :N_EXPERTS] > 0.0).reshape(n, TD, N_EXPERTS), axis=1, dtype=jnp.int32)
    nch = (cnt + (CH - 1)) // CH
    hi16 = jnp.cumsum(nch, axis=1)
    lo16 = hi16 - nch
    nct = hi16[:, -1:]
    pad = jnp.broadcast_to(nct, (n, LANES - N_EXPERTS))
    lohi = jnp.zeros((n, 8, LANES), F32)
    lohi = lohi.at[:, 0, :].set((jnp.concatenate([lo16, pad], axis=1) * CH).astype(F32))
    lohi = lohi.at[:, 1, :].set((jnp.concatenate([hi16, pad], axis=1) * CH).astype(F32))
    tot = jnp.sum(nch, axis=0)
    seg_len = (tot + (CPM - 1)) // CPM * CPM
    seg_end = jnp.cumsum(seg_len)
    seg_start = seg_end - seg_len
    gpos = seg_start[None, :] + jnp.cumsum(nch, axis=0) - nch
    assert (n * NCH) % CPM == 0
    n_steps = (n * NCH + N_EXPERTS * CPM) // CPM
    n_used = (seg_end[-1] // CPM).astype(jnp.int32).reshape(1)
    step = jnp.arange(n_steps, dtype=jnp.int32)
    tile_expert = jnp.sum(seg_end[None, :] // CPM <= jnp.minimum(step, n_used - 1)[:, None], axis=1, dtype=jnp.int32)
    tile_expert = jnp.minimum(tile_expert, N_EXPERTS - 1)
    exact = functools.partial(jnp.dot, precision=lax.Precision.HIGHEST)
    experts = jnp.arange(N_EXPERTS, dtype=jnp.int32)
    g = jnp.arange(n_steps * CPM, dtype=jnp.int32)
    e_of_g = jnp.minimum(jnp.sum(seg_end[None, :] <= g[:, None], axis=1, dtype=jnp.int32), N_EXPERTS - 1)
    pick_e = (e_of_g[:, None] == experts[None, :]).astype(F32)
    first = exact(pick_e, gpos.T.astype(F32))
    count = exact(pick_e, nch.T.astype(F32))
    base = exact(pick_e, (jnp.arange(n, dtype=jnp.int32)[:, None] * NCH + lo16).T.astype(F32))
    gf = g.astype(F32)[:, None]
    inside = (first <= gf) & (gf < first + count)
    src_chunk = jnp.sum(jnp.where(inside, base + gf - first, 0.0), axis=1).astype(jnp.int32)
    k = jnp.arange(NCH, dtype=jnp.int32)
    e_of_k = jnp.minimum(jnp.sum(hi16[:, None, :] <= k[None, :, None], axis=2, dtype=jnp.int32), N_EXPERTS - 1)
    pick_k = e_of_k[:, :, None] == experts[None, None, :]
    pos = jnp.sum(jnp.where(pick_k, (gpos - lo16)[:, None, :], 0), axis=2) + k[None, :]
    dst_chunk = jnp.where(k[None, :] < nct, pos, 0).astype(jnp.int32).reshape(-1)
    return lohi, tile_expert, src_chunk, n_used, dst_chunk, nct.reshape(-1)


def _moe(h1, gates, wgu, wd, wsgu, wsd, ln_g, ln_b, alpha):
    lohi, tile_expert, src_chunk, n_used, dst_chunk, tile_chunks = _routing_tables(gates)
    x_disp, w_t = _dispatch(h1, gates, lohi)
    y_sorted = _experts(x_disp, wgu, wd, tile_expert, src_chunk, n_used)
    return _combine(y_sorted, w_t, h1, dst_chunk, tile_chunks, wsgu, wsd, ln_g, ln_b, alpha)


def _t5_bucket(rel):
    half = T5_BUCKETS // 2
    max_exact = half // 2
    n = np.abs(rel)
    ratio = np.log(np.maximum(n, 1).astype(np.float32) / np.float32(max_exact))
    ratio = ratio / np.float32(math.log(T5_MAX_DIST / max_exact)) * np.float32(half - max_exact)
    large = np.minimum(max_exact + ratio.astype(np.int32), half - 1)
    return np.where(rel > 0, half, 0) + np.where(n < max_exact, n, large)


def _t5_tables(t5_table, s_max):
    def bias(rel, valid):
        onehot = np.eye(T5_BUCKETS, dtype=np.float32)[_t5_bucket(rel)]
        b = jnp.einsum("qkb,bh->hqk", jnp.asarray(onehot), t5_table.astype(F32), precision=lax.Precision.HIGHEST)
        return jnp.where(jnp.asarray(valid)[None], b, NEG_INF)

    ii = np.arange(BLOCK)[:, None]
    jj = np.arange(3 * BLOCK)[None, :]
    rel = jj - ii - BLOCK
    bband = bias(rel, np.abs(rel) <= WINDOW)
    t = np.arange(s_max)[:, None]
    m = np.arange(N_META)[None, :]
    bmeta = bias(m - (N_META + t), np.ones((s_max, N_META), bool))
    mpos = np.arange(N_META)[:, None]
    kpos = np.arange(N_META + BLOCK)[None, :]
    relq = kpos - mpos
    bq = bias(relq, (kpos < N_META) | (np.abs(relq) <= WINDOW))
    return bband, bmeta, bq[:, :, :N_META], bq[:, :, N_META:]


def _na_bias_cases(rpb):
    W = GRID_W
    c = np.arange(W)[:, None]
    kc = np.arange(W)[None, :]
    cs = np.clip(c - NA_COLS // 2, 0, W - NA_COLS)
    valid = (kc >= cs) & (kc < cs + NA_COLS)
    dc = np.clip(kc - c + (NA_COLS - 1), 0, 2 * NA_COLS - 2)
    onehot = np.eye(2 * NA_COLS - 1, dtype=np.float32)[dc]
    t = jnp.einsum("hrd,ckd->hrck", rpb.astype(F32), jnp.asarray(onehot), precision=lax.Precision.HIGHEST)
    t = jnp.where(jnp.asarray(valid), t, NEG_INF)
    cases = [jnp.transpose(t[:, NA_ROWS - 1 - d:2 * NA_ROWS - 1 - d], (0, 2, 1, 3)) for d in range(NA_ROWS)]
    return jnp.stack(cases, axis=0).reshape(NA_ROWS, N_HEADS, W, NA_ROWS * W)


def _rope_tables(s_max, n_meta_rows):
    half = HEAD_DIM // 4
    freq = ROPE_THETA ** (-jnp.arange(half, dtype=F32) / half)
    t = np.arange(s_max)
    mp = np.tile(np.arange(N_META) - N_META, n_meta_rows // N_META)
    pos_row = jnp.asarray(np.concatenate([t // GRID_W, mp]), jnp.int32).astype(F32)
    pos_col = jnp.asarray(np.concatenate([t % GRID_W, mp]), jnp.int32).astype(F32)
    ar = pos_row[:, None] * freq
    ac = pos_col[:, None] * freq
    cos = jnp.concatenate([jnp.cos(ar), jnp.cos(ar), jnp.cos(ac), jnp.cos(ac)], axis=1)
    sin = jnp.concatenate([-jnp.sin(ar), jnp.sin(ar), -jnp.sin(ac), jnp.sin(ac)], axis=1)
    return jnp.tile(cos, (1, 2)), jnp.tile(sin, (1, 2))


def kernel(x_prompt, x_sample, meta_tokens, ln_in_g, ln_in_b, t5_table, w_in, q_gain, k_gain, sink,
           na_rpb, na_meta_bias, w_branch, w_out, ln1_g, ln1_b, w_router, router_bias,
           w_expert_gate_up, w_expert_down, w_shared_gate_up, w_shared_down, ln2_g, ln2_b):
    depth = w_in.shape[0]
    alpha = (2 * depth) ** 0.25
    B0, S0, D = x_prompt.shape
    B1, S1, _ = x_sample.shape
    assert D == D_MODEL
    real = B0 * S0 + B1 * S1
    n_meta_rows = -(-(B0 + B1) * N_META // TM) * TM
    R = real + n_meta_rows
    n_meta_blocks = n_meta_rows // N_META
    g0 = _Group(B0, S0, 0, 0, real, B0)
    g1 = _Group(B1, S1, B0 * S0, B0, real, n_meta_blocks - B0)
    s_max = max(S0, S1)

    h = _embed_ln(x_prompt.reshape(B0 * S0, D), x_sample.reshape(B1 * S1, D),
                  jnp.tile(meta_tokens, (TM // N_META, 1)), ln_in_g.reshape(1, D), ln_in_b.reshape(1, D), R)

    tm = 2 * TM if all(v % (2 * TM) == 0 for v in (S0, S1, n_meta_rows)) else TM
    cos_tab, sin_tab = _rope_tables(s_max, tm)
    n0, n1 = B0 * S0 // tm, real // tm
    p0, p1, pm = S0 // tm, S1 // tm, s_max // tm

    def pos_block(i):
        return jnp.where(i < n0, i % p0, jnp.where(i < n1, (i - n0) % p1, pm))

    bband, bmeta, bq_meta, bq_blk = (t * LOG2E for t in _t5_tables(t5_table, s_max))
    bmeta_wide = jnp.pad(bmeta, ((0, 0), (0, 0), (0, LANES - N_META)), constant_values=NEG_INF)
    ones_bd = jnp.asarray(np.kron(np.eye(N_HEADS), np.ones((HEAD_DIM, HEAD_DIM))), BF16)

    qs, ks, vs = [], [], []
    for n in range(N_MIXERS):
        off = n * QKV_WIDTH
        qs.append(w_in[:, :, off:off + MIX_WIDTH])
        ks.append(w_in[:, :, off + MIX_WIDTH:off + MIX_WIDTH + KV_WIDTH])
        vs.append(w_in[:, :, off + MIX_WIDTH + KV_WIDTH:off + QKV_WIDTH])
    wr_t = jnp.swapaxes(w_router, 1, 2)
    wr_hi = wr_t.astype(BF16)
    layers = dict(
        w_qkv=jnp.concatenate(qs + ks + vs, axis=2).astype(BF16),
        w_gate=w_in[:, :, N_MIXERS * QKV_WIDTH:].astype(BF16),
        q_gain=jnp.tile(q_gain, (1, N_HEADS)).reshape(depth, 1, MIX_WIDTH),
        k_gain=jnp.tile(k_gain, (1, N_KV_HEADS)).reshape(depth, 1, KV_WIDTH),
        sink=sink.astype(F32) * LOG2E,
        na_bias=jax.vmap(_na_bias_cases)(na_rpb) * LOG2E,
        na_mbias=na_meta_bias.astype(F32) * LOG2E,
        na_mbias_wide=jnp.pad(na_meta_bias.astype(F32) * LOG2E, ((0, 0), (0, 0), (0, LANES - N_META)),
                              constant_values=NEG_INF),
        w_branch=w_branch.astype(BF16),
        w_out=w_out.astype(BF16),
        ln1_g=ln1_g.reshape(depth, 1, D), ln1_b=ln1_b.reshape(depth, 1, D),
        wr_hi=wr_hi, wr_lo=(wr_t - wr_hi.astype(F32)).astype(BF16),
        rbias=router_bias.astype(F32).reshape(depth, N_EXPERTS, 1),
        wgu=w_expert_gate_up.astype(BF16), wd=w_expert_down.astype(BF16),
        wsgu=w_shared_gate_up.astype(BF16), wsd=w_shared_down.astype(BF16),
        ln2_g=ln2_g.reshape(depth, 1, D), ln2_b=ln2_b.reshape(depth, 1, D),
    )

    def layer(h, p):
        q_all, k_all, v_all, v_ones = _inproj(h, p["w_qkv"], cos_tab, sin_tab, p["q_gain"], p["k_gain"],
                                      ones_bd, pos_block, tm)
        o = None
        for grp in (g0, g1):
            o = _global_attn(grp, q_all, k_all, v_ones, o)
            o = _window_attn(grp, q_all, k_all, v_all, p["sink"], bband, bmeta_wide[:, :grp.S], o)
            o = _na_attn(grp, q_all, k_all, v_all, p["na_bias"], p["na_mbias_wide"], o)
        for grp in (g0, g1):
            o = _meta_attn(grp, q_all, k_all, v_all, p["sink"], bq_meta, bq_blk, p["na_mbias"], o)
        h1, gates = _merge(h, o, p["w_gate"], p["w_branch"], p["w_out"], p["ln1_g"], p["ln1_b"],
                           p["wr_hi"], p["wr_lo"], p["rbias"], alpha, tm)
        h2 = _moe(h1, gates, p["wgu"], p["wd"], p["wsgu"], p["wsd"], p["ln2_g"], p["ln2_b"],
                  alpha)
        return h2, None

    h, _ = lax.scan(layer, h, layers)
    y_prompt = h[:B0 * S0].reshape(B0, S0, D)
    y_sample = h[B0 * S0:real].reshape(B1, S1, D)
    return (y_prompt, y_sample)
```

```python
import functools
import math

import numpy as np
import jax
import jax.numpy as jnp
from jax import lax
from jax.experimental import pallas as pl
from jax.experimental.pallas import tpu as pltpu

F32 = jnp.float32
BF16 = jnp.bfloat16

D_MODEL = 1024
HEAD_DIM = 64
N_HEADS = 8
N_KV_HEADS = 2
GROUP = N_HEADS // N_KV_HEADS
MIX_WIDTH = N_HEADS * HEAD_DIM
KV_WIDTH = N_KV_HEADS * HEAD_DIM
N_MIXERS = 3
QKV_WIDTH = MIX_WIDTH + 2 * KV_WIDTH
N_META = 16
GRID_W = 64
BLOCK = 128
WINDOW = 128
NA_ROWS = 8
NA_COLS = 16
T5_BUCKETS = 32
T5_MAX_DIST = 128
ROPE_THETA = 10000.0
N_EXPERTS = 64
TOP_K = 8
N_EXPERT_GROUPS = 8
TOPK_GROUPS = 4
D_EXPERT = 256
ROUTED_SCALE = 2.5
NEG_INF = -1e30
LOG2E = math.log2(math.e)
LANES = 128

TM = 512
TQ_GLOBAL = 256
TK_GLOBAL = 512
NA_QROWS = 8
WIN_QBLOCKS = 4
VMEM_LIMIT = 56 * 1024 * 1024


def _cparams(sem):
    return pltpu.CompilerParams(dimension_semantics=sem, vmem_limit_bytes=VMEM_LIMIT)


def _dot(a, b):
    return jnp.dot(a, b, preferred_element_type=F32)


def _dot_nt(a, b):
    return lax.dot_general(a, b, (((1,), (1,)), ((), ())), preferred_element_type=F32)


def _split_bf16(x):
    hi = x.astype(BF16)
    lo = (x - hi.astype(F32)).astype(BF16)
    return hi, lo


def _layer_norm(x, g, b):
    mu = jnp.mean(x, axis=-1, keepdims=True)
    xc = x - mu
    var = jnp.mean(xc * xc, axis=-1, keepdims=True)
    return xc * lax.rsqrt(var + 1e-5) * g + b


def _embed_ln_kernel(x0_ref, x1_ref, xm_ref, g_ref, b_ref, o_ref, *, n0, n1):
    i = pl.program_id(0)
    for src, pred in ((x0_ref, i < n0), (x1_ref, (i >= n0) & (i < n0 + n1)), (xm_ref, i >= n0 + n1)):
        @pl.when(pred)
        def _(src=src):
            o_ref[...] = _layer_norm(src[...], g_ref[...], b_ref[...])


def _embed_ln(x0, x1, meta_tile, g, b, n_rows):
    n0, n1 = x0.shape[0] // TM, x1.shape[0] // TM
    return pl.pallas_call(
        functools.partial(_embed_ln_kernel, n0=n0, n1=n1),
        grid=(n_rows // TM,),
        in_specs=[pl.BlockSpec((TM, D_MODEL), lambda i: (jnp.minimum(i, n0 - 1), 0)),
                  pl.BlockSpec((TM, D_MODEL), lambda i: (jnp.clip(i - n0, 0, n1 - 1), 0)),
                  pl.BlockSpec((TM, D_MODEL), lambda i: (0, 0)),
                  pl.BlockSpec((1, D_MODEL), lambda i: (0, 0)),
                  pl.BlockSpec((1, D_MODEL), lambda i: (0, 0))],
        out_specs=pl.BlockSpec((TM, D_MODEL), lambda i: (i, 0)),
        out_shape=jax.ShapeDtypeStruct((n_rows, D_MODEL), F32),
        compiler_params=_cparams(("arbitrary",)),
        name="embed_ln",
    )(x0, x1, meta_tile, g, b)


def _rope_slot(x, cos, sin_signed, first_half):
    fwd = pltpu.roll(x, LANES - 16, 1)
    bwd = pltpu.roll(x, 16, 1)
    return x * cos + jnp.where(first_half, fwd, bwd) * sin_signed


def _head_rms(x, ones_bd, gain):
    hi, lo = _split_bf16(x * x)
    ss = _dot(hi, ones_bd) + _dot(lo, ones_bd)
    return x * lax.rsqrt(ss * (1.0 / HEAD_DIM) + 1e-6) * gain


def _inproj_kernel(h_ref, w_ref, cos_ref, sin_ref, qg_ref, kg_ref, ones_ref, q_ref, k_ref, v_ref, v1_ref):
    x = h_ref[...].astype(BF16)
    cos = cos_ref[...]
    sin = sin_ref[...]
    lane = lax.broadcasted_iota(jnp.int32, cos.shape, 1)
    first_half = (lane % 32) < 16
    scale = HEAD_DIM ** -0.5
    qw = N_MIXERS * MIX_WIDTH
    qa = _head_rms(_dot(x, w_ref[:, 0:MIX_WIDTH]), ones_ref[...], qg_ref[...])
    for s in range(MIX_WIDTH // LANES):
        sl = slice(s * LANES, (s + 1) * LANES)
        q_ref[:, sl] = (_rope_slot(qa[:, sl], cos, sin, first_half) * (scale * LOG2E)).astype(BF16)
    ka = _head_rms(_dot(x, w_ref[:, qw:qw + KV_WIDTH]), ones_ref[0:LANES, 0:LANES], kg_ref[...])
    k_ref[:, 0:KV_WIDTH] = _rope_slot(ka, cos, sin, first_half).astype(BF16)
    for n in range(1, N_MIXERS):
        q_ref[:, n * MIX_WIDTH:(n + 1) * MIX_WIDTH] = (
            _dot(x, w_ref[:, n * MIX_WIDTH:(n + 1) * MIX_WIDTH]) * (scale * LOG2E)).astype(BF16)
        k_ref[:, n * KV_WIDTH:(n + 1) * KV_WIDTH] = _dot(
            x, w_ref[:, qw + n * KV_WIDTH:qw + (n + 1) * KV_WIDTH]).astype(BF16)
    vw = qw + N_MIXERS * KV_WIDTH
    v = _dot(x, w_ref[:, vw:vw + N_MIXERS * KV_WIDTH])
    v_ref[...] = v.astype(BF16)
    va = v[:, 0:KV_WIDTH]
    lo = lane < HEAD_DIM
    v1_ref[:, 0:LANES] = jnp.where(lo, va, 1.0).astype(BF16)
    v1_ref[:, LANES:2 * LANES] = jnp.where(lo, pltpu.roll(va, HEAD_DIM, 1), 1.0).astype(BF16)


def _inproj(h, w_qkv, cos_tab, sin_tab, q_gain, k_gain, ones_bd, pos_block, tm):
    R = h.shape[0]
    const = lambda i: (0, 0)
    return pl.pallas_call(
        _inproj_kernel,
        grid=(R // tm,),
        in_specs=[pl.BlockSpec((tm, D_MODEL), lambda i: (i, 0)),
                  pl.BlockSpec(w_qkv.shape, const),
                  pl.BlockSpec((tm, LANES), lambda i: (pos_block(i), 0)),
                  pl.BlockSpec((tm, LANES), lambda i: (pos_block(i), 0)),
                  pl.BlockSpec((1, MIX_WIDTH), const),
                  pl.BlockSpec((1, KV_WIDTH), const),
                  pl.BlockSpec((MIX_WIDTH, MIX_WIDTH), const)],
        out_specs=[pl.BlockSpec((tm, N_MIXERS * MIX_WIDTH), lambda i: (i, 0)),
                   pl.BlockSpec((tm, N_MIXERS * KV_WIDTH), lambda i: (i, 0)),
                   pl.BlockSpec((tm, N_MIXERS * KV_WIDTH), lambda i: (i, 0)),
                   pl.BlockSpec((tm, N_KV_HEADS * LANES), lambda i: (i, 0))],
        out_shape=[jax.ShapeDtypeStruct((R, N_MIXERS * MIX_WIDTH), BF16),
                   jax.ShapeDtypeStruct((R, N_MIXERS * KV_WIDTH), BF16),
                   jax.ShapeDtypeStruct((R, N_MIXERS * KV_WIDTH), BF16),
                   jax.ShapeDtypeStruct((R, N_KV_HEADS * LANES), BF16)],
        compiler_params=_cparams(("parallel",)),
        name="inproj",
    )(h, w_qkv, cos_tab, sin_tab, q_gain, k_gain, ones_bd)


def _group_queries(q, j):
    lane = lax.broadcasted_iota(jnp.int32, (q.shape[0], LANES), 1)
    keep = (lane < HEAD_DIM) if j == 0 else (lane >= HEAD_DIM)
    parts = []
    for hh in range(GROUP):
        h = GROUP * j + hh
        slot = q[:, (h // 2) * LANES:(h // 2 + 1) * LANES]
        if h % 2 != j:
            slot = pltpu.roll(slot, HEAD_DIM, 1)
        parts.append(jnp.where(keep, slot, 0.0))
    return jnp.concatenate(parts, axis=0).astype(BF16)


def _ungroup_outputs(out, j, T):
    lane = lax.broadcasted_iota(jnp.int32, (T, LANES), 1)
    lo = lane < HEAD_DIM
    slots = []
    for s in range(2):
        even = out[(2 * s) * T:(2 * s + 1) * T]
        odd = out[(2 * s + 1) * T:(2 * s + 2) * T]
        if j == 0:
            slots.append(jnp.where(lo, even, pltpu.roll(odd, HEAD_DIM, 1)))
        else:
            slots.append(jnp.where(lo, pltpu.roll(even, HEAD_DIM, 1), odd))
    return jnp.concatenate(slots, axis=1)


def _fold_lanes(op, *parts):
    cols = [p[:, c * LANES:(c + 1) * LANES] for p in parts for c in range(p.shape[1] // LANES)]
    return functools.reduce(op, cols)


def _row_max(*parts):
    return jnp.max(_fold_lanes(jnp.maximum, *parts), axis=-1, keepdims=True)


def _row_sum(*parts):
    return jnp.sum(_fold_lanes(jnp.add, *parts), axis=-1, keepdims=True)


def _pad_meta_rows(x):
    return jnp.concatenate([x, jnp.zeros((LANES - N_META, x.shape[1]), x.dtype)], axis=0)


def _head_rows(vals, T):
    return jnp.concatenate([jnp.broadcast_to(v, (T, v.shape[-1])) for v in vals], axis=0)


def _global_kernel(q_ref, k_ref, v_ref, km_ref, vm_ref, *rest, S):
    o_ref = rest[-1]
    T = q_ref.shape[0]
    q = q_ref[...].astype(F32)
    km = km_ref[...]
    qs, state = [], []
    for j in range(N_KV_HEADS):
        qj = _group_queries(q, j)
        s_m = _dot_nt(qj, km)
        m0 = jnp.max(s_m, axis=-1, keepdims=True)
        p_m = jnp.exp2((s_m - m0).astype(BF16))
        qs.append(qj)
        state.append((m0, _dot(p_m, vm_ref[:, j * LANES:(j + 1) * LANES])))
    for c in range(S // TK_GLOBAL):
        kc = k_ref[c * TK_GLOBAL:(c + 1) * TK_GLOBAL, :]
        for j in range(N_KV_HEADS):
            m, acc = state[j]
            s = _dot_nt(qs[j], kc)
            m_new = jnp.maximum(m, jnp.max(s, axis=-1, keepdims=True))
            p = jnp.exp2((s - m_new).astype(BF16))
            vc = v_ref[c * TK_GLOBAL:(c + 1) * TK_GLOBAL, j * LANES:(j + 1) * LANES]
            state[j] = (m_new, jnp.exp2(m - m_new) * acc + _dot(p, vc))
    for j in range(N_KV_HEADS):
        acc = state[j][1]
        out = acc / pltpu.roll(acc, HEAD_DIM, 1)
        o_ref[:, j * 2 * LANES:(j + 1) * 2 * LANES] = _ungroup_outputs(out, 0, T).astype(BF16)


def _window_kernel(sink_ref, q_ref, kp_ref, kc_ref, kn_ref, vp_ref, vc_ref, vn_ref, km_ref, vm_ref,
                   bband_ref, bmeta_ref, *rest, nb):
    o_ref = rest[-1]
    i = pl.program_id(1)
    T = BLOCK
    k_span = jnp.concatenate([kp_ref[...], kc_ref[...], kn_ref[...]], axis=0)
    v_span = jnp.concatenate([vp_ref[...], vc_ref[...], vn_ref[...]], axis=0)
    col = lax.broadcasted_iota(jnp.int32, (1, 3 * BLOCK), 1)
    km = _pad_meta_rows(km_ref[...])
    vm = _pad_meta_rows(vm_ref[...])
    for u in range(WIN_QBLOCKS):
        q = q_ref[u * BLOCK:(u + 1) * BLOCK, :].astype(F32)
        kband = k_span[u * BLOCK:(u + 3) * BLOCK]
        vband = v_span[u * BLOCK:(u + 3) * BLOCK]
        in_range = None
        if u == 0:
            in_range = (col >= BLOCK) | (i > 0)
        if u == WIN_QBLOCKS - 1:
            after = (col < 2 * BLOCK) | (i < nb - 1)
            in_range = after if in_range is None else in_range & after
        for j in range(N_KV_HEADS):
            qj = _group_queries(q, j)
            hs = slice(GROUP * j, GROUP * (j + 1))
            s_b = _dot_nt(qj, kband) + bband_ref[hs].reshape(GROUP * T, 3 * BLOCK)
            if in_range is not None:
                s_b = jnp.where(in_range, s_b, NEG_INF)
            s_m = _dot_nt(qj, km) + bmeta_ref[hs, u * BLOCK:(u + 1) * BLOCK].reshape(GROUP * T, LANES)
            sink = jnp.concatenate([jnp.full((T, 1), sink_ref[GROUP * j + hh], F32) for hh in range(GROUP)], axis=0)
            m = jnp.maximum(_row_max(s_b, s_m), sink)
            e_b = jnp.exp2(s_b - m)
            e_m = jnp.exp2(s_m - m)
            denom = _row_sum(e_b, e_m) + jnp.exp2(sink - m)
            acc = _dot(e_b.astype(BF16), vband) + _dot(e_m.astype(BF16), vm)
            o_ref[u * BLOCK:(u + 1) * BLOCK, j * 2 * LANES:(j + 1) * 2 * LANES] = (
                _ungroup_outputs(acc / denom, j, T).astype(BF16))


def _na_kernel(q_ref, k_ref, v_ref, km_ref, vm_ref, bias_ref, mbias_ref, *rest, rows):
    o_ref = rest[-1]
    blk = pl.program_id(1)
    W = GRID_W
    nkeys = NA_ROWS * W
    km = _pad_meta_rows(km_ref[...])
    vm = _pad_meta_rows(vm_ref[...])
    mb = _head_rows([mbias_ref[h:h + 1, :] for h in range(N_HEADS)], W)
    half = GROUP * W
    for rr in range(NA_QROWS):
        r = blk * NA_QROWS + rr
        rs = jnp.clip(r - NA_ROWS // 2, 0, rows - NA_ROWS)
        delta = r - rs
        koff = pl.multiple_of(rs * W, W)
        q = q_ref[rr * W:(rr + 1) * W, :].astype(F32)
        q8 = jnp.concatenate([_group_queries(q, j) for j in range(N_KV_HEADS)], axis=0)
        kw = k_ref[pl.ds(koff, nkeys), :]
        vw = v_ref[pl.ds(koff, nkeys), :]
        s_w = _dot_nt(q8, kw) + bias_ref[delta].reshape(N_HEADS * W, nkeys)
        s_m = _dot_nt(q8, km) + mb
        m = _row_max(s_w, s_m)
        e_w = jnp.exp2(s_w - m)
        e_m = jnp.exp2(s_m - m)
        out = (_dot(e_w.astype(BF16), vw) + _dot(e_m.astype(BF16), vm)) / _row_sum(e_w, e_m)
        for j in range(N_KV_HEADS):
            o_ref[rr * W:(rr + 1) * W, j * 2 * LANES:(j + 1) * 2 * LANES] = (
                _ungroup_outputs(out[j * half:(j + 1) * half], j, W).astype(BF16))


def _meta_kernel(sink_ref, q_ref, ka_ref, va_ref, kb_ref, vb_ref, km_ref, vm_ref,
                 bq_meta_ref, bq_blk_ref, mbias_ref, o_in_ref, o_ref):
    del o_in_ref
    T = N_META
    q = q_ref[...].astype(F32)
    km_all = km_ref[...]
    vm_all = vm_ref[...]

    def finish(n, j, acc, denom):
        lo = n * MIX_WIDTH + j * 2 * LANES
        o_ref[:, lo:lo + 2 * LANES] = _ungroup_outputs(acc / denom, j, T).astype(BF16)

    for j in range(N_KV_HEADS):
        hs = slice(GROUP * j, GROUP * (j + 1))
        qj = _group_queries(q[:, 0:MIX_WIDTH], j)
        km, vm = km_all[:, 0:KV_WIDTH], vm_all[:, 0:KV_WIDTH]
        s_r = _dot_nt(qj, ka_ref[...])
        s_m = _dot_nt(qj, km)
        m = jnp.maximum(jnp.max(s_r, axis=-1, keepdims=True), jnp.max(s_m, axis=-1, keepdims=True))
        e_r = jnp.exp2(s_r - m)
        e_m = jnp.exp2(s_m - m)
        denom = jnp.sum(e_r, axis=-1, keepdims=True) + jnp.sum(e_m, axis=-1, keepdims=True)
        finish(0, j, _dot(e_r.astype(BF16), va_ref[...]) + _dot(e_m.astype(BF16), vm), denom)
        qj = _group_queries(q[:, MIX_WIDTH:2 * MIX_WIDTH], j)
        km, vm = km_all[:, KV_WIDTH:2 * KV_WIDTH], vm_all[:, KV_WIDTH:2 * KV_WIDTH]
        s_r = _dot_nt(qj, kb_ref[...]) + bq_blk_ref[hs].reshape(GROUP * T, BLOCK)
        s_m = _dot_nt(qj, km) + bq_meta_ref[hs].reshape(GROUP * T, N_META)
        sink = jnp.concatenate([jnp.full((T, 1), sink_ref[GROUP * j + hh], F32) for hh in range(GROUP)], axis=0)
        m = jnp.maximum(jnp.maximum(jnp.max(s_r, axis=-1, keepdims=True),
                                    jnp.max(s_m, axis=-1, keepdims=True)), sink)
        e_r = jnp.exp2(s_r - m)
        e_m = jnp.exp2(s_m - m)
        denom = (jnp.sum(e_r, axis=-1, keepdims=True) + jnp.sum(e_m, axis=-1, keepdims=True)
                 + jnp.exp2(sink - m))
        finish(1, j, _dot(e_r.astype(BF16), vb_ref[...]) + _dot(e_m.astype(BF16), vm), denom)
        qj = _group_queries(q[:, 2 * MIX_WIDTH:3 * MIX_WIDTH], j)
        km, vm = km_all[:, 2 * KV_WIDTH:3 * KV_WIDTH], vm_all[:, 2 * KV_WIDTH:3 * KV_WIDTH]
        mb = _head_rows([mbias_ref[GROUP * j + hh:GROUP * j + hh + 1, :] for hh in range(GROUP)], T)
        s_m = _dot_nt(qj, km) + mb
        m = jnp.max(s_m, axis=-1, keepdims=True)
        e_m = jnp.exp2(s_m - m)
        finish(2, j, _dot(e_m.astype(BF16), vm), jnp.sum(e_m, axis=-1, keepdims=True))


class _Group:
    def __init__(self, B, S, real_base, meta_batch_base, meta_base, n_meta_blocks):
        self.B, self.S = B, S
        self.real_base = real_base
        self.meta_blk0 = meta_base // N_META + meta_batch_base
        self.n_meta_blocks = n_meta_blocks
        assert real_base % S == 0 and S % TM == 0 and meta_base % N_META == 0


def _alias_args(o_prev, n_inputs):
    if o_prev is None:
        return [], [], {}
    return [o_prev], [pl.BlockSpec(memory_space=pl.ANY)], {n_inputs: 0}


def _global_attn(grp, q_all, k_all, v_ones, o_prev):
    B, S = grp.B, grp.S
    nq = S // TQ_GLOBAL
    qb0 = grp.real_base // TQ_GLOBAL
    sb0 = grp.real_base // S
    mb0 = grp.meta_blk0
    in_specs = [pl.BlockSpec((TQ_GLOBAL, MIX_WIDTH), lambda b, i: (qb0 + b * nq + i, 0)),
                pl.BlockSpec((S, KV_WIDTH), lambda b, i: (sb0 + b, 0)),
                pl.BlockSpec((S, N_KV_HEADS * LANES), lambda b, i: (sb0 + b, 0)),
                pl.BlockSpec((N_META, KV_WIDTH), lambda b, i: (mb0 + b, 0)),
                pl.BlockSpec((N_META, N_KV_HEADS * LANES), lambda b, i: (mb0 + b, 0))]
    extra, extra_specs, aliases = _alias_args(o_prev, len(in_specs))
    return pl.pallas_call(
        functools.partial(_global_kernel, S=S),
        grid=(B, nq),
        in_specs=in_specs + extra_specs,
        out_specs=pl.BlockSpec((TQ_GLOBAL, MIX_WIDTH), lambda b, i: (qb0 + b * nq + i, 0)),
        out_shape=jax.ShapeDtypeStruct((q_all.shape[0], N_MIXERS * MIX_WIDTH), BF16),
        input_output_aliases=aliases,
        compiler_params=_cparams(("parallel", "arbitrary")),
        name="mixer_global",
    )(q_all, k_all, v_ones, k_all, v_ones, *extra)


def _window_attn(grp, q_all, k_all, v_all, sink, bband, bmeta, o_prev):
    B, S = grp.B, grp.S
    span = WIN_QBLOCKS * BLOCK
    nb, ns = S // BLOCK, S // span
    qb0 = grp.real_base // BLOCK
    sp0 = grp.real_base // span
    mb0 = grp.meta_blk0
    cur = lambda b, i, sink: (sp0 + b * ns + i, 1)
    prv = lambda b, i, sink: (qb0 + b * nb + jnp.maximum(i * WIN_QBLOCKS - 1, 0), 1)
    nxt = lambda b, i, sink: (qb0 + b * nb + jnp.minimum((i + 1) * WIN_QBLOCKS, nb - 1), 1)
    met = lambda b, i, sink: (mb0 + b, 1)
    edge = lambda im: pl.BlockSpec((BLOCK, KV_WIDTH), im)
    mid = pl.BlockSpec((span, KV_WIDTH), cur)
    in_specs = [pl.BlockSpec((span, MIX_WIDTH), cur),
                edge(prv), mid, edge(nxt), edge(prv), mid, edge(nxt),
                pl.BlockSpec((N_META, KV_WIDTH), met), pl.BlockSpec((N_META, KV_WIDTH), met),
                pl.BlockSpec((N_HEADS, BLOCK, 3 * BLOCK), lambda b, i, sink: (0, 0, 0)),
                pl.BlockSpec((N_HEADS, span, LANES), lambda b, i, sink: (0, i, 0))]
    extra, extra_specs, aliases = _alias_args(o_prev, len(in_specs) + 1)
    return pl.pallas_call(
        functools.partial(_window_kernel, nb=ns),
        grid_spec=pltpu.PrefetchScalarGridSpec(
            num_scalar_prefetch=1,
            grid=(B, ns),
            in_specs=in_specs + extra_specs,
            out_specs=pl.BlockSpec((span, MIX_WIDTH), cur)),
        out_shape=jax.ShapeDtypeStruct((q_all.shape[0], N_MIXERS * MIX_WIDTH), BF16),
        input_output_aliases=aliases,
        compiler_params=_cparams(("parallel", "arbitrary")),
        name="mixer_window",
    )(sink, q_all, k_all, k_all, k_all, v_all, v_all, v_all, k_all, v_all, bband, bmeta, *extra)


def _na_attn(grp, q_all, k_all, v_all, na_bias, na_mbias, o_prev):
    B, S = grp.B, grp.S
    rows = S // GRID_W
    tq = NA_QROWS * GRID_W
    nq = S // tq
    qb0 = grp.real_base // tq
    sb0 = grp.real_base // S
    mb0 = grp.meta_blk0
    in_specs = [pl.BlockSpec((tq, MIX_WIDTH), lambda b, i: (qb0 + b * nq + i, 2)),
                pl.BlockSpec((S, KV_WIDTH), lambda b, i: (sb0 + b, 2)),
                pl.BlockSpec((S, KV_WIDTH), lambda b, i: (sb0 + b, 2)),
                pl.BlockSpec((N_META, KV_WIDTH), lambda b, i: (mb0 + b, 2)),
                pl.BlockSpec((N_META, KV_WIDTH), lambda b, i: (mb0 + b, 2)),
                pl.BlockSpec(na_bias.shape, lambda b, i: (0, 0, 0, 0)),
                pl.BlockSpec(na_mbias.shape, lambda b, i: (0, 0))]
    extra, extra_specs, aliases = _alias_args(o_prev, len(in_specs))
    return pl.pallas_call(
        functools.partial(_na_kernel, rows=rows),
        grid=(B, nq),
        in_specs=in_specs + extra_specs,
        out_specs=pl.BlockSpec((tq, MIX_WIDTH), lambda b, i: (qb0 + b * nq + i, 2)),
        out_shape=jax.ShapeDtypeStruct((q_all.shape[0], N_MIXERS * MIX_WIDTH), BF16),
        input_output_aliases=aliases,
        compiler_params=_cparams(("parallel", "arbitrary")),
        name="mixer_neighbourhood",
    )(q_all, k_all, v_all, k_all, v_all, na_bias, na_mbias, *extra)


def _meta_attn(grp, q_all, k_all, v_all, sink, bq_meta, bq_blk, na_mbias, o_prev):
    B, S = grp.B, grp.S
    sb0 = grp.real_base // S
    bb0 = grp.real_base // BLOCK
    nb = S // BLOCK
    mb0 = grp.meta_blk0
    clamp = lambda b: jnp.minimum(b, B - 1)
    mrow = lambda b, sink: (mb0 + clamp(b), 0)
    in_specs = [pl.BlockSpec((N_META, N_MIXERS * MIX_WIDTH), mrow),
                pl.BlockSpec((S, KV_WIDTH), lambda b, sink: (sb0 + clamp(b), 0)),
                pl.BlockSpec((S, KV_WIDTH), lambda b, sink: (sb0 + clamp(b), 0)),
                pl.BlockSpec((BLOCK, KV_WIDTH), lambda b, sink: (bb0 + clamp(b) * nb, 1)),
                pl.BlockSpec((BLOCK, KV_WIDTH), lambda b, sink: (bb0 + clamp(b) * nb, 1)),
                pl.BlockSpec((N_META, N_MIXERS * KV_WIDTH), mrow),
                pl.BlockSpec((N_META, N_MIXERS * KV_WIDTH), mrow),
                pl.BlockSpec(bq_meta.shape, lambda b, sink: (0, 0, 0)),
                pl.BlockSpec(bq_blk.shape, lambda b, sink: (0, 0, 0)),
                pl.BlockSpec(na_mbias.shape, lambda b, sink: (0, 0)),
                pl.BlockSpec(memory_space=pl.ANY)]
    return pl.pallas_call(
        _meta_kernel,
        grid_spec=pltpu.PrefetchScalarGridSpec(
            num_scalar_prefetch=1,
            grid=(grp.n_meta_blocks,),
            in_specs=in_specs,
            out_specs=pl.BlockSpec((N_META, N_MIXERS * MIX_WIDTH), lambda b, sink: (mb0 + b, 0))),
        out_shape=jax.ShapeDtypeStruct((q_all.shape[0], N_MIXERS * MIX_WIDTH), BF16),
        input_output_aliases={len(in_specs): 0},
        compiler_params=_cparams(("arbitrary",)),
        name="mixer_meta_queries",
    )(sink, q_all, k_all, v_all, k_all, v_all, k_all, v_all, bq_meta, bq_blk, na_mbias, o_prev)


def _route(h1, wr_hi, wr_lo, rbias):
    T = h1.shape[0]
    x_hi, x_lo = _split_bf16(h1)
    logits = _dot_nt(wr_hi, x_hi) + _dot_nt(wr_hi, x_lo) + _dot_nt(wr_lo, x_hi)
    scores = 1.0 / (1.0 + jnp.exp(-logits))
    sel = scores + rbias
    per_group = N_EXPERTS // N_EXPERT_GROUPS
    sel3 = sel.reshape(N_EXPERT_GROUPS, per_group, T)
    idx3 = lax.broadcasted_iota(jnp.int32, sel3.shape, 1).astype(F32)
    m1 = jnp.max(sel3, axis=1, keepdims=True)
    first = jnp.min(jnp.where(sel3 == m1, idx3, float(per_group)), axis=1, keepdims=True)
    m2 = jnp.max(jnp.where(idx3 == first, -jnp.inf, sel3), axis=1, keepdims=True)
    gscore = (m1 + m2).reshape(N_EXPERT_GROUPS, T)

    def rank_of(vals):
        idx = lax.broadcasted_iota(jnp.int32, vals.shape, 0)
        rank = jnp.zeros(vals.shape, F32)
        for r in range(vals.shape[0]):
            row = vals[r:r + 1, :]
            ge = jnp.where(row >= vals, 1.0, 0.0)
            gt = jnp.where(row > vals, 1.0, 0.0)
            rank = rank + jnp.where(idx > r, ge, gt)
        return rank

    gkeep = jnp.where(rank_of(gscore) < TOPK_GROUPS, 1.0, 0.0)
    ekeep = jnp.broadcast_to(gkeep.reshape(N_EXPERT_GROUPS, 1, T), sel3.shape).reshape(N_EXPERTS, T)
    masked = jnp.where(ekeep > 0.5, sel, NEG_INF)
    eidx = lax.broadcasted_iota(jnp.int32, masked.shape, 0).astype(F32)
    chosen = jnp.zeros(masked.shape, F32)
    for _ in range(TOP_K):
        best = jnp.max(masked, axis=0, keepdims=True)
        first = jnp.min(jnp.where(masked == best, eidx, float(N_EXPERTS)), axis=0, keepdims=True)
        hit = eidx == first
        chosen = jnp.where(hit, 1.0, chosen)
        masked = jnp.where(hit, -jnp.inf, masked)
    w = jnp.where(chosen > 0.5, scores, 0.0)
    return w / jnp.sum(w, axis=0, keepdims=True) * ROUTED_SCALE


def _merge_kernel(h_ref, o_ref, wg_ref, wb_ref, wo_ref, g_ref, b_ref, wrh_ref, wrl_ref, rb_ref,
                  h1_ref, gates_ref, *, alpha):
    h = h_ref[...]
    x = h.astype(BF16)
    merged = None
    for n in range(N_MIXERS):
        logit = _dot(x, wg_ref[:, n * D_MODEL:(n + 1) * D_MODEL])
        branch = _dot(o_ref[:, n * MIX_WIDTH:(n + 1) * MIX_WIDTH], wb_ref[n])
        term = branch / (1.0 + jnp.exp(-logit))
        merged = term if merged is None else merged + term
    mix = _dot(merged.astype(BF16), wo_ref[...])
    h1 = _layer_norm(alpha * h + mix, g_ref[...], b_ref[...])
    h1_ref[...] = h1
    gates_t = _route(h1, wrh_ref[...], wrl_ref[...], rb_ref[...])
    pad = jnp.zeros((LANES - N_EXPERTS, gates_t.shape[1]), F32)
    gates_ref[...] = jnp.concatenate([gates_t, pad], axis=0).T


def _merge(h, o_all, w_gate, w_branch, w_out, ln_g, ln_b, wr_hi, wr_lo, rbias, alpha, tm):
    R = h.shape[0]
    c2 = lambda i: (0, 0)
    once = pl.Buffered(1)
    return pl.pallas_call(
        functools.partial(_merge_kernel, alpha=alpha),
        grid=(R // tm,),
        in_specs=[pl.BlockSpec((tm, D_MODEL), lambda i: (i, 0)),
                  pl.BlockSpec((tm, N_MIXERS * MIX_WIDTH), lambda i: (i, 0)),
                  pl.BlockSpec(w_gate.shape, c2, pipeline_mode=once),
                  pl.BlockSpec(w_branch.shape, lambda i: (0, 0, 0), pipeline_mode=once),
                  pl.BlockSpec(w_out.shape, c2, pipeline_mode=once),
                  pl.BlockSpec((1, D_MODEL), c2),
                  pl.BlockSpec((1, D_MODEL), c2),
                  pl.BlockSpec(wr_hi.shape, c2),
                  pl.BlockSpec(wr_lo.shape, c2),
                  pl.BlockSpec(rbias.shape, c2)],
        out_specs=[pl.BlockSpec((tm, D_MODEL), lambda i: (i, 0)),
                   pl.BlockSpec((tm, LANES), lambda i: (i, 0))],
        out_shape=[jax.ShapeDtypeStruct((R, D_MODEL), F32),
                   jax.ShapeDtypeStruct((R, LANES), F32)],
        compiler_params=_cparams(("parallel",)),
        name="merge_ln_route",
    )(h, o_all, w_gate, w_branch, w_out, ln_g, ln_b, wr_hi, wr_lo, rbias)


TD = 256
CH = 16
SLOTS = 3072
NCH = SLOTS // CH
MT = 512
ME = 1024
CPM = ME // CH
XBUFS = 4
CBUFS = 4
CORE_CHUNKS = TOP_K * TD // CH
TAIL_GROUP = 8


def _swiglu_act(gu):
    g = gu[:, :D_EXPERT]
    return g / (1.0 + jnp.exp(-g)) * gu[:, D_EXPERT:]


def _slot_of_token(gates, lo_row):
    routed = gates > 0.0
    r = lax.broadcasted_iota(jnp.int32, (TD, TD), 0)
    c = lax.broadcasted_iota(jnp.int32, (TD, TD), 1)
    earlier = jnp.where(c < r, 1.0, 0.0).astype(BF16)
    rank = _dot(earlier, jnp.where(routed, 1.0, 0.0).astype(BF16))
    return jnp.where(routed, lo_row + rank + 1.0, 0.0)


def _split64(x):
    hi = 64.0 * jnp.floor(x * (1.0 / 64.0))
    return jnp.concatenate([hi.astype(BF16), (x - hi).astype(BF16)], axis=0)


def _dispatch_kernel(h_ref, g_ref, lohi_ref, x_ref, w_ref):
    lohi = lohi_ref[0]
    lo_row, hi_row = lohi[0:1], lohi[1:2]
    gates = g_ref[...]
    slot_t = _split64(_slot_of_token(gates, lo_row).T)
    gates_t = gates.T.astype(BF16)
    x = h_ref[...].astype(BF16)
    for blk in range(SLOTS // MT):
        s = (lax.broadcasted_iota(jnp.int32, (MT, LANES), 0) + blk * MT).astype(F32)
        owner = jnp.where(s >= lo_row, jnp.where(s < hi_row, 1.0, 0.0), 0.0).astype(BF16)
        want = _dot(jnp.concatenate([owner, owner], axis=1), slot_t)
        s1 = (lax.broadcasted_iota(jnp.int32, (MT, TD), 0) + (blk * MT + 1)).astype(F32)
        hit = want == s1
        x_ref[blk * MT:(blk + 1) * MT, :] = _dot(jnp.where(hit, 1.0, 0.0).astype(BF16), x).astype(BF16)
        weight = jnp.where(hit, _dot(owner, gates_t), 0.0)
        w_ref[:, blk * MT:(blk + 1) * MT] = weight.T.astype(BF16)


def _dispatch(h1, gates, lohi):
    n = h1.shape[0] // TD
    return pl.pallas_call(
        _dispatch_kernel,
        grid=(n,),
        in_specs=[pl.BlockSpec((TD, D_MODEL), lambda i: (i, 0)),
                  pl.BlockSpec((TD, LANES), lambda i: (i, 0)),
                  pl.BlockSpec((1, 8, LANES), lambda i: (i, 0, 0))],
        out_specs=[pl.BlockSpec((SLOTS, D_MODEL), lambda i: (i, 0)),
                   pl.BlockSpec((TD, SLOTS), lambda i: (i, 0))],
        out_shape=[jax.ShapeDtypeStruct((n * SLOTS, D_MODEL), BF16),
                   jax.ShapeDtypeStruct((n * TD, SLOTS), BF16)],
        compiler_params=_cparams(("parallel",)),
        name="moe_dispatch",
    )(h1, gates, lohi)


def _chunk_gather(table_ref, first, n_chunks, src_hbm, buf, sem):
    copies = []
    for c in range(n_chunks):
        row = pl.multiple_of(table_ref[first + c], CH)
        copies.append(pltpu.make_async_copy(src_hbm.at[pl.ds(row, CH)], buf.at[pl.ds(c * CH, CH)], sem))
    return copies


def _chunk_wait(n_chunks, src_hbm, buf, sem):
    for c in range(n_chunks):
        pltpu.make_async_copy(src_hbm.at[pl.ds(0, CH)], buf.at[pl.ds(c * CH, CH)], sem).wait()


def _expert_kernel(te_ref, src_ref, nu_ref, x_hbm, wgu_ref, wd_ref, y_ref, xbuf, sem):
    del te_ref
    m = pl.program_id(0)
    n_used = nu_ref[0]

    def start(step):
        slot = step % XBUFS
        for cp in _chunk_gather(src_ref, step * CPM, CPM, x_hbm, xbuf.at[slot], sem.at[slot]):
            cp.start()

    for ahead in range(XBUFS - 1):
        @pl.when((m == 0) & (ahead < n_used))
        def _(ahead=ahead):
            start(ahead)

    @pl.when(m + (XBUFS - 1) < n_used)
    def _():
        start(m + (XBUFS - 1))

    @pl.when(m < n_used)
    def _():
        slot = m % XBUFS
        _chunk_wait(CPM, x_hbm, xbuf.at[slot], sem.at[slot])
        act = _swiglu_act(_dot(xbuf[slot], wgu_ref[0]))
        y_ref[...] = _dot(act.astype(BF16), wd_ref[0]).astype(BF16)


def _experts(x_disp, wgu, wd, tile_expert, src_chunk, n_used):
    n_steps = tile_expert.shape[0]
    return pl.pallas_call(
        _expert_kernel,
        grid_spec=pltpu.PrefetchScalarGridSpec(
            num_scalar_prefetch=3,
            grid=(n_steps,),
            in_specs=[pl.BlockSpec(memory_space=pl.ANY),
                      pl.BlockSpec((1, D_MODEL, 2 * D_EXPERT), lambda m, te, src, nu: (te[m], 0, 0)),
                      pl.BlockSpec((1, D_EXPERT, D_MODEL), lambda m, te, src, nu: (te[m], 0, 0))],
            out_specs=pl.BlockSpec((ME, D_MODEL), lambda m, te, src, nu: (jnp.minimum(m, nu[0] - 1), 0)),
            scratch_shapes=[pltpu.VMEM((XBUFS, ME, D_MODEL), BF16), pltpu.SemaphoreType.DMA((XBUFS,))]),
        out_shape=jax.ShapeDtypeStruct((n_steps * ME, D_MODEL), BF16),
        compiler_params=_cparams(("arbitrary",)),
        name="moe_experts",
    )(tile_expert, src_chunk, n_used, x_disp, wgu, wd)


def _combine_kernel(dst_ref, nct_ref, y_hbm, w_ref, h_ref, wsgu_ref, wsd_ref, lg_ref, lb_ref, o_ref, ybuf, sem,
                    *, alpha):
    i = pl.program_id(0)
    n_tiles = pl.num_programs(0)

    def transfer(tile, wait):
        slot = tile % CBUFS
        buf, s = ybuf.at[slot], sem.at[slot]
        used = nct_ref[tile]

        def run(first, count):
            if wait:
                _chunk_wait(count, y_hbm, buf.at[pl.ds(first * CH, count * CH)], s)
            else:
                for cp in _chunk_gather(dst_ref, tile * NCH + first, count, y_hbm,
                                        buf.at[pl.ds(first * CH, count * CH)], s):
                    cp.start()

        run(0, CORE_CHUNKS)
        for first in range(CORE_CHUNKS, NCH, TAIL_GROUP):
            pl.when(used > first)(functools.partial(run, first, TAIL_GROUP))

    @pl.when(i == 0)
    def _():
        for slot in range(CBUFS):
            ybuf[slot] = jnp.zeros((SLOTS, D_MODEL), BF16)
        for ahead in range(CBUFS - 1):
            pl.when(ahead < n_tiles)(functools.partial(transfer, ahead, False))

    @pl.when(i + (CBUFS - 1) < n_tiles)
    def _():
        transfer(i + (CBUFS - 1), False)

    h = h_ref[...]
    shared = _dot(_swiglu_act(_dot(h.astype(BF16), wsgu_ref[...])).astype(BF16), wsd_ref[...])
    transfer(i, True)
    routed = _dot(w_ref[...], ybuf[i % CBUFS])
    o_ref[...] = _layer_norm(alpha * h + shared + routed, lg_ref[...], lb_ref[...])


def _combine(y_sorted, w_t, h1, dst_chunk, tile_chunks, wsgu, wsd, ln_g, ln_b, alpha):
    n = h1.shape[0] // TD
    c2 = lambda i, dst, nct: (0, 0)
    return pl.pallas_call(
        functools.partial(_combine_kernel, alpha=alpha),
        grid_spec=pltpu.PrefetchScalarGridSpec(
            num_scalar_prefetch=2,
            grid=(n,),
            in_specs=[pl.BlockSpec(memory_space=pl.ANY),
                      pl.BlockSpec((TD, SLOTS), lambda i, dst, nct: (i, 0)),
                      pl.BlockSpec((TD, D_MODEL), lambda i, dst, nct: (i, 0)),
                      pl.BlockSpec(wsgu.shape, c2),
                      pl.BlockSpec(wsd.shape, c2),
                      pl.BlockSpec((1, D_MODEL), c2),
                      pl.BlockSpec((1, D_MODEL), c2)],
            out_specs=pl.BlockSpec((TD, D_MODEL), lambda i, dst, nct: (i, 0)),
            scratch_shapes=[pltpu.VMEM((CBUFS, SLOTS, D_MODEL), BF16), pltpu.SemaphoreType.DMA((CBUFS,))]),
        out_shape=jax.ShapeDtypeStruct((h1.shape[0], D_MODEL), F32),
        compiler_params=_cparams(("arbitrary",)),
        name="moe_combine_ln",
    )(dst_chunk, tile_chunks, y_sorted, w_t, h1, wsgu, wsd, ln_g, ln_b)


def _routing_tables(gates):
    n = gates.shape[0] // TD
    cnt = jnp.sum((gates[:, :N_EXPERTS] > 0.0).reshape(n, TD, N_EXPERTS), axis=1, dtype=jnp.int32)
    nch = (cnt + (CH - 1)) // CH
    hi16 = jnp.cumsum(nch, axis=1)
    lo16 = hi16 - nch
    nct = hi16[:, -1:]
    pad = jnp.broadcast_to(nct, (n, LANES - N_EXPERTS))
    lohi = jnp.zeros((n, 8, LANES), F32)
    lohi = lohi.at[:, 0, :].set((jnp.concatenate([lo16, pad], axis=1) * CH).astype(F32))
    lohi = lohi.at[:, 1, :].set((jnp.concatenate([hi16, pad], axis=1) * CH).astype(F32))
    tot = jnp.sum(nch, axis=0)
    seg_len = (tot + (CPM - 1)) // CPM * CPM
    seg_end = jnp.cumsum(seg_len)
    seg_start = seg_end - seg_len
    gpos = seg_start[None, :] + jnp.cumsum(nch, axis=0) - nch
    assert (n * NCH) % CPM == 0
    n_steps = (n * NCH + N_EXPERTS * CPM) // CPM
    n_used = (seg_end[-1] // CPM).astype(jnp.int32).reshape(1)
    step = jnp.arange(n_steps, dtype=jnp.int32)
    tile_expert = jnp.sum(seg_end[None, :] // CPM <= jnp.minimum(step, n_used - 1)[:, None], axis=1, dtype=jnp.int32)
    tile_expert = jnp.minimum(tile_expert, N_EXPERTS - 1)
    exact = functools.partial(jnp.dot, precision=lax.Precision.HIGHEST)
    experts = jnp.arange(N_EXPERTS, dtype=jnp.int32)
    g = jnp.arange(n_steps * CPM, dtype=jnp.int32)
    e_of_g = jnp.minimum(jnp.sum(seg_end[None, :] <= g[:, None], axis=1, dtype=jnp.int32), N_EXPERTS - 1)
    pick_e = (e_of_g[:, None] == experts[None, :]).astype(F32)
    first = exact(pick_e, gpos.T.astype(F32))
    count = exact(pick_e, nch.T.astype(F32))
    base = exact(pick_e, (jnp.arange(n, dtype=jnp.int32)[:, None] * NCH + lo16).T.astype(F32))
    gf = g.astype(F32)[:, None]
    inside = (first <= gf) & (gf < first + count)
    src_chunk = jnp.sum(jnp.where(inside, base + gf - first, 0.0), axis=1).astype(jnp.int32)
    k = jnp.arange(NCH, dtype=jnp.int32)
    e_of_k = jnp.minimum(jnp.sum(hi16[:, None, :] <= k[None, :, None], axis=2, dtype=jnp.int32), N_EXPERTS - 1)
    pick_k = e_of_k[:, :, None] == experts[None, None, :]
    pos = jnp.sum(jnp.where(pick_k, (gpos - lo16)[:, None, :], 0), axis=2) + k[None, :]
    dst_chunk = jnp.where(k[None, :] < nct, pos, 0).astype(jnp.int32).reshape(-1)
    return lohi, tile_expert, src_chunk * CH, n_used, dst_chunk * CH, nct.reshape(-1)


def _moe(h1, gates, wgu, wd, wsgu, wsd, ln_g, ln_b, alpha):
    lohi, tile_expert, src_chunk, n_used, dst_chunk, tile_chunks = _routing_tables(gates)
    x_disp, w_t = _dispatch(h1, gates, lohi)
    y_sorted = _experts(x_disp, wgu, wd, tile_expert, src_chunk, n_used)
    return _combine(y_sorted, w_t, h1, dst_chunk, tile_chunks, wsgu, wsd, ln_g, ln_b, alpha)


def _t5_bucket(rel):
    half = T5_BUCKETS // 2
    max_exact = half // 2
    n = np.abs(rel)
    ratio = np.log(np.maximum(n, 1).astype(np.float32) / np.float32(max_exact))
    ratio = ratio / np.float32(math.log(T5_MAX_DIST / max_exact)) * np.float32(half - max_exact)
    large = np.minimum(max_exact + ratio.astype(np.int32), half - 1)
    return np.where(rel > 0, half, 0) + np.where(n < max_exact, n, large)


def _t5_tables(t5_table, s_max):
    def bias(rel, valid):
        onehot = np.eye(T5_BUCKETS, dtype=np.float32)[_t5_bucket(rel)]
        b = jnp.einsum("qkb,bh->hqk", jnp.asarray(onehot), t5_table.astype(F32), precision=lax.Precision.HIGHEST)
        return jnp.where(jnp.asarray(valid)[None], b, NEG_INF)

    ii = np.arange(BLOCK)[:, None]
    jj = np.arange(3 * BLOCK)[None, :]
    rel = jj - ii - BLOCK
    bband = bias(rel, np.abs(rel) <= WINDOW)
    t = np.arange(s_max)[:, None]
    m = np.arange(N_META)[None, :]
    bmeta = bias(m - (N_META + t), np.ones((s_max, N_META), bool))
    mpos = np.arange(N_META)[:, None]
    kpos = np.arange(N_META + BLOCK)[None, :]
    relq = kpos - mpos
    bq = bias(relq, (kpos < N_META) | (np.abs(relq) <= WINDOW))
    return bband, bmeta, bq[:, :, :N_META], bq[:, :, N_META:]


def _na_bias_cases(rpb):
    W = GRID_W
    c = np.arange(W)[:, None]
    kc = np.arange(W)[None, :]
    cs = np.clip(c - NA_COLS // 2, 0, W - NA_COLS)
    valid = (kc >= cs) & (kc < cs + NA_COLS)
    dc = np.clip(kc - c + (NA_COLS - 1), 0, 2 * NA_COLS - 2)
    onehot = np.eye(2 * NA_COLS - 1, dtype=np.float32)[dc]
    t = jnp.einsum("hrd,ckd->hrck", rpb.astype(F32), jnp.asarray(onehot), precision=lax.Precision.HIGHEST)
    t = jnp.where(jnp.asarray(valid), t, NEG_INF)
    cases = [jnp.transpose(t[:, NA_ROWS - 1 - d:2 * NA_ROWS - 1 - d], (0, 2, 1, 3)) for d in range(NA_ROWS)]
    return jnp.stack(cases, axis=0).reshape(NA_ROWS, N_HEADS, W, NA_ROWS * W)


def _rope_tables(s_max, n_meta_rows):
    half = HEAD_DIM // 4
    freq = ROPE_THETA ** (-jnp.arange(half, dtype=F32) / half)
    t = np.arange(s_max)
    mp = np.tile(np.arange(N_META) - N_META, n_meta_rows // N_META)
    pos_row = jnp.asarray(np.concatenate([t // GRID_W, mp]), jnp.int32).astype(F32)
    pos_col = jnp.asarray(np.concatenate([t % GRID_W, mp]), jnp.int32).astype(F32)
    ar = pos_row[:, None] * freq
    ac = pos_col[:, None] * freq
    cos = jnp.concatenate([jnp.cos(ar), jnp.cos(ar), jnp.cos(ac), jnp.cos(ac)], axis=1)
    sin = jnp.concatenate([-jnp.sin(ar), jnp.sin(ar), -jnp.sin(ac), jnp.sin(ac)], axis=1)
    return jnp.tile(cos, (1, 2)), jnp.tile(sin, (1, 2))


def kernel(x_prompt, x_sample, meta_tokens, ln_in_g, ln_in_b, t5_table, w_in, q_gain, k_gain, sink,
           na_rpb, na_meta_bias, w_branch, w_out, ln1_g, ln1_b, w_router, router_bias,
           w_expert_gate_up, w_expert_down, w_shared_gate_up, w_shared_down, ln2_g, ln2_b):
    depth = w_in.shape[0]
    alpha = (2 * depth) ** 0.25
    B0, S0, D = x_prompt.shape
    B1, S1, _ = x_sample.shape
    assert D == D_MODEL
    real = B0 * S0 + B1 * S1
    n_meta_rows = -(-(B0 + B1) * N_META // TM) * TM
    R = real + n_meta_rows
    n_meta_blocks = n_meta_rows // N_META
    g0 = _Group(B0, S0, 0, 0, real, B0)
    g1 = _Group(B1, S1, B0 * S0, B0, real, n_meta_blocks - B0)
    s_max = max(S0, S1)

    h = _embed_ln(x_prompt.reshape(B0 * S0, D), x_sample.reshape(B1 * S1, D),
                  jnp.tile(meta_tokens, (TM // N_META, 1)), ln_in_g.reshape(1, D), ln_in_b.reshape(1, D), R)

    tm = 2 * TM if all(v % (2 * TM) == 0 for v in (S0, S1, n_meta_rows)) else TM
    cos_tab, sin_tab = _rope_tables(s_max, tm)
    n0, n1 = B0 * S0 // tm, real // tm
    p0, p1, pm = S0 // tm, S1 // tm, s_max // tm

    def pos_block(i):
        return jnp.where(i < n0, i % p0, jnp.where(i < n1, (i - n0) % p1, pm))

    bband, bmeta, bq_meta, bq_blk = (t * LOG2E for t in _t5_tables(t5_table, s_max))
    bmeta_wide = jnp.pad(bmeta, ((0, 0), (0, 0), (0, LANES - N_META)), constant_values=NEG_INF)
    ones_bd = jnp.asarray(np.kron(np.eye(N_HEADS), np.ones((HEAD_DIM, HEAD_DIM))), BF16)

    qs, ks, vs = [], [], []
    for n in range(N_MIXERS):
        off = n * QKV_WIDTH
        qs.append(w_in[:, :, off:off + MIX_WIDTH])
        ks.append(w_in[:, :, off + MIX_WIDTH:off + MIX_WIDTH + KV_WIDTH])
        vs.append(w_in[:, :, off + MIX_WIDTH + KV_WIDTH:off + QKV_WIDTH])
    wr_t = jnp.swapaxes(w_router, 1, 2)
    wr_hi = wr_t.astype(BF16)
    layers = dict(
        w_qkv=jnp.concatenate(qs + ks + vs, axis=2).astype(BF16),
        w_gate=w_in[:, :, N_MIXERS * QKV_WIDTH:].astype(BF16),
        q_gain=jnp.tile(q_gain, (1, N_HEADS)).reshape(depth, 1, MIX_WIDTH),
        k_gain=jnp.tile(k_gain, (1, N_KV_HEADS)).reshape(depth, 1, KV_WIDTH),
        sink=sink.astype(F32) * LOG2E,
        na_bias=jax.vmap(_na_bias_cases)(na_rpb) * LOG2E,
        na_mbias=na_meta_bias.astype(F32) * LOG2E,
        na_mbias_wide=jnp.pad(na_meta_bias.astype(F32) * LOG2E, ((0, 0), (0, 0), (0, LANES - N_META)),
                              constant_values=NEG_INF),
        w_branch=w_branch.astype(BF16),
        w_out=w_out.astype(BF16),
        ln1_g=ln1_g.reshape(depth, 1, D), ln1_b=ln1_b.reshape(depth, 1, D),
        wr_hi=wr_hi, wr_lo=(wr_t - wr_hi.astype(F32)).astype(BF16),
        rbias=router_bias.astype(F32).reshape(depth, N_EXPERTS, 1),
        wgu=w_expert_gate_up.astype(BF16), wd=w_expert_down.astype(BF16),
        wsgu=w_shared_gate_up.astype(BF16), wsd=w_shared_down.astype(BF16),
        ln2_g=ln2_g.reshape(depth, 1, D), ln2_b=ln2_b.reshape(depth, 1, D),
    )

    def layer(h, p):
        q_all, k_all, v_all, v_ones = _inproj(h, p["w_qkv"], cos_tab, sin_tab, p["q_gain"], p["k_gain"],
                                      ones_bd, pos_block, tm)
        o = None
        for grp in (g0, g1):
            o = _global_attn(grp, q_all, k_all, v_ones, o)
            o = _window_attn(grp, q_all, k_all, v_all, p["sink"], bband, bmeta_wide[:, :grp.S], o)
            o = _na_attn(grp, q_all, k_all, v_all, p["na_bias"], p["na_mbias_wide"], o)
        for grp in (g0, g1):
            o = _meta_attn(grp, q_all, k_all, v_all, p["sink"], bq_meta, bq_blk, p["na_mbias"], o)
        h1, gates = _merge(h, o, p["w_gate"], p["w_branch"], p["w_out"], p["ln1_g"], p["ln1_b"],
                           p["wr_hi"], p["wr_lo"], p["rbias"], alpha, tm)
        h2 = _moe(h1, gates, p["wgu"], p["wd"], p["wsgu"], p["wsd"], p["ln2_g"], p["ln2_b"],
                  alpha)
        return h2, None

    h, _ = lax.scan(layer, h, layers)
    y_prompt = h[:B0 * S0].reshape(B0, S0, D)
    y_sample = h[B0 * S0:real].reshape(B1, S1, D)
    return (y_prompt, y_sample)
```

```python
import functools
import math

import numpy as np
import jax
import jax.numpy as jnp
from jax import lax
from jax.experimental import pallas as pl
from jax.experimental.pallas import tpu as pltpu

F32 = jnp.float32
BF16 = jnp.bfloat16

D_MODEL = 1024
HEAD_DIM = 64
N_HEADS = 8
N_KV_HEADS = 2
GROUP = N_HEADS // N_KV_HEADS
MIX_WIDTH = N_HEADS * HEAD_DIM
KV_WIDTH = N_KV_HEADS * HEAD_DIM
N_MIXERS = 3
QKV_WIDTH = MIX_WIDTH + 2 * KV_WIDTH
N_META = 16
GRID_W = 64
BLOCK = 128
WINDOW = 128
NA_ROWS = 8
NA_COLS = 16
T5_BUCKETS = 32
T5_MAX_DIST = 128
ROPE_THETA = 10000.0
N_EXPERTS = 64
TOP_K = 8
N_EXPERT_GROUPS = 8
TOPK_GROUPS = 4
D_EXPERT = 256
ROUTED_SCALE = 2.5
NEG_INF = -1e30
LOG2E = math.log2(math.e)
LANES = 128

TM = 512
TQ_GLOBAL = 256
TK_GLOBAL = 512
NA_QROWS = 8
WIN_QBLOCKS = 4
VMEM_LIMIT = 56 * 1024 * 1024


def _cparams(sem):
    return pltpu.CompilerParams(dimension_semantics=sem, vmem_limit_bytes=VMEM_LIMIT)


def _dot(a, b):
    return jnp.dot(a, b, preferred_element_type=F32)


def _dot_nt(a, b):
    return lax.dot_general(a, b, (((1,), (1,)), ((), ())), preferred_element_type=F32)


def _split_bf16(x):
    hi = x.astype(BF16)
    lo = (x - hi.astype(F32)).astype(BF16)
    return hi, lo


def _layer_norm(x, g, b):
    mu = jnp.mean(x, axis=-1, keepdims=True)
    xc = x - mu
    var = jnp.mean(xc * xc, axis=-1, keepdims=True)
    return xc * lax.rsqrt(var + 1e-5) * g + b


def _embed_ln_kernel(x0_ref, x1_ref, xm_ref, g_ref, b_ref, o_ref, *, n0, n1):
    i = pl.program_id(0)
    for src, pred in ((x0_ref, i < n0), (x1_ref, (i >= n0) & (i < n0 + n1)), (xm_ref, i >= n0 + n1)):
        @pl.when(pred)
        def _(src=src):
            o_ref[...] = _layer_norm(src[...], g_ref[...], b_ref[...])


def _embed_ln(x0, x1, meta_tile, g, b, n_rows):
    n0, n1 = x0.shape[0] // TM, x1.shape[0] // TM
    return pl.pallas_call(
        functools.partial(_embed_ln_kernel, n0=n0, n1=n1),
        grid=(n_rows // TM,),
        in_specs=[pl.BlockSpec((TM, D_MODEL), lambda i: (jnp.minimum(i, n0 - 1), 0)),
                  pl.BlockSpec((TM, D_MODEL), lambda i: (jnp.clip(i - n0, 0, n1 - 1), 0)),
                  pl.BlockSpec((TM, D_MODEL), lambda i: (0, 0)),
                  pl.BlockSpec((1, D_MODEL), lambda i: (0, 0)),
                  pl.BlockSpec((1, D_MODEL), lambda i: (0, 0))],
        out_specs=pl.BlockSpec((TM, D_MODEL), lambda i: (i, 0)),
        out_shape=jax.ShapeDtypeStruct((n_rows, D_MODEL), F32),
        compiler_params=_cparams(("arbitrary",)),
        name="embed_ln",
    )(x0, x1, meta_tile, g, b)


def _rope_slot(x, cos, sin_signed, first_half):
    fwd = pltpu.roll(x, LANES - 16, 1)
    bwd = pltpu.roll(x, 16, 1)
    return x * cos + jnp.where(first_half, fwd, bwd) * sin_signed


def _head_rms(x, ones_bd, gain):
    hi, lo = _split_bf16(x * x)
    ss = _dot(hi, ones_bd) + _dot(lo, ones_bd)
    return x * lax.rsqrt(ss * (1.0 / HEAD_DIM) + 1e-6) * gain


def _inproj_kernel(h_ref, w_ref, cos_ref, sin_ref, qg_ref, kg_ref, ones_ref, q_ref, k_ref, v_ref, v1_ref):
    x = h_ref[...].astype(BF16)
    cos = cos_ref[...]
    sin = sin_ref[...]
    lane = lax.broadcasted_iota(jnp.int32, cos.shape, 1)
    first_half = (lane % 32) < 16
    scale = HEAD_DIM ** -0.5
    qw = N_MIXERS * MIX_WIDTH
    qa = _head_rms(_dot(x, w_ref[:, 0:MIX_WIDTH]), ones_ref[...], qg_ref[...])
    for s in range(MIX_WIDTH // LANES):
        sl = slice(s * LANES, (s + 1) * LANES)
        q_ref[:, sl] = (_rope_slot(qa[:, sl], cos, sin, first_half) * (scale * LOG2E)).astype(BF16)
    ka = _head_rms(_dot(x, w_ref[:, qw:qw + KV_WIDTH]), ones_ref[0:LANES, 0:LANES], kg_ref[...])
    k_ref[:, 0:KV_WIDTH] = _rope_slot(ka, cos, sin, first_half).astype(BF16)
    for n in range(1, N_MIXERS):
        q_ref[:, n * MIX_WIDTH:(n + 1) * MIX_WIDTH] = (
            _dot(x, w_ref[:, n * MIX_WIDTH:(n + 1) * MIX_WIDTH]) * (scale * LOG2E)).astype(BF16)
        k_ref[:, n * KV_WIDTH:(n + 1) * KV_WIDTH] = _dot(
            x, w_ref[:, qw + n * KV_WIDTH:qw + (n + 1) * KV_WIDTH]).astype(BF16)
    vw = qw + N_MIXERS * KV_WIDTH
    v = _dot(x, w_ref[:, vw:vw + N_MIXERS * KV_WIDTH])
    v_ref[...] = v.astype(BF16)
    va = v[:, 0:KV_WIDTH]
    lo = lane < HEAD_DIM
    v1_ref[:, 0:LANES] = jnp.where(lo, va, 1.0).astype(BF16)
    v1_ref[:, LANES:2 * LANES] = jnp.where(lo, pltpu.roll(va, HEAD_DIM, 1), 1.0).astype(BF16)


def _inproj(h, w_qkv, cos_tab, sin_tab, q_gain, k_gain, ones_bd, pos_block, tm):
    R = h.shape[0]
    const = lambda i: (0, 0)
    return pl.pallas_call(
        _inproj_kernel,
        grid=(R // tm,),
        in_specs=[pl.BlockSpec((tm, D_MODEL), lambda i: (i, 0)),
                  pl.BlockSpec(w_qkv.shape, const),
                  pl.BlockSpec((tm, LANES), lambda i: (pos_block(i), 0)),
                  pl.BlockSpec((tm, LANES), lambda i: (pos_block(i), 0)),
                  pl.BlockSpec((1, MIX_WIDTH), const),
                  pl.BlockSpec((1, KV_WIDTH), const),
                  pl.BlockSpec((MIX_WIDTH, MIX_WIDTH), const)],
        out_specs=[pl.BlockSpec((tm, N_MIXERS * MIX_WIDTH), lambda i: (i, 0)),
                   pl.BlockSpec((tm, N_MIXERS * KV_WIDTH), lambda i: (i, 0)),
                   pl.BlockSpec((tm, N_MIXERS * KV_WIDTH), lambda i: (i, 0)),
                   pl.BlockSpec((tm, N_KV_HEADS * LANES), lambda i: (i, 0))],
        out_shape=[jax.ShapeDtypeStruct((R, N_MIXERS * MIX_WIDTH), BF16),
                   jax.ShapeDtypeStruct((R, N_MIXERS * KV_WIDTH), BF16),
                   jax.ShapeDtypeStruct((R, N_MIXERS * KV_WIDTH), BF16),
                   jax.ShapeDtypeStruct((R, N_KV_HEADS * LANES), BF16)],
        compiler_params=_cparams(("parallel",)),
        name="inproj",
    )(h, w_qkv, cos_tab, sin_tab, q_gain, k_gain, ones_bd)


def _group_queries(q, j):
    lane = lax.broadcasted_iota(jnp.int32, (q.shape[0], LANES), 1)
    keep = (lane < HEAD_DIM) if j == 0 else (lane >= HEAD_DIM)
    parts = []
    for hh in range(GROUP):
        h = GROUP * j + hh
        slot = q[:, (h // 2) * LANES:(h // 2 + 1) * LANES]
        if h % 2 != j:
            slot = pltpu.roll(slot, HEAD_DIM, 1)
        parts.append(jnp.where(keep, slot, 0.0))
    return jnp.concatenate(parts, axis=0).astype(BF16)


def _ungroup_outputs(out, j, T):
    lane = lax.broadcasted_iota(jnp.int32, (T, LANES), 1)
    lo = lane < HEAD_DIM
    slots = []
    for s in range(2):
        even = out[(2 * s) * T:(2 * s + 1) * T]
        odd = out[(2 * s + 1) * T:(2 * s + 2) * T]
        if j == 0:
            slots.append(jnp.where(lo, even, pltpu.roll(odd, HEAD_DIM, 1)))
        else:
            slots.append(jnp.where(lo, pltpu.roll(even, HEAD_DIM, 1), odd))
    return jnp.concatenate(slots, axis=1)


def _fold_lanes(op, *parts):
    cols = [p[:, c * LANES:(c + 1) * LANES] for p in parts for c in range(p.shape[1] // LANES)]
    return functools.reduce(op, cols)


def _row_max(*parts):
    return jnp.max(_fold_lanes(jnp.maximum, *parts), axis=-1, keepdims=True)


def _row_sum(*parts):
    return jnp.sum(_fold_lanes(jnp.add, *parts), axis=-1, keepdims=True)


def _pad_meta_rows(x):
    return jnp.concatenate([x, jnp.zeros((LANES - N_META, x.shape[1]), x.dtype)], axis=0)


def _head_rows(vals, T):
    return jnp.concatenate([jnp.broadcast_to(v, (T, v.shape[-1])) for v in vals], axis=0)


def _global_kernel(q_ref, k_ref, v_ref, km_ref, vm_ref, *rest, S):
    o_ref = rest[-1]
    T = q_ref.shape[0]
    q = q_ref[...].astype(F32)
    km = km_ref[...]
    qs, state = [], []
    for j in range(N_KV_HEADS):
        qj = _group_queries(q, j)
        s_m = _dot_nt(qj, km)
        m0 = jnp.max(s_m, axis=-1, keepdims=True)
        p_m = jnp.exp2((s_m - m0).astype(BF16))
        qs.append(qj)
        state.append((m0, _dot(p_m, vm_ref[:, j * LANES:(j + 1) * LANES])))
    for c in range(S // TK_GLOBAL):
        kc = k_ref[c * TK_GLOBAL:(c + 1) * TK_GLOBAL, :]
        for j in range(N_KV_HEADS):
            m, acc = state[j]
            s = _dot_nt(qs[j], kc)
            m_new = jnp.maximum(m, jnp.max(s, axis=-1, keepdims=True))
            p = jnp.exp2((s - m_new).astype(BF16))
            vc = v_ref[c * TK_GLOBAL:(c + 1) * TK_GLOBAL, j * LANES:(j + 1) * LANES]
            state[j] = (m_new, jnp.exp2(m - m_new) * acc + _dot(p, vc))
    for j in range(N_KV_HEADS):
        acc = state[j][1]
        out = acc / pltpu.roll(acc, HEAD_DIM, 1)
        o_ref[:, j * 2 * LANES:(j + 1) * 2 * LANES] = _ungroup_outputs(out, 0, T).astype(BF16)


def _window_kernel(sink_ref, q_ref, kp_ref, kc_ref, kn_ref, vp_ref, vc_ref, vn_ref, km_ref, vm_ref,
                   bband_ref, bmeta_ref, *rest, nb):
    o_ref = rest[-1]
    i = pl.program_id(1)
    T = BLOCK
    k_span = jnp.concatenate([kp_ref[...], kc_ref[...], kn_ref[...]], axis=0)
    v_span = jnp.concatenate([vp_ref[...], vc_ref[...], vn_ref[...]], axis=0)
    col = lax.broadcasted_iota(jnp.int32, (1, 3 * BLOCK), 1)
    km = _pad_meta_rows(km_ref[...])
    vm = _pad_meta_rows(vm_ref[...])
    for u in range(WIN_QBLOCKS):
        q = q_ref[u * BLOCK:(u + 1) * BLOCK, :].astype(F32)
        kband = k_span[u * BLOCK:(u + 3) * BLOCK]
        vband = v_span[u * BLOCK:(u + 3) * BLOCK]
        in_range = None
        if u == 0:
            in_range = (col >= BLOCK) | (i > 0)
        if u == WIN_QBLOCKS - 1:
            after = (col < 2 * BLOCK) | (i < nb - 1)
            in_range = after if in_range is None else in_range & after
        for j in range(N_KV_HEADS):
            qj = _group_queries(q, j)
            hs = slice(GROUP * j, GROUP * (j + 1))
            s_b = _dot_nt(qj, kband) + bband_ref[hs].reshape(GROUP * T, 3 * BLOCK)
            if in_range is not None:
                s_b = jnp.where(in_range, s_b, NEG_INF)
            s_m = _dot_nt(qj, km) + bmeta_ref[hs, u * BLOCK:(u + 1) * BLOCK].reshape(GROUP * T, LANES)
            sink = jnp.concatenate([jnp.full((T, 1), sink_ref[GROUP * j + hh], F32) for hh in range(GROUP)], axis=0)
            m = jnp.maximum(_row_max(s_b, s_m), sink)
            e_b = jnp.exp2(s_b - m)
            e_m = jnp.exp2(s_m - m)
            denom = _row_sum(e_b, e_m) + jnp.exp2(sink - m)
            acc = _dot(e_b.astype(BF16), vband) + _dot(e_m.astype(BF16), vm)
            o_ref[u * BLOCK:(u + 1) * BLOCK, j * 2 * LANES:(j + 1) * 2 * LANES] = (
                _ungroup_outputs(acc / denom, j, T).astype(BF16))


def _na_kernel(q_ref, k_ref, v_ref, km_ref, vm_ref, bias_ref, mbias_ref, *rest, rows):
    o_ref = rest[-1]
    blk = pl.program_id(1)
    W = GRID_W
    nkeys = NA_ROWS * W
    km = _pad_meta_rows(km_ref[...])
    vm = _pad_meta_rows(vm_ref[...])
    mb = _head_rows([mbias_ref[h:h + 1, :] for h in range(N_HEADS)], W)
    half = GROUP * W
    for rr in range(NA_QROWS):
        r = blk * NA_QROWS + rr
        rs = jnp.clip(r - NA_ROWS // 2, 0, rows - NA_ROWS)
        delta = r - rs
        koff = pl.multiple_of(rs * W, W)
        q = q_ref[rr * W:(rr + 1) * W, :].astype(F32)
        q8 = jnp.concatenate([_group_queries(q, j) for j in range(N_KV_HEADS)], axis=0)
        kw = k_ref[pl.ds(koff, nkeys), :]
        vw = v_ref[pl.ds(koff, nkeys), :]
        s_w = _dot_nt(q8, kw) + bias_ref[delta].reshape(N_HEADS * W, nkeys)
        s_m = _dot_nt(q8, km) + mb
        m = _row_max(s_w, s_m)
        e_w = jnp.exp2(s_w - m)
        e_m = jnp.exp2(s_m - m)
        out = (_dot(e_w.astype(BF16), vw) + _dot(e_m.astype(BF16), vm)) / _row_sum(e_w, e_m)
        for j in range(N_KV_HEADS):
            o_ref[rr * W:(rr + 1) * W, j * 2 * LANES:(j + 1) * 2 * LANES] = (
                _ungroup_outputs(out[j * half:(j + 1) * half], j, W).astype(BF16))


def _meta_kernel(sink_ref, q_ref, ka_ref, va_ref, kb_ref, vb_ref, km_ref, vm_ref,
                 bq_meta_ref, bq_blk_ref, mbias_ref, o_in_ref, o_ref):
    del o_in_ref
    T = N_META
    q = q_ref[...].astype(F32)
    km_all = km_ref[...]
    vm_all = vm_ref[...]

    def finish(n, j, acc, denom):
        lo = n * MIX_WIDTH + j * 2 * LANES
        o_ref[:, lo:lo + 2 * LANES] = _ungroup_outputs(acc / denom, j, T).astype(BF16)

    for j in range(N_KV_HEADS):
        hs = slice(GROUP * j, GROUP * (j + 1))
        qj = _group_queries(q[:, 0:MIX_WIDTH], j)
        km, vm = km_all[:, 0:KV_WIDTH], vm_all[:, 0:KV_WIDTH]
        s_r = _dot_nt(qj, ka_ref[...])
        s_m = _dot_nt(qj, km)
        m = jnp.maximum(jnp.max(s_r, axis=-1, keepdims=True), jnp.max(s_m, axis=-1, keepdims=True))
        e_r = jnp.exp2(s_r - m)
        e_m = jnp.exp2(s_m - m)
        denom = jnp.sum(e_r, axis=-1, keepdims=True) + jnp.sum(e_m, axis=-1, keepdims=True)
        finish(0, j, _dot(e_r.astype(BF16), va_ref[...]) + _dot(e_m.astype(BF16), vm), denom)
        qj = _group_queries(q[:, MIX_WIDTH:2 * MIX_WIDTH], j)
        km, vm = km_all[:, KV_WIDTH:2 * KV_WIDTH], vm_all[:, KV_WIDTH:2 * KV_WIDTH]
        s_r = _dot_nt(qj, kb_ref[...]) + bq_blk_ref[hs].reshape(GROUP * T, BLOCK)
        s_m = _dot_nt(qj, km) + bq_meta_ref[hs].reshape(GROUP * T, N_META)
        sink = jnp.concatenate([jnp.full((T, 1), sink_ref[GROUP * j + hh], F32) for hh in range(GROUP)], axis=0)
        m = jnp.maximum(jnp.maximum(jnp.max(s_r, axis=-1, keepdims=True),
                                    jnp.max(s_m, axis=-1, keepdims=True)), sink)
        e_r = jnp.exp2(s_r - m)
        e_m = jnp.exp2(s_m - m)
        denom = (jnp.sum(e_r, axis=-1, keepdims=True) + jnp.sum(e_m, axis=-1, keepdims=True)
                 + jnp.exp2(sink - m))
        finish(1, j, _dot(e_r.astype(BF16), vb_ref[...]) + _dot(e_m.astype(BF16), vm), denom)
        qj = _group_queries(q[:, 2 * MIX_WIDTH:3 * MIX_WIDTH], j)
        km, vm = km_all[:, 2 * KV_WIDTH:3 * KV_WIDTH], vm_all[:, 2 * KV_WIDTH:3 * KV_WIDTH]
        mb = _head_rows([mbias_ref[GROUP * j + hh:GROUP * j + hh + 1, :] for hh in range(GROUP)], T)
        s_m = _dot_nt(qj, km) + mb
        m = jnp.max(s_m, axis=-1, keepdims=True)
        e_m = jnp.exp2(s_m - m)
        finish(2, j, _dot(e_m.astype(BF16), vm), jnp.sum(e_m, axis=-1, keepdims=True))


class _Group:
    def __init__(self, B, S, real_base, meta_batch_base, meta_base, n_meta_blocks):
        self.B, self.S = B, S
        self.real_base = real_base
        self.meta_blk0 = meta_base // N_META + meta_batch_base
        self.n_meta_blocks = n_meta_blocks
        assert real_base % S == 0 and S % TM == 0 and meta_base % N_META == 0


def _alias_args(o_prev, n_inputs):
    if o_prev is None:
        return [], [], {}
    return [o_prev], [pl.BlockSpec(memory_space=pl.ANY)], {n_inputs: 0}


def _global_attn(grp, q_all, k_all, v_ones, o_prev):
    B, S = grp.B, grp.S
    nq = S // TQ_GLOBAL
    qb0 = grp.real_base // TQ_GLOBAL
    sb0 = grp.real_base // S
    mb0 = grp.meta_blk0
    in_specs = [pl.BlockSpec((TQ_GLOBAL, MIX_WIDTH), lambda b, i: (qb0 + b * nq + i, 0)),
                pl.BlockSpec((S, KV_WIDTH), lambda b, i: (sb0 + b, 0)),
                pl.BlockSpec((S, N_KV_HEADS * LANES), lambda b, i: (sb0 + b, 0)),
                pl.BlockSpec((N_META, KV_WIDTH), lambda b, i: (mb0 + b, 0)),
                pl.BlockSpec((N_META, N_KV_HEADS * LANES), lambda b, i: (mb0 + b, 0))]
    extra, extra_specs, aliases = _alias_args(o_prev, len(in_specs))
    return pl.pallas_call(
        functools.partial(_global_kernel, S=S),
        grid=(B, nq),
        in_specs=in_specs + extra_specs,
        out_specs=pl.BlockSpec((TQ_GLOBAL, MIX_WIDTH), lambda b, i: (qb0 + b * nq + i, 0)),
        out_shape=jax.ShapeDtypeStruct((q_all.shape[0], N_MIXERS * MIX_WIDTH), BF16),
        input_output_aliases=aliases,
        compiler_params=_cparams(("parallel", "arbitrary")),
        name="mixer_global",
    )(q_all, k_all, v_ones, k_all, v_ones, *extra)


def _window_attn(grp, q_all, k_all, v_all, sink, bband, bmeta, o_prev):
    B, S = grp.B, grp.S
    span = WIN_QBLOCKS * BLOCK
    nb, ns = S // BLOCK, S // span
    qb0 = grp.real_base // BLOCK
    sp0 = grp.real_base // span
    mb0 = grp.meta_blk0
    cur = lambda b, i, sink: (sp0 + b * ns + i, 1)
    prv = lambda b, i, sink: (qb0 + b * nb + jnp.maximum(i * WIN_QBLOCKS - 1, 0), 1)
    nxt = lambda b, i, sink: (qb0 + b * nb + jnp.minimum((i + 1) * WIN_QBLOCKS, nb - 1), 1)
    met = lambda b, i, sink: (mb0 + b, 1)
    edge = lambda im: pl.BlockSpec((BLOCK, KV_WIDTH), im)
    mid = pl.BlockSpec((span, KV_WIDTH), cur)
    in_specs = [pl.BlockSpec((span, MIX_WIDTH), cur),
                edge(prv), mid, edge(nxt), edge(prv), mid, edge(nxt),
                pl.BlockSpec((N_META, KV_WIDTH), met), pl.BlockSpec((N_META, KV_WIDTH), met),
                pl.BlockSpec((N_HEADS, BLOCK, 3 * BLOCK), lambda b, i, sink: (0, 0, 0)),
                pl.BlockSpec((N_HEADS, span, LANES), lambda b, i, sink: (0, i, 0))]
    extra, extra_specs, aliases = _alias_args(o_prev, len(in_specs) + 1)
    return pl.pallas_call(
        functools.partial(_window_kernel, nb=ns),
        grid_spec=pltpu.PrefetchScalarGridSpec(
            num_scalar_prefetch=1,
            grid=(B, ns),
            in_specs=in_specs + extra_specs,
            out_specs=pl.BlockSpec((span, MIX_WIDTH), cur)),
        out_shape=jax.ShapeDtypeStruct((q_all.shape[0], N_MIXERS * MIX_WIDTH), BF16),
        input_output_aliases=aliases,
        compiler_params=_cparams(("parallel", "arbitrary")),
        name="mixer_window",
    )(sink, q_all, k_all, k_all, k_all, v_all, v_all, v_all, k_all, v_all, bband, bmeta, *extra)


def _na_attn(grp, q_all, k_all, v_all, na_bias, na_mbias, o_prev):
    B, S = grp.B, grp.S
    rows = S // GRID_W
    tq = NA_QROWS * GRID_W
    nq = S // tq
    qb0 = grp.real_base // tq
    sb0 = grp.real_base // S
    mb0 = grp.meta_blk0
    in_specs = [pl.BlockSpec((tq, MIX_WIDTH), lambda b, i: (qb0 + b * nq + i, 2)),
                pl.BlockSpec((S, KV_WIDTH), lambda b, i: (sb0 + b, 2)),
                pl.BlockSpec((S, KV_WIDTH), lambda b, i: (sb0 + b, 2)),
                pl.BlockSpec((N_META, KV_WIDTH), lambda b, i: (mb0 + b, 2)),
                pl.BlockSpec((N_META, KV_WIDTH), lambda b, i: (mb0 + b, 2)),
                pl.BlockSpec(na_bias.shape, lambda b, i: (0, 0, 0, 0)),
                pl.BlockSpec(na_mbias.shape, lambda b, i: (0, 0))]
    extra, extra_specs, aliases = _alias_args(o_prev, len(in_specs))
    return pl.pallas_call(
        functools.partial(_na_kernel, rows=rows),
        grid=(B, nq),
        in_specs=in_specs + extra_specs,
        out_specs=pl.BlockSpec((tq, MIX_WIDTH), lambda b, i: (qb0 + b * nq + i, 2)),
        out_shape=jax.ShapeDtypeStruct((q_all.shape[0], N_MIXERS * MIX_WIDTH), BF16),
        input_output_aliases=aliases,
        compiler_params=_cparams(("parallel", "arbitrary")),
        name="mixer_neighbourhood",
    )(q_all, k_all, v_all, k_all, v_all, na_bias, na_mbias, *extra)


def _meta_attn(grp, q_all, k_all, v_all, sink, bq_meta, bq_blk, na_mbias, o_prev):
    B, S = grp.B, grp.S
    sb0 = grp.real_base // S
    bb0 = grp.real_base // BLOCK
    nb = S // BLOCK
    mb0 = grp.meta_blk0
    clamp = lambda b: jnp.minimum(b, B - 1)
    mrow = lambda b, sink: (mb0 + clamp(b), 0)
    in_specs = [pl.BlockSpec((N_META, N_MIXERS * MIX_WIDTH), mrow),
                pl.BlockSpec((S, KV_WIDTH), lambda b, sink: (sb0 + clamp(b), 0)),
                pl.BlockSpec((S, KV_WIDTH), lambda b, sink: (sb0 + clamp(b), 0)),
                pl.BlockSpec((BLOCK, KV_WIDTH), lambda b, sink: (bb0 + clamp(b) * nb, 1)),
                pl.BlockSpec((BLOCK, KV_WIDTH), lambda b, sink: (bb0 + clamp(b) * nb, 1)),
                pl.BlockSpec((N_META, N_MIXERS * KV_WIDTH), mrow),
                pl.BlockSpec((N_META, N_MIXERS * KV_WIDTH), mrow),
                pl.BlockSpec(bq_meta.shape, lambda b, sink: (0, 0, 0)),
                pl.BlockSpec(bq_blk.shape, lambda b, sink: (0, 0, 0)),
                pl.BlockSpec(na_mbias.shape, lambda b, sink: (0, 0)),
                pl.BlockSpec(memory_space=pl.ANY)]
    return pl.pallas_call(
        _meta_kernel,
        grid_spec=pltpu.PrefetchScalarGridSpec(
            num_scalar_prefetch=1,
            grid=(grp.n_meta_blocks,),
            in_specs=in_specs,
            out_specs=pl.BlockSpec((N_META, N_MIXERS * MIX_WIDTH), lambda b, sink: (mb0 + b, 0))),
        out_shape=jax.ShapeDtypeStruct((q_all.shape[0], N_MIXERS * MIX_WIDTH), BF16),
        input_output_aliases={len(in_specs): 0},
        compiler_params=_cparams(("arbitrary",)),
        name="mixer_meta_queries",
    )(sink, q_all, k_all, v_all, k_all, v_all, k_all, v_all, bq_meta, bq_blk, na_mbias, o_prev)


def _route(h1, wr_hi, wr_lo, rbias):
    T = h1.shape[0]
    x_hi, x_lo = _split_bf16(h1)
    logits = _dot_nt(wr_hi, x_hi) + _dot_nt(wr_hi, x_lo) + _dot_nt(wr_lo, x_hi)
    scores = 1.0 / (1.0 + jnp.exp(-logits))
    sel = scores + rbias
    per_group = N_EXPERTS // N_EXPERT_GROUPS
    sel3 = sel.reshape(N_EXPERT_GROUPS, per_group, T)
    idx3 = lax.broadcasted_iota(jnp.int32, sel3.shape, 1).astype(F32)
    m1 = jnp.max(sel3, axis=1, keepdims=True)
    first = jnp.min(jnp.where(sel3 == m1, idx3, float(per_group)), axis=1, keepdims=True)
    m2 = jnp.max(jnp.where(idx3 == first, -jnp.inf, sel3), axis=1, keepdims=True)
    gscore = (m1 + m2).reshape(N_EXPERT_GROUPS, T)

    def rank_of(vals):
        idx = lax.broadcasted_iota(jnp.int32, vals.shape, 0)
        rank = jnp.zeros(vals.shape, F32)
        for r in range(vals.shape[0]):
            row = vals[r:r + 1, :]
            ge = jnp.where(row >= vals, 1.0, 0.0)
            gt = jnp.where(row > vals, 1.0, 0.0)
            rank = rank + jnp.where(idx > r, ge, gt)
        return rank

    gkeep = jnp.where(rank_of(gscore) < TOPK_GROUPS, 1.0, 0.0)
    ekeep = jnp.broadcast_to(gkeep.reshape(N_EXPERT_GROUPS, 1, T), sel3.shape).reshape(N_EXPERTS, T)
    masked = jnp.where(ekeep > 0.5, sel, NEG_INF)
    eidx = lax.broadcasted_iota(jnp.int32, masked.shape, 0).astype(F32)
    chosen = jnp.zeros(masked.shape, F32)
    for _ in range(TOP_K):
        best = jnp.max(masked, axis=0, keepdims=True)
        first = jnp.min(jnp.where(masked == best, eidx, float(N_EXPERTS)), axis=0, keepdims=True)
        hit = eidx == first
        chosen = jnp.where(hit, 1.0, chosen)
        masked = jnp.where(hit, -jnp.inf, masked)
    w = jnp.where(chosen > 0.5, scores, 0.0)
    return w / jnp.sum(w, axis=0, keepdims=True) * ROUTED_SCALE


def _merge_kernel(h_ref, o_ref, wg_ref, wb_ref, wo_ref, g_ref, b_ref, wrh_ref, wrl_ref, rb_ref,
                  h1_ref, gates_ref, *, alpha):
    h = h_ref[...]
    x = h.astype(BF16)
    merged = None
    for n in range(N_MIXERS):
        logit = _dot(x, wg_ref[:, n * D_MODEL:(n + 1) * D_MODEL])
        branch = _dot(o_ref[:, n * MIX_WIDTH:(n + 1) * MIX_WIDTH], wb_ref[n])
        term = branch / (1.0 + jnp.exp(-logit))
        merged = term if merged is None else merged + term
    mix = _dot(merged.astype(BF16), wo_ref[...])
    h1 = _layer_norm(alpha * h + mix, g_ref[...], b_ref[...])
    h1_ref[...] = h1
    gates_t = _route(h1, wrh_ref[...], wrl_ref[...], rb_ref[...])
    pad = jnp.zeros((LANES - N_EXPERTS, gates_t.shape[1]), F32)
    gates_ref[...] = jnp.concatenate([gates_t, pad], axis=0).T


def _merge(h, o_all, w_gate, w_branch, w_out, ln_g, ln_b, wr_hi, wr_lo, rbias, alpha, tm):
    R = h.shape[0]
    c2 = lambda i: (0, 0)
    once = pl.Buffered(1)
    return pl.pallas_call(
        functools.partial(_merge_kernel, alpha=alpha),
        grid=(R // tm,),
        in_specs=[pl.BlockSpec((tm, D_MODEL), lambda i: (i, 0)),
                  pl.BlockSpec((tm, N_MIXERS * MIX_WIDTH), lambda i: (i, 0)),
                  pl.BlockSpec(w_gate.shape, c2, pipeline_mode=once),
                  pl.BlockSpec(w_branch.shape, lambda i: (0, 0, 0), pipeline_mode=once),
                  pl.BlockSpec(w_out.shape, c2, pipeline_mode=once),
                  pl.BlockSpec((1, D_MODEL), c2),
                  pl.BlockSpec((1, D_MODEL), c2),
                  pl.BlockSpec(wr_hi.shape, c2),
                  pl.BlockSpec(wr_lo.shape, c2),
                  pl.BlockSpec(rbias.shape, c2)],
        out_specs=[pl.BlockSpec((tm, D_MODEL), lambda i: (i, 0)),
                   pl.BlockSpec((tm, LANES), lambda i: (i, 0))],
        out_shape=[jax.ShapeDtypeStruct((R, D_MODEL), F32),
                   jax.ShapeDtypeStruct((R, LANES), F32)],
        compiler_params=_cparams(("parallel",)),
        name="merge_ln_route",
    )(h, o_all, w_gate, w_branch, w_out, ln_g, ln_b, wr_hi, wr_lo, rbias)


TD = 256
CH = 16
SLOTS = 3072
NCH = SLOTS // CH
MT = 512
ME = 1024
CPM = ME // CH
XBUFS = 4
CBUFS = 4
CORE_CHUNKS = TOP_K * TD // CH
TAIL_GROUP = 8


def _swiglu_act(gu):
    g = gu[:, :D_EXPERT]
    return g / (1.0 + jnp.exp(-g)) * gu[:, D_EXPERT:]


def _slot_of_token(gates, lo_row):
    routed = gates > 0.0
    r = lax.broadcasted_iota(jnp.int32, (TD, TD), 0)
    c = lax.broadcasted_iota(jnp.int32, (TD, TD), 1)
    earlier = jnp.where(c < r, 1.0, 0.0).astype(BF16)
    rank = _dot(earlier, jnp.where(routed, 1.0, 0.0).astype(BF16))
    return jnp.where(routed, lo_row + rank + 1.0, 0.0)


def _split64(x):
    hi = 64.0 * jnp.floor(x * (1.0 / 64.0))
    return jnp.concatenate([hi.astype(BF16), (x - hi).astype(BF16)], axis=0)


def _dispatch_kernel(h_ref, g_ref, lohi_ref, x_ref, w_ref):
    lohi = lohi_ref[0]
    lo_row, hi_row = lohi[0:1], lohi[1:2]
    gates = g_ref[...]
    slot_t = _split64(_slot_of_token(gates, lo_row).T)
    gates_t = gates.T.astype(BF16)
    x = h_ref[...].astype(BF16)
    for blk in range(SLOTS // MT):
        s = (lax.broadcasted_iota(jnp.int32, (MT, LANES), 0) + blk * MT).astype(F32)
        owner = jnp.where(s >= lo_row, jnp.where(s < hi_row, 1.0, 0.0), 0.0).astype(BF16)
        want = _dot(jnp.concatenate([owner, owner], axis=1), slot_t)
        s1 = (lax.broadcasted_iota(jnp.int32, (MT, TD), 0) + (blk * MT + 1)).astype(F32)
        hit = want == s1
        x_ref[blk * MT:(blk + 1) * MT, :] = _dot(jnp.where(hit, 1.0, 0.0).astype(BF16), x).astype(BF16)
        weight = jnp.where(hit, _dot(owner, gates_t), 0.0)
        w_ref[:, blk * MT:(blk + 1) * MT] = weight.T.astype(BF16)


def _dispatch(h1, gates, lohi):
    n = h1.shape[0] // TD
    return pl.pallas_call(
        _dispatch_kernel,
        grid=(n,),
        in_specs=[pl.BlockSpec((TD, D_MODEL), lambda i: (i, 0)),
                  pl.BlockSpec((TD, LANES), lambda i: (i, 0)),
                  pl.BlockSpec((1, 8, LANES), lambda i: (i, 0, 0))],
        out_specs=[pl.BlockSpec((SLOTS, D_MODEL), lambda i: (i, 0)),
                   pl.BlockSpec((TD, SLOTS), lambda i: (i, 0))],
        out_shape=[jax.ShapeDtypeStruct((n * SLOTS, D_MODEL), BF16),
                   jax.ShapeDtypeStruct((n * TD, SLOTS), BF16)],
        compiler_params=_cparams(("parallel",)),
        name="moe_dispatch",
    )(h1, gates, lohi)


def _chunk_gather(table_ref, first, n_chunks, src_hbm, buf, c0, sem):
    return [pltpu.make_async_copy(src_hbm.at[table_ref[first + c]], buf.at[c0 + c], sem) for c in range(n_chunks)]


def _chunk_wait(n_chunks, src_hbm, buf, c0, sem):
    for c in range(n_chunks):
        pltpu.make_async_copy(src_hbm.at[0], buf.at[c0 + c], sem).wait()


def _expert_kernel(te_ref, src_ref, nu_ref, x_hbm, wgu_ref, wd_ref, y_ref, xbuf, sem):
    del te_ref
    m = pl.program_id(0)
    n_used = nu_ref[0]

    def start(step):
        slot = step % XBUFS
        for cp in _chunk_gather(src_ref, step * CPM, CPM, x_hbm, xbuf.at[slot], 0, sem.at[slot]):
            cp.start()

    for ahead in range(XBUFS - 1):
        @pl.when((m == 0) & (ahead < n_used))
        def _(ahead=ahead):
            start(ahead)

    @pl.when(m + (XBUFS - 1) < n_used)
    def _():
        start(m + (XBUFS - 1))

    @pl.when(m < n_used)
    def _():
        slot = m % XBUFS
        _chunk_wait(CPM, x_hbm, xbuf.at[slot], 0, sem.at[slot])
        act = _swiglu_act(_dot(xbuf[slot].reshape(ME, D_MODEL), wgu_ref[0]))
        y_ref[...] = _dot(act.astype(BF16), wd_ref[0]).astype(BF16)


def _experts(x_disp, wgu, wd, tile_expert, src_chunk, n_used):
    n_steps = tile_expert.shape[0]
    return pl.pallas_call(
        _expert_kernel,
        grid_spec=pltpu.PrefetchScalarGridSpec(
            num_scalar_prefetch=3,
            grid=(n_steps,),
            in_specs=[pl.BlockSpec(memory_space=pl.ANY),
                      pl.BlockSpec((1, D_MODEL, 2 * D_EXPERT), lambda m, te, src, nu: (te[m], 0, 0)),
                      pl.BlockSpec((1, D_EXPERT, D_MODEL), lambda m, te, src, nu: (te[m], 0, 0))],
            out_specs=pl.BlockSpec((ME, D_MODEL), lambda m, te, src, nu: (jnp.minimum(m, nu[0] - 1), 0)),
            scratch_shapes=[pltpu.VMEM((XBUFS, CPM, CH, D_MODEL), BF16), pltpu.SemaphoreType.DMA((XBUFS,))]),
        out_shape=jax.ShapeDtypeStruct((n_steps * ME, D_MODEL), BF16),
        compiler_params=_cparams(("arbitrary",)),
        name="moe_experts",
    )(tile_expert, src_chunk, n_used, x_disp.reshape(-1, CH, D_MODEL), wgu, wd)


def _combine_kernel(dst_ref, nct_ref, y_hbm, w_ref, h_ref, wsgu_ref, wsd_ref, lg_ref, lb_ref, o_ref, ybuf, sem,
                    *, alpha):
    i = pl.program_id(0)
    n_tiles = pl.num_programs(0)

    def transfer(tile, wait):
        slot = tile % CBUFS
        buf, s = ybuf.at[slot], sem.at[slot]
        used = nct_ref[tile]

        def run(first, count):
            if wait:
                _chunk_wait(count, y_hbm, buf, first, s)
            else:
                for cp in _chunk_gather(dst_ref, tile * NCH + first, count, y_hbm, buf, first, s):
                    cp.start()

        run(0, CORE_CHUNKS)
        for first in range(CORE_CHUNKS, NCH, TAIL_GROUP):
            pl.when(used > first)(functools.partial(run, first, TAIL_GROUP))

    @pl.when(i == 0)
    def _():
        for slot in range(CBUFS):
            ybuf[slot] = jnp.zeros((NCH, CH, D_MODEL), BF16)
        for ahead in range(CBUFS - 1):
            pl.when(ahead < n_tiles)(functools.partial(transfer, ahead, False))

    @pl.when(i + (CBUFS - 1) < n_tiles)
    def _():
        transfer(i + (CBUFS - 1), False)

    h = h_ref[...]
    shared = _dot(_swiglu_act(_dot(h.astype(BF16), wsgu_ref[...])).astype(BF16), wsd_ref[...])
    transfer(i, True)
    routed = _dot(w_ref[...], ybuf[i % CBUFS].reshape(SLOTS, D_MODEL))
    o_ref[...] = _layer_norm(alpha * h + shared + routed, lg_ref[...], lb_ref[...])


def _combine(y_sorted, w_t, h1, dst_chunk, tile_chunks, wsgu, wsd, ln_g, ln_b, alpha):
    n = h1.shape[0] // TD
    c2 = lambda i, dst, nct: (0, 0)
    return pl.pallas_call(
        functools.partial(_combine_kernel, alpha=alpha),
        grid_spec=pltpu.PrefetchScalarGridSpec(
            num_scalar_prefetch=2,
            grid=(n,),
            in_specs=[pl.BlockSpec(memory_space=pl.ANY),
                      pl.BlockSpec((TD, SLOTS), lambda i, dst, nct: (i, 0)),
                      pl.BlockSpec((TD, D_MODEL), lambda i, dst, nct: (i, 0)),
                      pl.BlockSpec(wsgu.shape, c2),
                      pl.BlockSpec(wsd.shape, c2),
                      pl.BlockSpec((1, D_MODEL), c2),
                      pl.BlockSpec((1, D_MODEL), c2)],
            out_specs=pl.BlockSpec((TD, D_MODEL), lambda i, dst, nct: (i, 0)),
            scratch_shapes=[pltpu.VMEM((CBUFS, NCH, CH, D_MODEL), BF16), pltpu.SemaphoreType.DMA((CBUFS,))]),
        out_shape=jax.ShapeDtypeStruct((h1.shape[0], D_MODEL), F32),
        compiler_params=_cparams(("arbitrary",)),
        name="moe_combine_ln",
    )(dst_chunk, tile_chunks, y_sorted.reshape(-1, CH, D_MODEL), w_t, h1, wsgu, wsd, ln_g, ln_b)


def _routing_tables(gates):
    n = gates.shape[0] // TD
    cnt = jnp.sum((gates[:, :N_EXPERTS] > 0.0).reshape(n, TD, N_EXPERTS), axis=1, dtype=jnp.int32)
    nch = (cnt + (CH - 1)) // CH
    hi16 = jnp.cumsum(nch, axis=1)
    lo16 = hi16 - nch
    nct = hi16[:, -1:]
    pad = jnp.broadcast_to(nct, (n, LANES - N_EXPERTS))
    lohi = jnp.zeros((n, 8, LANES), F32)
    lohi = lohi.at[:, 0, :].set((jnp.concatenate([lo16, pad], axis=1) * CH).astype(F32))
    lohi = lohi.at[:, 1, :].set((jnp.concatenate([hi16, pad], axis=1) * CH).astype(F32))
    tot = jnp.sum(nch, axis=0)
    seg_len = (tot + (CPM - 1)) // CPM * CPM
    seg_end = jnp.cumsum(seg_len)
    seg_start = seg_end - seg_len
    gpos = seg_start[None, :] + jnp.cumsum(nch, axis=0) - nch
    assert (n * NCH) % CPM == 0
    n_steps = (n * NCH + N_EXPERTS * CPM) // CPM
    n_used = (seg_end[-1] // CPM).astype(jnp.int32).reshape(1)
    step = jnp.arange(n_steps, dtype=jnp.int32)
    tile_expert = jnp.sum(seg_end[None, :] // CPM <= jnp.minimum(step, n_used - 1)[:, None], axis=1, dtype=jnp.int32)
    tile_expert = jnp.minimum(tile_expert, N_EXPERTS - 1)
    exact = functools.partial(jnp.dot, precision=lax.Precision.HIGHEST)
    experts = jnp.arange(N_EXPERTS, dtype=jnp.int32)
    g = jnp.arange(n_steps * CPM, dtype=jnp.int32)
    e_of_g = jnp.minimum(jnp.sum(seg_end[None, :] <= g[:, None], axis=1, dtype=jnp.int32), N_EXPERTS - 1)
    pick_e = (e_of_g[:, None] == experts[None, :]).astype(F32)
    first = exact(pick_e, gpos.T.astype(F32))
    count = exact(pick_e, nch.T.astype(F32))
    base = exact(pick_e, (jnp.arange(n, dtype=jnp.int32)[:, None] * NCH + lo16).T.astype(F32))
    gf = g.astype(F32)[:, None]
    inside = (first <= gf) & (gf < first + count)
    src_chunk = jnp.sum(jnp.where(inside, base + gf - first, 0.0), axis=1).astype(jnp.int32)
    k = jnp.arange(NCH, dtype=jnp.int32)
    e_of_k = jnp.minimum(jnp.sum(hi16[:, None, :] <= k[None, :, None], axis=2, dtype=jnp.int32), N_EXPERTS - 1)
    pick_k = e_of_k[:, :, None] == experts[None, None, :]
    pos = jnp.sum(jnp.where(pick_k, (gpos - lo16)[:, None, :], 0), axis=2) + k[None, :]
    dst_chunk = jnp.where(k[None, :] < nct, pos, 0).astype(jnp.int32).reshape(-1)
    return lohi, tile_expert, src_chunk, n_used, dst_chunk, nct.reshape(-1)


def _moe(h1, gates, wgu, wd, wsgu, wsd, ln_g, ln_b, alpha):
    lohi, tile_expert, src_chunk, n_used, dst_chunk, tile_chunks = _routing_tables(gates)
    x_disp, w_t = _dispatch(h1, gates, lohi)
    y_sorted = _experts(x_disp, wgu, wd, tile_expert, src_chunk, n_used)
    return _combine(y_sorted, w_t, h1, dst_chunk, tile_chunks, wsgu, wsd, ln_g, ln_b, alpha)


def _t5_bucket(rel):
    half = T5_BUCKETS // 2
    max_exact = half // 2
    n = np.abs(rel)
    ratio = np.log(np.maximum(n, 1).astype(np.float32) / np.float32(max_exact))
    ratio = ratio / np.float32(math.log(T5_MAX_DIST / max_exact)) * np.float32(half - max_exact)
    large = np.minimum(max_exact + ratio.astype(np.int32), half - 1)
    return np.where(rel > 0, half, 0) + np.where(n < max_exact, n, large)


def _t5_tables(t5_table, s_max):
    def bias(rel, valid):
        onehot = np.eye(T5_BUCKETS, dtype=np.float32)[_t5_bucket(rel)]
        b = jnp.einsum("qkb,bh->hqk", jnp.asarray(onehot), t5_table.astype(F32), precision=lax.Precision.HIGHEST)
        return jnp.where(jnp.asarray(valid)[None], b, NEG_INF)

    ii = np.arange(BLOCK)[:, None]
    jj = np.arange(3 * BLOCK)[None, :]
    rel = jj - ii - BLOCK
    bband = bias(rel, np.abs(rel) <= WINDOW)
    t = np.arange(s_max)[:, None]
    m = np.arange(N_META)[None, :]
    bmeta = bias(m - (N_META + t), np.ones((s_max, N_META), bool))
    mpos = np.arange(N_META)[:, None]
    kpos = np.arange(N_META + BLOCK)[None, :]
    relq = kpos - mpos
    bq = bias(relq, (kpos < N_META) | (np.abs(relq) <= WINDOW))
    return bband, bmeta, bq[:, :, :N_META], bq[:, :, N_META:]


def _na_bias_cases(rpb):
    W = GRID_W
    c = np.arange(W)[:, None]
    kc = np.arange(W)[None, :]
    cs = np.clip(c - NA_COLS // 2, 0, W - NA_COLS)
    valid = (kc >= cs) & (kc < cs + NA_COLS)
    dc = np.clip(kc - c + (NA_COLS - 1), 0, 2 * NA_COLS - 2)
    onehot = np.eye(2 * NA_COLS - 1, dtype=np.float32)[dc]
    t = jnp.einsum("hrd,ckd->hrck", rpb.astype(F32), jnp.asarray(onehot), precision=lax.Precision.HIGHEST)
    t = jnp.where(jnp.asarray(valid), t, NEG_INF)
    cases = [jnp.transpose(t[:, NA_ROWS - 1 - d:2 * NA_ROWS - 1 - d], (0, 2, 1, 3)) for d in range(NA_ROWS)]
    return jnp.stack(cases, axis=0).reshape(NA_ROWS, N_HEADS, W, NA_ROWS * W)


def _rope_tables(s_max, n_meta_rows):
    half = HEAD_DIM // 4
    freq = ROPE_THETA ** (-jnp.arange(half, dtype=F32) / half)
    t = np.arange(s_max)
    mp = np.tile(np.arange(N_META) - N_META, n_meta_rows // N_META)
    pos_row = jnp.asarray(np.concatenate([t // GRID_W, mp]), jnp.int32).astype(F32)
    pos_col = jnp.asarray(np.concatenate([t % GRID_W, mp]), jnp.int32).astype(F32)
    ar = pos_row[:, None] * freq
    ac = pos_col[:, None] * freq
    cos = jnp.concatenate([jnp.cos(ar), jnp.cos(ar), jnp.cos(ac), jnp.cos(ac)], axis=1)
    sin = jnp.concatenate([-jnp.sin(ar), jnp.sin(ar), -jnp.sin(ac), jnp.sin(ac)], axis=1)
    return jnp.tile(cos, (1, 2)), jnp.tile(sin, (1, 2))


def kernel(x_prompt, x_sample, meta_tokens, ln_in_g, ln_in_b, t5_table, w_in, q_gain, k_gain, sink,
           na_rpb, na_meta_bias, w_branch, w_out, ln1_g, ln1_b, w_router, router_bias,
           w_expert_gate_up, w_expert_down, w_shared_gate_up, w_shared_down, ln2_g, ln2_b):
    depth = w_in.shape[0]
    alpha = (2 * depth) ** 0.25
    B0, S0, D = x_prompt.shape
    B1, S1, _ = x_sample.shape
    assert D == D_MODEL
    real = B0 * S0 + B1 * S1
    n_meta_rows = -(-(B0 + B1) * N_META // TM) * TM
    R = real + n_meta_rows
    n_meta_blocks = n_meta_rows // N_META
    g0 = _Group(B0, S0, 0, 0, real, B0)
    g1 = _Group(B1, S1, B0 * S0, B0, real, n_meta_blocks - B0)
    s_max = max(S0, S1)

    h = _embed_ln(x_prompt.reshape(B0 * S0, D), x_sample.reshape(B1 * S1, D),
                  jnp.tile(meta_tokens, (TM // N_META, 1)), ln_in_g.reshape(1, D), ln_in_b.reshape(1, D), R)

    tm = 2 * TM if all(v % (2 * TM) == 0 for v in (S0, S1, n_meta_rows)) else TM
    cos_tab, sin_tab = _rope_tables(s_max, tm)
    n0, n1 = B0 * S0 // tm, real // tm
    p0, p1, pm = S0 // tm, S1 // tm, s_max // tm

    def pos_block(i):
        return jnp.where(i < n0, i % p0, jnp.where(i < n1, (i - n0) % p1, pm))

    bband, bmeta, bq_meta, bq_blk = (t * LOG2E for t in _t5_tables(t5_table, s_max))
    bmeta_wide = jnp.pad(bmeta, ((0, 0), (0, 0), (0, LANES - N_META)), constant_values=NEG_INF)
    ones_bd = jnp.asarray(np.kron(np.eye(N_HEADS), np.ones((HEAD_DIM, HEAD_DIM))), BF16)

    qs, ks, vs = [], [], []
    for n in range(N_MIXERS):
        off = n * QKV_WIDTH
        qs.append(w_in[:, :, off:off + MIX_WIDTH])
        ks.append(w_in[:, :, off + MIX_WIDTH:off + MIX_WIDTH + KV_WIDTH])
        vs.append(w_in[:, :, off + MIX_WIDTH + KV_WIDTH:off + QKV_WIDTH])
    wr_t = jnp.swapaxes(w_router, 1, 2)
    wr_hi = wr_t.astype(BF16)
    layers = dict(
        w_qkv=jnp.concatenate(qs + ks + vs, axis=2).astype(BF16),
        w_gate=w_in[:, :, N_MIXERS * QKV_WIDTH:].astype(BF16),
        q_gain=jnp.tile(q_gain, (1, N_HEADS)).reshape(depth, 1, MIX_WIDTH),
        k_gain=jnp.tile(k_gain, (1, N_KV_HEADS)).reshape(depth, 1, KV_WIDTH),
        sink=sink.astype(F32) * LOG2E,
        na_bias=jax.vmap(_na_bias_cases)(na_rpb) * LOG2E,
        na_mbias=na_meta_bias.astype(F32) * LOG2E,
        na_mbias_wide=jnp.pad(na_meta_bias.astype(F32) * LOG2E, ((0, 0), (0, 0), (0, LANES - N_META)),
                              constant_values=NEG_INF),
        w_branch=w_branch.astype(BF16),
        w_out=w_out.astype(BF16),
        ln1_g=ln1_g.reshape(depth, 1, D), ln1_b=ln1_b.reshape(depth, 1, D),
        wr_hi=wr_hi, wr_lo=(wr_t - wr_hi.astype(F32)).astype(BF16),
        rbias=router_bias.astype(F32).reshape(depth, N_EXPERTS, 1),
        wgu=w_expert_gate_up.astype(BF16), wd=w_expert_down.astype(BF16),
        wsgu=w_shared_gate_up.astype(BF16), wsd=w_shared_down.astype(BF16),
        ln2_g=ln2_g.reshape(depth, 1, D), ln2_b=ln2_b.reshape(depth, 1, D),
    )

    def layer(h, p):
        q_all, k_all, v_all, v_ones = _inproj(h, p["w_qkv"], cos_tab, sin_tab, p["q_gain"], p["k_gain"],
                                      ones_bd, pos_block, tm)
        o = None
        for grp in (g0, g1):
            o = _global_attn(grp, q_all, k_all, v_ones, o)
            o = _window_attn(grp, q_all, k_all, v_all, p["sink"], bband, bmeta_wide[:, :grp.S], o)
            o = _na_attn(grp, q_all, k_all, v_all, p["na_bias"], p["na_mbias_wide"], o)
        for grp in (g0, g1):
            o = _meta_attn(grp, q_all, k_all, v_all, p["sink"], bq_meta, bq_blk, p["na_mbias"], o)
        h1, gates = _merge(h, o, p["w_gate"], p["w_branch"], p["w_out"], p["ln1_g"], p["ln1_b"],
                           p["wr_hi"], p["wr_lo"], p["rbias"], alpha, tm)
        h2 = _moe(h1, gates, p["wgu"], p["wd"], p["wsgu"], p["wsd"], p["ln2_g"], p["ln2_b"],
                  alpha)
        return h2, None

    h, _ = lax.scan(layer, h, layers)
    y_prompt = h[:B0 * S0].reshape(B0, S0, D)
    y_sample = h[B0 * S0:real].reshape(B1, S1, D)
    return (y_prompt, y_sample)
```

```python
import functools
import math

import numpy as np
import jax
import jax.numpy as jnp
from jax import lax
from jax.experimental import pallas as pl
from jax.experimental.pallas import tpu as pltpu

F32 = jnp.float32
BF16 = jnp.bfloat16

D_MODEL = 1024
HEAD_DIM = 64
N_HEADS = 8
N_KV_HEADS = 2
GROUP = N_HEADS // N_KV_HEADS
MIX_WIDTH = N_HEADS * HEAD_DIM
KV_WIDTH = N_KV_HEADS * HEAD_DIM
N_MIXERS = 3
QKV_WIDTH = MIX_WIDTH + 2 * KV_WIDTH
N_META = 16
GRID_W = 64
BLOCK = 128
WINDOW = 128
NA_ROWS = 8
NA_COLS = 16
T5_BUCKETS = 32
T5_MAX_DIST = 128
ROPE_THETA = 10000.0
N_EXPERTS = 64
TOP_K = 8
N_EXPERT_GROUPS = 8
TOPK_GROUPS = 4
D_EXPERT = 256
ROUTED_SCALE = 2.5
NEG_INF = -1e30
LOG2E = math.log2(math.e)
LANES = 128

TM = 512
TQ_GLOBAL = 256
TK_GLOBAL = 512
NA_QROWS = 16
WIN_QBLOCKS = 8
VMEM_LIMIT = 56 * 1024 * 1024


def _cparams(sem):
    return pltpu.CompilerParams(dimension_semantics=sem, vmem_limit_bytes=VMEM_LIMIT)


def _dot(a, b):
    return jnp.dot(a, b, preferred_element_type=F32)


def _dot_nt(a, b):
    return lax.dot_general(a, b, (((1,), (1,)), ((), ())), preferred_element_type=F32)


def _split_bf16(x):
    hi = x.astype(BF16)
    lo = (x - hi.astype(F32)).astype(BF16)
    return hi, lo


def _layer_norm(x, g, b):
    mu = jnp.mean(x, axis=-1, keepdims=True)
    xc = x - mu
    var = jnp.mean(xc * xc, axis=-1, keepdims=True)
    return xc * lax.rsqrt(var + 1e-5) * g + b


def _embed_ln_kernel(x0_ref, x1_ref, xm_ref, g_ref, b_ref, o_ref, *, n0, n1):
    i = pl.program_id(0)
    for src, pred in ((x0_ref, i < n0), (x1_ref, (i >= n0) & (i < n0 + n1)), (xm_ref, i >= n0 + n1)):
        @pl.when(pred)
        def _(src=src):
            o_ref[...] = _layer_norm(src[...], g_ref[...], b_ref[...])


def _embed_ln(x0, x1, meta_tile, g, b, n_rows):
    n0, n1 = x0.shape[0] // TM, x1.shape[0] // TM
    return pl.pallas_call(
        functools.partial(_embed_ln_kernel, n0=n0, n1=n1),
        grid=(n_rows // TM,),
        in_specs=[pl.BlockSpec((TM, D_MODEL), lambda i: (jnp.minimum(i, n0 - 1), 0)),
                  pl.BlockSpec((TM, D_MODEL), lambda i: (jnp.clip(i - n0, 0, n1 - 1), 0)),
                  pl.BlockSpec((TM, D_MODEL), lambda i: (0, 0)),
                  pl.BlockSpec((1, D_MODEL), lambda i: (0, 0)),
                  pl.BlockSpec((1, D_MODEL), lambda i: (0, 0))],
        out_specs=pl.BlockSpec((TM, D_MODEL), lambda i: (i, 0)),
        out_shape=jax.ShapeDtypeStruct((n_rows, D_MODEL), F32),
        compiler_params=_cparams(("arbitrary",)),
        name="embed_ln",
    )(x0, x1, meta_tile, g, b)


def _rope_slot(x, cos, sin_signed, first_half):
    fwd = pltpu.roll(x, LANES - 16, 1)
    bwd = pltpu.roll(x, 16, 1)
    return x * cos + jnp.where(first_half, fwd, bwd) * sin_signed


def _head_rms(x, ones_bd, gain):
    hi, lo = _split_bf16(x * x)
    ss = _dot(hi, ones_bd) + _dot(lo, ones_bd)
    return x * lax.rsqrt(ss * (1.0 / HEAD_DIM) + 1e-6) * gain


def _inproj_kernel(h_ref, w_ref, cos_ref, sin_ref, qg_ref, kg_ref, ones_ref, q_ref, k_ref, v_ref, v1_ref):
    x = h_ref[...].astype(BF16)
    cos = cos_ref[...]
    sin = sin_ref[...]
    lane = lax.broadcasted_iota(jnp.int32, cos.shape, 1)
    first_half = (lane % 32) < 16
    scale = HEAD_DIM ** -0.5
    qw = N_MIXERS * MIX_WIDTH
    qa = _head_rms(_dot(x, w_ref[:, 0:MIX_WIDTH]), ones_ref[...], qg_ref[...])
    for s in range(MIX_WIDTH // LANES):
        sl = slice(s * LANES, (s + 1) * LANES)
        q_ref[:, sl] = (_rope_slot(qa[:, sl], cos, sin, first_half) * (scale * LOG2E)).astype(BF16)
    ka = _head_rms(_dot(x, w_ref[:, qw:qw + KV_WIDTH]), ones_ref[0:LANES, 0:LANES], kg_ref[...])
    k_ref[:, 0:KV_WIDTH] = _rope_slot(ka, cos, sin, first_half).astype(BF16)
    for n in range(1, N_MIXERS):
        q_ref[:, n * MIX_WIDTH:(n + 1) * MIX_WIDTH] = (
            _dot(x, w_ref[:, n * MIX_WIDTH:(n + 1) * MIX_WIDTH]) * (scale * LOG2E)).astype(BF16)
        k_ref[:, n * KV_WIDTH:(n + 1) * KV_WIDTH] = _dot(
            x, w_ref[:, qw + n * KV_WIDTH:qw + (n + 1) * KV_WIDTH]).astype(BF16)
    vw = qw + N_MIXERS * KV_WIDTH
    v = _dot(x, w_ref[:, vw:vw + N_MIXERS * KV_WIDTH])
    v_ref[...] = v.astype(BF16)
    va = v[:, 0:KV_WIDTH]
    lo = lane < HEAD_DIM
    v1_ref[:, 0:LANES] = jnp.where(lo, va, 1.0).astype(BF16)
    v1_ref[:, LANES:2 * LANES] = jnp.where(lo, pltpu.roll(va, HEAD_DIM, 1), 1.0).astype(BF16)


def _inproj(h, w_qkv, cos_tab, sin_tab, q_gain, k_gain, ones_bd, pos_block, tm):
    R = h.shape[0]
    const = lambda i: (0, 0)
    return pl.pallas_call(
        _inproj_kernel,
        grid=(R // tm,),
        in_specs=[pl.BlockSpec((tm, D_MODEL), lambda i: (i, 0)),
                  pl.BlockSpec(w_qkv.shape, const),
                  pl.BlockSpec((tm, LANES), lambda i: (pos_block(i), 0)),
                  pl.BlockSpec((tm, LANES), lambda i: (pos_block(i), 0)),
                  pl.BlockSpec((1, MIX_WIDTH), const),
                  pl.BlockSpec((1, KV_WIDTH), const),
                  pl.BlockSpec((MIX_WIDTH, MIX_WIDTH), const)],
        out_specs=[pl.BlockSpec((tm, N_MIXERS * MIX_WIDTH), lambda i: (i, 0)),
                   pl.BlockSpec((tm, N_MIXERS * KV_WIDTH), lambda i: (i, 0)),
                   pl.BlockSpec((tm, N_MIXERS * KV_WIDTH), lambda i: (i, 0)),
                   pl.BlockSpec((tm, N_KV_HEADS * LANES), lambda i: (i, 0))],
        out_shape=[jax.ShapeDtypeStruct((R, N_MIXERS * MIX_WIDTH), BF16),
                   jax.ShapeDtypeStruct((R, N_MIXERS * KV_WIDTH), BF16),
                   jax.ShapeDtypeStruct((R, N_MIXERS * KV_WIDTH), BF16),
                   jax.ShapeDtypeStruct((R, N_KV_HEADS * LANES), BF16)],
        compiler_params=_cparams(("parallel",)),
        name="inproj",
    )(h, w_qkv, cos_tab, sin_tab, q_gain, k_gain, ones_bd)


def _group_queries(q, j):
    lane = lax.broadcasted_iota(jnp.int32, (q.shape[0], LANES), 1)
    keep = (lane < HEAD_DIM) if j == 0 else (lane >= HEAD_DIM)
    parts = []
    for hh in range(GROUP):
        h = GROUP * j + hh
        slot = q[:, (h // 2) * LANES:(h // 2 + 1) * LANES]
        if h % 2 != j:
            slot = pltpu.roll(slot, HEAD_DIM, 1)
        parts.append(jnp.where(keep, slot, 0.0))
    return jnp.concatenate(parts, axis=0).astype(BF16)


def _ungroup_outputs(out, j, T):
    lane = lax.broadcasted_iota(jnp.int32, (T, LANES), 1)
    lo = lane < HEAD_DIM
    slots = []
    for s in range(2):
        even = out[(2 * s) * T:(2 * s + 1) * T]
        odd = out[(2 * s + 1) * T:(2 * s + 2) * T]
        if j == 0:
            slots.append(jnp.where(lo, even, pltpu.roll(odd, HEAD_DIM, 1)))
        else:
            slots.append(jnp.where(lo, pltpu.roll(even, HEAD_DIM, 1), odd))
    return jnp.concatenate(slots, axis=1)


def _fold_lanes(op, *parts):
    cols = [p[:, c * LANES:(c + 1) * LANES] for p in parts for c in range(p.shape[1] // LANES)]
    return functools.reduce(op, cols)


def _row_max(*parts):
    return jnp.max(_fold_lanes(jnp.maximum, *parts), axis=-1, keepdims=True)


def _row_sum(*parts):
    return jnp.sum(_fold_lanes(jnp.add, *parts), axis=-1, keepdims=True)


def _pad_meta_rows(x):
    return jnp.concatenate([x, jnp.zeros((LANES - N_META, x.shape[1]), x.dtype)], axis=0)


def _head_rows(vals, T):
    return jnp.concatenate([jnp.broadcast_to(v, (T, v.shape[-1])) for v in vals], axis=0)


def _global_kernel(q_ref, k_ref, v_ref, km_ref, vm_ref, *rest, S):
    o_ref = rest[-1]
    T = q_ref.shape[0]
    q = q_ref[...].astype(F32)
    km = km_ref[...]
    qs, state = [], []
    for j in range(N_KV_HEADS):
        qj = _group_queries(q, j)
        s_m = _dot_nt(qj, km)
        m0 = jnp.max(s_m, axis=-1, keepdims=True)
        p_m = jnp.exp2((s_m - m0).astype(BF16))
        qs.append(qj)
        state.append((m0, _dot(p_m, vm_ref[:, j * LANES:(j + 1) * LANES])))
    for c in range(S // TK_GLOBAL):
        kc = k_ref[c * TK_GLOBAL:(c + 1) * TK_GLOBAL, :]
        for j in range(N_KV_HEADS):
            m, acc = state[j]
            s = _dot_nt(qs[j], kc)
            m_new = jnp.maximum(m, jnp.max(s, axis=-1, keepdims=True))
            p = jnp.exp2((s - m_new).astype(BF16))
            vc = v_ref[c * TK_GLOBAL:(c + 1) * TK_GLOBAL, j * LANES:(j + 1) * LANES]
            state[j] = (m_new, jnp.exp2(m - m_new) * acc + _dot(p, vc))
    for j in range(N_KV_HEADS):
        acc = state[j][1]
        out = acc / pltpu.roll(acc, HEAD_DIM, 1)
        o_ref[:, j * 2 * LANES:(j + 1) * 2 * LANES] = _ungroup_outputs(out, 0, T).astype(BF16)


def _window_kernel(sink_ref, q_ref, kp_ref, kc_ref, kn_ref, vp_ref, vc_ref, vn_ref, km_ref, vm_ref,
                   bband_ref, bmeta_ref, *rest, nb, wq):
    o_ref = rest[-1]
    i = pl.program_id(1)
    T = BLOCK
    k_span = jnp.concatenate([kp_ref[...], kc_ref[...], kn_ref[...]], axis=0)
    v_span = jnp.concatenate([vp_ref[...], vc_ref[...], vn_ref[...]], axis=0)
    col = lax.broadcasted_iota(jnp.int32, (1, 3 * BLOCK), 1)
    km = _pad_meta_rows(km_ref[...])
    vm = _pad_meta_rows(vm_ref[...])
    for u in range(wq):
        q = q_ref[u * BLOCK:(u + 1) * BLOCK, :].astype(F32)
        kband = k_span[u * BLOCK:(u + 3) * BLOCK]
        vband = v_span[u * BLOCK:(u + 3) * BLOCK]
        in_range = None
        if u == 0:
            in_range = (col >= BLOCK) | (i > 0)
        if u == wq - 1:
            after = (col < 2 * BLOCK) | (i < nb - 1)
            in_range = after if in_range is None else in_range & after
        for j in range(N_KV_HEADS):
            qj = _group_queries(q, j)
            hs = slice(GROUP * j, GROUP * (j + 1))
            s_b = _dot_nt(qj, kband) + bband_ref[hs].reshape(GROUP * T, 3 * BLOCK)
            if in_range is not None:
                s_b = jnp.where(in_range, s_b, NEG_INF)
            s_m = _dot_nt(qj, km) + bmeta_ref[hs, u * BLOCK:(u + 1) * BLOCK].reshape(GROUP * T, LANES)
            sink = jnp.concatenate([jnp.full((T, 1), sink_ref[GROUP * j + hh], F32) for hh in range(GROUP)], axis=0)
            m = jnp.maximum(_row_max(s_b, s_m), sink)
            e_b = jnp.exp2(s_b - m)
            e_m = jnp.exp2(s_m - m)
            denom = _row_sum(e_b, e_m) + jnp.exp2(sink - m)
            acc = _dot(e_b.astype(BF16), vband) + _dot(e_m.astype(BF16), vm)
            o_ref[u * BLOCK:(u + 1) * BLOCK, j * 2 * LANES:(j + 1) * 2 * LANES] = (
                _ungroup_outputs(acc / denom, j, T).astype(BF16))


def _na_kernel(q_ref, k_ref, v_ref, km_ref, vm_ref, bias_ref, mbias_ref, *rest, rows, qrows):
    o_ref = rest[-1]
    blk = pl.program_id(1)
    W = GRID_W
    nkeys = NA_ROWS * W
    km = _pad_meta_rows(km_ref[...])
    vm = _pad_meta_rows(vm_ref[...])
    mb = _head_rows([mbias_ref[h:h + 1, :] for h in range(N_HEADS)], W)
    half = GROUP * W
    for rr in range(qrows):
        r = blk * qrows + rr
        rs = jnp.clip(r - NA_ROWS // 2, 0, rows - NA_ROWS)
        delta = r - rs
        koff = pl.multiple_of(rs * W, W)
        q = q_ref[rr * W:(rr + 1) * W, :].astype(F32)
        q8 = jnp.concatenate([_group_queries(q, j) for j in range(N_KV_HEADS)], axis=0)
        kw = k_ref[pl.ds(koff, nkeys), :]
        vw = v_ref[pl.ds(koff, nkeys), :]
        s_w = _dot_nt(q8, kw) + bias_ref[delta].reshape(N_HEADS * W, nkeys)
        s_m = _dot_nt(q8, km) + mb
        m = _row_max(s_w, s_m)
        e_w = jnp.exp2(s_w - m)
        e_m = jnp.exp2(s_m - m)
        out = (_dot(e_w.astype(BF16), vw) + _dot(e_m.astype(BF16), vm)) / _row_sum(e_w, e_m)
        for j in range(N_KV_HEADS):
            o_ref[rr * W:(rr + 1) * W, j * 2 * LANES:(j + 1) * 2 * LANES] = (
                _ungroup_outputs(out[j * half:(j + 1) * half], j, W).astype(BF16))


def _meta_kernel(sink_ref, q_ref, ka_ref, va_ref, kb_ref, vb_ref, km_ref, vm_ref,
                 bq_meta_ref, bq_blk_ref, mbias_ref, o_in_ref, o_ref):
    del o_in_ref
    T = N_META
    q = q_ref[...].astype(F32)
    km_all = km_ref[...]
    vm_all = vm_ref[...]

    def finish(n, j, acc, denom):
        lo = n * MIX_WIDTH + j * 2 * LANES
        o_ref[:, lo:lo + 2 * LANES] = _ungroup_outputs(acc / denom, j, T).astype(BF16)

    for j in range(N_KV_HEADS):
        hs = slice(GROUP * j, GROUP * (j + 1))
        qj = _group_queries(q[:, 0:MIX_WIDTH], j)
        km, vm = km_all[:, 0:KV_WIDTH], vm_all[:, 0:KV_WIDTH]
        s_r = _dot_nt(qj, ka_ref[...])
        s_m = _dot_nt(qj, km)
        m = jnp.maximum(jnp.max(s_r, axis=-1, keepdims=True), jnp.max(s_m, axis=-1, keepdims=True))
        e_r = jnp.exp2(s_r - m)
        e_m = jnp.exp2(s_m - m)
        denom = jnp.sum(e_r, axis=-1, keepdims=True) + jnp.sum(e_m, axis=-1, keepdims=True)
        finish(0, j, _dot(e_r.astype(BF16), va_ref[...]) + _dot(e_m.astype(BF16), vm), denom)
        qj = _group_queries(q[:, MIX_WIDTH:2 * MIX_WIDTH], j)
        km, vm = km_all[:, KV_WIDTH:2 * KV_WIDTH], vm_all[:, KV_WIDTH:2 * KV_WIDTH]
        s_r = _dot_nt(qj, kb_ref[...]) + bq_blk_ref[hs].reshape(GROUP * T, BLOCK)
        s_m = _dot_nt(qj, km) + bq_meta_ref[hs].reshape(GROUP * T, N_META)
        sink = jnp.concatenate([jnp.full((T, 1), sink_ref[GROUP * j + hh], F32) for hh in range(GROUP)], axis=0)
        m = jnp.maximum(jnp.maximum(jnp.max(s_r, axis=-1, keepdims=True),
                                    jnp.max(s_m, axis=-1, keepdims=True)), sink)
        e_r = jnp.exp2(s_r - m)
        e_m = jnp.exp2(s_m - m)
        denom = (jnp.sum(e_r, axis=-1, keepdims=True) + jnp.sum(e_m, axis=-1, keepdims=True)
                 + jnp.exp2(sink - m))
        finish(1, j, _dot(e_r.astype(BF16), vb_ref[...]) + _dot(e_m.astype(BF16), vm), denom)
        qj = _group_queries(q[:, 2 * MIX_WIDTH:3 * MIX_WIDTH], j)
        km, vm = km_all[:, 2 * KV_WIDTH:3 * KV_WIDTH], vm_all[:, 2 * KV_WIDTH:3 * KV_WIDTH]
        mb = _head_rows([mbias_ref[GROUP * j + hh:GROUP * j + hh + 1, :] for hh in range(GROUP)], T)
        s_m = _dot_nt(qj, km) + mb
        m = jnp.max(s_m, axis=-1, keepdims=True)
        e_m = jnp.exp2(s_m - m)
        finish(2, j, _dot(e_m.astype(BF16), vm), jnp.sum(e_m, axis=-1, keepdims=True))


class _Group:
    def __init__(self, B, S, real_base, meta_batch_base, meta_base, n_meta_blocks):
        self.B, self.S = B, S
        self.real_base = real_base
        self.meta_blk0 = meta_base // N_META + meta_batch_base
        self.n_meta_blocks = n_meta_blocks
        assert real_base % S == 0 and S % TM == 0 and meta_base % N_META == 0


def _alias_args(o_prev, n_inputs):
    if o_prev is None:
        return [], [], {}
    return [o_prev], [pl.BlockSpec(memory_space=pl.ANY)], {n_inputs: 0}


def _global_attn(grp, q_all, k_all, v_ones, o_prev):
    B, S = grp.B, grp.S
    nq = S // TQ_GLOBAL
    qb0 = grp.real_base // TQ_GLOBAL
    sb0 = grp.real_base // S
    mb0 = grp.meta_blk0
    in_specs = [pl.BlockSpec((TQ_GLOBAL, MIX_WIDTH), lambda b, i: (qb0 + b * nq + i, 0)),
                pl.BlockSpec((S, KV_WIDTH), lambda b, i: (sb0 + b, 0)),
                pl.BlockSpec((S, N_KV_HEADS * LANES), lambda b, i: (sb0 + b, 0)),
                pl.BlockSpec((N_META, KV_WIDTH), lambda b, i: (mb0 + b, 0)),
                pl.BlockSpec((N_META, N_KV_HEADS * LANES), lambda b, i: (mb0 + b, 0))]
    extra, extra_specs, aliases = _alias_args(o_prev, len(in_specs))
    return pl.pallas_call(
        functools.partial(_global_kernel, S=S),
        grid=(B, nq),
        in_specs=in_specs + extra_specs,
        out_specs=pl.BlockSpec((TQ_GLOBAL, MIX_WIDTH), lambda b, i: (qb0 + b * nq + i, 0)),
        out_shape=jax.ShapeDtypeStruct((q_all.shape[0], N_MIXERS * MIX_WIDTH), BF16),
        input_output_aliases=aliases,
        compiler_params=_cparams(("parallel", "arbitrary")),
        name="mixer_global",
    )(q_all, k_all, v_ones, k_all, v_ones, *extra)


def _window_attn(grp, q_all, k_all, v_all, sink, bband, bmeta, o_prev):
    B, S = grp.B, grp.S
    wq = math.gcd(WIN_QBLOCKS, S // BLOCK)
    span = wq * BLOCK
    nb, ns = S // BLOCK, S // span
    qb0 = grp.real_base // BLOCK
    sp0 = grp.real_base // span
    mb0 = grp.meta_blk0
    cur = lambda b, i, sink: (sp0 + b * ns + i, 1)
    prv = lambda b, i, sink: (qb0 + b * nb + jnp.maximum(i * wq - 1, 0), 1)
    nxt = lambda b, i, sink: (qb0 + b * nb + jnp.minimum((i + 1) * wq, nb - 1), 1)
    met = lambda b, i, sink: (mb0 + b, 1)
    edge = lambda im: pl.BlockSpec((BLOCK, KV_WIDTH), im)
    mid = pl.BlockSpec((span, KV_WIDTH), cur)
    in_specs = [pl.BlockSpec((span, MIX_WIDTH), cur),
                edge(prv), mid, edge(nxt), edge(prv), mid, edge(nxt),
                pl.BlockSpec((N_META, KV_WIDTH), met), pl.BlockSpec((N_META, KV_WIDTH), met),
                pl.BlockSpec((N_HEADS, BLOCK, 3 * BLOCK), lambda b, i, sink: (0, 0, 0)),
                pl.BlockSpec((N_HEADS, span, LANES), lambda b, i, sink: (0, i, 0))]
    extra, extra_specs, aliases = _alias_args(o_prev, len(in_specs) + 1)
    return pl.pallas_call(
        functools.partial(_window_kernel, nb=ns, wq=wq),
        grid_spec=pltpu.PrefetchScalarGridSpec(
            num_scalar_prefetch=1,
            grid=(B, ns),
            in_specs=in_specs + extra_specs,
            out_specs=pl.BlockSpec((span, MIX_WIDTH), cur)),
        out_shape=jax.ShapeDtypeStruct((q_all.shape[0], N_MIXERS * MIX_WIDTH), BF16),
        input_output_aliases=aliases,
        compiler_params=_cparams(("parallel", "arbitrary")),
        name="mixer_window",
    )(sink, q_all, k_all, k_all, k_all, v_all, v_all, v_all, k_all, v_all, bband, bmeta, *extra)


def _na_attn(grp, q_all, k_all, v_all, na_bias, na_mbias, o_prev):
    B, S = grp.B, grp.S
    rows = S // GRID_W
    qrows = math.gcd(NA_QROWS, rows)
    tq = qrows * GRID_W
    nq = S // tq
    qb0 = grp.real_base // tq
    sb0 = grp.real_base // S
    mb0 = grp.meta_blk0
    in_specs = [pl.BlockSpec((tq, MIX_WIDTH), lambda b, i: (qb0 + b * nq + i, 2)),
                pl.BlockSpec((S, KV_WIDTH), lambda b, i: (sb0 + b, 2)),
                pl.BlockSpec((S, KV_WIDTH), lambda b, i: (sb0 + b, 2)),
                pl.BlockSpec((N_META, KV_WIDTH), lambda b, i: (mb0 + b, 2)),
                pl.BlockSpec((N_META, KV_WIDTH), lambda b, i: (mb0 + b, 2)),
                pl.BlockSpec(na_bias.shape, lambda b, i: (0, 0, 0, 0)),
                pl.BlockSpec(na_mbias.shape, lambda b, i: (0, 0))]
    extra, extra_specs, aliases = _alias_args(o_prev, len(in_specs))
    return pl.pallas_call(
        functools.partial(_na_kernel, rows=rows, qrows=qrows),
        grid=(B, nq),
        in_specs=in_specs + extra_specs,
        out_specs=pl.BlockSpec((tq, MIX_WIDTH), lambda b, i: (qb0 + b * nq + i, 2)),
        out_shape=jax.ShapeDtypeStruct((q_all.shape[0], N_MIXERS * MIX_WIDTH), BF16),
        input_output_aliases=aliases,
        compiler_params=_cparams(("parallel", "arbitrary")),
        name="mixer_neighbourhood",
    )(q_all, k_all, v_all, k_all, v_all, na_bias, na_mbias, *extra)


def _meta_attn(grp, q_all, k_all, v_all, sink, bq_meta, bq_blk, na_mbias, o_prev):
    B, S = grp.B, grp.S
    sb0 = grp.real_base // S
    bb0 = grp.real_base // BLOCK
    nb = S // BLOCK
    mb0 = grp.meta_blk0
    clamp = lambda b: jnp.minimum(b, B - 1)
    mrow = lambda b, sink: (mb0 + clamp(b), 0)
    in_specs = [pl.BlockSpec((N_META, N_MIXERS * MIX_WIDTH), mrow),
                pl.BlockSpec((S, KV_WIDTH), lambda b, sink: (sb0 + clamp(b), 0)),
                pl.BlockSpec((S, KV_WIDTH), lambda b, sink: (sb0 + clamp(b), 0)),
                pl.BlockSpec((BLOCK, KV_WIDTH), lambda b, sink: (bb0 + clamp(b) * nb, 1)),
                pl.BlockSpec((BLOCK, KV_WIDTH), lambda b, sink: (bb0 + clamp(b) * nb, 1)),
                pl.BlockSpec((N_META, N_MIXERS * KV_WIDTH), mrow),
                pl.BlockSpec((N_META, N_MIXERS * KV_WIDTH), mrow),
                pl.BlockSpec(bq_meta.shape, lambda b, sink: (0, 0, 0)),
                pl.BlockSpec(bq_blk.shape, lambda b, sink: (0, 0, 0)),
                pl.BlockSpec(na_mbias.shape, lambda b, sink: (0, 0)),
                pl.BlockSpec(memory_space=pl.ANY)]
    return pl.pallas_call(
        _meta_kernel,
        grid_spec=pltpu.PrefetchScalarGridSpec(
            num_scalar_prefetch=1,
            grid=(grp.n_meta_blocks,),
            in_specs=in_specs,
            out_specs=pl.BlockSpec((N_META, N_MIXERS * MIX_WIDTH), lambda b, sink: (mb0 + b, 0))),
        out_shape=jax.ShapeDtypeStruct((q_all.shape[0], N_MIXERS * MIX_WIDTH), BF16),
        input_output_aliases={len(in_specs): 0},
        compiler_params=_cparams(("arbitrary",)),
        name="mixer_meta_queries",
    )(sink, q_all, k_all, v_all, k_all, v_all, k_all, v_all, bq_meta, bq_blk, na_mbias, o_prev)


def _route(h1, wr_hi, wr_lo, rbias):
    T = h1.shape[0]
    x_hi, x_lo = _split_bf16(h1)
    logits = _dot_nt(wr_hi, x_hi) + _dot_nt(wr_hi, x_lo) + _dot_nt(wr_lo, x_hi)
    scores = 1.0 / (1.0 + jnp.exp(-logits))
    sel = scores + rbias
    per_group = N_EXPERTS // N_EXPERT_GROUPS
    sel3 = sel.reshape(N_EXPERT_GROUPS, per_group, T)
    idx3 = lax.broadcasted_iota(jnp.int32, sel3.shape, 1).astype(F32)
    m1 = jnp.max(sel3, axis=1, keepdims=True)
    first = jnp.min(jnp.where(sel3 == m1, idx3, float(per_group)), axis=1, keepdims=True)
    m2 = jnp.max(jnp.where(idx3 == first, -jnp.inf, sel3), axis=1, keepdims=True)
    gscore = (m1 + m2).reshape(N_EXPERT_GROUPS, T)

    def rank_of(vals):
        idx = lax.broadcasted_iota(jnp.int32, vals.shape, 0)
        rank = jnp.zeros(vals.shape, F32)
        for r in range(vals.shape[0]):
            row = vals[r:r + 1, :]
            ge = jnp.where(row >= vals, 1.0, 0.0)
            gt = jnp.where(row > vals, 1.0, 0.0)
            rank = rank + jnp.where(idx > r, ge, gt)
        return rank

    gkeep = jnp.where(rank_of(gscore) < TOPK_GROUPS, 1.0, 0.0)
    ekeep = jnp.broadcast_to(gkeep.reshape(N_EXPERT_GROUPS, 1, T), sel3.shape).reshape(N_EXPERTS, T)
    masked = jnp.where(ekeep > 0.5, sel, NEG_INF)
    eidx = lax.broadcasted_iota(jnp.int32, masked.shape, 0).astype(F32)
    chosen = jnp.zeros(masked.shape, F32)
    for _ in range(TOP_K):
        best = jnp.max(masked, axis=0, keepdims=True)
        first = jnp.min(jnp.where(masked == best, eidx, float(N_EXPERTS)), axis=0, keepdims=True)
        hit = eidx == first
        chosen = jnp.where(hit, 1.0, chosen)
        masked = jnp.where(hit, -jnp.inf, masked)
    w = jnp.where(chosen > 0.5, scores, 0.0)
    return w / jnp.sum(w, axis=0, keepdims=True) * ROUTED_SCALE


def _merge_kernel(h_ref, o_ref, wg_ref, wb_ref, wo_ref, g_ref, b_ref, wrh_ref, wrl_ref, rb_ref,
                  h1_ref, gates_ref, *, alpha):
    h = h_ref[...]
    x = h.astype(BF16)
    merged = None
    for n in range(N_MIXERS):
        logit = _dot(x, wg_ref[:, n * D_MODEL:(n + 1) * D_MODEL])
        branch = _dot(o_ref[:, n * MIX_WIDTH:(n + 1) * MIX_WIDTH], wb_ref[n])
        term = branch / (1.0 + jnp.exp(-logit))
        merged = term if merged is None else merged + term
    mix = _dot(merged.astype(BF16), wo_ref[...])
    h1 = _layer_norm(alpha * h + mix, g_ref[...], b_ref[...])
    h1_ref[...] = h1
    gates_t = _route(h1, wrh_ref[...], wrl_ref[...], rb_ref[...])
    pad = jnp.zeros((LANES - N_EXPERTS, gates_t.shape[1]), F32)
    gates_ref[...] = jnp.concatenate([gates_t, pad], axis=0).T


def _merge(h, o_all, w_gate, w_branch, w_out, ln_g, ln_b, wr_hi, wr_lo, rbias, alpha, tm):
    R = h.shape[0]
    c2 = lambda i: (0, 0)
    once = pl.Buffered(1)
    return pl.pallas_call(
        functools.partial(_merge_kernel, alpha=alpha),
        grid=(R // tm,),
        in_specs=[pl.BlockSpec((tm, D_MODEL), lambda i: (i, 0)),
                  pl.BlockSpec((tm, N_MIXERS * MIX_WIDTH), lambda i: (i, 0)),
                  pl.BlockSpec(w_gate.shape, c2, pipeline_mode=once),
                  pl.BlockSpec(w_branch.shape, lambda i: (0, 0, 0), pipeline_mode=once),
                  pl.BlockSpec(w_out.shape, c2, pipeline_mode=once),
                  pl.BlockSpec((1, D_MODEL), c2),
                  pl.BlockSpec((1, D_MODEL), c2),
                  pl.BlockSpec(wr_hi.shape, c2),
                  pl.BlockSpec(wr_lo.shape, c2),
                  pl.BlockSpec(rbias.shape, c2)],
        out_specs=[pl.BlockSpec((tm, D_MODEL), lambda i: (i, 0)),
                   pl.BlockSpec((tm, LANES), lambda i: (i, 0))],
        out_shape=[jax.ShapeDtypeStruct((R, D_MODEL), F32),
                   jax.ShapeDtypeStruct((R, LANES), F32)],
        compiler_params=_cparams(("parallel",)),
        name="merge_ln_route",
    )(h, o_all, w_gate, w_branch, w_out, ln_g, ln_b, wr_hi, wr_lo, rbias)


TD = 256
CH = 16
SLOTS = 3072
NCH = SLOTS // CH
MT = 512
ME = 1024
CPM = ME // CH
XBUFS = 4
CBUFS = 4
CORE_CHUNKS = TOP_K * TD // CH
TAIL_GROUP = 8


def _swiglu_act(gu):
    g = gu[:, :D_EXPERT]
    return g / (1.0 + jnp.exp(-g)) * gu[:, D_EXPERT:]


def _slot_of_token(gates, lo_row):
    routed = gates > 0.0
    r = lax.broadcasted_iota(jnp.int32, (TD, TD), 0)
    c = lax.broadcasted_iota(jnp.int32, (TD, TD), 1)
    earlier = jnp.where(c < r, 1.0, 0.0).astype(BF16)
    rank = _dot(earlier, jnp.where(routed, 1.0, 0.0).astype(BF16))
    return jnp.where(routed, lo_row + rank + 1.0, 0.0)


def _split64(x):
    hi = 64.0 * jnp.floor(x * (1.0 / 64.0))
    return jnp.concatenate([hi.astype(BF16), (x - hi).astype(BF16)], axis=0)


def _dispatch_kernel(h_ref, g_ref, lohi_ref, x_ref, w_ref):
    lohi = lohi_ref[0]
    lo_row, hi_row = lohi[0:1], lohi[1:2]
    gates = g_ref[...]
    slot_t = _split64(_slot_of_token(gates, lo_row).T)
    gates_t = gates.T.astype(BF16)
    x = h_ref[...].astype(BF16)
    for blk in range(SLOTS // MT):
        s = (lax.broadcasted_iota(jnp.int32, (MT, LANES), 0) + blk * MT).astype(F32)
        owner = jnp.where(s >= lo_row, jnp.where(s < hi_row, 1.0, 0.0), 0.0).astype(BF16)
        want = _dot(jnp.concatenate([owner, owner], axis=1), slot_t)
        s1 = (lax.broadcasted_iota(jnp.int32, (MT, TD), 0) + (blk * MT + 1)).astype(F32)
        hit = want == s1
        x_ref[blk * MT:(blk + 1) * MT, :] = _dot(jnp.where(hit, 1.0, 0.0).astype(BF16), x).astype(BF16)
        weight = jnp.where(hit, _dot(owner, gates_t), 0.0)
        w_ref[:, blk * MT:(blk + 1) * MT] = weight.T.astype(BF16)


def _dispatch(h1, gates, lohi):
    n = h1.shape[0] // TD
    return pl.pallas_call(
        _dispatch_kernel,
        grid=(n,),
        in_specs=[pl.BlockSpec((TD, D_MODEL), lambda i: (i, 0)),
                  pl.BlockSpec((TD, LANES), lambda i: (i, 0)),
                  pl.BlockSpec((1, 8, LANES), lambda i: (i, 0, 0))],
        out_specs=[pl.BlockSpec((SLOTS, D_MODEL), lambda i: (i, 0)),
                   pl.BlockSpec((TD, SLOTS), lambda i: (i, 0))],
        out_shape=[jax.ShapeDtypeStruct((n * SLOTS, D_MODEL), BF16),
                   jax.ShapeDtypeStruct((n * TD, SLOTS), BF16)],
        compiler_params=_cparams(("parallel",)),
        name="moe_dispatch",
    )(h1, gates, lohi)


def _chunk_gather(table_ref, first, n_chunks, src_hbm, buf, c0, sem):
    return [pltpu.make_async_copy(src_hbm.at[table_ref[first + c]], buf.at[c0 + c], sem) for c in range(n_chunks)]


def _chunk_wait(n_chunks, src_hbm, buf, c0, sem):
    for c in range(n_chunks):
        pltpu.make_async_copy(src_hbm.at[0], buf.at[c0 + c], sem).wait()


def _expert_kernel(te_ref, src_ref, nu_ref, x_hbm, wgu_ref, wd_ref, y_ref, xbuf, sem):
    del te_ref
    m = pl.program_id(0)
    n_used = nu_ref[0]

    def start(step):
        slot = step % XBUFS
        for cp in _chunk_gather(src_ref, step * CPM, CPM, x_hbm, xbuf.at[slot], 0, sem.at[slot]):
            cp.start()

    for ahead in range(XBUFS - 1):
        @pl.when((m == 0) & (ahead < n_used))
        def _(ahead=ahead):
            start(ahead)

    @pl.when(m + (XBUFS - 1) < n_used)
    def _():
        start(m + (XBUFS - 1))

    @pl.when(m < n_used)
    def _():
        slot = m % XBUFS
        _chunk_wait(CPM, x_hbm, xbuf.at[slot], 0, sem.at[slot])
        act = _swiglu_act(_dot(xbuf[slot].reshape(ME, D_MODEL), wgu_ref[0]))
        y_ref[...] = _dot(act.astype(BF16), wd_ref[0]).astype(BF16)


def _experts(x_disp, wgu, wd, tile_expert, src_chunk, n_used):
    n_steps = tile_expert.shape[0]
    return pl.pallas_call(
        _expert_kernel,
        grid_spec=pltpu.PrefetchScalarGridSpec(
            num_scalar_prefetch=3,
            grid=(n_steps,),
            in_specs=[pl.BlockSpec(memory_space=pl.ANY),
                      pl.BlockSpec((1, D_MODEL, 2 * D_EXPERT), lambda m, te, src, nu: (te[m], 0, 0)),
                      pl.BlockSpec((1, D_EXPERT, D_MODEL), lambda m, te, src, nu: (te[m], 0, 0))],
            out_specs=pl.BlockSpec((ME, D_MODEL), lambda m, te, src, nu: (jnp.minimum(m, nu[0] - 1), 0)),
            scratch_shapes=[pltpu.VMEM((XBUFS, CPM, CH, D_MODEL), BF16), pltpu.SemaphoreType.DMA((XBUFS,))]),
        out_shape=jax.ShapeDtypeStruct((n_steps * ME, D_MODEL), BF16),
        compiler_params=_cparams(("arbitrary",)),
        name="moe_experts",
    )(tile_expert, src_chunk, n_used, x_disp.reshape(-1, CH, D_MODEL), wgu, wd)


def _combine_kernel(dst_ref, nct_ref, y_hbm, w_ref, h_ref, wsgu_ref, wsd_ref, lg_ref, lb_ref, o_ref, ybuf, sem,
                    *, alpha):
    i = pl.program_id(0)
    n_tiles = pl.num_programs(0)

    def transfer(tile, wait):
        slot = tile % CBUFS
        buf, s = ybuf.at[slot], sem.at[slot]
        used = nct_ref[tile]

        def run(first, count):
            if wait:
                _chunk_wait(count, y_hbm, buf, first, s)
            else:
                for cp in _chunk_gather(dst_ref, tile * NCH + first, count, y_hbm, buf, first, s):
                    cp.start()

        run(0, CORE_CHUNKS)
        for first in range(CORE_CHUNKS, NCH, TAIL_GROUP):
            pl.when(used > first)(functools.partial(run, first, TAIL_GROUP))

    @pl.when(i == 0)
    def _():
        for slot in range(CBUFS):
            ybuf[slot] = jnp.zeros((NCH, CH, D_MODEL), BF16)
        for ahead in range(CBUFS - 1):
            pl.when(ahead < n_tiles)(functools.partial(transfer, ahead, False))

    @pl.when(i + (CBUFS - 1) < n_tiles)
    def _():
        transfer(i + (CBUFS - 1), False)

    h = h_ref[...]
    shared = _dot(_swiglu_act(_dot(h.astype(BF16), wsgu_ref[...])).astype(BF16), wsd_ref[...])
    transfer(i, True)
    routed = _dot(w_ref[...], ybuf[i % CBUFS].reshape(SLOTS, D_MODEL))
    o_ref[...] = _layer_norm(alpha * h + shared + routed, lg_ref[...], lb_ref[...])


def _combine(y_sorted, w_t, h1, dst_chunk, tile_chunks, wsgu, wsd, ln_g, ln_b, alpha):
    n = h1.shape[0] // TD
    c2 = lambda i, dst, nct: (0, 0)
    return pl.pallas_call(
        functools.partial(_combine_kernel, alpha=alpha),
        grid_spec=pltpu.PrefetchScalarGridSpec(
            num_scalar_prefetch=2,
            grid=(n,),
            in_specs=[pl.BlockSpec(memory_space=pl.ANY),
                      pl.BlockSpec((TD, SLOTS), lambda i, dst, nct: (i, 0)),
                      pl.BlockSpec((TD, D_MODEL), lambda i, dst, nct: (i, 0)),
                      pl.BlockSpec(wsgu.shape, c2),
                      pl.BlockSpec(wsd.shape, c2),
                      pl.BlockSpec((1, D_MODEL), c2),
                      pl.BlockSpec((1, D_MODEL), c2)],
            out_specs=pl.BlockSpec((TD, D_MODEL), lambda i, dst, nct: (i, 0)),
            scratch_shapes=[pltpu.VMEM((CBUFS, NCH, CH, D_MODEL), BF16), pltpu.SemaphoreType.DMA((CBUFS,))]),
        out_shape=jax.ShapeDtypeStruct((h1.shape[0], D_MODEL), F32),
        compiler_params=_cparams(("arbitrary",)),
        name="moe_combine_ln",
    )(dst_chunk, tile_chunks, y_sorted.reshape(-1, CH, D_MODEL), w_t, h1, wsgu, wsd, ln_g, ln_b)


def _routing_tables(gates):
    n = gates.shape[0] // TD
    cnt = jnp.sum((gates[:, :N_EXPERTS] > 0.0).reshape(n, TD, N_EXPERTS), axis=1, dtype=jnp.int32)
    nch = (cnt + (CH - 1)) // CH
    hi16 = jnp.cumsum(nch, axis=1)
    lo16 = hi16 - nch
    nct = hi16[:, -1:]
    pad = jnp.broadcast_to(nct, (n, LANES - N_EXPERTS))
    lohi = jnp.zeros((n, 8, LANES), F32)
    lohi = lohi.at[:, 0, :].set((jnp.concatenate([lo16, pad], axis=1) * CH).astype(F32))
    lohi = lohi.at[:, 1, :].set((jnp.concatenate([hi16, pad], axis=1) * CH).astype(F32))
    tot = jnp.sum(nch, axis=0)
    seg_len = (tot + (CPM - 1)) // CPM * CPM
    seg_end = jnp.cumsum(seg_len)
    seg_start = seg_end - seg_len
    gpos = seg_start[None, :] + jnp.cumsum(nch, axis=0) - nch
    assert (n * NCH) % CPM == 0
    n_steps = (n * NCH + N_EXPERTS * CPM) // CPM
    n_used = (seg_end[-1] // CPM).astype(jnp.int32).reshape(1)
    step = jnp.arange(n_steps, dtype=jnp.int32)
    tile_expert = jnp.sum(seg_end[None, :] // CPM <= jnp.minimum(step, n_used - 1)[:, None], axis=1, dtype=jnp.int32)
    tile_expert = jnp.minimum(tile_expert, N_EXPERTS - 1)
    exact = functools.partial(jnp.dot, precision=lax.Precision.HIGHEST)
    experts = jnp.arange(N_EXPERTS, dtype=jnp.int32)
    g = jnp.arange(n_steps * CPM, dtype=jnp.int32)
    e_of_g = jnp.minimum(jnp.sum(seg_end[None, :] <= g[:, None], axis=1, dtype=jnp.int32), N_EXPERTS - 1)
    pick_e = (e_of_g[:, None] == experts[None, :]).astype(F32)
    first = exact(pick_e, gpos.T.astype(F32))
    count = exact(pick_e, nch.T.astype(F32))
    base = exact(pick_e, (jnp.arange(n, dtype=jnp.int32)[:, None] * NCH + lo16).T.astype(F32))
    gf = g.astype(F32)[:, None]
    inside = (first <= gf) & (gf < first + count)
    src_chunk = jnp.sum(jnp.where(inside, base + gf - first, 0.0), axis=1).astype(jnp.int32)
    k = jnp.arange(NCH, dtype=jnp.int32)
    e_of_k = jnp.minimum(jnp.sum(hi16[:, None, :] <= k[None, :, None], axis=2, dtype=jnp.int32), N_EXPERTS - 1)
    pick_k = e_of_k[:, :, None] == experts[None, None, :]
    pos = jnp.sum(jnp.where(pick_k, (gpos - lo16)[:, None, :], 0), axis=2) + k[None, :]
    dst_chunk = jnp.where(k[None, :] < nct, pos, 0).astype(jnp.int32).reshape(-1)
    return lohi, tile_expert, src_chunk, n_used, dst_chunk, nct.reshape(-1)


def _moe(h1, gates, wgu, wd, wsgu, wsd, ln_g, ln_b, alpha):
    lohi, tile_expert, src_chunk, n_used, dst_chunk, tile_chunks = _routing_tables(gates)
    x_disp, w_t = _dispatch(h1, gates, lohi)
    y_sorted = _experts(x_disp, wgu, wd, tile_expert, src_chunk, n_used)
    return _combine(y_sorted, w_t, h1, dst_chunk, tile_chunks, wsgu, wsd, ln_g, ln_b, alpha)


def _t5_bucket(rel):
    half = T5_BUCKETS // 2
    max_exact = half // 2
    n = np.abs(rel)
    ratio = np.log(np.maximum(n, 1).astype(np.float32) / np.float32(max_exact))
    ratio = ratio / np.float32(math.log(T5_MAX_DIST / max_exact)) * np.float32(half - max_exact)
    large = np.minimum(max_exact + ratio.astype(np.int32), half - 1)
    return np.where(rel > 0, half, 0) + np.where(n < max_exact, n, large)


def _t5_tables(t5_table, s_max):
    def bias(rel, valid):
        onehot = np.eye(T5_BUCKETS, dtype=np.float32)[_t5_bucket(rel)]
        b = jnp.einsum("qkb,bh->hqk", jnp.asarray(onehot), t5_table.astype(F32), precision=lax.Precision.HIGHEST)
        return jnp.where(jnp.asarray(valid)[None], b, NEG_INF)

    ii = np.arange(BLOCK)[:, None]
    jj = np.arange(3 * BLOCK)[None, :]
    rel = jj - ii - BLOCK
    bband = bias(rel, np.abs(rel) <= WINDOW)
    t = np.arange(s_max)[:, None]
    m = np.arange(N_META)[None, :]
    bmeta = bias(m - (N_META + t), np.ones((s_max, N_META), bool))
    mpos = np.arange(N_META)[:, None]
    kpos = np.arange(N_META + BLOCK)[None, :]
    relq = kpos - mpos
    bq = bias(relq, (kpos < N_META) | (np.abs(relq) <= WINDOW))
    return bband, bmeta, bq[:, :, :N_META], bq[:, :, N_META:]


def _na_bias_cases(rpb):
    W = GRID_W
    c = np.arange(W)[:, None]
    kc = np.arange(W)[None, :]
    cs = np.clip(c - NA_COLS // 2, 0, W - NA_COLS)
    valid = (kc >= cs) & (kc < cs + NA_COLS)
    dc = np.clip(kc - c + (NA_COLS - 1), 0, 2 * NA_COLS - 2)
    onehot = np.eye(2 * NA_COLS - 1, dtype=np.float32)[dc]
    t = jnp.einsum("hrd,ckd->hrck", rpb.astype(F32), jnp.asarray(onehot), precision=lax.Precision.HIGHEST)
    t = jnp.where(jnp.asarray(valid), t, NEG_INF)
    cases = [jnp.transpose(t[:, NA_ROWS - 1 - d:2 * NA_ROWS - 1 - d], (0, 2, 1, 3)) for d in range(NA_ROWS)]
    return jnp.stack(cases, axis=0).reshape(NA_ROWS, N_HEADS, W, NA_ROWS * W)


def _rope_tables(s_max, n_meta_rows):
    half = HEAD_DIM // 4
    freq = ROPE_THETA ** (-jnp.arange(half, dtype=F32) / half)
    t = np.arange(s_max)
    mp = np.tile(np.arange(N_META) - N_META, n_meta_rows // N_META)
    pos_row = jnp.asarray(np.concatenate([t // GRID_W, mp]), jnp.int32).astype(F32)
    pos_col = jnp.asarray(np.concatenate([t % GRID_W, mp]), jnp.int32).astype(F32)
    ar = pos_row[:, None] * freq
    ac = pos_col[:, None] * freq
    cos = jnp.concatenate([jnp.cos(ar), jnp.cos(ar), jnp.cos(ac), jnp.cos(ac)], axis=1)
    sin = jnp.concatenate([-jnp.sin(ar), jnp.sin(ar), -jnp.sin(ac), jnp.sin(ac)], axis=1)
    return jnp.tile(cos, (1, 2)), jnp.tile(sin, (1, 2))


def kernel(x_prompt, x_sample, meta_tokens, ln_in_g, ln_in_b, t5_table, w_in, q_gain, k_gain, sink,
           na_rpb, na_meta_bias, w_branch, w_out, ln1_g, ln1_b, w_router, router_bias,
           w_expert_gate_up, w_expert_down, w_shared_gate_up, w_shared_down, ln2_g, ln2_b):
    depth = w_in.shape[0]
    alpha = (2 * depth) ** 0.25
    B0, S0, D = x_prompt.shape
    B1, S1, _ = x_sample.shape
    assert D == D_MODEL
    real = B0 * S0 + B1 * S1
    n_meta_rows = -(-(B0 + B1) * N_META // TM) * TM
    R = real + n_meta_rows
    n_meta_blocks = n_meta_rows // N_META
    g0 = _Group(B0, S0, 0, 0, real, B0)
    g1 = _Group(B1, S1, B0 * S0, B0, real, n_meta_blocks - B0)
    s_max = max(S0, S1)

    h = _embed_ln(x_prompt.reshape(B0 * S0, D), x_sample.reshape(B1 * S1, D),
                  jnp.tile(meta_tokens, (TM // N_META, 1)), ln_in_g.reshape(1, D), ln_in_b.reshape(1, D), R)

    tm = 2 * TM if all(v % (2 * TM) == 0 for v in (S0, S1, n_meta_rows)) else TM
    cos_tab, sin_tab = _rope_tables(s_max, tm)
    n0, n1 = B0 * S0 // tm, real // tm
    p0, p1, pm = S0 // tm, S1 // tm, s_max // tm

    def pos_block(i):
        return jnp.where(i < n0, i % p0, jnp.where(i < n1, (i - n0) % p1, pm))

    bband, bmeta, bq_meta, bq_blk = (t * LOG2E for t in _t5_tables(t5_table, s_max))
    bmeta_wide = jnp.pad(bmeta, ((0, 0), (0, 0), (0, LANES - N_META)), constant_values=NEG_INF)
    ones_bd = jnp.asarray(np.kron(np.eye(N_HEADS), np.ones((HEAD_DIM, HEAD_DIM))), BF16)

    qs, ks, vs = [], [], []
    for n in range(N_MIXERS):
        off = n * QKV_WIDTH
        qs.append(w_in[:, :, off:off + MIX_WIDTH])
        ks.append(w_in[:, :, off + MIX_WIDTH:off + MIX_WIDTH + KV_WIDTH])
        vs.append(w_in[:, :, off + MIX_WIDTH + KV_WIDTH:off + QKV_WIDTH])
    wr_t = jnp.swapaxes(w_router, 1, 2)
    wr_hi = wr_t.astype(BF16)
    layers = dict(
        w_qkv=jnp.concatenate(qs + ks + vs, axis=2).astype(BF16),
        w_gate=w_in[:, :, N_MIXERS * QKV_WIDTH:].astype(BF16),
        q_gain=jnp.tile(q_gain, (1, N_HEADS)).reshape(depth, 1, MIX_WIDTH),
        k_gain=jnp.tile(k_gain, (1, N_KV_HEADS)).reshape(depth, 1, KV_WIDTH),
        sink=sink.astype(F32) * LOG2E,
        na_bias=jax.vmap(_na_bias_cases)(na_rpb) * LOG2E,
        na_mbias=na_meta_bias.astype(F32) * LOG2E,
        na_mbias_wide=jnp.pad(na_meta_bias.astype(F32) * LOG2E, ((0, 0), (0, 0), (0, LANES - N_META)),
                              constant_values=NEG_INF),
        w_branch=w_branch.astype(BF16),
        w_out=w_out.astype(BF16),
        ln1_g=ln1_g.reshape(depth, 1, D), ln1_b=ln1_b.reshape(depth, 1, D),
        wr_hi=wr_hi, wr_lo=(wr_t - wr_hi.astype(F32)).astype(BF16),
        rbias=router_bias.astype(F32).reshape(depth, N_EXPERTS, 1),
        wgu=w_expert_gate_up.astype(BF16), wd=w_expert_down.astype(BF16),
        wsgu=w_shared_gate_up.astype(BF16), wsd=w_shared_down.astype(BF16),
        ln2_g=ln2_g.reshape(depth, 1, D), ln2_b=ln2_b.reshape(depth, 1, D),
    )

    def layer(h, p):
        q_all, k_all, v_all, v_ones = _inproj(h, p["w_qkv"], cos_tab, sin_tab, p["q_gain"], p["k_gain"],
                                      ones_bd, pos_block, tm)
        o = None
        for grp in (g0, g1):
            o = _global_attn(grp, q_all, k_all, v_ones, o)
            o = _window_attn(grp, q_all, k_all, v_all, p["sink"], bband, bmeta_wide[:, :grp.S], o)
            o = _na_attn(grp, q_all, k_all, v_all, p["na_bias"], p["na_mbias_wide"], o)
        for grp in (g0, g1):
            o = _meta_attn(grp, q_all, k_all, v_all, p["sink"], bq_meta, bq_blk, p["na_mbias"], o)
        h1, gates = _merge(h, o, p["w_gate"], p["w_branch"], p["w_out"], p["ln1_g"], p["ln1_b"],
                           p["wr_hi"], p["wr_lo"], p["rbias"], alpha, tm)
        h2 = _moe(h1, gates, p["wgu"], p["wd"], p["wsgu"], p["wsd"], p["ln2_g"], p["ln2_b"],
                  alpha)
        return h2, None

    h, _ = lax.scan(layer, h, layers)
    y_prompt = h[:B0 * S0].reshape(B0, S0, D)
    y_sample = h[B0 * S0:real].reshape(B1, S1, D)
    return (y_prompt, y_sample)
```
